```python
import math
import jax, jax.numpy as jnp
from jax import lax
import numpy as np

D_MODEL = 1024
BATCH = 8
SEQ = 2048
DEPTH = 1

RET_HEADS = 4
RET_DK = 256
RET_DV = 512
RET_CHUNK = 128
RET_QK_W = RET_HEADS * RET_DK
RET_V_W = RET_HEADS * RET_DV
ATT_GROUPS = ((128, 1), (512, 4), (2048, 16))
N_ATT_GROUPS = len(ATT_GROUPS)
ATT_HEADS_PER_GROUP = 4
ATT_HEAD_DIM = 128
ATT_GROUP_W = ATT_HEADS_PER_GROUP * ATT_HEAD_DIM
N_ATT_HEADS = N_ATT_GROUPS * ATT_HEADS_PER_GROUP
REL_BUCKETS = 32
REL_MAX_DIST = 2048
D_FF = 4 * D_MODEL
N_BRANCHES = 2
RMS_EPS = 1e-6
GN_EPS = 1e-5
ROPE_BASE = 10000.0

IN_SIZES = ([RET_QK_W, RET_QK_W, RET_V_W, RET_V_W]
            + [ATT_GROUP_W] * (3 * N_ATT_GROUPS)
            + [D_MODEL] * N_BRANCHES)
IN_COLS = sum(IN_SIZES)
IN_OFFSETS = [sum(IN_SIZES[:i + 1]) for i in range(len(IN_SIZES) - 1)]

kernel_name = "hybrid_retention_dilated_attn_block"


def rms_norm(x, g):
    xf = x.astype(jnp.float32)
    y = xf * lax.rsqrt(jnp.mean(xf * xf, axis=-1, keepdims=True) + RMS_EPS)
    return (y * g.astype(jnp.float32)).astype(x.dtype)


def modulate(h, shift, scale):
    return h * (1 + scale[:, None, :]) + shift[:, None, :]


def t5_bucket(dist):
    max_exact = REL_BUCKETS // 2
    d_f = jnp.maximum(dist, 1).astype(jnp.float32)
    large = max_exact + (jnp.log(d_f / max_exact) / math.log(REL_MAX_DIST / max_exact)
                         * (REL_BUCKETS - max_exact)).astype(jnp.int32)
    large = jnp.minimum(large, REL_BUCKETS - 1)
    return jnp.where(dist < max_exact, dist, large)


def rotary(x, pos):
    half = x.shape[-1] // 2
    inv = ROPE_BASE ** (-jnp.arange(half, dtype=jnp.float32) / half)
    ang = pos.astype(jnp.float32)[:, None] * inv[None, :]
    cos, sin = jnp.cos(ang).astype(x.dtype), jnp.sin(ang).astype(x.dtype)
    x1, x2 = x[..., :half], x[..., half:]
    return jnp.concatenate([x1 * cos - x2 * sin, x1 * sin + x2 * cos], axis=-1)


def retention(q, k, v):
    B, H, S, dk = q.shape
    dv = v.shape[-1]
    C = RET_CHUNK
    nc = S // C
    log_g = jnp.log1p(-(2.0 ** (-5.0 - jnp.arange(H, dtype=jnp.float32))))
    idx = jnp.arange(C, dtype=jnp.float32)
    rel = idx[:, None] - idx[None, :]
    inner_decay = jnp.where(rel >= 0, jnp.exp(log_g[:, None, None] * jnp.maximum(rel, 0.0)), 0.0)
    q_decay = jnp.exp(log_g[:, None] * (idx + 1.0))
    k_decay = jnp.exp(log_g[:, None] * (C - 1.0 - idx))
    chunk_decay = jnp.exp(log_g * C)

    def to_chunks(t):
        return jnp.moveaxis(t.astype(jnp.float32).reshape(B, H, nc, C, t.shape[-1]), 2, 0)

    qc, kc, vc = to_chunks(q), to_chunks(k), to_chunks(v)

    def step(state, inp):
        qi, ki, vi = inp
        s = jnp.einsum('bhid,bhjd->bhij', qi, ki) * inner_decay[None]
        o = (jnp.einsum('bhij,bhje->bhie', s, vi)
             + jnp.einsum('bhid,bhde->bhie', qi, state) * q_decay[None, :, :, None])
        state = (state * chunk_decay[None, :, None, None]
                 + jnp.einsum('bhjd,bhje->bhde', ki * k_decay[None, :, :, None], vi))
        return state, o

    state0 = jnp.zeros((B, H, dk, dv), jnp.float32)
    _, o = lax.scan(step, state0, (qc, kc, vc))
    return jnp.moveaxis(o, 0, 2).reshape(B, H, S, dv)


def dilated_group(q, k, v, bias_tab, window, dilation):
    B, H, S, dh = q.shape
    w = window // dilation
    blk = w
    span = dilation * blk
    Sp = -(-S // span) * span
    L = Sp // dilation
    nb = L // blk

    def split(t):
        t = jnp.pad(t, ((0, 0), (0, 0), (0, Sp - S), (0, 0)))
        t = t.reshape(B, H, L, dilation, dh).transpose(0, 1, 3, 2, 4)
        return t.reshape(B, H, dilation, nb, blk, dh)

    qs, ks, vs = split(q), split(k), split(v)

    def with_prev(t):
        prev = jnp.pad(t[:, :, :, :-1], ((0, 0), (0, 0), (0, 0), (1, 0), (0, 0), (0, 0)))
        return jnp.concatenate([prev, t], axis=4)

    kb, vb = with_prev(ks), with_prev(vs)
    qi = jnp.arange(blk)[:, None]
    kj = jnp.arange(2 * blk)[None, :]
    m = blk + qi - kj
    band = (m >= 0) & (m <= w)
    first_ok = kj >= blk
    valid = band[None] & ((jnp.arange(nb)[:, None, None] > 0) | first_ok[None])
    bias = bias_tab[t5_bucket(jnp.clip(m, 0, w) * dilation)]
    bias = jnp.moveaxis(bias, -1, 0).astype(jnp.float32)

    s = (jnp.einsum('bhrnid,bhrnjd->bhrnij', qs, kb).astype(jnp.float32) * (dh ** -0.5)
         + bias[None, :, None, None])
    s = jnp.where(valid[None, None, None], s, -1e30)
    mx = jnp.max(s, axis=-1, keepdims=True)
    e = jnp.exp(s - mx)
    den = jnp.sum(e, axis=-1, keepdims=True)
    p = (e / den).astype(v.dtype)
    lse = (mx + jnp.log(den))[..., 0]
    o = jnp.einsum('bhrnij,bhrnjd->bhrnid', p, vb)

    def merge(t):
        t = t.reshape(B, H, dilation, L, *t.shape[5:])
        t = jnp.swapaxes(t, 2, 3)
        t = t.reshape(B, H, Sp, *t.shape[4:])
        return t[:, :, :S]

    return merge(o), merge(lse)


def token_mixer(h, w_in, rel_bias, gn_g, gn_b, w_ret_out, w_att_out, w_o):
    B, S, _ = h.shape
    proj = h @ w_in
    parts = jnp.split(proj, IN_OFFSETS, axis=-1)

    def heads(t, n):
        return t.reshape(B, S, n, -1).transpose(0, 2, 1, 3)

    pos = jnp.arange(S)
    rq, rk, rv, rg = parts[0], parts[1], parts[2], parts[3]
    rq = rotary(heads(rq, RET_HEADS), pos)
    rk = rotary(heads(rk, RET_HEADS), pos) * (RET_DK ** -0.5)
    ro = retention(rq, rk, heads(rv, RET_HEADS))
    mu = jnp.mean(ro, axis=-1, keepdims=True)
    var = jnp.mean(jnp.square(ro - mu), axis=-1, keepdims=True)
    ro = ((ro - mu) * lax.rsqrt(var + GN_EPS)).transpose(0, 2, 1, 3).reshape(B, S, RET_V_W)
    ro = (ro * gn_g.astype(jnp.float32) + gn_b.astype(jnp.float32)).astype(h.dtype)
    ret_out = (jax.nn.silu(rg) * ro) @ w_ret_out

    outs, lses = [], []
    for gi, (win, dil) in enumerate(ATT_GROUPS):
        aq, ak, av = parts[4 + 3 * gi], parts[5 + 3 * gi], parts[6 + 3 * gi]
        tab = rel_bias[:, gi * ATT_HEADS_PER_GROUP:(gi + 1) * ATT_HEADS_PER_GROUP]
        o, lse = dilated_group(heads(aq, ATT_HEADS_PER_GROUP), heads(ak, ATT_HEADS_PER_GROUP),
                               heads(av, ATT_HEADS_PER_GROUP), tab, win, dil)
        outs.append(o)
        lses.append(lse)
    o_all = jnp.stack(outs, axis=0)
    wts = jax.nn.softmax(jnp.stack(lses, axis=0), axis=0)
    att = jnp.einsum('gbhs,gbhsd->bshd', wts.astype(o_all.dtype), o_all).reshape(B, S, ATT_GROUP_W)
    att_out = att @ w_att_out

    gate_a, gate_b = parts[-2], parts[-1]
    merged = jax.nn.sigmoid(gate_a) * ret_out + jax.nn.sigmoid(gate_b) * att_out
    return merged @ w_o


def squared_relu_mlp(h, w1, w2):
    return jnp.square(jax.nn.relu(h @ w1)) @ w2


def setup_inputs(seed: int = 0) -> dict:
    key = jax.random.key(seed)
    ks = jax.random.split(key, 18)
    nrm = jax.random.normal
    f32 = jnp.float32
    return {
        "x": nrm(ks[0], (BATCH, SEQ, D_MODEL), f32),
        "c": nrm(ks[1], (BATCH, D_MODEL), f32),
        "w_ada": nrm(ks[2], (DEPTH, D_MODEL, 6 * D_MODEL), f32) * D_MODEL ** -0.5,
        "b_ada": nrm(ks[3], (DEPTH, 6 * D_MODEL), f32) * 0.02,
        "norm1_g": 1.0 + 0.02 * nrm(ks[4], (DEPTH, D_MODEL), f32),
        "w_in": nrm(ks[5], (DEPTH, D_MODEL, IN_COLS), f32) * D_MODEL ** -0.5,
        "rel_bias": nrm(ks[6], (REL_BUCKETS, N_ATT_HEADS), f32) * 0.5,
        "ret_gn_g": 1.0 + 0.02 * nrm(ks[7], (DEPTH, RET_V_W), f32),
        "ret_gn_b": 0.02 * nrm(ks[8], (DEPTH, RET_V_W), f32),
        "w_ret_out": nrm(ks[9], (DEPTH, RET_V_W, D_MODEL), f32) * RET_V_W ** -0.5,
        "w_att_out": nrm(ks[10], (DEPTH, ATT_GROUP_W, D_MODEL), f32) * ATT_GROUP_W ** -0.5,
        "w_o": nrm(ks[11], (DEPTH, D_MODEL, D_MODEL), f32) * D_MODEL ** -0.5,
        "norm2_g": 1.0 + 0.02 * nrm(ks[12], (DEPTH, D_MODEL), f32),
        "w_ff1": nrm(ks[13], (DEPTH, D_MODEL, D_FF), f32) * D_MODEL ** -0.5,
        "w_ff2": nrm(ks[14], (DEPTH, D_FF, D_MODEL), f32) * D_FF ** -0.5,
        "norm_f_g": 1.0 + 0.02 * nrm(ks[15], (D_MODEL,), f32),
    }


def reference(x, c, w_ada, b_ada, norm1_g, w_in, rel_bias, ret_gn_g, ret_gn_b,
              w_ret_out, w_att_out, w_o, norm2_g, w_ff1, w_ff2, norm_f_g):
    for l in range(DEPTH):
        mod = jax.nn.silu(c) @ w_ada[l] + b_ada[l]
        sh1, sc1, g1, sh2, sc2, g2 = jnp.split(mod, 6, axis=-1)
        h = modulate(rms_norm(x, norm1_g[l]), sh1, sc1)
        x = x + g1[:, None, :] * token_mixer(h, w_in[l], rel_bias, ret_gn_g[l], ret_gn_b[l],
                                             w_ret_out[l], w_att_out[l], w_o[l])
        h = modulate(rms_norm(x, norm2_g[l]), sh2, sc2)
        x = x + g2[:, None, :] * squared_relu_mlp(h, w_ff1[l], w_ff2[l])
    return rms_norm(x, norm_f_g)
```

```python
import functools
import math

import jax
import jax.numpy as jnp
from jax import lax
from jax.experimental import pallas as pl
from jax.experimental.pallas import tpu as pltpu

F32 = jnp.float32
BF16 = jnp.bfloat16

D_MODEL = 1024
RET_HEADS = 4
RET_DK = 256
RET_DV = 512
RET_CHUNK = 128
RET_QK_W = RET_HEADS * RET_DK
RET_V_W = RET_HEADS * RET_DV
ATT_GROUPS = ((128, 1), (512, 4), (2048, 16))
ATT_HEADS_PER_GROUP = 4
ATT_HEAD_DIM = 128
ATT_GROUP_W = ATT_HEADS_PER_GROUP * ATT_HEAD_DIM
ATT_BLK = 128
REL_BUCKETS = 32
REL_MAX_DIST = 2048
D_FF = 4 * D_MODEL
RMS_EPS = 1e-6
GN_EPS = 1e-5
ROPE_BASE = 10000.0
NEG = -1e30

IN_COLS = 2 * RET_QK_W + 2 * RET_V_W + 9 * ATT_GROUP_W + 2 * D_MODEL
COL_ATT = 2 * RET_QK_W + 2 * RET_V_W
COL_GATE = COL_ATT + 9 * ATT_GROUP_W

VMEM_LIMIT = 56 * 1024 * 1024

PROJ_TM = 1024
PROJ_TN = 1280
ROW_TM = 512


def _params(sem):
    return pltpu.CompilerParams(dimension_semantics=sem, vmem_limit_bytes=VMEM_LIMIT)


def _const_spec(shape):
    zeros = (0,) * len(shape)
    return pl.BlockSpec(shape, lambda *_: zeros, pipeline_mode=pl.Buffered(1))


def _silu(t):
    return t * jax.nn.sigmoid(t)


def _rms(x, g):
    return x * lax.rsqrt(jnp.mean(x * x, axis=-1, keepdims=True) + RMS_EPS) * g


def _mod_kernel(c_ref, w_ref, b_ref, o_ref):
    o_ref[...] = jnp.dot(_silu(c_ref[...]), w_ref[...], preferred_element_type=F32) + b_ref[...]


def _mod(c, w_ada, b_ada):
    B = c.shape[0]
    n = w_ada.shape[1]
    tn = D_MODEL
    return pl.pallas_call(
        _mod_kernel,
        out_shape=jax.ShapeDtypeStruct((B, n), F32),
        grid=(n // tn,),
        in_specs=[pl.BlockSpec((B, D_MODEL), lambda j: (0, 0)),
                  pl.BlockSpec((D_MODEL, tn), lambda j: (0, j)),
                  pl.BlockSpec((1, tn), lambda j: (0, j))],
        out_specs=pl.BlockSpec((B, tn), lambda j: (0, j)),
        compiler_params=_params(("arbitrary",)),
        name="mod",
    )(c, w_ada, b_ada.reshape(1, n))


def _proj_kernel(x_ref, g_ref, mod_ref, w_ref, o_ref, h_scr):
    @pl.when(pl.program_id(1) == 0)
    def _():
        shift = mod_ref[0, :, 0:D_MODEL]
        scale = mod_ref[0, :, D_MODEL:2 * D_MODEL]
        h_scr[...] = (_rms(x_ref[...], g_ref[...]) * (1.0 + scale) + shift).astype(BF16)

    o_ref[...] = jnp.dot(h_scr[...], w_ref[...], preferred_element_type=F32).astype(BF16)


def _proj(x2, norm_g, mod3, w_in, seq):
    rows = x2.shape[0]
    tiles_per_seq = seq // PROJ_TM
    return pl.pallas_call(
        _proj_kernel,
        out_shape=jax.ShapeDtypeStruct((rows, IN_COLS), BF16),
        grid=(rows // PROJ_TM, IN_COLS // PROJ_TN),
        in_specs=[pl.BlockSpec((PROJ_TM, D_MODEL), lambda i, j: (i, 0)),
                  pl.BlockSpec((1, D_MODEL), lambda i, j: (0, 0)),
                  pl.BlockSpec((1, 1, 6 * D_MODEL), lambda i, j: (i // tiles_per_seq, 0, 0)),
                  pl.BlockSpec((D_MODEL, PROJ_TN), lambda i, j: (0, j))],
        out_specs=pl.BlockSpec((PROJ_TM, PROJ_TN), lambda i, j: (i, j)),
        scratch_shapes=[pltpu.VMEM((PROJ_TM, D_MODEL), BF16)],
        compiler_params=_params(("arbitrary", "arbitrary")),
        name="proj",
    )(x2, norm_g, mod3, w_in)


def _ret_kernel(q_ref, k_ref, v_ref, rg_ref, cos_ref, sin_ref, idec_ref, qdec_ref, kdec_ref,
                cdec_ref, gng_ref, gnb_ref, o_ref, state):
    C = RET_CHUNK
    half = RET_DK // 2
    nc = q_ref.shape[1] // C
    state[...] = jnp.zeros_like(state)
    idec = idec_ref[0]
    qdec = qdec_ref[0]
    kdec = kdec_ref[0]
    cdec = cdec_ref[0]
    gng = gng_ref[0]
    gnb = gnb_ref[0]

    def body(ci, carry):
        rows = pl.ds(pl.multiple_of(ci * C, C), C)
        cos = cos_ref[rows, :]
        sin = sin_ref[rows, :]

        def rot(t):
            t1, t2 = t[:, :half], t[:, half:]
            return jnp.concatenate([t1 * cos - t2 * sin, t1 * sin + t2 * cos], axis=-1)

        q = rot(q_ref[0, rows, :].astype(F32)).astype(BF16)
        k = rot(k_ref[0, rows, :].astype(F32)) * (RET_DK ** -0.5)
        v = v_ref[0, rows, :]
        s = lax.dot_general(q, k.astype(BF16), (((1,), (1,)), ((), ())),
                            preferred_element_type=F32) * idec
        st = state[...]
        o = (jnp.dot(s.astype(BF16), v, preferred_element_type=F32)
             + jnp.dot(q, st.astype(BF16), preferred_element_type=F32) * qdec)
        kd = (k * kdec).astype(BF16)
        state[...] = st * cdec + lax.dot_general(kd, v, (((0,), (0,)), ((), ())),
                                                 preferred_element_type=F32)
        mu = jnp.mean(o, axis=-1, keepdims=True)
        oc = o - mu
        var = jnp.mean(oc * oc, axis=-1, keepdims=True)
        on = oc * lax.rsqrt(var + GN_EPS) * gng + gnb
        o_ref[0, rows, :] = (_silu(rg_ref[0, rows, :].astype(F32)) * on).astype(BF16)
        return carry

    lax.fori_loop(0, nc, body, 0)


def _retention(proj3, cos, sin, gn_g, gn_b):
    B, S, _ = proj3.shape
    H, C = RET_HEADS, RET_CHUNK
    log_g = jnp.log1p(-(2.0 ** (-5.0 - jnp.arange(H, dtype=F32))))
    idx = jnp.arange(C, dtype=F32)
    rel = idx[:, None] - idx[None, :]
    idec = jnp.where(rel >= 0, jnp.exp(log_g[:, None, None] * jnp.maximum(rel, 0.0)), 0.0)
    qdec = jnp.exp(log_g[:, None] * (idx + 1.0))[:, :, None]
    kdec = jnp.exp(log_g[:, None] * (C - 1.0 - idx))[:, :, None]
    cdec = jnp.exp(log_g * C)[:, None, None]
    kq = RET_QK_W // RET_DK
    kv = 2 * RET_QK_W // RET_DV
    kg = kv + RET_V_W // RET_DV
    return pl.pallas_call(
        _ret_kernel,
        out_shape=jax.ShapeDtypeStruct((B, S, RET_V_W), BF16),
        grid=(B, H),
        in_specs=[pl.BlockSpec((1, S, RET_DK), lambda b, h: (b, 0, h)),
                  pl.BlockSpec((1, S, RET_DK), lambda b, h: (b, 0, kq + h)),
                  pl.BlockSpec((1, S, RET_DV), lambda b, h: (b, 0, kv + h)),
                  pl.BlockSpec((1, S, RET_DV), lambda b, h: (b, 0, kg + h)),
                  _const_spec((S, RET_DK // 2)),
                  _const_spec((S, RET_DK // 2)),
                  pl.BlockSpec((1, C, C), lambda b, h: (h, 0, 0)),
                  pl.BlockSpec((1, C, 1), lambda b, h: (h, 0, 0)),
                  pl.BlockSpec((1, C, 1), lambda b, h: (h, 0, 0)),
                  pl.BlockSpec((1, 1, 1), lambda b, h: (h, 0, 0)),
                  pl.BlockSpec((1, 1, RET_DV), lambda b, h: (h, 0, 0)),
                  pl.BlockSpec((1, 1, RET_DV), lambda b, h: (h, 0, 0))],
        out_specs=pl.BlockSpec((1, S, RET_DV), lambda b, h: (b, 0, h)),
        scratch_shapes=[pltpu.VMEM((RET_DK, RET_DV), F32)],
        compiler_params=_params(("arbitrary", "arbitrary")),
        name="ret",
    )(proj3, proj3, proj3, proj3, cos, sin, idec, qdec, kdec, cdec,
      gn_g.reshape(H, 1, RET_DV), gn_b.reshape(H, 1, RET_DV))


def _t5_bucket(dist):
    max_exact = REL_BUCKETS // 2
    d_f = jnp.maximum(dist, 1).astype(F32)
    large = max_exact + (jnp.log(d_f / max_exact) / math.log(REL_MAX_DIST / max_exact)
                         * (REL_BUCKETS - max_exact)).astype(jnp.int32)
    large = jnp.minimum(large, REL_BUCKETS - 1)
    return jnp.where(dist < max_exact, dist, large)


def _attn_unit(gi, first, q_ref, k_refs, v_refs, bias_scr, o_ref, lse_ref):
    blk = ATT_BLK
    nk = blk * len(k_refs)
    if len(k_refs) == 2:
        kj = lax.broadcasted_iota(jnp.int32, (blk, nk), 1)
        pen = jnp.where(first, NEG, 0.0)
        first_pen = jnp.where(kj < blk, pen, 0.0)
    lane = lax.broadcasted_iota(jnp.int32, (blk, 128), 1)
    lse_all = jnp.zeros((blk, 128), F32)
    for h in range(ATT_HEADS_PER_GROUP):
        hs = slice(h * ATT_HEAD_DIM, (h + 1) * ATT_HEAD_DIM)
        q = q_ref[0, :, hs]
        if len(k_refs) == 2:
            kb = jnp.concatenate([k_refs[0][0, :, hs], k_refs[1][0, :, hs]], axis=0)
            vb = jnp.concatenate([v_refs[0][0, :, hs], v_refs[1][0, :, hs]], axis=0)
            bias = bias_scr[h] + first_pen
        else:
            kb = k_refs[0][0, :, hs]
            vb = v_refs[0][0, :, hs]
            bias = bias_scr[h, :, blk:]
        s = lax.dot_general(q, kb, (((1,), (1,)), ((), ())),
                            preferred_element_type=F32) * (ATT_HEAD_DIM ** -0.5) + bias
        mx = jnp.max(s, axis=-1, keepdims=True)
        e = jnp.exp(s - mx)
        den = jnp.sum(e, axis=-1, keepdims=True)
        p = (e / den).astype(BF16)
        o_ref[0, :, hs] = jnp.dot(p, vb, preferred_element_type=F32).astype(BF16)
        lse_all = jnp.where(lane == h, mx + jnp.log(den), lse_all)
    lse_ref[0] = lse_all


def _attn_kernel(tab_ref,
                 q0, kp0, kc0, vp0, vc0, q1, kp1, kc1, vp1, vc1, q2, kc2, vc2,
                 bk0, bk1, bk2,
                 o0, l0, o1, l1, o2, l2,
                 bias0, bias1, bias2):
    b = pl.program_id(0)
    i = pl.program_id(1)
    blk = ATT_BLK

    @pl.when((b == 0) & (i == 0))
    def _():
        qi = lax.broadcasted_iota(jnp.int32, (blk, 2 * blk), 0)
        kj = lax.broadcasted_iota(jnp.int32, (blk, 2 * blk), 1)
        m = blk + qi - kj
        band = (m >= 0) & (m <= blk)
        for gi, (bk_ref, bias_scr) in enumerate(((bk0, bias0), (bk1, bias1), (bk2, bias2))):
            bucket = bk_ref[...]
            for h in range(ATT_HEADS_PER_GROUP):
                col = gi * ATT_HEADS_PER_GROUP + h
                acc = jnp.zeros((blk, 2 * blk), F32)
                for t in range(REL_BUCKETS):
                    acc = jnp.where(bucket == t, tab_ref[t, col], acc)
                bias_scr[h] = jnp.where(band, acc, NEG)

    n1 = i % ATT_GROUPS[1][1]
    _attn_unit(0, i == 0, q0, (kp0, kc0), (vp0, vc0), bias0, o0, l0)
    _attn_unit(1, n1 == 0, q1, (kp1, kc1), (vp1, vc1), bias1, o1, l1)
    _attn_unit(2, True, q2, (kc2,), (vc2,), bias2, o2, l2)


def _attention(proj3, rel_bias):
    B, S, _ = proj3.shape
    blk, gw = ATT_BLK, ATT_GROUP_W
    steps = S // blk
    cpt = IN_COLS // gw
    in_specs = [pl.BlockSpec(memory_space=pltpu.SMEM)]
    args = [rel_bias]
    out_specs, out_shapes, buckets = [], [], []
    qi = jnp.arange(blk)[:, None]
    kj = jnp.arange(2 * blk)[None, :]
    m = blk + qi - kj
    for gi, (win, dil) in enumerate(ATT_GROUPS):
        assert win // dil == blk and S % (dil * blk) == 0
        L = S // dil
        nb = L // blk
        view = proj3.reshape(B, L, dil * IN_COLS)
        cq = (COL_ATT + 3 * gw * gi) // gw

        def rn(i, nb=nb):
            return i // nb, i % nb

        def spec(part, prev, cq=cq, rn=rn):
            def imap(b, i):
                r, n = rn(i)
                return b, (jnp.maximum(n - 1, 0) if prev else n), r * cpt + cq + part
            return pl.BlockSpec((1, blk, gw), imap)

        if nb > 1:
            parts = [(0, False), (1, True), (1, False), (2, True), (2, False)]
        else:
            parts = [(0, False), (1, False), (2, False)]
        for part, prev in parts:
            in_specs.append(spec(part, prev))
            args.append(view)

        def omap(b, i, rn=rn):
            r, n = rn(i)
            return b, n, r
        out_specs += [pl.BlockSpec((1, blk, gw), omap), pl.BlockSpec((1, blk, 128), omap)]
        out_shapes += [jax.ShapeDtypeStruct((B, L, dil * gw), BF16),
                       jax.ShapeDtypeStruct((B, L, dil * 128), F32)]
        buckets.append(_t5_bucket(jnp.clip(m, 0, blk) * dil).astype(jnp.int32))
    for bk in buckets:
        in_specs.append(_const_spec((blk, 2 * blk)))
        args.append(bk)
    outs = pl.pallas_call(
        _attn_kernel,
        out_shape=out_shapes,
        grid=(B, steps),
        in_specs=in_specs,
        out_specs=out_specs,
        scratch_shapes=[pltpu.VMEM((ATT_HEADS_PER_GROUP, blk, 2 * blk), F32)] * 3,
        compiler_params=_params(("arbitrary", "arbitrary")),
        name="attn",
    )(*args)
    o = [outs[2 * g].reshape(B, S, gw) for g in range(3)]
    lse = [outs[2 * g + 1].reshape(B, S, 128) for g in range(3)]
    return o, lse


def _merge_kernel(x_ref, mod_ref, retg_ref, o0_ref, o1_ref, o2_ref, l0_ref, l1_ref, l2_ref,
                  ga0_ref, ga1_ref, gb0_ref, gb1_ref, wr_ref, wa_ref, wo_ref, out_ref):
    ret_out = jnp.dot(retg_ref[...], wr_ref[...], preferred_element_type=F32)
    l0, l1, l2 = l0_ref[...], l1_ref[...], l2_ref[...]
    lm = jnp.maximum(jnp.maximum(l0, l1), l2)
    e0, e1, e2 = jnp.exp(l0 - lm), jnp.exp(l1 - lm), jnp.exp(l2 - lm)
    tot = e0 + e1 + e2
    w0, w1, w2 = e0 / tot, e1 / tot, e2 / tot
    parts = []
    for h in range(ATT_HEADS_PER_GROUP):
        hs = slice(h * ATT_HEAD_DIM, (h + 1) * ATT_HEAD_DIM)
        parts.append(w0[:, h:h + 1] * o0_ref[:, hs].astype(F32)
                     + w1[:, h:h + 1] * o1_ref[:, hs].astype(F32)
                     + w2[:, h:h + 1] * o2_ref[:, hs].astype(F32))
    att = jnp.concatenate(parts, axis=-1).astype(BF16)
    att_out = jnp.dot(att, wa_ref[...], preferred_element_type=F32)
    gate_a = jnp.concatenate([ga0_ref[...], ga1_ref[...]], axis=-1).astype(F32)
    gate_b = jnp.concatenate([gb0_ref[...], gb1_ref[...]], axis=-1).astype(F32)
    merged = jax.nn.sigmoid(gate_a) * ret_out + jax.nn.sigmoid(gate_b) * att_out
    y = jnp.dot(merged.astype(BF16), wo_ref[...], preferred_element_type=F32)
    gate = mod_ref[0, :, 2 * D_MODEL:3 * D_MODEL]
    out_ref[...] = x_ref[...] + gate * y


def _merge(x2, mod3, retg2, o, lse, proj2, w_ret_out, w_att_out, w_o, seq):
    rows = x2.shape[0]
    tm = ROW_TM
    tiles_per_seq = seq // tm
    row = lambda w: pl.BlockSpec((tm, w), lambda i: (i, 0))
    gw = D_MODEL // 2
    assert COL_GATE % gw == 0
    gate = lambda k: pl.BlockSpec((tm, gw), lambda i: (i, COL_GATE // gw + k))
    return pl.pallas_call(
        _merge_kernel,
        out_shape=jax.ShapeDtypeStruct((rows, D_MODEL), F32),
        grid=(rows // tm,),
        in_specs=[row(D_MODEL),
                  pl.BlockSpec((1, 1, 6 * D_MODEL), lambda i: (i // tiles_per_seq, 0, 0)),
                  row(RET_V_W),
                  row(ATT_GROUP_W), row(ATT_GROUP_W), row(ATT_GROUP_W),
                  row(128), row(128), row(128),
                  gate(0), gate(1), gate(2), gate(3),
                  _const_spec((RET_V_W, D_MODEL)),
                  _const_spec((ATT_GROUP_W, D_MODEL)),
                  _const_spec((D_MODEL, D_MODEL))],
        out_specs=row(D_MODEL),
        compiler_params=_params(("arbitrary",)),
        name="merge",
    )(x2, mod3, retg2, *o, *lse, proj2, proj2, proj2, proj2, w_ret_out, w_att_out, w_o)


def _mlp_kernel(x_ref, mod_ref, g2_ref, gf_ref, w1_ref, w2_ref, out_ref, *, final):
    x = x_ref[...]
    shift = mod_ref[0, :, 3 * D_MODEL:4 * D_MODEL]
    scale = mod_ref[0, :, 4 * D_MODEL:5 * D_MODEL]
    gate = mod_ref[0, :, 5 * D_MODEL:6 * D_MODEL]
    h = (_rms(x, g2_ref[...]) * (1.0 + scale) + shift).astype(BF16)
    u = jnp.maximum(jnp.dot(h, w1_ref[...], preferred_element_type=F32), 0.0)
    y = jnp.dot((u * u).astype(BF16), w2_ref[...], preferred_element_type=F32)
    x = x + gate * y
    out_ref[...] = _rms(x, gf_ref[...]) if final else x


def _mlp(x2, mod3, norm2_g, norm_f_g, w1, w2, seq, final):
    rows = x2.shape[0]
    tm = ROW_TM
    tiles_per_seq = seq // tm
    return pl.pallas_call(
        functools.partial(_mlp_kernel, final=final),
        out_shape=jax.ShapeDtypeStruct((rows, D_MODEL), F32),
        grid=(rows // tm,),
        in_specs=[pl.BlockSpec((tm, D_MODEL), lambda i: (i, 0)),
                  pl.BlockSpec((1, 1, 6 * D_MODEL), lambda i: (i // tiles_per_seq, 0, 0)),
                  _const_spec((1, D_MODEL)),
                  _const_spec((1, D_MODEL)),
                  _const_spec((D_MODEL, D_FF)),
                  _const_spec((D_FF, D_MODEL))],
        out_specs=pl.BlockSpec((tm, D_MODEL), lambda i: (i, 0)),
        compiler_params=_params(("arbitrary",)),
        name="mlp",
    )(x2, mod3, norm2_g, norm_f_g, w1, w2)


def kernel(x, c, w_ada, b_ada, norm1_g, w_in, rel_bias, ret_gn_g, ret_gn_b, w_ret_out, w_att_out,
           w_o, norm2_g, w_ff1, w_ff2, norm_f_g):
    B, S, D = x.shape
    depth = w_ada.shape[0]
    half = RET_DK // 2
    inv = ROPE_BASE ** (-jnp.arange(half, dtype=F32) / half)
    ang = jnp.arange(S).astype(F32)[:, None] * inv[None, :]
    cos, sin = jnp.cos(ang), jnp.sin(ang)
    x2 = x.reshape(B * S, D)
    for l in range(depth):
        mod3 = _mod(c, w_ada[l], b_ada[l]).reshape(B, 1, 6 * D)
        proj2 = _proj(x2, norm1_g[l].reshape(1, D), mod3, w_in[l].astype(BF16), S)
        proj3 = proj2.reshape(B, S, IN_COLS)
        retg = _retention(proj3, cos, sin, ret_gn_g[l], ret_gn_b[l])
        o, lse = _attention(proj3, rel_bias)
        x2 = _merge(x2, mod3, retg.reshape(B * S, RET_V_W),
                    [t.reshape(B * S, ATT_GROUP_W) for t in o],
                    [t.reshape(B * S, 128) for t in lse],
                    proj2, w_ret_out[l].astype(BF16), w_att_out[l].astype(BF16),
                    w_o[l].astype(BF16), S)
        x2 = _mlp(x2, mod3, norm2_g[l].reshape(1, D), norm_f_g.reshape(1, D),
                  w_ff1[l].astype(BF16), w_ff2[l].astype(BF16), S, final=l == depth - 1)
    return x2.reshape(B, S, D)
```

```python
import functools
import math

import jax
import jax.numpy as jnp
from jax import lax
from jax.experimental import pallas as pl
from jax.experimental.pallas import tpu as pltpu

F32 = jnp.float32
BF16 = jnp.bfloat16

D_MODEL = 1024
RET_HEADS = 4
RET_DK = 256
RET_DV = 512
RET_CHUNK = 128
RET_QK_W = RET_HEADS * RET_DK
RET_V_W = RET_HEADS * RET_DV
ATT_GROUPS = ((128, 1), (512, 4), (2048, 16))
ATT_HEADS_PER_GROUP = 4
ATT_HEAD_DIM = 128
ATT_GROUP_W = ATT_HEADS_PER_GROUP * ATT_HEAD_DIM
ATT_BLK = 128
LANES = 128
LSE_W = LANES
REL_BUCKETS = 32
REL_MAX_DIST = 2048
D_FF = 4 * D_MODEL
RMS_EPS = 1e-6
GN_EPS = 1e-5
ROPE_BASE = 10000.0
NEG = -1e30

COL_ATT = 2 * RET_QK_W + 2 * RET_V_W
COL_GATE = COL_ATT + 9 * ATT_GROUP_W

VMEM_LIMIT = 56 * 1024 * 1024

PROJ_TM = 512
PROJ_TN = 3 * ATT_GROUP_W
PROJ_MAIN_TILES = (COL_ATT + PROJ_TN) // PROJ_TN
ROW_TM = 512


def _params(sem):
    return pltpu.CompilerParams(dimension_semantics=sem, vmem_limit_bytes=VMEM_LIMIT)


def _const_spec(shape):
    zeros = (0,) * len(shape)
    return pl.BlockSpec(shape, lambda *_: zeros, pipeline_mode=pl.Buffered(1))


def _silu(t):
    return t * jax.nn.sigmoid(t)


def _rms(x, g):
    return x * lax.rsqrt(jnp.mean(x * x, axis=-1, keepdims=True) + RMS_EPS) * g


def _modulated_norm(x, g, mod_ref, k):
    shift = mod_ref[0, :, k * D_MODEL:(k + 1) * D_MODEL]
    scale = mod_ref[0, :, (k + 1) * D_MODEL:(k + 2) * D_MODEL]
    return _rms(x, g) * (1.0 + scale) + shift


def _mod_kernel(c_ref, w_ref, b_ref, o_ref):
    o_ref[...] = jnp.dot(_silu(c_ref[...]), w_ref[...], preferred_element_type=F32) + b_ref[...]


def _mod(c, w_ada, b_ada):
    B = c.shape[0]
    n = w_ada.shape[1]
    tn = D_MODEL
    return pl.pallas_call(
        _mod_kernel,
        out_shape=jax.ShapeDtypeStruct((B, n), F32),
        grid=(n // tn,),
        in_specs=[pl.BlockSpec((B, D_MODEL), lambda j: (0, 0)),
                  pl.BlockSpec((D_MODEL, tn), lambda j: (0, j)),
                  pl.BlockSpec((1, tn), lambda j: (0, j))],
        out_specs=pl.BlockSpec((B, tn), lambda j: (0, j)),
        compiler_params=_params(("arbitrary",)),
        name="mod",
    )(c, w_ada, b_ada.reshape(1, n))


def _proj_kernel(x_ref, g_ref, mod_ref, w_ref, main_ref, a1_ref, a2_ref, hf_scr, h0_scr, h1_scr, h2_scr):
    j = pl.program_id(1)
    tm = x_ref.shape[0]
    perms = ((ATT_GROUPS[1][1], h1_scr, a1_ref), (ATT_GROUPS[2][1], h2_scr, a2_ref))

    @pl.when(j == 0)
    def _():
        hf = _modulated_norm(x_ref[...], g_ref[...], mod_ref, 0)
        h0_scr[...] = hf.astype(BF16)
        for c in range(D_MODEL // LANES):
            cs = slice(c * LANES, (c + 1) * LANES)
            hf_scr[c] = hf[:, cs]
            for dil, h_scr, _ in perms:
                n = tm // dil
                for r in range(dil):
                    h_scr[r * n:(r + 1) * n, cs] = hf_scr[c, pl.ds(r, n, stride=dil), :].astype(BF16)

    @pl.when(j < PROJ_MAIN_TILES)
    def _():
        main_ref[...] = jnp.dot(h0_scr[...], w_ref[...], preferred_element_type=F32).astype(BF16)

    for t, (dil, h_scr, a_ref) in enumerate(perms):
        @pl.when(j == PROJ_MAIN_TILES + t)
        def _(dil=dil, h_scr=h_scr, a_ref=a_ref):
            n = tm // dil
            acc = jnp.dot(h_scr[...], w_ref[...], preferred_element_type=F32).astype(BF16)
            for r in range(dil):
                a_ref[0, r] = acc[r * n:(r + 1) * n, :]


def _proj(x2, norm_g, mod3, w_att, batch, seq):
    tm, tn = PROJ_TM, PROJ_TN
    tps = seq // tm
    d1, d2 = ATT_GROUPS[1][1], ATT_GROUPS[2][1]
    sub = lambda d: pl.BlockSpec((1, d, tm // d, tn), lambda i, j: (i // tps, 0, i % tps, 0))
    return pl.pallas_call(
        _proj_kernel,
        out_shape=[jax.ShapeDtypeStruct((batch * seq, PROJ_MAIN_TILES * tn), BF16),
                   jax.ShapeDtypeStruct((batch, d1, seq // d1, tn), BF16),
                   jax.ShapeDtypeStruct((batch, d2, seq // d2, tn), BF16)],
        grid=(batch * seq // tm, PROJ_MAIN_TILES + 2),
        in_specs=[pl.BlockSpec((tm, D_MODEL), lambda i, j: (i, 0)),
                  pl.BlockSpec((1, D_MODEL), lambda i, j: (0, 0)),
                  pl.BlockSpec((1, 1, 6 * D_MODEL), lambda i, j: (i // tps, 0, 0)),
                  pl.BlockSpec((D_MODEL, tn), lambda i, j: (0, j))],
        out_specs=[pl.BlockSpec((tm, tn), lambda i, j: (i, jnp.minimum(j, PROJ_MAIN_TILES - 1))),
                   sub(d1), sub(d2)],
        scratch_shapes=[pltpu.VMEM((D_MODEL // LANES, tm, LANES), F32)]
                       + [pltpu.VMEM((tm, D_MODEL), BF16)] * 3,
        compiler_params=_params(("arbitrary", "arbitrary")),
        name="proj",
    )(x2, norm_g, mod3, w_att)


def _ret_kernel(q_ref, k_ref, v_ref, rg_ref, cos_ref, sin_ref, idec_ref, qdec_ref, kdec_ref,
                cdec_ref, gng_ref, gnb_ref, o_ref, state):
    C = RET_CHUNK
    half = RET_DK // 2
    nc = q_ref.shape[1] // C
    state[...] = jnp.zeros_like(state)
    idec = idec_ref[0]
    qdec = qdec_ref[0]
    kdec = kdec_ref[0]
    cdec = cdec_ref[0]
    gng = gng_ref[0]
    gnb = gnb_ref[0]

    def body(ci, carry):
        rows = pl.ds(pl.multiple_of(ci * C, C), C)
        cos = cos_ref[rows, :]
        sin = sin_ref[rows, :]

        def rot(t):
            t1, t2 = t[:, :half], t[:, half:]
            return jnp.concatenate([t1 * cos - t2 * sin, t1 * sin + t2 * cos], axis=-1)

        q = rot(q_ref[0, rows, :].astype(F32)).astype(BF16)
        k = rot(k_ref[0, rows, :].astype(F32)) * (RET_DK ** -0.5)
        v = v_ref[0, rows, :]
        s = lax.dot_general(q, k.astype(BF16), (((1,), (1,)), ((), ())),
                            preferred_element_type=F32) * idec
        st = state[...]
        o = (jnp.dot(s.astype(BF16), v, preferred_element_type=F32)
             + jnp.dot(q, st.astype(BF16), preferred_element_type=F32) * qdec)
        kd = (k * kdec).astype(BF16)
        state[...] = st * cdec + lax.dot_general(kd, v, (((0,), (0,)), ((), ())),
                                                 preferred_element_type=F32)
        mu = jnp.mean(o, axis=-1, keepdims=True)
        oc = o - mu
        var = jnp.mean(oc * oc, axis=-1, keepdims=True)
        on = oc * lax.rsqrt(var + GN_EPS) * gng + gnb
        o_ref[0, rows, :] = (_silu(rg_ref[0, rows, :].astype(F32)) * on).astype(BF16)
        return carry

    lax.fori_loop(0, nc, body, 0)


def _retention(main3, cos, sin, gn_g, gn_b):
    B, S, _ = main3.shape
    H, C = RET_HEADS, RET_CHUNK
    log_g = jnp.log1p(-(2.0 ** (-5.0 - jnp.arange(H, dtype=F32))))
    idx = jnp.arange(C, dtype=F32)
    rel = idx[:, None] - idx[None, :]
    idec = jnp.where(rel >= 0, jnp.exp(log_g[:, None, None] * jnp.maximum(rel, 0.0)), 0.0)
    qdec = jnp.exp(log_g[:, None] * (idx + 1.0))[:, :, None]
    kdec = jnp.exp(log_g[:, None] * (C - 1.0 - idx))[:, :, None]
    cdec = jnp.exp(log_g * C)[:, None, None]
    kq = RET_QK_W // RET_DK
    kv = 2 * RET_QK_W // RET_DV
    kg = kv + RET_V_W // RET_DV
    return pl.pallas_call(
        _ret_kernel,
        out_shape=jax.ShapeDtypeStruct((B, S, RET_V_W), BF16),
        grid=(B, H),
        in_specs=[pl.BlockSpec((1, S, RET_DK), lambda b, h: (b, 0, h)),
                  pl.BlockSpec((1, S, RET_DK), lambda b, h: (b, 0, kq + h)),
                  pl.BlockSpec((1, S, RET_DV), lambda b, h: (b, 0, kv + h)),
                  pl.BlockSpec((1, S, RET_DV), lambda b, h: (b, 0, kg + h)),
                  _const_spec((S, RET_DK // 2)),
                  _const_spec((S, RET_DK // 2)),
                  pl.BlockSpec((1, C, C), lambda b, h: (h, 0, 0)),
                  pl.BlockSpec((1, C, 1), lambda b, h: (h, 0, 0)),
                  pl.BlockSpec((1, C, 1), lambda b, h: (h, 0, 0)),
                  pl.BlockSpec((1, 1, 1), lambda b, h: (h, 0, 0)),
                  pl.BlockSpec((1, 1, RET_DV), lambda b, h: (h, 0, 0)),
                  pl.BlockSpec((1, 1, RET_DV), lambda b, h: (h, 0, 0))],
        out_specs=pl.BlockSpec((1, S, RET_DV), lambda b, h: (b, 0, h)),
        scratch_shapes=[pltpu.VMEM((RET_DK, RET_DV), F32)],
        compiler_params=_params(("arbitrary", "arbitrary")),
        name="ret",
    )(main3, main3, main3, main3, cos, sin, idec, qdec, kdec, cdec,
      gn_g.reshape(H, 1, RET_DV), gn_b.reshape(H, 1, RET_DV))


def _t5_bucket(dist):
    max_exact = REL_BUCKETS // 2
    d_f = jnp.maximum(dist, 1).astype(F32)
    large = max_exact + (jnp.log(d_f / max_exact) / math.log(REL_MAX_DIST / max_exact)
                         * (REL_BUCKETS - max_exact)).astype(jnp.int32)
    large = jnp.minimum(large, REL_BUCKETS - 1)
    return jnp.where(dist < max_exact, dist, large)


def _attn_unit(first, q_ref, k_refs, v_refs, bias_scr, o_ref, lse_ref):
    blk = ATT_BLK
    nk = blk * len(k_refs)
    if len(k_refs) == 2:
        kj = lax.broadcasted_iota(jnp.int32, (blk, nk), 1)
        pen = jnp.where(first, NEG, 0.0)
        first_pen = jnp.where(kj < blk, pen, 0.0)
    lane = lax.broadcasted_iota(jnp.int32, (blk, LSE_W), 1)
    lse_all = jnp.zeros((blk, LSE_W), F32)
    for h in range(ATT_HEADS_PER_GROUP):
        hs = slice(h * ATT_HEAD_DIM, (h + 1) * ATT_HEAD_DIM)
        q = q_ref[:, hs]
        if len(k_refs) == 2:
            kb = jnp.concatenate([k_refs[0][:, hs], k_refs[1][:, hs]], axis=0)
            vb = jnp.concatenate([v_refs[0][:, hs], v_refs[1][:, hs]], axis=0)
            bias = bias_scr[h] + first_pen
        else:
            kb = k_refs[0][:, hs]
            vb = v_refs[0][:, hs]
            bias = bias_scr[h, :, blk:]
        s = lax.dot_general(q, kb, (((1,), (1,)), ((), ())),
                            preferred_element_type=F32) * (ATT_HEAD_DIM ** -0.5) + bias
        mx = jnp.max(s, axis=-1, keepdims=True)
        e = jnp.exp(s - mx)
        den = jnp.sum(e, axis=-1, keepdims=True)
        p = (e / den).astype(BF16)
        o_ref[:, hs] = jnp.dot(p, vb, preferred_element_type=F32).astype(BF16)
        lse_all = jnp.where(lane == h, mx + jnp.log(den), lse_all)
    lse_ref[...] = lse_all


def _attn_kernel(tab_ref,
                 q0, kp0, kc0, vp0, vc0, q1, kp1, kc1, vp1, vc1, q2, kc2, vc2,
                 bk0, bk1, bk2,
                 o0, l0, o1, l1, o2, l2,
                 bias0, bias1, bias2, *, nb1):
    b = pl.program_id(0)
    i = pl.program_id(1)
    blk = ATT_BLK

    @pl.when((b == 0) & (i == 0))
    def _():
        qi = lax.broadcasted_iota(jnp.int32, (blk, 2 * blk), 0)
        kj = lax.broadcasted_iota(jnp.int32, (blk, 2 * blk), 1)
        m = blk + qi - kj
        band = (m >= 0) & (m <= blk)
        for gi, (bk_ref, bias_scr) in enumerate(((bk0, bias0), (bk1, bias1), (bk2, bias2))):
            bucket = bk_ref[...]
            for h in range(ATT_HEADS_PER_GROUP):
                col = gi * ATT_HEADS_PER_GROUP + h
                acc = jnp.zeros((blk, 2 * blk), F32)
                for t in range(REL_BUCKETS):
                    acc = jnp.where(bucket == t, tab_ref[t, col], acc)
                bias_scr[h] = jnp.where(band, acc, NEG)

    _attn_unit(i == 0, q0, (kp0, kc0), (vp0, vc0), bias0, o0, l0)
    _attn_unit(i % nb1 == 0, q1, (kp1, kc1), (vp1, vc1), bias1, o1, l1)
    _attn_unit(True, q2, (kc2,), (vc2,), bias2, o2, l2)


def _attention(main3, att1, att2, rel_bias):
    B, S, _ = main3.shape
    blk, gw = ATT_BLK, ATT_GROUP_W
    steps = S // blk
    nbs = [S // dil // blk for _, dil in ATT_GROUPS]
    assert all(win // dil == blk for win, dil in ATT_GROUPS) and nbs[0] == steps and nbs[2] == 1
    c0 = COL_ATT // gw

    def spec0(part, prev):
        return pl.BlockSpec((None, blk, gw),
                            lambda b, i: (b, jnp.maximum(i - 1, 0) if prev else i, c0 + part))

    def spec_sub(nb, part, prev):
        def imap(b, i):
            n = i % nb
            return b, i // nb, (jnp.maximum(n - 1, 0) if prev else n), part
        return pl.BlockSpec((None, None, blk, gw), imap)

    both = [(0, False), (1, True), (1, False), (2, True), (2, False)]
    in_specs = [pl.BlockSpec(memory_space=pltpu.SMEM)]
    in_specs += [spec0(p, pv) for p, pv in both]
    in_specs += [spec_sub(nbs[1], p, pv) for p, pv in both]
    in_specs += [spec_sub(nbs[2], p, False) for p in range(3)]
    args = [rel_bias] + [main3] * 5 + [att1] * 5 + [att2] * 3
    qi = jnp.arange(blk)[:, None]
    kj = jnp.arange(2 * blk)[None, :]
    m = blk + qi - kj
    for _, dil in ATT_GROUPS:
        in_specs.append(_const_spec((blk, 2 * blk)))
        args.append(_t5_bucket(jnp.clip(m, 0, blk) * dil).astype(jnp.int32))
    out_specs, out_shapes = [], []
    for (_, dil), nb in zip(ATT_GROUPS, nbs):
        for w, dt in ((gw, BF16), (LSE_W, F32)):
            if dil == 1:
                out_specs.append(pl.BlockSpec((None, blk, w), lambda b, i: (b, i, 0)))
                out_shapes.append(jax.ShapeDtypeStruct((B, S, w), dt))
            else:
                out_specs.append(pl.BlockSpec((None, None, blk, w),
                                              lambda b, i, nb=nb: (b, i // nb, i % nb, 0)))
                out_shapes.append(jax.ShapeDtypeStruct((B, dil, S // dil, w), dt))
    outs = pl.pallas_call(
        functools.partial(_attn_kernel, nb1=nbs[1]),
        out_shape=out_shapes,
        grid=(B, steps),
        in_specs=in_specs,
        out_specs=out_specs,
        scratch_shapes=[pltpu.VMEM((ATT_HEADS_PER_GROUP, blk, 2 * blk), F32)] * 3,
        compiler_params=_params(("arbitrary", "arbitrary")),
        name="attn",
    )(*args)
    return outs[0::2], outs[1::2]


def _merge_kernel(x_ref, g_ref, mod_ref, retg_ref, o0_ref, o1_ref, o2_ref, l0_ref, l1_ref, l2_ref,
                  wg_ref, wr_ref, wa_ref, wo_ref, out_ref, o1_scr, o2_scr, l1_scr, l2_scr):
    tm = x_ref.shape[0]
    for dil, o_ref, l_ref, o_scr, l_scr in ((ATT_GROUPS[1][1], o1_ref, l1_ref, o1_scr, l1_scr),
                                            (ATT_GROUPS[2][1], o2_ref, l2_ref, o2_scr, l2_scr)):
        n = tm // dil
        for r in range(dil):
            l_scr[pl.ds(r, n, stride=dil), :] = l_ref[0, r]
            for h in range(ATT_HEADS_PER_GROUP):
                hs = slice(h * ATT_HEAD_DIM, (h + 1) * ATT_HEAD_DIM)
                o_scr[h, pl.ds(r, n, stride=dil), :] = o_ref[0, r, :, hs].astype(F32)
    l0, l1, l2 = l0_ref[...], l1_scr[...], l2_scr[...]
    lm = jnp.maximum(jnp.maximum(l0, l1), l2)
    e0, e1, e2 = jnp.exp(l0 - lm), jnp.exp(l1 - lm), jnp.exp(l2 - lm)
    tot = e0 + e1 + e2
    w0, w1, w2 = e0 / tot, e1 / tot, e2 / tot
    parts = []
    for h in range(ATT_HEADS_PER_GROUP):
        hs = slice(h * ATT_HEAD_DIM, (h + 1) * ATT_HEAD_DIM)
        parts.append(w0[:, h:h + 1] * o0_ref[:, hs].astype(F32)
                     + w1[:, h:h + 1] * o1_scr[h]
                     + w2[:, h:h + 1] * o2_scr[h])
    att = jnp.concatenate(parts, axis=-1).astype(BF16)
    att_out = jnp.dot(att, wa_ref[...], preferred_element_type=F32)
    ret_out = jnp.dot(retg_ref[...], wr_ref[...], preferred_element_type=F32)
    x = x_ref[...]
    h = _modulated_norm(x, g_ref[...], mod_ref, 0).astype(BF16)
    gates = jax.nn.sigmoid(jnp.dot(h, wg_ref[...], preferred_element_type=F32))
    merged = gates[:, :D_MODEL] * ret_out + gates[:, D_MODEL:] * att_out
    y = jnp.dot(merged.astype(BF16), wo_ref[...], preferred_element_type=F32)
    out_ref[...] = x + mod_ref[0, :, 2 * D_MODEL:3 * D_MODEL] * y


def _merge(x2, norm_g, mod3, retg2, o, lse, w_gate, w_ret_out, w_att_out, w_o, batch, seq):
    tm = ROW_TM
    tps = seq // tm
    row = lambda w: pl.BlockSpec((tm, w), lambda i: (i, 0))
    sub = lambda d, w: pl.BlockSpec((1, d, tm // d, w), lambda i: (i // tps, 0, i % tps, 0))
    d1, d2 = ATT_GROUPS[1][1], ATT_GROUPS[2][1]
    return pl.pallas_call(
        _merge_kernel,
        out_shape=jax.ShapeDtypeStruct((batch * seq, D_MODEL), F32),
        grid=(batch * seq // tm,),
        in_specs=[row(D_MODEL),
                  _const_spec((1, D_MODEL)),
                  pl.BlockSpec((1, 1, 6 * D_MODEL), lambda i: (i // tps, 0, 0)),
                  row(RET_V_W),
                  row(ATT_GROUP_W), sub(d1, ATT_GROUP_W), sub(d2, ATT_GROUP_W),
                  row(LSE_W), sub(d1, LSE_W), sub(d2, LSE_W),
                  _const_spec((D_MODEL, 2 * D_MODEL)),
                  _const_spec((RET_V_W, D_MODEL)),
                  _const_spec((ATT_GROUP_W, D_MODEL)),
                  _const_spec((D_MODEL, D_MODEL))],
        out_specs=row(D_MODEL),
        scratch_shapes=[pltpu.VMEM((ATT_HEADS_PER_GROUP, tm, ATT_HEAD_DIM), F32)] * 2
                       + [pltpu.VMEM((tm, LSE_W), F32)] * 2,
        compiler_params=_params(("arbitrary",)),
        name="merge",
    )(x2, norm_g, mod3, retg2, *o, *lse, w_gate, w_ret_out, w_att_out, w_o)


def _mlp_kernel(x_ref, mod_ref, g2_ref, gf_ref, w1_ref, w2_ref, out_ref, *, final):
    x = x_ref[...]
    h = _modulated_norm(x, g2_ref[...], mod_ref, 3).astype(BF16)
    u = jnp.maximum(jnp.dot(h, w1_ref[...], preferred_element_type=F32), 0.0)
    y = jnp.dot((u * u).astype(BF16), w2_ref[...], preferred_element_type=F32)
    x = x + mod_ref[0, :, 5 * D_MODEL:6 * D_MODEL] * y
    out_ref[...] = _rms(x, gf_ref[...]) if final else x


def _mlp(x2, mod3, norm2_g, norm_f_g, w1, w2, seq, final):
    rows = x2.shape[0]
    tm = ROW_TM
    tps = seq // tm
    return pl.pallas_call(
        functools.partial(_mlp_kernel, final=final),
        out_shape=jax.ShapeDtypeStruct((rows, D_MODEL), F32),
        grid=(rows // tm,),
        in_specs=[pl.BlockSpec((tm, D_MODEL), lambda i: (i, 0)),
                  pl.BlockSpec((1, 1, 6 * D_MODEL), lambda i: (i // tps, 0, 0)),
                  _const_spec((1, D_MODEL)),
                  _const_spec((1, D_MODEL)),
                  _const_spec((D_MODEL, D_FF)),
                  _const_spec((D_FF, D_MODEL))],
        out_specs=pl.BlockSpec((tm, D_MODEL), lambda i: (i, 0)),
        compiler_params=_params(("arbitrary",)),
        name="mlp",
    )(x2, mod3, norm2_g, norm_f_g, w1, w2)


def kernel(x, c, w_ada, b_ada, norm1_g, w_in, rel_bias, ret_gn_g, ret_gn_b, w_ret_out, w_att_out,
           w_o, norm2_g, w_ff1, w_ff2, norm_f_g):
    B, S, D = x.shape
    depth = w_ada.shape[0]
    half = RET_DK // 2
    inv = ROPE_BASE ** (-jnp.arange(half, dtype=F32) / half)
    ang = jnp.arange(S).astype(F32)[:, None] * inv[None, :]
    cos, sin = jnp.cos(ang), jnp.sin(ang)
    x2 = x.reshape(B * S, D)
    for l in range(depth):
        g1 = norm1_g[l].reshape(1, D)
        mod3 = _mod(c, w_ada[l], b_ada[l]).reshape(B, 1, 6 * D)
        main2, att1, att2 = _proj(x2, g1, mod3, w_in[l, :, :COL_GATE].astype(BF16), B, S)
        main3 = main2.reshape(B, S, main2.shape[1])
        retg = _retention(main3, cos, sin, ret_gn_g[l], ret_gn_b[l])
        o, lse = _attention(main3, att1, att2, rel_bias)
        o = [o[0].reshape(B * S, ATT_GROUP_W), o[1], o[2]]
        lse = [lse[0].reshape(B * S, LSE_W), lse[1], lse[2]]
        x2 = _merge(x2, g1, mod3, retg.reshape(B * S, RET_V_W), o, lse,
                    w_in[l, :, COL_GATE:].astype(BF16), w_ret_out[l].astype(BF16),
                    w_att_out[l].astype(BF16), w_o[l].astype(BF16), B, S)
        x2 = _mlp(x2, mod3, norm2_g[l].reshape(1, D), norm_f_g.reshape(1, D),
                  w_ff1[l].astype(BF16), w_ff2[l].astype(BF16), S, final=l == depth - 1)
    return x2.reshape(B, S, D)
```

```python
import functools
import math

import jax
import jax.numpy as jnp
from jax import lax
from jax.experimental import pallas as pl
from jax.experimental.pallas import tpu as pltpu

F32 = jnp.float32
BF16 = jnp.bfloat16

D_MODEL = 1024
RET_HEADS = 4
RET_DK = 256
RET_DV = 512
RET_C = 256
RET_LOG_GAMMA = tuple(math.log1p(-(2.0 ** (-5.0 - h))) for h in range(RET_HEADS))
RET_QK_W = RET_HEADS * RET_DK
RET_V_W = RET_HEADS * RET_DV
ATT_GROUPS = ((128, 1), (512, 4), (2048, 16))
ATT_HEADS_PER_GROUP = 4
ATT_HEAD_DIM = 128
ATT_GROUP_W = ATT_HEADS_PER_GROUP * ATT_HEAD_DIM
ATT_BLK = 128
LANES = 128
LSE_W = LANES
REL_BUCKETS = 32
REL_MAX_DIST = 2048
D_FF = 4 * D_MODEL
RMS_EPS = 1e-6
GN_EPS = 1e-5
ROPE_BASE = 10000.0
NEG = -1e30

COL_ATT = 2 * RET_QK_W + 2 * RET_V_W
COL_GATE = COL_ATT + 9 * ATT_GROUP_W

VMEM_LIMIT = 56 * 1024 * 1024

PROJ_TM = 512
PROJ_TN = 3 * ATT_GROUP_W
PROJ_MAIN_TILES = (COL_ATT + PROJ_TN) // PROJ_TN
ROW_TM = 512


def _params(sem):
    return pltpu.CompilerParams(dimension_semantics=sem, vmem_limit_bytes=VMEM_LIMIT)


def _const_spec(shape):
    zeros = (0,) * len(shape)
    return pl.BlockSpec(shape, lambda *_: zeros, pipeline_mode=pl.Buffered(1))


def _silu(t):
    return t * jax.nn.sigmoid(t)


def _rms(x, g):
    return x * lax.rsqrt(jnp.mean(x * x, axis=-1, keepdims=True) + RMS_EPS) * g


def _modulated_norm(x, g, mod_ref, k):
    shift = mod_ref[0, :, k * D_MODEL:(k + 1) * D_MODEL]
    scale = mod_ref[0, :, (k + 1) * D_MODEL:(k + 2) * D_MODEL]
    return _rms(x, g) * (1.0 + scale) + shift


def _mod_kernel(c_ref, w_ref, b_ref, o_ref):
    o_ref[...] = jnp.dot(_silu(c_ref[...]), w_ref[...], preferred_element_type=F32) + b_ref[...]


def _mod(c, w_ada, b_ada):
    B = c.shape[0]
    n = w_ada.shape[1]
    tn = D_MODEL
    return pl.pallas_call(
        _mod_kernel,
        out_shape=jax.ShapeDtypeStruct((B, n), F32),
        grid=(n // tn,),
        in_specs=[pl.BlockSpec((B, D_MODEL), lambda j: (0, 0)),
                  pl.BlockSpec((D_MODEL, tn), lambda j: (0, j)),
                  pl.BlockSpec((1, tn), lambda j: (0, j))],
        out_specs=pl.BlockSpec((B, tn), lambda j: (0, j)),
        compiler_params=_params(("arbitrary",)),
        name="mod",
    )(c, w_ada, b_ada.reshape(1, n))


def _proj_main_layout():
    spans = [(h * RET_DK, (h + 1) * RET_DK, "rot", (h % RET_HEADS, h >= RET_HEADS))
             for h in range(2 * RET_HEADS)]
    spans += [(2 * RET_QK_W, 2 * RET_QK_W + RET_V_W, "copy", None),
              (2 * RET_QK_W + RET_V_W, COL_ATT, "silu", None),
              (COL_ATT, PROJ_MAIN_TILES * PROJ_TN, "copy", None)]
    tiles = [[] for _ in range(PROJ_MAIN_TILES)]
    for c0, c1, kind, arg in spans:
        for t in range(c0 // PROJ_TN, (c1 - 1) // PROJ_TN + 1):
            lo, hi = max(c0, t * PROJ_TN), min(c1, (t + 1) * PROJ_TN)
            assert kind != "rot" or (lo, hi) == (c0, c1)
            tiles[t].append((lo - t * PROJ_TN, hi - t * PROJ_TN, kind, arg))
    return tiles


PROJ_MAIN_LAYOUT = _proj_main_layout()


def _proj_kernel(x_ref, g_ref, mod_ref, cos_ref, sin_ref, w_ref, main_ref, a1_ref, a2_ref,
                 hf_scr, h0_scr, h1_scr, h2_scr, *, tps):
    j = pl.program_id(1)
    tm = x_ref.shape[0]
    perms = ((ATT_GROUPS[1][1], h1_scr, a1_ref), (ATT_GROUPS[2][1], h2_scr, a2_ref))

    @pl.when(j == 0)
    def _():
        hf = _modulated_norm(x_ref[...], g_ref[...], mod_ref, 0)
        h0_scr[...] = hf.astype(BF16)
        for c in range(D_MODEL // LANES):
            cs = slice(c * LANES, (c + 1) * LANES)
            hf_scr[c] = hf[:, cs]
            for dil, h_scr, _ in perms:
                n = tm // dil
                for r in range(dil):
                    h_scr[r * n:(r + 1) * n, cs] = hf_scr[c, pl.ds(r, n, stride=dil), :].astype(BF16)

    def main_tile(segments):
        acc = jnp.dot(h0_scr[...], w_ref[...], preferred_element_type=F32)
        if any(kind == "rot" for _, _, kind, _ in segments):
            cos, sin = cos_ref[...], sin_ref[...]
            row = lax.broadcasted_iota(jnp.int32, cos.shape, 0)
            pos = (((pl.program_id(0) % tps) * tm + row) % RET_C + 1).astype(F32)
        for c0, c1, kind, arg in segments:
            t = acc[:, c0:c1]
            if kind == "rot":
                head, is_key = arg
                rate = -RET_LOG_GAMMA[head] if is_key else RET_LOG_GAMMA[head]
                dec = jnp.exp(rate * pos) * ((RET_DK ** -0.5) if is_key else 1.0)
                cd, sd = cos * dec, sin * dec
                cm = (c0 + c1) // 2
                t1, t2 = acc[:, c0:cm], acc[:, cm:c1]
                main_ref[:, c0:cm] = (t1 * cd - t2 * sd).astype(BF16)
                main_ref[:, cm:c1] = (t1 * sd + t2 * cd).astype(BF16)
            else:
                main_ref[:, c0:c1] = (_silu(t) if kind == "silu" else t).astype(BF16)

    for t, segments in enumerate(PROJ_MAIN_LAYOUT):
        @pl.when(j == t)
        def _(segments=segments):
            main_tile(segments)

    for t, (dil, h_scr, a_ref) in enumerate(perms):
        @pl.when(j == PROJ_MAIN_TILES + t)
        def _(dil=dil, h_scr=h_scr, a_ref=a_ref):
            n = tm // dil
            acc = jnp.dot(h_scr[...], w_ref[...], preferred_element_type=F32).astype(BF16)
            for r in range(dil):
                a_ref[0, r] = acc[r * n:(r + 1) * n, :]


def _proj(x2, norm_g, mod3, cos, sin, w_att, batch, seq):
    tm, tn = PROJ_TM, PROJ_TN
    tps = seq // tm
    assert seq % tm == 0 and tm % RET_C == 0
    d1, d2 = ATT_GROUPS[1][1], ATT_GROUPS[2][1]
    sub = lambda d: pl.BlockSpec((1, d, tm // d, tn), lambda i, j: (i // tps, 0, i % tps, 0))
    return pl.pallas_call(
        functools.partial(_proj_kernel, tps=tps),
        out_shape=[jax.ShapeDtypeStruct((batch * seq, PROJ_MAIN_TILES * tn), BF16),
                   jax.ShapeDtypeStruct((batch, d1, seq // d1, tn), BF16),
                   jax.ShapeDtypeStruct((batch, d2, seq // d2, tn), BF16)],
        grid=(batch * seq // tm, PROJ_MAIN_TILES + 2),
        in_specs=[pl.BlockSpec((tm, D_MODEL), lambda i, j: (i, 0)),
                  pl.BlockSpec((1, D_MODEL), lambda i, j: (0, 0)),
                  pl.BlockSpec((1, 1, 6 * D_MODEL), lambda i, j: (i // tps, 0, 0)),
                  pl.BlockSpec((tm, RET_DK // 2), lambda i, j: (i % tps, 0)),
                  pl.BlockSpec((tm, RET_DK // 2), lambda i, j: (i % tps, 0)),
                  pl.BlockSpec((D_MODEL, tn), lambda i, j: (0, j))],
        out_specs=[pl.BlockSpec((tm, tn), lambda i, j: (i, jnp.minimum(j, PROJ_MAIN_TILES - 1))),
                   sub(d1), sub(d2)],
        scratch_shapes=[pltpu.VMEM((D_MODEL // LANES, tm, LANES), F32)]
                       + [pltpu.VMEM((tm, D_MODEL), BF16)] * 3,
        compiler_params=_params(("arbitrary", "arbitrary")),
        name="proj",
    )(x2, norm_g, mod3, cos, sin, w_att)


def _ret_kernel(q_ref, k_ref, v_ref, gate_ref, cdec_ref, gng_ref, gnb_ref, o_ref, state, st_in):
    C = RET_C
    nc = q_ref.shape[1] // C
    cdec = cdec_ref[0]
    gng = gng_ref[0]
    gnb = gnb_ref[0]
    chunk = lambda ci: pl.ds(pl.multiple_of(ci * C, C), C)

    state[...] = jnp.zeros_like(state)
    st_in[0] = jnp.zeros(st_in.shape[1:], BF16)

    def advance(ci, carry):
        rows = chunk(ci)
        kv = lax.dot_general(k_ref[0, rows, :], v_ref[0, rows, :], (((0,), (0,)), ((), ())),
                             preferred_element_type=F32)
        new = (state[...] + kv) * cdec
        state[...] = new
        st_in[ci + 1] = new.astype(BF16)
        return carry

    lax.fori_loop(0, nc - 1, advance, 0, unroll=True)

    qi = lax.broadcasted_iota(jnp.int32, (C, C), 0)
    kj = lax.broadcasted_iota(jnp.int32, (C, C), 1)
    causal = qi >= kj

    def emit(ci, carry):
        rows = chunk(ci)
        q = q_ref[0, rows, :]
        s = lax.dot_general(q, k_ref[0, rows, :], (((1,), (1,)), ((), ())), preferred_element_type=F32)
        s = jnp.where(causal, s, 0.0).astype(BF16)
        o = (jnp.dot(s, v_ref[0, rows, :], preferred_element_type=F32)
             + jnp.dot(q, st_in[ci], preferred_element_type=F32))
        mu = jnp.mean(o, axis=-1, keepdims=True)
        oc = o - mu
        var = jnp.mean(oc * oc, axis=-1, keepdims=True)
        on = oc * lax.rsqrt(var + GN_EPS) * gng + gnb
        o_ref[0, rows, :] = (gate_ref[0, rows, :].astype(F32) * on).astype(BF16)
        return carry

    lax.fori_loop(0, nc, emit, 0, unroll=2)


def _retention(main3, gn_g, gn_b):
    B, S, _ = main3.shape
    H, C = RET_HEADS, RET_C
    assert S % C == 0
    cdec = jnp.exp(jnp.asarray(RET_LOG_GAMMA, F32) * C)[:, None, None]
    kq = RET_QK_W // RET_DK
    kv = 2 * RET_QK_W // RET_DV
    kg = kv + RET_V_W // RET_DV
    return pl.pallas_call(
        _ret_kernel,
        out_shape=jax.ShapeDtypeStruct((B, S, RET_V_W), BF16),
        grid=(B, H),
        in_specs=[pl.BlockSpec((1, S, RET_DK), lambda b, h: (b, 0, h)),
                  pl.BlockSpec((1, S, RET_DK), lambda b, h: (b, 0, kq + h)),
                  pl.BlockSpec((1, S, RET_DV), lambda b, h: (b, 0, kv + h)),
                  pl.BlockSpec((1, S, RET_DV), lambda b, h: (b, 0, kg + h)),
                  pl.BlockSpec((1, 1, 1), lambda b, h: (h, 0, 0)),
                  pl.BlockSpec((1, 1, RET_DV), lambda b, h: (h, 0, 0)),
                  pl.BlockSpec((1, 1, RET_DV), lambda b, h: (h, 0, 0))],
        out_specs=pl.BlockSpec((1, S, RET_DV), lambda b, h: (b, 0, h)),
        scratch_shapes=[pltpu.VMEM((RET_DK, RET_DV), F32), pltpu.VMEM((S // C, RET_DK, RET_DV), BF16)],
        compiler_params=_params(("arbitrary", "arbitrary")),
        name="ret",
    )(main3, main3, main3, main3, cdec, gn_g.reshape(H, 1, RET_DV), gn_b.reshape(H, 1, RET_DV))


def _t5_bucket(dist):
    max_exact = REL_BUCKETS // 2
    d_f = jnp.maximum(dist, 1).astype(F32)
    large = max_exact + (jnp.log(d_f / max_exact) / math.log(REL_MAX_DIST / max_exact)
                         * (REL_BUCKETS - max_exact)).astype(jnp.int32)
    large = jnp.minimum(large, REL_BUCKETS - 1)
    return jnp.where(dist < max_exact, dist, large)


def _attn_unit(first, q_ref, k_refs, v_refs, bias_scr, o_ref, lse_ref):
    blk = ATT_BLK
    nk = blk * len(k_refs)
    if len(k_refs) == 2:
        kj = lax.broadcasted_iota(jnp.int32, (blk, nk), 1)
        pen = jnp.where(first, NEG, 0.0)
        first_pen = jnp.where(kj < blk, pen, 0.0)
    lane = lax.broadcasted_iota(jnp.int32, (blk, LSE_W), 1)
    lse_all = jnp.zeros((blk, LSE_W), F32)
    for h in range(ATT_HEADS_PER_GROUP):
        hs = slice(h * ATT_HEAD_DIM, (h + 1) * ATT_HEAD_DIM)
        q = q_ref[:, hs]
        if len(k_refs) == 2:
            kb = jnp.concatenate([k_refs[0][:, hs], k_refs[1][:, hs]], axis=0)
            vb = jnp.concatenate([v_refs[0][:, hs], v_refs[1][:, hs]], axis=0)
            bias = bias_scr[h] + first_pen
        else:
            kb = k_refs[0][:, hs]
            vb = v_refs[0][:, hs]
            bias = bias_scr[h, :, blk:]
        s = lax.dot_general(q, kb, (((1,), (1,)), ((), ())),
                            preferred_element_type=F32) * (ATT_HEAD_DIM ** -0.5) + bias
        mx = jnp.max(s, axis=-1, keepdims=True)
        e = jnp.exp(s - mx)
        den = jnp.sum(e, axis=-1, keepdims=True)
        p = (e / den).astype(BF16)
        o_ref[:, hs] = jnp.dot(p, vb, preferred_element_type=F32).astype(BF16)
        lse_all = jnp.where(lane == h, mx + jnp.log(den), lse_all)
    lse_ref[...] = lse_all


def _attn_kernel(tab_ref,
                 q0, kp0, kc0, vp0, vc0, q1, kp1, kc1, vp1, vc1, q2, kc2, vc2,
                 bk0, bk1, bk2,
                 o0, l0, o1, l1, o2, l2,
                 bias0, bias1, bias2, *, nb1):
    b = pl.program_id(0)
    i = pl.program_id(1)
    blk = ATT_BLK

    @pl.when((b == 0) & (i == 0))
    def _():
        qi = lax.broadcasted_iota(jnp.int32, (blk, 2 * blk), 0)
        kj = lax.broadcasted_iota(jnp.int32, (blk, 2 * blk), 1)
        m = blk + qi - kj
        band = (m >= 0) & (m <= blk)
        for gi, (bk_ref, bias_scr) in enumerate(((bk0, bias0), (bk1, bias1), (bk2, bias2))):
            bucket = bk_ref[...]
            for h in range(ATT_HEADS_PER_GROUP):
                col = gi * ATT_HEADS_PER_GROUP + h
                acc = jnp.zeros((blk, 2 * blk), F32)
                for t in range(REL_BUCKETS):
                    acc = jnp.where(bucket == t, tab_ref[t, col], acc)
                bias_scr[h] = jnp.where(band, acc, NEG)

    _attn_unit(i == 0, q0, (kp0, kc0), (vp0, vc0), bias0, o0, l0)
    _attn_unit(i % nb1 == 0, q1, (kp1, kc1), (vp1, vc1), bias1, o1, l1)
    _attn_unit(True, q2, (kc2,), (vc2,), bias2, o2, l2)


def _attention(main3, att1, att2, rel_bias):
    B, S, _ = main3.shape
    blk, gw = ATT_BLK, ATT_GROUP_W
    steps = S // blk
    nbs = [S // dil // blk for _, dil in ATT_GROUPS]
    assert all(win // dil == blk for win, dil in ATT_GROUPS) and nbs[0] == steps and nbs[2] == 1
    c0 = COL_ATT // gw

    def spec0(part, prev):
        return pl.BlockSpec((None, blk, gw),
                            lambda b, i: (b, jnp.maximum(i - 1, 0) if prev else i, c0 + part))

    def spec_sub(nb, part, prev):
        def imap(b, i):
            n = i % nb
            return b, i // nb, (jnp.maximum(n - 1, 0) if prev else n), part
        return pl.BlockSpec((None, None, blk, gw), imap)

    both = [(0, False), (1, True), (1, False), (2, True), (2, False)]
    in_specs = [pl.BlockSpec(memory_space=pltpu.SMEM)]
    in_specs += [spec0(p, pv) for p, pv in both]
    in_specs += [spec_sub(nbs[1], p, pv) for p, pv in both]
    in_specs += [spec_sub(nbs[2], p, False) for p in range(3)]
    args = [rel_bias] + [main3] * 5 + [att1] * 5 + [att2] * 3
    qi = jnp.arange(blk)[:, None]
    kj = jnp.arange(2 * blk)[None, :]
    m = blk + qi - kj
    for _, dil in ATT_GROUPS:
        in_specs.append(_const_spec((blk, 2 * blk)))
        args.append(_t5_bucket(jnp.clip(m, 0, blk) * dil).astype(jnp.int32))
    out_specs, out_shapes = [], []
    for (_, dil), nb in zip(ATT_GROUPS, nbs):
        for w, dt in ((gw, BF16), (LSE_W, F32)):
            if dil == 1:
                out_specs.append(pl.BlockSpec((None, blk, w), lambda b, i: (b, i, 0)))
                out_shapes.append(jax.ShapeDtypeStruct((B, S, w), dt))
            else:
                out_specs.append(pl.BlockSpec((None, None, blk, w),
                                              lambda b, i, nb=nb: (b, i // nb, i % nb, 0)))
                out_shapes.append(jax.ShapeDtypeStruct((B, dil, S // dil, w), dt))
    outs = pl.pallas_call(
        functools.partial(_attn_kernel, nb1=nbs[1]),
        out_shape=out_shapes,
        grid=(B, steps),
        in_specs=in_specs,
        out_specs=out_specs,
        scratch_shapes=[pltpu.VMEM((ATT_HEADS_PER_GROUP, blk, 2 * blk), F32)] * 3,
        compiler_params=_params(("arbitrary", "arbitrary")),
        name="attn",
    )(*args)
    return outs[0::2], outs[1::2]


def _merge_kernel(x_ref, g_ref, mod_ref, retg_ref, o0_ref, o1_ref, o2_ref, l0_ref, l1_ref, l2_ref,
                  wg_ref, wr_ref, wa_ref, wo_ref, out_ref, o1_scr, o2_scr, l1_scr, l2_scr):
    tm = x_ref.shape[0]
    for dil, o_ref, l_ref, o_scr, l_scr in ((ATT_GROUPS[1][1], o1_ref, l1_ref, o1_scr, l1_scr),
                                            (ATT_GROUPS[2][1], o2_ref, l2_ref, o2_scr, l2_scr)):
        n = tm // dil
        for r in range(dil):
            l_scr[pl.ds(r, n, stride=dil), :] = l_ref[0, r]
            for h in range(ATT_HEADS_PER_GROUP):
                hs = slice(h * ATT_HEAD_DIM, (h + 1) * ATT_HEAD_DIM)
                o_scr[h, pl.ds(r, n, stride=dil), :] = o_ref[0, r, :, hs].astype(F32)
    l0, l1, l2 = l0_ref[...], l1_scr[...], l2_scr[...]
    lm = jnp.maximum(jnp.maximum(l0, l1), l2)
    e0, e1, e2 = jnp.exp(l0 - lm), jnp.exp(l1 - lm), jnp.exp(l2 - lm)
    tot = e0 + e1 + e2
    w0, w1, w2 = e0 / tot, e1 / tot, e2 / tot
    parts = []
    for h in range(ATT_HEADS_PER_GROUP):
        hs = slice(h * ATT_HEAD_DIM, (h + 1) * ATT_HEAD_DIM)
        parts.append(w0[:, h:h + 1] * o0_ref[:, hs].astype(F32)
                     + w1[:, h:h + 1] * o1_scr[h]
                     + w2[:, h:h + 1] * o2_scr[h])
    att = jnp.concatenate(parts, axis=-1).astype(BF16)
    att_out = jnp.dot(att, wa_ref[...], preferred_element_type=F32)
    ret_out = jnp.dot(retg_ref[...], wr_ref[...], preferred_element_type=F32)
    x = x_ref[...]
    h = _modulated_norm(x, g_ref[...], mod_ref, 0).astype(BF16)
    gates = jax.nn.sigmoid(jnp.dot(h, wg_ref[...], preferred_element_type=F32))
    merged = gates[:, :D_MODEL] * ret_out + gates[:, D_MODEL:] * att_out
    y = jnp.dot(merged.astype(BF16), wo_ref[...], preferred_element_type=F32)
    out_ref[...] = x + mod_ref[0, :, 2 * D_MODEL:3 * D_MODEL] * y


def _merge(x2, norm_g, mod3, retg2, o, lse, w_gate, w_ret_out, w_att_out, w_o, batch, seq):
    tm = ROW_TM
    tps = seq // tm
    row = lambda w: pl.BlockSpec((tm, w), lambda i: (i, 0))
    sub = lambda d, w: pl.BlockSpec((1, d, tm // d, w), lambda i: (i // tps, 0, i % tps, 0))
    d1, d2 = ATT_GROUPS[1][1], ATT_GROUPS[2][1]
    return pl.pallas_call(
        _merge_kernel,
        out_shape=jax.ShapeDtypeStruct((batch * seq, D_MODEL), F32),
        grid=(batch * seq // tm,),
        in_specs=[row(D_MODEL),
                  _const_spec((1, D_MODEL)),
                  pl.BlockSpec((1, 1, 6 * D_MODEL), lambda i: (i // tps, 0, 0)),
                  row(RET_V_W),
                  row(ATT_GROUP_W), sub(d1, ATT_GROUP_W), sub(d2, ATT_GROUP_W),
                  row(LSE_W), sub(d1, LSE_W), sub(d2, LSE_W),
                  _const_spec((D_MODEL, 2 * D_MODEL)),
                  _const_spec((RET_V_W, D_MODEL)),
                  _const_spec((ATT_GROUP_W, D_MODEL)),
                  _const_spec((D_MODEL, D_MODEL))],
        out_specs=row(D_MODEL),
        scratch_shapes=[pltpu.VMEM((ATT_HEADS_PER_GROUP, tm, ATT_HEAD_DIM), F32)] * 2
                       + [pltpu.VMEM((tm, LSE_W), F32)] * 2,
        compiler_params=_params(("arbitrary",)),
        name="merge",
    )(x2, norm_g, mod3, retg2, *o, *lse, w_gate, w_ret_out, w_att_out, w_o)


def _mlp_kernel(x_ref, mod_ref, g2_ref, gf_ref, w1_ref, w2_ref, out_ref, *, final):
    x = x_ref[...]
    h = _modulated_norm(x, g2_ref[...], mod_ref, 3).astype(BF16)
    u = jnp.maximum(jnp.dot(h, w1_ref[...], preferred_element_type=F32), 0.0)
    y = jnp.dot((u * u).astype(BF16), w2_ref[...], preferred_element_type=F32)
    x = x + mod_ref[0, :, 5 * D_MODEL:6 * D_MODEL] * y
    out_ref[...] = _rms(x, gf_ref[...]) if final else x


def _mlp(x2, mod3, norm2_g, norm_f_g, w1, w2, seq, final):
    rows = x2.shape[0]
    tm = ROW_TM
    tps = seq // tm
    return pl.pallas_call(
        functools.partial(_mlp_kernel, final=final),
        out_shape=jax.ShapeDtypeStruct((rows, D_MODEL), F32),
        grid=(rows // tm,),
        in_specs=[pl.BlockSpec((tm, D_MODEL), lambda i: (i, 0)),
                  pl.BlockSpec((1, 1, 6 * D_MODEL), lambda i: (i // tps, 0, 0)),
                  _const_spec((1, D_MODEL)),
                  _const_spec((1, D_MODEL)),
                  _const_spec((D_MODEL, D_FF)),
                  _const_spec((D_FF, D_MODEL))],
        out_specs=pl.BlockSpec((tm, D_MODEL), lambda i: (i, 0)),
        compiler_params=_params(("arbitrary",)),
        name="mlp",
    )(x2, mod3, norm2_g, norm_f_g, w1, w2)


def kernel(x, c, w_ada, b_ada, norm1_g, w_in, rel_bias, ret_gn_g, ret_gn_b, w_ret_out, w_att_out,
           w_o, norm2_g, w_ff1, w_ff2, norm_f_g):
    B, S, D = x.shape
    depth = w_ada.shape[0]
    half = RET_DK // 2
    inv = ROPE_BASE ** (-jnp.arange(half, dtype=F32) / half)
    ang = jnp.arange(S).astype(F32)[:, None] * inv[None, :]
    cos, sin = jnp.cos(ang), jnp.sin(ang)
    x2 = x.reshape(B * S, D)
    for l in range(depth):
        g1 = norm1_g[l].reshape(1, D)
        mod3 = _mod(c, w_ada[l], b_ada[l]).reshape(B, 1, 6 * D)
        main2, att1, att2 = _proj(x2, g1, mod3, cos, sin, w_in[l, :, :COL_GATE].astype(BF16), B, S)
        main3 = main2.reshape(B, S, main2.shape[1])
        retg = _retention(main3, ret_gn_g[l], ret_gn_b[l])
        o, lse = _attention(main3, att1, att2, rel_bias)
        o = [o[0].reshape(B * S, ATT_GROUP_W), o[1], o[2]]
        lse = [lse[0].reshape(B * S, LSE_W), lse[1], lse[2]]
        x2 = _merge(x2, g1, mod3, retg.reshape(B * S, RET_V_W), o, lse,
                    w_in[l, :, COL_GATE:].astype(BF16), w_ret_out[l].astype(BF16),
                    w_att_out[l].astype(BF16), w_o[l].astype(BF16), B, S)
        x2 = _mlp(x2, mod3, norm2_g[l].reshape(1, D), norm_f_g.reshape(1, D),
                  w_ff1[l].astype(BF16), w_ff2[l].astype(BF16), S, final=l == depth - 1)
    return x2.reshape(B, S, D)
```

```python
import functools
import math

import jax
import jax.numpy as jnp
from jax import lax
from jax.experimental import pallas as pl
from jax.experimental.pallas import tpu as pltpu

F32 = jnp.float32
BF16 = jnp.bfloat16

D_MODEL = 1024
RET_HEADS = 4
RET_DK = 256
RET_DV = 512
RET_C = 256
RET_LOG_GAMMA = tuple(math.log1p(-(2.0 ** (-5.0 - h))) for h in range(RET_HEADS))
RET_QK_W = RET_HEADS * RET_DK
RET_V_W = RET_HEADS * RET_DV
ATT_GROUPS = ((128, 1), (512, 4), (2048, 16))
ATT_HEADS_PER_GROUP = 4
ATT_HEAD_DIM = 128
ATT_GROUP_W = ATT_HEADS_PER_GROUP * ATT_HEAD_DIM
ATT_BLK = 128
LANES = 128
LSE_W = LANES
REL_BUCKETS = 32
REL_MAX_DIST = 2048
D_FF = 4 * D_MODEL
RMS_EPS = 1e-6
GN_EPS = 1e-5
ROPE_BASE = 10000.0
NEG = -1e30
ATT_QSCALE = ATT_HEAD_DIM ** -0.5

COL_ATT = 2 * RET_QK_W + 2 * RET_V_W
COL_GATE = COL_ATT + 9 * ATT_GROUP_W

VMEM_LIMIT = 56 * 1024 * 1024

PROJ_TM = 512
PROJ_TN = 3 * ATT_GROUP_W
PROJ_MAIN_TILES = (COL_ATT + PROJ_TN) // PROJ_TN
ROW_TM = 512


def _params(sem):
    return pltpu.CompilerParams(dimension_semantics=sem, vmem_limit_bytes=VMEM_LIMIT)


def _const_spec(shape):
    zeros = (0,) * len(shape)
    return pl.BlockSpec(shape, lambda *_: zeros, pipeline_mode=pl.Buffered(1))


def _silu(t):
    return t * jax.nn.sigmoid(t)


def _rms(x, g):
    return x * lax.rsqrt(jnp.mean(x * x, axis=-1, keepdims=True) + RMS_EPS) * g


def _modulated_norm(x, g, mod_ref, k):
    shift = mod_ref[0, :, k * D_MODEL:(k + 1) * D_MODEL]
    scale = mod_ref[0, :, (k + 1) * D_MODEL:(k + 2) * D_MODEL]
    return _rms(x, g) * (1.0 + scale) + shift


def _mod_kernel(c_ref, w_ref, b_ref, o_ref):
    o_ref[...] = jnp.dot(_silu(c_ref[...]), w_ref[...], preferred_element_type=F32) + b_ref[...]


def _mod(c, w_ada, b_ada):
    B = c.shape[0]
    n = w_ada.shape[1]
    tn = D_MODEL
    return pl.pallas_call(
        _mod_kernel,
        out_shape=jax.ShapeDtypeStruct((B, n), F32),
        grid=(n // tn,),
        in_specs=[pl.BlockSpec((B, D_MODEL), lambda j: (0, 0)),
                  pl.BlockSpec((D_MODEL, tn), lambda j: (0, j)),
                  pl.BlockSpec((1, tn), lambda j: (0, j))],
        out_specs=pl.BlockSpec((B, tn), lambda j: (0, j)),
        compiler_params=_params(("arbitrary",)),
        name="mod",
    )(c, w_ada, b_ada.reshape(1, n))


def _proj_main_layout():
    spans = [(h * RET_DK, (h + 1) * RET_DK, "rot", (h % RET_HEADS, h >= RET_HEADS))
             for h in range(2 * RET_HEADS)]
    spans += [(2 * RET_QK_W, 2 * RET_QK_W + RET_V_W, "copy", None),
              (2 * RET_QK_W + RET_V_W, COL_ATT, "silu", None),
              (COL_ATT, COL_ATT + ATT_GROUP_W, "scale", ATT_QSCALE),
              (COL_ATT + ATT_GROUP_W, PROJ_MAIN_TILES * PROJ_TN, "copy", None)]
    tiles = [[] for _ in range(PROJ_MAIN_TILES)]
    for c0, c1, kind, arg in spans:
        for t in range(c0 // PROJ_TN, (c1 - 1) // PROJ_TN + 1):
            lo, hi = max(c0, t * PROJ_TN), min(c1, (t + 1) * PROJ_TN)
            assert kind != "rot" or (lo, hi) == (c0, c1)
            tiles[t].append((lo - t * PROJ_TN, hi - t * PROJ_TN, kind, arg))
    return tiles


PROJ_MAIN_LAYOUT = _proj_main_layout()


def _proj_kernel(x_ref, g_ref, mod_ref, cos_ref, sin_ref, w_ref, main_ref, a1_ref, a2_ref,
                 hf_scr, hd1_scr, h0_scr, h1_scr, h2_scr, *, tps):
    j = pl.program_id(1)
    tm = x_ref.shape[0]
    perms = ((ATT_GROUPS[1][1], h1_scr, a1_ref), (ATT_GROUPS[2][1], h2_scr, a2_ref))

    @pl.when(j == 0)
    def _():
        hf = _modulated_norm(x_ref[...], g_ref[...], mod_ref, 0)
        h0_scr[...] = hf.astype(BF16)
        d1, d2 = ATT_GROUPS[1][1], ATT_GROUPS[2][1]
        n1, n2 = tm // d1, tm // d2
        for c in range(D_MODEL // LANES):
            cs = slice(c * LANES, (c + 1) * LANES)
            hf_scr[c] = hf[:, cs]
            for r in range(d1):
                part = hf_scr[c, pl.ds(r, n1, stride=d1), :]
                hd1_scr[c, r * n1:(r + 1) * n1, :] = part
                h1_scr[r * n1:(r + 1) * n1, cs] = part.astype(BF16)
            for r in range(d2):
                src = pl.ds((r % d1) * n1 + r // d1, n2, stride=d1)
                h2_scr[r * n2:(r + 1) * n2, cs] = hd1_scr[c, src, :].astype(BF16)

    def w_tile(t):
        return w_ref[:, t * PROJ_TN:(t + 1) * PROJ_TN]

    def main_tile(t, segments):
        acc = jnp.dot(h0_scr[...], w_tile(t), preferred_element_type=F32)
        if any(kind == "rot" for _, _, kind, _ in segments):
            cos, sin = cos_ref[...], sin_ref[...]
            row = lax.broadcasted_iota(jnp.int32, cos.shape, 0)
            pos = (((pl.program_id(0) % tps) * tm + row) % RET_C + 1).astype(F32)
        for c0, c1, kind, arg in segments:
            t = acc[:, c0:c1]
            if kind == "rot":
                head, is_key = arg
                rate = -RET_LOG_GAMMA[head] if is_key else RET_LOG_GAMMA[head]
                dec = jnp.exp(rate * pos) * ((RET_DK ** -0.5) if is_key else 1.0)
                cd, sd = cos * dec, sin * dec
                cm = (c0 + c1) // 2
                t1, t2 = acc[:, c0:cm], acc[:, cm:c1]
                main_ref[:, c0:cm] = (t1 * cd - t2 * sd).astype(BF16)
                main_ref[:, cm:c1] = (t1 * sd + t2 * cd).astype(BF16)
            elif kind == "silu":
                main_ref[:, c0:c1] = _silu(t).astype(BF16)
            elif kind == "scale":
                main_ref[:, c0:c1] = (t * arg).astype(BF16)
            else:
                main_ref[:, c0:c1] = t.astype(BF16)

    for t, segments in enumerate(PROJ_MAIN_LAYOUT):
        @pl.when(j == t)
        def _(t=t, segments=segments):
            main_tile(t, segments)

    for t, (dil, h_scr, a_ref) in enumerate(perms):
        @pl.when(j == PROJ_MAIN_TILES + t)
        def _(t=t, dil=dil, h_scr=h_scr, a_ref=a_ref):
            n = tm // dil
            acc = jnp.dot(h_scr[...], w_tile(PROJ_MAIN_TILES + t), preferred_element_type=F32)
            gw = ATT_GROUP_W
            for r in range(dil):
                rows = slice(r * n, (r + 1) * n)
                a_ref[0, r, :, :gw] = (acc[rows, :gw] * ATT_QSCALE).astype(BF16)
                a_ref[0, r, :, gw:] = acc[rows, gw:].astype(BF16)


def _proj(x2, norm_g, mod3, cos, sin, w_att, batch, seq):
    tm, tn = PROJ_TM, PROJ_TN
    tps = seq // tm
    assert seq % tm == 0 and tm % RET_C == 0
    d1, d2 = ATT_GROUPS[1][1], ATT_GROUPS[2][1]
    sub = lambda d: pl.BlockSpec((1, d, tm // d, tn), lambda i, j: (i // tps, 0, i % tps, 0))
    return pl.pallas_call(
        functools.partial(_proj_kernel, tps=tps),
        out_shape=[jax.ShapeDtypeStruct((batch * seq, PROJ_MAIN_TILES * tn), BF16),
                   jax.ShapeDtypeStruct((batch, d1, seq // d1, tn), BF16),
                   jax.ShapeDtypeStruct((batch, d2, seq // d2, tn), BF16)],
        grid=(batch * seq // tm, PROJ_MAIN_TILES + 2),
        in_specs=[pl.BlockSpec((tm, D_MODEL), lambda i, j: (i, 0)),
                  pl.BlockSpec((1, D_MODEL), lambda i, j: (0, 0)),
                  pl.BlockSpec((1, 1, 6 * D_MODEL), lambda i, j: (i // tps, 0, 0)),
                  pl.BlockSpec((tm, RET_DK // 2), lambda i, j: (i % tps, 0)),
                  pl.BlockSpec((tm, RET_DK // 2), lambda i, j: (i % tps, 0)),
                  _const_spec(w_att.shape)],
        out_specs=[pl.BlockSpec((tm, tn), lambda i, j: (i, jnp.minimum(j, PROJ_MAIN_TILES - 1))),
                   sub(d1), sub(d2)],
        scratch_shapes=[pltpu.VMEM((D_MODEL // LANES, tm, LANES), F32)] * 2
                       + [pltpu.VMEM((tm, D_MODEL), BF16)] * 3,
        compiler_params=_params(("arbitrary", "arbitrary")),
        name="proj",
    )(x2, norm_g, mod3, cos, sin, w_att)


def _ret_kernel(q_ref, k_ref, v_ref, gate_ref, cdec_ref, gng_ref, gnb_ref, o_ref, state, st_in):
    C = RET_C
    nc = q_ref.shape[1] // C
    cdec = cdec_ref[0]
    gng = gng_ref[0]
    gnb = gnb_ref[0]
    chunk = lambda ci: pl.ds(pl.multiple_of(ci * C, C), C)

    state[...] = jnp.zeros_like(state)
    st_in[0] = jnp.zeros(st_in.shape[1:], BF16)

    def advance(ci, carry):
        rows = chunk(ci)
        kv = lax.dot_general(k_ref[0, rows, :], v_ref[0, rows, :], (((0,), (0,)), ((), ())),
                             preferred_element_type=F32)
        new = (state[...] + kv) * cdec
        state[...] = new
        st_in[ci + 1] = new.astype(BF16)
        return carry

    lax.fori_loop(0, nc - 1, advance, 0, unroll=True)

    qi = lax.broadcasted_iota(jnp.int32, (C, C), 0)
    kj = lax.broadcasted_iota(jnp.int32, (C, C), 1)
    causal = qi >= kj

    def emit(ci, carry):
        rows = chunk(ci)
        q = q_ref[0, rows, :]
        s = lax.dot_general(q, k_ref[0, rows, :], (((1,), (1,)), ((), ())), preferred_element_type=F32)
        s = jnp.where(causal, s, 0.0).astype(BF16)
        o = (jnp.dot(s, v_ref[0, rows, :], preferred_element_type=F32)
             + jnp.dot(q, st_in[ci], preferred_element_type=F32))
        mu = jnp.mean(o, axis=-1, keepdims=True)
        oc = o - mu
        var = jnp.mean(oc * oc, axis=-1, keepdims=True)
        on = oc * lax.rsqrt(var + GN_EPS) * gng + gnb
        o_ref[0, rows, :] = (gate_ref[0, rows, :].astype(F32) * on).astype(BF16)
        return carry

    lax.fori_loop(0, nc, emit, 0, unroll=2)


def _retention(main3, gn_g, gn_b):
    B, S, _ = main3.shape
    H, C = RET_HEADS, RET_C
    assert S % C == 0
    cdec = jnp.exp(jnp.asarray(RET_LOG_GAMMA, F32) * C)[:, None, None]
    kq = RET_QK_W // RET_DK
    kv = 2 * RET_QK_W // RET_DV
    kg = kv + RET_V_W // RET_DV
    return pl.pallas_call(
        _ret_kernel,
        out_shape=jax.ShapeDtypeStruct((B, S, RET_V_W), BF16),
        grid=(B, H),
        in_specs=[pl.BlockSpec((1, S, RET_DK), lambda b, h: (b, 0, h)),
                  pl.BlockSpec((1, S, RET_DK), lambda b, h: (b, 0, kq + h)),
                  pl.BlockSpec((1, S, RET_DV), lambda b, h: (b, 0, kv + h)),
                  pl.BlockSpec((1, S, RET_DV), lambda b, h: (b, 0, kg + h)),
                  pl.BlockSpec((1, 1, 1), lambda b, h: (h, 0, 0)),
                  pl.BlockSpec((1, 1, RET_DV), lambda b, h: (h, 0, 0)),
                  pl.BlockSpec((1, 1, RET_DV), lambda b, h: (h, 0, 0))],
        out_specs=pl.BlockSpec((1, S, RET_DV), lambda b, h: (b, 0, h)),
        scratch_shapes=[pltpu.VMEM((RET_DK, RET_DV), F32), pltpu.VMEM((S // C, RET_DK, RET_DV), BF16)],
        compiler_params=_params(("arbitrary", "arbitrary")),
        name="ret",
    )(main3, main3, main3, main3, cdec, gn_g.reshape(H, 1, RET_DV), gn_b.reshape(H, 1, RET_DV))


def _t5_bucket(dist):
    max_exact = REL_BUCKETS // 2
    d_f = jnp.maximum(dist, 1).astype(F32)
    large = max_exact + (jnp.log(d_f / max_exact) / math.log(REL_MAX_DIST / max_exact)
                         * (REL_BUCKETS - max_exact)).astype(jnp.int32)
    large = jnp.minimum(large, REL_BUCKETS - 1)
    return jnp.where(dist < max_exact, dist, large)


def _attn_unit(n, q_ref, k_ref, v_ref, bias_scr, vaug_scr, o_ref, lse_ref):
    blk, dh = ATT_BLK, ATT_HEAD_DIM
    windowed = k_ref.shape[0] > blk
    if windowed:
        win = pl.ds(pl.multiple_of(jnp.maximum(n - 1, 0) * blk, blk), 2 * blk)
        first = (n == 0).astype(jnp.int32)
    lane = lax.broadcasted_iota(jnp.int32, (blk, LSE_W), 1)
    lse_all = jnp.zeros((blk, LSE_W), F32)
    for h in range(ATT_HEADS_PER_GROUP):
        hs = slice(h * dh, (h + 1) * dh)
        if windowed:
            kb, vb, bias = k_ref[win, hs], v_ref[win, hs], bias_scr[first, h]
        else:
            kb, vb, bias = k_ref[:, hs], v_ref[:, hs], bias_scr[h]
        vaug_scr[h, :, :dh] = vb
        s = lax.dot_general(q_ref[:, hs], kb, (((1,), (1,)), ((), ())), preferred_element_type=F32) + bias
        mx = jnp.max(s, axis=-1, keepdims=True)
        e = jnp.exp(s - mx).astype(BF16)
        o_den = jnp.dot(e, vaug_scr[h], preferred_element_type=F32)
        den = o_den[:, dh:]
        o_ref[:, hs] = (o_den[:, :dh] * (1.0 / den)).astype(BF16)
        lse_all = jnp.where(lane == h, mx + jnp.log(den), lse_all)
    lse_ref[...] = lse_all


def _attn_kernel(tab_ref, q0, k0, v0, q1, k1, v1, q2, k2, v2, bk0, bk1, bk2,
                 o0, l0, o1, l1, o2, l2, bias0, bias1, bias2, vaug0, vaug1, vaug2, *, nb1):
    b = pl.program_id(0)
    i = pl.program_id(1)
    blk = ATT_BLK

    @pl.when((b == 0) & (i == 0))
    def _():
        for vaug in (vaug0, vaug1, vaug2):
            vaug[...] = jnp.ones(vaug.shape, BF16)
        qi = lax.broadcasted_iota(jnp.int32, (blk, 2 * blk), 0)
        kj = lax.broadcasted_iota(jnp.int32, (blk, 2 * blk), 1)
        m = blk + qi - kj
        band = (m >= 0) & (m <= blk)
        masked = jnp.full((blk, blk), NEG, F32)
        for gi, (bk_ref, bias_scr) in enumerate(((bk0, bias0), (bk1, bias1), (bk2, bias2))):
            bucket = bk_ref[...]
            for h in range(ATT_HEADS_PER_GROUP):
                col = gi * ATT_HEADS_PER_GROUP + h
                acc = jnp.zeros((blk, 2 * blk), F32)
                for t in range(REL_BUCKETS):
                    acc = jnp.where(bucket == t, tab_ref[t, col], acc)
                tile = jnp.where(band, acc, NEG)
                if len(bias_scr.shape) == 4:
                    bias_scr[0, h] = tile
                    bias_scr[1, h] = jnp.concatenate([tile[:, blk:], masked], axis=1)
                else:
                    bias_scr[h] = tile[:, blk:]

    _attn_unit(i, q0, k0, v0, bias0, vaug0, o0, l0)
    _attn_unit(i % nb1, q1, k1, v1, bias1, vaug1, o1, l1)
    _attn_unit(0, q2, k2, v2, bias2, vaug2, o2, l2)


def _attention(main3, att1, att2, rel_bias):
    B, S, _ = main3.shape
    blk, gw = ATT_BLK, ATT_GROUP_W
    steps = S // blk
    lens = [S // dil for _, dil in ATT_GROUPS]
    nbs = [L // blk for L in lens]
    assert all(win // dil == blk for win, dil in ATT_GROUPS)
    assert nbs[0] == steps and nbs[1] >= 2 and nbs[2] == 1
    c0 = COL_ATT // gw
    nb1 = nbs[1]
    in_specs = [pl.BlockSpec(memory_space=pltpu.SMEM),
                pl.BlockSpec((None, blk, gw), lambda b, i: (b, i, c0)),
                pl.BlockSpec((None, S, gw), lambda b, i: (b, 0, c0 + 1)),
                pl.BlockSpec((None, S, gw), lambda b, i: (b, 0, c0 + 2)),
                pl.BlockSpec((None, None, blk, gw), lambda b, i: (b, i // nb1, i % nb1, 0)),
                pl.BlockSpec((None, None, lens[1], gw), lambda b, i: (b, i // nb1, 0, 1)),
                pl.BlockSpec((None, None, lens[1], gw), lambda b, i: (b, i // nb1, 0, 2)),
                pl.BlockSpec((None, None, blk, gw), lambda b, i: (b, i, 0, 0)),
                pl.BlockSpec((None, None, blk, gw), lambda b, i: (b, i, 0, 1)),
                pl.BlockSpec((None, None, blk, gw), lambda b, i: (b, i, 0, 2))]
    args = [rel_bias] + [main3] * 3 + [att1] * 3 + [att2] * 3
    qi = jnp.arange(blk)[:, None]
    kj = jnp.arange(2 * blk)[None, :]
    m = blk + qi - kj
    for _, dil in ATT_GROUPS:
        in_specs.append(_const_spec((blk, 2 * blk)))
        args.append(_t5_bucket(jnp.clip(m, 0, blk) * dil).astype(jnp.int32))
    out_specs, out_shapes = [], []
    for (_, dil), nb in zip(ATT_GROUPS, nbs):
        for w, dt in ((gw, BF16), (LSE_W, F32)):
            if dil == 1:
                out_specs.append(pl.BlockSpec((None, blk, w), lambda b, i: (b, i, 0)))
                out_shapes.append(jax.ShapeDtypeStruct((B, S, w), dt))
            else:
                out_specs.append(pl.BlockSpec((None, None, blk, w),
                                              lambda b, i, nb=nb: (b, i // nb, i % nb, 0)))
                out_shapes.append(jax.ShapeDtypeStruct((B, dil, S // dil, w), dt))
    heads = ATT_HEADS_PER_GROUP
    outs = pl.pallas_call(
        functools.partial(_attn_kernel, nb1=nb1),
        out_shape=out_shapes,
        grid=(B, steps),
        in_specs=in_specs,
        out_specs=out_specs,
        scratch_shapes=[pltpu.VMEM((2, heads, blk, 2 * blk), F32), pltpu.VMEM((2, heads, blk, 2 * blk), F32),
                        pltpu.VMEM((heads, blk, blk), F32),
                        pltpu.VMEM((heads, 2 * blk, 2 * ATT_HEAD_DIM), BF16),
                        pltpu.VMEM((heads, 2 * blk, 2 * ATT_HEAD_DIM), BF16),
                        pltpu.VMEM((heads, blk, 2 * ATT_HEAD_DIM), BF16)],
        compiler_params=_params(("arbitrary", "arbitrary")),
        name="attn",
    )(*args)
    return outs[0::2], outs[1::2]


def _merge_kernel(x_ref, g_ref, mod_ref, retg_ref, o0_ref, o1_ref, o2_ref, l0_ref, l1_ref, l2_ref,
                  wg_ref, wr_ref, wa_ref, wo_ref, out_ref, o1_scr, o2_scr, l1_scr, l2_scr):
    tm = x_ref.shape[0]
    for dil, o_ref, l_ref, o_scr, l_scr in ((ATT_GROUPS[1][1], o1_ref, l1_ref, o1_scr, l1_scr),
                                            (ATT_GROUPS[2][1], o2_ref, l2_ref, o2_scr, l2_scr)):
        n = tm // dil
        for r in range(dil):
            l_scr[pl.ds(r, n, stride=dil), :] = l_ref[0, r]
            for h in range(ATT_HEADS_PER_GROUP):
                hs = slice(h * ATT_HEAD_DIM, (h + 1) * ATT_HEAD_DIM)
                o_scr[h, pl.ds(r, n, stride=dil), :] = o_ref[0, r, :, hs].astype(F32)
    l0, l1, l2 = l0_ref[...], l1_scr[...], l2_scr[...]
    lm = jnp.maximum(jnp.maximum(l0, l1), l2)
    e0, e1, e2 = jnp.exp(l0 - lm), jnp.exp(l1 - lm), jnp.exp(l2 - lm)
    tot = e0 + e1 + e2
    w0, w1, w2 = e0 / tot, e1 / tot, e2 / tot
    parts = []
    for h in range(ATT_HEADS_PER_GROUP):
        hs = slice(h * ATT_HEAD_DIM, (h + 1) * ATT_HEAD_DIM)
        parts.append(w0[:, h:h + 1] * o0_ref[:, hs].astype(F32)
                     + w1[:, h:h + 1] * o1_scr[h]
                     + w2[:, h:h + 1] * o2_scr[h])
    att = jnp.concatenate(parts, axis=-1).astype(BF16)
    att_out = jnp.dot(att, wa_ref[...], preferred_element_type=F32)
    ret_out = jnp.dot(retg_ref[...], wr_ref[...], preferred_element_type=F32)
    x = x_ref[...]
    h = _modulated_norm(x, g_ref[...], mod_ref, 0).astype(BF16)
    gates = jax.nn.sigmoid(jnp.dot(h, wg_ref[...], preferred_element_type=F32))
    merged = gates[:, :D_MODEL] * ret_out + gates[:, D_MODEL:] * att_out
    y = jnp.dot(merged.astype(BF16), wo_ref[...], preferred_element_type=F32)
    out_ref[...] = x + mod_ref[0, :, 2 * D_MODEL:3 * D_MODEL] * y


def _merge(x2, norm_g, mod3, retg2, o, lse, w_gate, w_ret_out, w_att_out, w_o, batch, seq):
    tm = ROW_TM
    tps = seq // tm
    row = lambda w: pl.BlockSpec((tm, w), lambda i: (i, 0))
    sub = lambda d, w: pl.BlockSpec((1, d, tm // d, w), lambda i: (i // tps, 0, i % tps, 0))
    d1, d2 = ATT_GROUPS[1][1], ATT_GROUPS[2][1]
    return pl.pallas_call(
        _merge_kernel,
        out_shape=jax.ShapeDtypeStruct((batch * seq, D_MODEL), F32),
        grid=(batch * seq // tm,),
        in_specs=[row(D_MODEL),
                  _const_spec((1, D_MODEL)),
                  pl.BlockSpec((1, 1, 6 * D_MODEL), lambda i: (i // tps, 0, 0)),
                  row(RET_V_W),
                  row(ATT_GROUP_W), sub(d1, ATT_GROUP_W), sub(d2, ATT_GROUP_W),
                  row(LSE_W), sub(d1, LSE_W), sub(d2, LSE_W),
                  _const_spec((D_MODEL, 2 * D_MODEL)),
                  _const_spec((RET_V_W, D_MODEL)),
                  _const_spec((ATT_GROUP_W, D_MODEL)),
                  _const_spec((D_MODEL, D_MODEL))],
        out_specs=row(D_MODEL),
        scratch_shapes=[pltpu.VMEM((ATT_HEADS_PER_GROUP, tm, ATT_HEAD_DIM), F32)] * 2
                       + [pltpu.VMEM((tm, LSE_W), F32)] * 2,
        compiler_params=_params(("arbitrary",)),
        name="merge",
    )(x2, norm_g, mod3, retg2, *o, *lse, w_gate, w_ret_out, w_att_out, w_o)


def _mlp_kernel(x_ref, mod_ref, g2_ref, gf_ref, w1_ref, w2_ref, out_ref, *, final):
    x = x_ref[...]
    h = _modulated_norm(x, g2_ref[...], mod_ref, 3).astype(BF16)
    u = jnp.maximum(jnp.dot(h, w1_ref[...], preferred_element_type=F32), 0.0)
    y = jnp.dot((u * u).astype(BF16), w2_ref[...], preferred_element_type=F32)
    x = x + mod_ref[0, :, 5 * D_MODEL:6 * D_MODEL] * y
    out_ref[...] = _rms(x, gf_ref[...]) if final else x


def _mlp(x2, mod3, norm2_g, norm_f_g, w1, w2, seq, final):
    rows = x2.shape[0]
    tm = ROW_TM
    tps = seq // tm
    return pl.pallas_call(
        functools.partial(_mlp_kernel, final=final),
        out_shape=jax.ShapeDtypeStruct((rows, D_MODEL), F32),
        grid=(rows // tm,),
        in_specs=[pl.BlockSpec((tm, D_MODEL), lambda i: (i, 0)),
                  pl.BlockSpec((1, 1, 6 * D_MODEL), lambda i: (i // tps, 0, 0)),
                  _const_spec((1, D_MODEL)),
                  _const_spec((1, D_MODEL)),
                  _const_spec((D_MODEL, D_FF)),
                  _const_spec((D_FF, D_MODEL))],
        out_specs=pl.BlockSpec((tm, D_MODEL), lambda i: (i, 0)),
        compiler_params=_params(("arbitrary",)),
        name="mlp",
    )(x2, mod3, norm2_g, norm_f_g, w1, w2)


def kernel(x, c, w_ada, b_ada, norm1_g, w_in, rel_bias, ret_gn_g, ret_gn_b, w_ret_out, w_att_out,
           w_o, norm2_g, w_ff1, w_ff2, norm_f_g):
    B, S, D = x.shape
    depth = w_ada.shape[0]
    half = RET_DK // 2
    inv = ROPE_BASE ** (-jnp.arange(half, dtype=F32) / half)
    ang = jnp.arange(S).astype(F32)[:, None] * inv[None, :]
    cos, sin = jnp.cos(ang), jnp.sin(ang)
    x2 = x.reshape(B * S, D)
    for l in range(depth):
        g1 = norm1_g[l].reshape(1, D)
        mod3 = _mod(c, w_ada[l], b_ada[l]).reshape(B, 1, 6 * D)
        main2, att1, att2 = _proj(x2, g1, mod3, cos, sin, w_in[l, :, :COL_GATE].astype(BF16), B, S)
        main3 = main2.reshape(B, S, main2.shape[1])
        retg = _retention(main3, ret_gn_g[l], ret_gn_b[l])
        o, lse = _attention(main3, att1, att2, rel_bias)
        o = [o[0].reshape(B * S, ATT_GROUP_W), o[1], o[2]]
        lse = [lse[0].reshape(B * S, LSE_W), lse[1], lse[2]]
        x2 = _merge(x2, g1, mod3, retg.reshape(B * S, RET_V_W), o, lse,
                    w_in[l, :, COL_GATE:].astype(BF16), w_ret_out[l].astype(BF16),
                    w_att_out[l].astype(BF16), w_o[l].astype(BF16), B, S)
        x2 = _mlp(x2, mod3, norm2_g[l].reshape(1, D), norm_f_g.reshape(1, D),
                  w_ff1[l].astype(BF16), w_ff2[l].astype(BF16), S, final=l == depth - 1)
    return x2.reshape(B, S, D)
```

```python
import functools
import math

import jax
import jax.numpy as jnp
from jax import lax
from jax.experimental import pallas as pl
from jax.experimental.pallas import tpu as pltpu

F32 = jnp.float32
BF16 = jnp.bfloat16

D_MODEL = 1024
RET_HEADS = 4
RET_DK = 256
RET_DV = 512
RET_C = 256
RET_LOG_GAMMA = tuple(math.log1p(-(2.0 ** (-5.0 - h))) for h in range(RET_HEADS))
RET_QK_W = RET_HEADS * RET_DK
RET_V_W = RET_HEADS * RET_DV
ATT_GROUPS = ((128, 1), (512, 4), (2048, 16))
ATT_HEADS_PER_GROUP = 4
ATT_HEAD_DIM = 128
ATT_GROUP_W = ATT_HEADS_PER_GROUP * ATT_HEAD_DIM
ATT_BLK = 128
LANES = 128
LSE_W = LANES
REL_BUCKETS = 32
REL_MAX_DIST = 2048
D_FF = 4 * D_MODEL
RMS_EPS = 1e-6
GN_EPS = 1e-5
ROPE_BASE = 10000.0
NEG = -1e30
ATT_QSCALE = ATT_HEAD_DIM ** -0.5

COL_ATT = 2 * RET_QK_W + 2 * RET_V_W
COL_GATE = COL_ATT + 9 * ATT_GROUP_W

VMEM_LIMIT = 56 * 1024 * 1024

PROJ_TM = 256
PROJ_TN = 3 * ATT_GROUP_W
PROJ_MAIN_TILES = (COL_ATT + PROJ_TN) // PROJ_TN
ROW_TM = 512


def _params(sem):
    return pltpu.CompilerParams(dimension_semantics=sem, vmem_limit_bytes=VMEM_LIMIT)


def _const_spec(shape):
    zeros = (0,) * len(shape)
    return pl.BlockSpec(shape, lambda *_: zeros, pipeline_mode=pl.Buffered(1))


def _silu(t):
    return t * jax.nn.sigmoid(t)


def _rms(x, g):
    return x * lax.rsqrt(jnp.mean(x * x, axis=-1, keepdims=True) + RMS_EPS) * g


def _modulated_norm(x, g, mod_ref, k):
    shift = mod_ref[0, :, k * D_MODEL:(k + 1) * D_MODEL]
    scale = mod_ref[0, :, (k + 1) * D_MODEL:(k + 2) * D_MODEL]
    return _rms(x, g) * (1.0 + scale) + shift


def _mod_kernel(c_ref, w_ref, b_ref, o_ref):
    o_ref[...] = jnp.dot(_silu(c_ref[...]), w_ref[...], preferred_element_type=F32) + b_ref[...]


def _mod(c, w_ada, b_ada):
    B = c.shape[0]
    n = w_ada.shape[1]
    tn = D_MODEL
    return pl.pallas_call(
        _mod_kernel,
        out_shape=jax.ShapeDtypeStruct((B, n), F32),
        grid=(n // tn,),
        in_specs=[pl.BlockSpec((B, D_MODEL), lambda j: (0, 0)),
                  pl.BlockSpec((D_MODEL, tn), lambda j: (0, j)),
                  pl.BlockSpec((1, tn), lambda j: (0, j))],
        out_specs=pl.BlockSpec((B, tn), lambda j: (0, j)),
        compiler_params=_params(("arbitrary",)),
        name="mod",
    )(c, w_ada, b_ada.reshape(1, n))


def _proj_main_layout():
    spans = [(h * RET_DK, (h + 1) * RET_DK, "rot", (h % RET_HEADS, h >= RET_HEADS))
             for h in range(2 * RET_HEADS)]
    spans += [(2 * RET_QK_W, 2 * RET_QK_W + RET_V_W, "copy", None),
              (2 * RET_QK_W + RET_V_W, COL_ATT, "silu", None),
              (COL_ATT, COL_ATT + ATT_GROUP_W, "scale", ATT_QSCALE),
              (COL_ATT + ATT_GROUP_W, PROJ_MAIN_TILES * PROJ_TN, "copy", None)]
    tiles = [[] for _ in range(PROJ_MAIN_TILES)]
    for c0, c1, kind, arg in spans:
        for t in range(c0 // PROJ_TN, (c1 - 1) // PROJ_TN + 1):
            lo, hi = max(c0, t * PROJ_TN), min(c1, (t + 1) * PROJ_TN)
            assert kind != "rot" or (lo, hi) == (c0, c1)
            tiles[t].append((lo - t * PROJ_TN, hi - t * PROJ_TN, kind, arg))
    return tiles


PROJ_MAIN_LAYOUT = _proj_main_layout()


def _proj_kernel(x_ref, g_ref, mod_ref, cos_ref, sin_ref, w_ref, main_ref, a1_ref, a2_ref,
                 hf_scr, hd1_scr, h0_scr, h1_scr, h2_scr, *, tps):
    tm = x_ref.shape[0]
    perms = ((ATT_GROUPS[1][1], h1_scr, a1_ref), (ATT_GROUPS[2][1], h2_scr, a2_ref))

    hf = _modulated_norm(x_ref[...], g_ref[...], mod_ref, 0)
    h0_scr[...] = hf.astype(BF16)
    d1, d2 = ATT_GROUPS[1][1], ATT_GROUPS[2][1]
    n1, n2 = tm // d1, tm // d2
    for c in range(D_MODEL // LANES):
        cs = slice(c * LANES, (c + 1) * LANES)
        hf_scr[c] = hf[:, cs]
        for r in range(d1):
            part = hf_scr[c, pl.ds(r, n1, stride=d1), :]
            hd1_scr[c, r * n1:(r + 1) * n1, :] = part
            h1_scr[r * n1:(r + 1) * n1, cs] = part.astype(BF16)
        for r in range(d2):
            src = pl.ds((r % d1) * n1 + r // d1, n2, stride=d1)
            h2_scr[r * n2:(r + 1) * n2, cs] = hd1_scr[c, src, :].astype(BF16)

    def w_tile(t):
        return w_ref[:, t * PROJ_TN:(t + 1) * PROJ_TN]

    def main_tile(t, segments):
        acc = jnp.dot(h0_scr[...], w_tile(t), preferred_element_type=F32)
        if any(kind == "rot" for _, _, kind, _ in segments):
            cos, sin = cos_ref[...], sin_ref[...]
            row = lax.broadcasted_iota(jnp.int32, cos.shape, 0)
            pos = (((pl.program_id(0) % tps) * tm + row) % RET_C + 1).astype(F32)
        base = t * PROJ_TN
        for c0, c1, kind, arg in segments:
            seg = acc[:, c0:c1]
            if kind == "rot":
                head, is_key = arg
                rate = -RET_LOG_GAMMA[head] if is_key else RET_LOG_GAMMA[head]
                dec = jnp.exp(rate * pos) * ((RET_DK ** -0.5) if is_key else 1.0)
                cd, sd = cos * dec, sin * dec
                cm = (c0 + c1) // 2
                t1, t2 = acc[:, c0:cm], acc[:, cm:c1]
                main_ref[:, base + c0:base + cm] = (t1 * cd - t2 * sd).astype(BF16)
                main_ref[:, base + cm:base + c1] = (t1 * sd + t2 * cd).astype(BF16)
            elif kind == "silu":
                main_ref[:, base + c0:base + c1] = _silu(seg).astype(BF16)
            elif kind == "scale":
                main_ref[:, base + c0:base + c1] = (seg * arg).astype(BF16)
            else:
                main_ref[:, base + c0:base + c1] = seg.astype(BF16)

    for t, segments in enumerate(PROJ_MAIN_LAYOUT):
        main_tile(t, segments)

    for t, (dil, h_scr, a_ref) in enumerate(perms):
        n = tm // dil
        acc = jnp.dot(h_scr[...], w_tile(PROJ_MAIN_TILES + t), preferred_element_type=F32)
        gw = ATT_GROUP_W
        for r in range(dil):
            rows = slice(r * n, (r + 1) * n)
            a_ref[0, r, :, :gw] = (acc[rows, :gw] * ATT_QSCALE).astype(BF16)
            a_ref[0, r, :, gw:] = acc[rows, gw:].astype(BF16)


def _proj(x2, norm_g, mod3, cos, sin, w_att, batch, seq):
    tm, tn = PROJ_TM, PROJ_TN
    tps = seq // tm
    assert seq % tm == 0 and tm % RET_C == 0
    d1, d2 = ATT_GROUPS[1][1], ATT_GROUPS[2][1]
    assert w_att.shape[1] == (PROJ_MAIN_TILES + 2) * tn
    sub = lambda d: pl.BlockSpec((1, d, tm // d, tn), lambda i: (i // tps, 0, i % tps, 0))
    return pl.pallas_call(
        functools.partial(_proj_kernel, tps=tps),
        out_shape=[jax.ShapeDtypeStruct((batch * seq, PROJ_MAIN_TILES * tn), BF16),
                   jax.ShapeDtypeStruct((batch, d1, seq // d1, tn), BF16),
                   jax.ShapeDtypeStruct((batch, d2, seq // d2, tn), BF16)],
        grid=(batch * seq // tm,),
        in_specs=[pl.BlockSpec((tm, D_MODEL), lambda i: (i, 0)),
                  _const_spec((1, D_MODEL)),
                  pl.BlockSpec((1, 1, 6 * D_MODEL), lambda i: (i // tps, 0, 0)),
                  pl.BlockSpec((tm, RET_DK // 2), lambda i: (i % tps, 0)),
                  pl.BlockSpec((tm, RET_DK // 2), lambda i: (i % tps, 0)),
                  _const_spec(w_att.shape)],
        out_specs=[pl.BlockSpec((tm, PROJ_MAIN_TILES * tn), lambda i: (i, 0)), sub(d1), sub(d2)],
        scratch_shapes=[pltpu.VMEM((D_MODEL // LANES, tm, LANES), F32)] * 2
                       + [pltpu.VMEM((tm, D_MODEL), BF16)] * 3,
        compiler_params=_params(("arbitrary",)),
        name="proj",
    )(x2, norm_g, mod3, cos, sin, w_att)


def _ret_kernel(q_ref, k_ref, v_ref, gate_ref, cdec_ref, gng_ref, gnb_ref, o_ref, state, st_in):
    C = RET_C
    nc = q_ref.shape[1] // C
    cdec = cdec_ref[0]
    gng = gng_ref[0]
    gnb = gnb_ref[0]
    chunk = lambda ci: pl.ds(pl.multiple_of(ci * C, C), C)

    state[...] = jnp.zeros_like(state)
    st_in[0] = jnp.zeros(st_in.shape[1:], BF16)

    def advance(ci, carry):
        rows = chunk(ci)
        kv = lax.dot_general(k_ref[0, rows, :], v_ref[0, rows, :], (((0,), (0,)), ((), ())),
                             preferred_element_type=F32)
        new = (state[...] + kv) * cdec
        state[...] = new
        st_in[ci + 1] = new.astype(BF16)
        return carry

    lax.fori_loop(0, nc - 1, advance, 0, unroll=True)

    qi = lax.broadcasted_iota(jnp.int32, (C, C), 0)
    kj = lax.broadcasted_iota(jnp.int32, (C, C), 1)
    causal = qi >= kj

    def emit(ci, carry):
        rows = chunk(ci)
        q = q_ref[0, rows, :]
        s = lax.dot_general(q, k_ref[0, rows, :], (((1,), (1,)), ((), ())), preferred_element_type=F32)
        s = jnp.where(causal, s, 0.0).astype(BF16)
        o = (jnp.dot(s, v_ref[0, rows, :], preferred_element_type=F32)
             + jnp.dot(q, st_in[ci], preferred_element_type=F32))
        mu = jnp.mean(o, axis=-1, keepdims=True)
        oc = o - mu
        var = jnp.mean(oc * oc, axis=-1, keepdims=True)
        on = oc * lax.rsqrt(var + GN_EPS) * gng + gnb
        o_ref[0, rows, :] = (gate_ref[0, rows, :].astype(F32) * on).astype(BF16)
        return carry

    lax.fori_loop(0, nc, emit, 0, unroll=2)


def _retention(main3, gn_g, gn_b):
    B, S, _ = main3.shape
    H, C = RET_HEADS, RET_C
    assert S % C == 0
    cdec = jnp.exp(jnp.asarray(RET_LOG_GAMMA, F32) * C)[:, None, None]
    kq = RET_QK_W // RET_DK
    kv = 2 * RET_QK_W // RET_DV
    kg = kv + RET_V_W // RET_DV
    return pl.pallas_call(
        _ret_kernel,
        out_shape=jax.ShapeDtypeStruct((B, S, RET_V_W), BF16),
        grid=(B, H),
        in_specs=[pl.BlockSpec((1, S, RET_DK), lambda b, h: (b, 0, h)),
                  pl.BlockSpec((1, S, RET_DK), lambda b, h: (b, 0, kq + h)),
                  pl.BlockSpec((1, S, RET_DV), lambda b, h: (b, 0, kv + h)),
                  pl.BlockSpec((1, S, RET_DV), lambda b, h: (b, 0, kg + h)),
                  pl.BlockSpec((1, 1, 1), lambda b, h: (h, 0, 0)),
                  pl.BlockSpec((1, 1, RET_DV), lambda b, h: (h, 0, 0)),
                  pl.BlockSpec((1, 1, RET_DV), lambda b, h: (h, 0, 0))],
        out_specs=pl.BlockSpec((1, S, RET_DV), lambda b, h: (b, 0, h)),
        scratch_shapes=[pltpu.VMEM((RET_DK, RET_DV), F32), pltpu.VMEM((S // C, RET_DK, RET_DV), BF16)],
        compiler_params=_params(("arbitrary", "arbitrary")),
        name="ret",
    )(main3, main3, main3, main3, cdec, gn_g.reshape(H, 1, RET_DV), gn_b.reshape(H, 1, RET_DV))


def _t5_bucket(dist):
    max_exact = REL_BUCKETS // 2
    d_f = jnp.maximum(dist, 1).astype(F32)
    large = max_exact + (jnp.log(d_f / max_exact) / math.log(REL_MAX_DIST / max_exact)
                         * (REL_BUCKETS - max_exact)).astype(jnp.int32)
    large = jnp.minimum(large, REL_BUCKETS - 1)
    return jnp.where(dist < max_exact, dist, large)


def _attn_unit(n, q_ref, k_ref, v_ref, bias_scr, vaug_scr, o_ref, lse_ref):
    blk, dh = ATT_BLK, ATT_HEAD_DIM
    windowed = k_ref.shape[0] > blk
    if windowed:
        win = pl.ds(pl.multiple_of(jnp.maximum(n - 1, 0) * blk, blk), 2 * blk)
        first = (n == 0).astype(jnp.int32)
    lane = lax.broadcasted_iota(jnp.int32, (blk, LSE_W), 1)
    lse_all = jnp.zeros((blk, LSE_W), F32)
    for h in range(ATT_HEADS_PER_GROUP):
        hs = slice(h * dh, (h + 1) * dh)
        if windowed:
            kb, vb, bias = k_ref[win, hs], v_ref[win, hs], bias_scr[first, h]
        else:
            kb, vb, bias = k_ref[:, hs], v_ref[:, hs], bias_scr[h]
        vaug_scr[h, :, :dh] = vb
        s = lax.dot_general(q_ref[:, hs], kb, (((1,), (1,)), ((), ())), preferred_element_type=F32) + bias
        mx = jnp.max(s, axis=-1, keepdims=True)
        e = jnp.exp(s - mx).astype(BF16)
        o_den = jnp.dot(e, vaug_scr[h], preferred_element_type=F32)
        den = o_den[:, dh:]
        o_ref[:, hs] = (o_den[:, :dh] * (1.0 / den)).astype(BF16)
        lse_all = jnp.where(lane == h, mx + jnp.log(den), lse_all)
    lse_ref[...] = lse_all


def _attn_kernel(tab_ref, q0, k0, v0, q1, k1, v1, q2, k2, v2, bk0, bk1, bk2,
                 o0, l0, o1, l1, o2, l2, bias0, bias1, bias2, vaug0, vaug1, vaug2, *, nb1):
    b = pl.program_id(0)
    i = pl.program_id(1)
    blk = ATT_BLK

    @pl.when((b == 0) & (i == 0))
    def _():
        for vaug in (vaug0, vaug1, vaug2):
            vaug[...] = jnp.ones(vaug.shape, BF16)
        qi = lax.broadcasted_iota(jnp.int32, (blk, 2 * blk), 0)
        kj = lax.broadcasted_iota(jnp.int32, (blk, 2 * blk), 1)
        m = blk + qi - kj
        band = (m >= 0) & (m <= blk)
        masked = jnp.full((blk, blk), NEG, F32)
        for gi, (bk_ref, bias_scr) in enumerate(((bk0, bias0), (bk1, bias1), (bk2, bias2))):
            bucket = bk_ref[...]
            for h in range(ATT_HEADS_PER_GROUP):
                col = gi * ATT_HEADS_PER_GROUP + h
                acc = jnp.zeros((blk, 2 * blk), F32)
                for t in range(REL_BUCKETS):
                    acc = jnp.where(bucket == t, tab_ref[t, col], acc)
                tile = jnp.where(band, acc, NEG)
                if len(bias_scr.shape) == 4:
                    bias_scr[0, h] = tile
                    bias_scr[1, h] = jnp.concatenate([tile[:, blk:], masked], axis=1)
                else:
                    bias_scr[h] = tile[:, blk:]

    _attn_unit(i, q0, k0, v0, bias0, vaug0, o0, l0)
    _attn_unit(i % nb1, q1, k1, v1, bias1, vaug1, o1, l1)
    _attn_unit(0, q2, k2, v2, bias2, vaug2, o2, l2)


def _attention(main3, att1, att2, rel_bias):
    B, S, _ = main3.shape
    blk, gw = ATT_BLK, ATT_GROUP_W
    steps = S // blk
    lens = [S // dil for _, dil in ATT_GROUPS]
    nbs = [L // blk for L in lens]
    assert all(win // dil == blk for win, dil in ATT_GROUPS)
    assert nbs[0] == steps and nbs[1] >= 2 and nbs[2] == 1
    c0 = COL_ATT // gw
    nb1 = nbs[1]
    in_specs = [pl.BlockSpec(memory_space=pltpu.SMEM),
                pl.BlockSpec((None, blk, gw), lambda b, i: (b, i, c0)),
                pl.BlockSpec((None, S, gw), lambda b, i: (b, 0, c0 + 1)),
                pl.BlockSpec((None, S, gw), lambda b, i: (b, 0, c0 + 2)),
                pl.BlockSpec((None, None, blk, gw), lambda b, i: (b, i // nb1, i % nb1, 0)),
                pl.BlockSpec((None, None, lens[1], gw), lambda b, i: (b, i // nb1, 0, 1)),
                pl.BlockSpec((None, None, lens[1], gw), lambda b, i: (b, i // nb1, 0, 2)),
                pl.BlockSpec((None, None, blk, gw), lambda b, i: (b, i, 0, 0)),
                pl.BlockSpec((None, None, blk, gw), lambda b, i: (b, i, 0, 1)),
                pl.BlockSpec((None, None, blk, gw), lambda b, i: (b, i, 0, 2))]
    args = [rel_bias] + [main3] * 3 + [att1] * 3 + [att2] * 3
    qi = jnp.arange(blk)[:, None]
    kj = jnp.arange(2 * blk)[None, :]
    m = blk + qi - kj
    for _, dil in ATT_GROUPS:
        in_specs.append(_const_spec((blk, 2 * blk)))
        args.append(_t5_bucket(jnp.clip(m, 0, blk) * dil).astype(jnp.int32))
    out_specs, out_shapes = [], []
    for (_, dil), nb in zip(ATT_GROUPS, nbs):
        for w, dt in ((gw, BF16), (LSE_W, F32)):
            if dil == 1:
                out_specs.append(pl.BlockSpec((None, blk, w), lambda b, i: (b, i, 0)))
                out_shapes.append(jax.ShapeDtypeStruct((B, S, w), dt))
            else:
                out_specs.append(pl.BlockSpec((None, None, blk, w),
                                              lambda b, i, nb=nb: (b, i // nb, i % nb, 0)))
                out_shapes.append(jax.ShapeDtypeStruct((B, dil, S // dil, w), dt))
    heads = ATT_HEADS_PER_GROUP
    outs = pl.pallas_call(
        functools.partial(_attn_kernel, nb1=nb1),
        out_shape=out_shapes,
        grid=(B, steps),
        in_specs=in_specs,
        out_specs=out_specs,
        scratch_shapes=[pltpu.VMEM((2, heads, blk, 2 * blk), F32), pltpu.VMEM((2, heads, blk, 2 * blk), F32),
                        pltpu.VMEM((heads, blk, blk), F32),
                        pltpu.VMEM((heads, 2 * blk, 2 * ATT_HEAD_DIM), BF16),
                        pltpu.VMEM((heads, 2 * blk, 2 * ATT_HEAD_DIM), BF16),
                        pltpu.VMEM((heads, blk, 2 * ATT_HEAD_DIM), BF16)],
        compiler_params=_params(("arbitrary", "arbitrary")),
        name="attn",
    )(*args)
    return outs[0::2], outs[1::2]


def _merge_kernel(x_ref, g_ref, mod_ref, retg_ref, o0_ref, o1_ref, o2_ref, l0_ref, l1_ref, l2_ref,
                  wg_ref, wr_ref, wa_ref, wo_ref, out_ref, o1_scr, o2_scr, l1_scr, l2_scr):
    tm = x_ref.shape[0]
    for dil, o_ref, l_ref, o_scr, l_scr in ((ATT_GROUPS[1][1], o1_ref, l1_ref, o1_scr, l1_scr),
                                            (ATT_GROUPS[2][1], o2_ref, l2_ref, o2_scr, l2_scr)):
        n = tm // dil
        for r in range(dil):
            l_scr[pl.ds(r, n, stride=dil), :] = l_ref[0, r]
            for h in range(ATT_HEADS_PER_GROUP):
                hs = slice(h * ATT_HEAD_DIM, (h + 1) * ATT_HEAD_DIM)
                o_scr[h, pl.ds(r, n, stride=dil), :] = o_ref[0, r, :, hs].astype(F32)
    l0, l1, l2 = l0_ref[...], l1_scr[...], l2_scr[...]
    lm = jnp.maximum(jnp.maximum(l0, l1), l2)
    e0, e1, e2 = jnp.exp(l0 - lm), jnp.exp(l1 - lm), jnp.exp(l2 - lm)
    tot = e0 + e1 + e2
    w0, w1, w2 = e0 / tot, e1 / tot, e2 / tot
    parts = []
    for h in range(ATT_HEADS_PER_GROUP):
        hs = slice(h * ATT_HEAD_DIM, (h + 1) * ATT_HEAD_DIM)
        parts.append(w0[:, h:h + 1] * o0_ref[:, hs].astype(F32)
                     + w1[:, h:h + 1] * o1_scr[h]
                     + w2[:, h:h + 1] * o2_scr[h])
    att = jnp.concatenate(parts, axis=-1).astype(BF16)
    att_out = jnp.dot(att, wa_ref[...], preferred_element_type=F32)
    ret_out = jnp.dot(retg_ref[...], wr_ref[...], preferred_element_type=F32)
    x = x_ref[...]
    h = _modulated_norm(x, g_ref[...], mod_ref, 0).astype(BF16)
    gates = jax.nn.sigmoid(jnp.dot(h, wg_ref[...], preferred_element_type=F32))
    merged = gates[:, :D_MODEL] * ret_out + gates[:, D_MODEL:] * att_out
    y = jnp.dot(merged.astype(BF16), wo_ref[...], preferred_element_type=F32)
    out_ref[...] = x + mod_ref[0, :, 2 * D_MODEL:3 * D_MODEL] * y


def _merge(x2, norm_g, mod3, retg2, o, lse, w_gate, w_ret_out, w_att_out, w_o, batch, seq):
    tm = ROW_TM
    tps = seq // tm
    row = lambda w: pl.BlockSpec((tm, w), lambda i: (i, 0))
    sub = lambda d, w: pl.BlockSpec((1, d, tm // d, w), lambda i: (i // tps, 0, i % tps, 0))
    d1, d2 = ATT_GROUPS[1][1], ATT_GROUPS[2][1]
    return pl.pallas_call(
        _merge_kernel,
        out_shape=jax.ShapeDtypeStruct((batch * seq, D_MODEL), F32),
        grid=(batch * seq // tm,),
        in_specs=[row(D_MODEL),
                  _const_spec((1, D_MODEL)),
                  pl.BlockSpec((1, 1, 6 * D_MODEL), lambda i: (i // tps, 0, 0)),
                  row(RET_V_W),
                  row(ATT_GROUP_W), sub(d1, ATT_GROUP_W), sub(d2, ATT_GROUP_W),
                  row(LSE_W), sub(d1, LSE_W), sub(d2, LSE_W),
                  _const_spec((D_MODEL, 2 * D_MODEL)),
                  _const_spec((RET_V_W, D_MODEL)),
                  _const_spec((ATT_GROUP_W, D_MODEL)),
                  _const_spec((D_MODEL, D_MODEL))],
        out_specs=row(D_MODEL),
        scratch_shapes=[pltpu.VMEM((ATT_HEADS_PER_GROUP, tm, ATT_HEAD_DIM), F32)] * 2
                       + [pltpu.VMEM((tm, LSE_W), F32)] * 2,
        compiler_params=_params(("arbitrary",)),
        name="merge",
    )(x2, norm_g, mod3, retg2, *o, *lse, w_gate, w_ret_out, w_att_out, w_o)


def _mlp_kernel(x_ref, mod_ref, g2_ref, gf_ref, w1_ref, w2_ref, out_ref, *, final):
    x = x_ref[...]
    h = _modulated_norm(x, g2_ref[...], mod_ref, 3).astype(BF16)
    u = jnp.maximum(jnp.dot(h, w1_ref[...], preferred_element_type=F32), 0.0)
    y = jnp.dot((u * u).astype(BF16), w2_ref[...], preferred_element_type=F32)
    x = x + mod_ref[0, :, 5 * D_MODEL:6 * D_MODEL] * y
    out_ref[...] = _rms(x, gf_ref[...]) if final else x


def _mlp(x2, mod3, norm2_g, norm_f_g, w1, w2, seq, final):
    rows = x2.shape[0]
    tm = ROW_TM
    tps = seq // tm
    return pl.pallas_call(
        functools.partial(_mlp_kernel, final=final),
        out_shape=jax.ShapeDtypeStruct((rows, D_MODEL), F32),
        grid=(rows // tm,),
        in_specs=[pl.BlockSpec((tm, D_MODEL), lambda i: (i, 0)),
                  pl.BlockSpec((1, 1, 6 * D_MODEL), lambda i: (i // tps, 0, 0)),
                  _const_spec((1, D_MODEL)),
                  _const_spec((1, D_MODEL)),
                  _const_spec((D_MODEL, D_FF)),
                  _const_spec((D_FF, D_MODEL))],
        out_specs=pl.BlockSpec((tm, D_MODEL), lambda i: (i, 0)),
        compiler_params=_params(("arbitrary",)),
        name="mlp",
    )(x2, mod3, norm2_g, norm_f_g, w1, w2)


def kernel(x, c, w_ada, b_ada, norm1_g, w_in, rel_bias, ret_gn_g, ret_gn_b, w_ret_out, w_att_out,
           w_o, norm2_g, w_ff1, w_ff2, norm_f_g):
    B, S, D = x.shape
    depth = w_ada.shape[0]
    half = RET_DK // 2
    inv = ROPE_BASE ** (-jnp.arange(half, dtype=F32) / half)
    ang = jnp.arange(S).astype(F32)[:, None] * inv[None, :]
    cos, sin = jnp.cos(ang), jnp.sin(ang)
    x2 = x.reshape(B * S, D)
    for l in range(depth):
        g1 = norm1_g[l].reshape(1, D)
        mod3 = _mod(c, w_ada[l], b_ada[l]).reshape(B, 1, 6 * D)
        main2, att1, att2 = _proj(x2, g1, mod3, cos, sin, w_in[l, :, :COL_GATE].astype(BF16), B, S)
        main3 = main2.reshape(B, S, main2.shape[1])
        retg = _retention(main3, ret_gn_g[l], ret_gn_b[l])
        o, lse = _attention(main3, att1, att2, rel_bias)
        o = [o[0].reshape(B * S, ATT_GROUP_W), o[1], o[2]]
        lse = [lse[0].reshape(B * S, LSE_W), lse[1], lse[2]]
        x2 = _merge(x2, g1, mod3, retg.reshape(B * S, RET_V_W), o, lse,
                    w_in[l, :, COL_GATE:].astype(BF16), w_ret_out[l].astype(BF16),
                    w_att_out[l].astype(BF16), w_o[l].astype(BF16), B, S)
        x2 = _mlp(x2, mod3, norm2_g[l].reshape(1, D), norm_f_g.reshape(1, D),
                  w_ff1[l].astype(BF16), w_ff2[l].astype(BF16), S, final=l == depth - 1)
    return x2.reshape(B, S, D)
```

```python
import functools
import math

import jax
import jax.numpy as jnp
from jax import lax
from jax.experimental import pallas as pl
from jax.experimental.pallas import tpu as pltpu

F32 = jnp.float32
BF16 = jnp.bfloat16

D_MODEL = 1024
RET_HEADS = 4
RET_DK = 256
RET_DV = 512
RET_C = 256
RET_LOG_GAMMA = tuple(math.log1p(-(2.0 ** (-5.0 - h))) for h in range(RET_HEADS))
RET_QK_W = RET_HEADS * RET_DK
RET_V_W = RET_HEADS * RET_DV
ATT_GROUPS = ((128, 1), (512, 4), (2048, 16))
ATT_HEADS_PER_GROUP = 4
ATT_HEAD_DIM = 128
ATT_GROUP_W = ATT_HEADS_PER_GROUP * ATT_HEAD_DIM
ATT_BLK = 128
ATT_BLOCKS_PER_STEP = 4
LANES = 128
LSE_W = LANES
REL_BUCKETS = 32
REL_MAX_DIST = 2048
D_FF = 4 * D_MODEL
RMS_EPS = 1e-6
GN_EPS = 1e-5
ROPE_BASE = 10000.0
NEG = -1e30
ATT_QSCALE = ATT_HEAD_DIM ** -0.5

COL_ATT = 2 * RET_QK_W + 2 * RET_V_W
COL_GATE = COL_ATT + 9 * ATT_GROUP_W

VMEM_LIMIT = 56 * 1024 * 1024

PROJ_TM = 256
PROJ_TN = 3 * ATT_GROUP_W
PROJ_MAIN_TILES = (COL_ATT + PROJ_TN) // PROJ_TN
ROW_TM = 512


def _params(sem):
    return pltpu.CompilerParams(dimension_semantics=sem, vmem_limit_bytes=VMEM_LIMIT)


def _const_spec(shape):
    zeros = (0,) * len(shape)
    return pl.BlockSpec(shape, lambda *_: zeros, pipeline_mode=pl.Buffered(1))


def _silu(t):
    return t * jax.nn.sigmoid(t)


def _rms(x, g):
    return x * lax.rsqrt(jnp.mean(x * x, axis=-1, keepdims=True) + RMS_EPS) * g


def _modulated_norm(x, g, mod_ref, k):
    shift = mod_ref[0, :, k * D_MODEL:(k + 1) * D_MODEL]
    scale = mod_ref[0, :, (k + 1) * D_MODEL:(k + 2) * D_MODEL]
    return _rms(x, g) * (1.0 + scale) + shift


def _mod_kernel(c_ref, w_ref, b_ref, o_ref):
    o_ref[...] = jnp.dot(_silu(c_ref[...]), w_ref[...], preferred_element_type=F32) + b_ref[...]


def _mod(c, w_ada, b_ada):
    B = c.shape[0]
    n = w_ada.shape[1]
    tn = D_MODEL
    return pl.pallas_call(
        _mod_kernel,
        out_shape=jax.ShapeDtypeStruct((B, n), F32),
        grid=(n // tn,),
        in_specs=[pl.BlockSpec((B, D_MODEL), lambda j: (0, 0)),
                  pl.BlockSpec((D_MODEL, tn), lambda j: (0, j)),
                  pl.BlockSpec((1, tn), lambda j: (0, j))],
        out_specs=pl.BlockSpec((B, tn), lambda j: (0, j)),
        compiler_params=_params(("arbitrary",)),
        name="mod",
    )(c, w_ada, b_ada.reshape(1, n))


def _proj_main_layout():
    spans = [(h * RET_DK, (h + 1) * RET_DK, "rot", (h % RET_HEADS, h >= RET_HEADS))
             for h in range(2 * RET_HEADS)]
    spans += [(2 * RET_QK_W, 2 * RET_QK_W + RET_V_W, "copy", None),
              (2 * RET_QK_W + RET_V_W, COL_ATT, "silu", None),
              (COL_ATT, COL_ATT + ATT_GROUP_W, "scale", ATT_QSCALE),
              (COL_ATT + ATT_GROUP_W, PROJ_MAIN_TILES * PROJ_TN, "copy", None)]
    tiles = [[] for _ in range(PROJ_MAIN_TILES)]
    for c0, c1, kind, arg in spans:
        for t in range(c0 // PROJ_TN, (c1 - 1) // PROJ_TN + 1):
            lo, hi = max(c0, t * PROJ_TN), min(c1, (t + 1) * PROJ_TN)
            assert kind != "rot" or (lo, hi) == (c0, c1)
            tiles[t].append((lo - t * PROJ_TN, hi - t * PROJ_TN, kind, arg))
    return tiles


PROJ_MAIN_LAYOUT = _proj_main_layout()


def _proj_kernel(x_ref, g_ref, mod_ref, cos_ref, sin_ref, w_ref, main_ref, a1_ref, a2_ref,
                 hf_scr, hd1_scr, h0_scr, h1_scr, h2_scr, *, tps):
    tm = x_ref.shape[0]
    perms = ((ATT_GROUPS[1][1], h1_scr, a1_ref), (ATT_GROUPS[2][1], h2_scr, a2_ref))

    hf = _modulated_norm(x_ref[...], g_ref[...], mod_ref, 0)
    h0_scr[...] = hf.astype(BF16)
    d1, d2 = ATT_GROUPS[1][1], ATT_GROUPS[2][1]
    n1, n2 = tm // d1, tm // d2
    for c in range(D_MODEL // LANES):
        cs = slice(c * LANES, (c + 1) * LANES)
        hf_scr[c] = hf[:, cs]
        for r in range(d1):
            part = hf_scr[c, pl.ds(r, n1, stride=d1), :]
            hd1_scr[c, r * n1:(r + 1) * n1, :] = part
            h1_scr[r * n1:(r + 1) * n1, cs] = part.astype(BF16)
        for r in range(d2):
            src = pl.ds((r % d1) * n1 + r // d1, n2, stride=d1)
            h2_scr[r * n2:(r + 1) * n2, cs] = hd1_scr[c, src, :].astype(BF16)

    def w_tile(t):
        return w_ref[:, t * PROJ_TN:(t + 1) * PROJ_TN]

    def main_tile(t, segments):
        acc = jnp.dot(h0_scr[...], w_tile(t), preferred_element_type=F32)
        if any(kind == "rot" for _, _, kind, _ in segments):
            cos, sin = cos_ref[...], sin_ref[...]
            row = lax.broadcasted_iota(jnp.int32, cos.shape, 0)
            pos = (((pl.program_id(0) % tps) * tm + row) % RET_C + 1).astype(F32)
        base = t * PROJ_TN
        for c0, c1, kind, arg in segments:
            seg = acc[:, c0:c1]
            if kind == "rot":
                head, is_key = arg
                rate = -RET_LOG_GAMMA[head] if is_key else RET_LOG_GAMMA[head]
                dec = jnp.exp(rate * pos) * ((RET_DK ** -0.5) if is_key else 1.0)
                cd, sd = cos * dec, sin * dec
                cm = (c0 + c1) // 2
                t1, t2 = acc[:, c0:cm], acc[:, cm:c1]
                main_ref[:, base + c0:base + cm] = (t1 * cd - t2 * sd).astype(BF16)
                main_ref[:, base + cm:base + c1] = (t1 * sd + t2 * cd).astype(BF16)
            elif kind == "silu":
                main_ref[:, base + c0:base + c1] = _silu(seg).astype(BF16)
            elif kind == "scale":
                main_ref[:, base + c0:base + c1] = (seg * arg).astype(BF16)
            else:
                main_ref[:, base + c0:base + c1] = seg.astype(BF16)

    for t, segments in enumerate(PROJ_MAIN_LAYOUT):
        main_tile(t, segments)

    for t, (dil, h_scr, a_ref) in enumerate(perms):
        n = tm // dil
        acc = jnp.dot(h_scr[...], w_tile(PROJ_MAIN_TILES + t), preferred_element_type=F32)
        gw = ATT_GROUP_W
        for r in range(dil):
            rows = slice(r * n, (r + 1) * n)
            a_ref[0, r, :, :gw] = (acc[rows, :gw] * ATT_QSCALE).astype(BF16)
            a_ref[0, r, :, gw:] = acc[rows, gw:].astype(BF16)


def _proj(x2, norm_g, mod3, cos, sin, w_att, batch, seq):
    tm, tn = PROJ_TM, PROJ_TN
    tps = seq // tm
    assert seq % tm == 0 and tm % RET_C == 0
    d1, d2 = ATT_GROUPS[1][1], ATT_GROUPS[2][1]
    assert w_att.shape[1] == (PROJ_MAIN_TILES + 2) * tn
    sub = lambda d: pl.BlockSpec((1, d, tm // d, tn), lambda i: (i // tps, 0, i % tps, 0))
    return pl.pallas_call(
        functools.partial(_proj_kernel, tps=tps),
        out_shape=[jax.ShapeDtypeStruct((batch * seq, PROJ_MAIN_TILES * tn), BF16),
                   jax.ShapeDtypeStruct((batch, d1, seq // d1, tn), BF16),
                   jax.ShapeDtypeStruct((batch, d2, seq // d2, tn), BF16)],
        grid=(batch * seq // tm,),
        in_specs=[pl.BlockSpec((tm, D_MODEL), lambda i: (i, 0)),
                  _const_spec((1, D_MODEL)),
                  pl.BlockSpec((1, 1, 6 * D_MODEL), lambda i: (i // tps, 0, 0)),
                  pl.BlockSpec((tm, RET_DK // 2), lambda i: (i % tps, 0)),
                  pl.BlockSpec((tm, RET_DK // 2), lambda i: (i % tps, 0)),
                  _const_spec(w_att.shape)],
        out_specs=[pl.BlockSpec((tm, PROJ_MAIN_TILES * tn), lambda i: (i, 0)), sub(d1), sub(d2)],
        scratch_shapes=[pltpu.VMEM((D_MODEL // LANES, tm, LANES), F32)] * 2
                       + [pltpu.VMEM((tm, D_MODEL), BF16)] * 3,
        compiler_params=_params(("arbitrary",)),
        name="proj",
    )(x2, norm_g, mod3, cos, sin, w_att)


def _ret_kernel(q_ref, k_ref, v_ref, gate_ref, cdec_ref, gng_ref, gnb_ref, o_ref, state, st_in):
    C = RET_C
    nc = q_ref.shape[1] // C
    cdec = cdec_ref[0]
    gng = gng_ref[0]
    gnb = gnb_ref[0]
    chunk = lambda ci: pl.ds(pl.multiple_of(ci * C, C), C)

    state[...] = jnp.zeros_like(state)
    st_in[0] = jnp.zeros(st_in.shape[1:], BF16)

    def advance(ci, carry):
        rows = chunk(ci)
        kv = lax.dot_general(k_ref[0, rows, :], v_ref[0, rows, :], (((0,), (0,)), ((), ())),
                             preferred_element_type=F32)
        new = (state[...] + kv) * cdec
        state[...] = new
        st_in[ci + 1] = new.astype(BF16)
        return carry

    lax.fori_loop(0, nc - 1, advance, 0, unroll=True)

    qi = lax.broadcasted_iota(jnp.int32, (C, C), 0)
    kj = lax.broadcasted_iota(jnp.int32, (C, C), 1)
    causal = qi >= kj

    def emit(ci, carry):
        rows = chunk(ci)
        q = q_ref[0, rows, :]
        s = lax.dot_general(q, k_ref[0, rows, :], (((1,), (1,)), ((), ())), preferred_element_type=F32)
        s = jnp.where(causal, s, 0.0).astype(BF16)
        o = (jnp.dot(s, v_ref[0, rows, :], preferred_element_type=F32)
             + jnp.dot(q, st_in[ci], preferred_element_type=F32))
        mu = jnp.mean(o, axis=-1, keepdims=True)
        oc = o - mu
        var = jnp.mean(oc * oc, axis=-1, keepdims=True)
        on = oc * lax.rsqrt(var + GN_EPS) * gng + gnb
        o_ref[0, rows, :] = (gate_ref[0, rows, :].astype(F32) * on).astype(BF16)
        return carry

    lax.fori_loop(0, nc, emit, 0, unroll=2)


def _retention(main3, gn_g, gn_b):
    B, S, _ = main3.shape
    H, C = RET_HEADS, RET_C
    assert S % C == 0
    cdec = jnp.exp(jnp.asarray(RET_LOG_GAMMA, F32) * C)[:, None, None]
    kq = RET_QK_W // RET_DK
    kv = 2 * RET_QK_W // RET_DV
    kg = kv + RET_V_W // RET_DV
    return pl.pallas_call(
        _ret_kernel,
        out_shape=jax.ShapeDtypeStruct((B, S, RET_V_W), BF16),
        grid=(B, H),
        in_specs=[pl.BlockSpec((1, S, RET_DK), lambda b, h: (b, 0, h)),
                  pl.BlockSpec((1, S, RET_DK), lambda b, h: (b, 0, kq + h)),
                  pl.BlockSpec((1, S, RET_DV), lambda b, h: (b, 0, kv + h)),
                  pl.BlockSpec((1, S, RET_DV), lambda b, h: (b, 0, kg + h)),
                  pl.BlockSpec((1, 1, 1), lambda b, h: (h, 0, 0)),
                  pl.BlockSpec((1, 1, RET_DV), lambda b, h: (h, 0, 0)),
                  pl.BlockSpec((1, 1, RET_DV), lambda b, h: (h, 0, 0))],
        out_specs=pl.BlockSpec((1, S, RET_DV), lambda b, h: (b, 0, h)),
        scratch_shapes=[pltpu.VMEM((RET_DK, RET_DV), F32), pltpu.VMEM((S // C, RET_DK, RET_DV), BF16)],
        compiler_params=_params(("arbitrary", "arbitrary")),
        name="ret",
    )(main3, main3, main3, main3, cdec, gn_g.reshape(H, 1, RET_DV), gn_b.reshape(H, 1, RET_DV))


def _t5_bucket(dist):
    max_exact = REL_BUCKETS // 2
    d_f = jnp.maximum(dist, 1).astype(F32)
    large = max_exact + (jnp.log(d_f / max_exact) / math.log(REL_MAX_DIST / max_exact)
                         * (REL_BUCKETS - max_exact)).astype(jnp.int32)
    large = jnp.minimum(large, REL_BUCKETS - 1)
    return jnp.where(dist < max_exact, dist, large)


def _attn_unit(n, q_ref, k_ref, v_ref, bias_scr, vaug_scr, o_ref, lse_ref):
    blk, dh = ATT_BLK, ATT_HEAD_DIM
    windowed = k_ref.shape[0] > blk
    if windowed:
        win = pl.ds(pl.multiple_of(jnp.maximum(n - 1, 0) * blk, blk), 2 * blk)
        first = (n == 0).astype(jnp.int32)
    lane = lax.broadcasted_iota(jnp.int32, (blk, LSE_W), 1)
    lse_all = jnp.zeros((blk, LSE_W), F32)
    for h in range(ATT_HEADS_PER_GROUP):
        hs = slice(h * dh, (h + 1) * dh)
        if windowed:
            kb, vb, bias = k_ref[win, hs], v_ref[win, hs], bias_scr[first, h]
        else:
            kb, vb, bias = k_ref[:, hs], v_ref[:, hs], bias_scr[h]
        vaug_scr[h, :, :dh] = vb
        s = lax.dot_general(q_ref[:, hs], kb, (((1,), (1,)), ((), ())), preferred_element_type=F32) + bias
        mx = jnp.max(s, axis=-1, keepdims=True)
        e = jnp.exp(s - mx).astype(BF16)
        o_den = jnp.dot(e, vaug_scr[h], preferred_element_type=F32)
        den = o_den[:, dh:]
        o_ref[:, hs] = (o_den[:, :dh] * (1.0 / den)).astype(BF16)
        lse_all = jnp.where(lane == h, mx + jnp.log(den), lse_all)
    lse_ref[...] = lse_all


def _attn_kernel(tab_ref, q0, k0, v0, q1, k1, v1, q2, k2, v2, bk0, bk1, bk2,
                 o0, l0, o1, l1, o2, l2, bias0, bias1, bias2, vaug0, vaug1, vaug2, *, nb1, per_step):
    b = pl.program_id(0)
    i = pl.program_id(1)
    blk = ATT_BLK

    @pl.when((b == 0) & (i == 0))
    def _():
        for vaug in (vaug0, vaug1, vaug2):
            vaug[...] = jnp.ones(vaug.shape, BF16)
        qi = lax.broadcasted_iota(jnp.int32, (blk, 2 * blk), 0)
        kj = lax.broadcasted_iota(jnp.int32, (blk, 2 * blk), 1)
        m = blk + qi - kj
        band = (m >= 0) & (m <= blk)
        masked = jnp.full((blk, blk), NEG, F32)
        for gi, (bk_ref, bias_scr) in enumerate(((bk0, bias0), (bk1, bias1), (bk2, bias2))):
            bucket = bk_ref[...]
            for h in range(ATT_HEADS_PER_GROUP):
                col = gi * ATT_HEADS_PER_GROUP + h
                acc = jnp.zeros((blk, 2 * blk), F32)
                for t in range(REL_BUCKETS):
                    acc = jnp.where(bucket == t, tab_ref[t, col], acc)
                tile = jnp.where(band, acc, NEG)
                if len(bias_scr.shape) == 4:
                    bias_scr[0, h] = tile
                    bias_scr[1, h] = jnp.concatenate([tile[:, blk:], masked], axis=1)
                else:
                    bias_scr[h] = tile[:, blk:]

    for u in range(per_step):
        rows = pl.ds(u * blk, blk)
        n = i * per_step + u
        _attn_unit(n, q0.at[rows], k0, v0, bias0, vaug0.at[u], o0.at[rows], l0.at[rows])
        _attn_unit(n % nb1, q1.at[rows], k1, v1, bias1, vaug1.at[u], o1.at[rows], l1.at[rows])
        _attn_unit(0, q2.at[u], k2.at[u], v2.at[u], bias2, vaug2.at[u], o2.at[u], l2.at[u])


def _attention(main3, att1, att2, rel_bias):
    B, S, _ = main3.shape
    blk, gw = ATT_BLK, ATT_GROUP_W
    steps = S // blk
    lens = [S // dil for _, dil in ATT_GROUPS]
    nbs = [L // blk for L in lens]
    assert all(win // dil == blk for win, dil in ATT_GROUPS)
    assert nbs[0] == steps and nbs[1] >= 2 and nbs[2] == 1
    c0 = COL_ATT // gw
    nb1 = nbs[1]
    G = ATT_BLOCKS_PER_STEP
    assert nb1 % G == 0 and steps % G == 0
    sub1 = lambda b, i: (b, (i * G) // nb1)
    in_specs = [pl.BlockSpec(memory_space=pltpu.SMEM),
                pl.BlockSpec((None, G * blk, gw), lambda b, i: (b, i, c0)),
                pl.BlockSpec((None, S, gw), lambda b, i: (b, 0, c0 + 1)),
                pl.BlockSpec((None, S, gw), lambda b, i: (b, 0, c0 + 2)),
                pl.BlockSpec((None, None, G * blk, gw), lambda b, i: (*sub1(b, i), (i * G) % nb1 // G, 0)),
                pl.BlockSpec((None, None, lens[1], gw), lambda b, i: (*sub1(b, i), 0, 1)),
                pl.BlockSpec((None, None, lens[1], gw), lambda b, i: (*sub1(b, i), 0, 2)),
                pl.BlockSpec((None, G, blk, gw), lambda b, i: (b, i, 0, 0)),
                pl.BlockSpec((None, G, blk, gw), lambda b, i: (b, i, 0, 1)),
                pl.BlockSpec((None, G, blk, gw), lambda b, i: (b, i, 0, 2))]
    args = [rel_bias] + [main3] * 3 + [att1] * 3 + [att2] * 3
    qi = jnp.arange(blk)[:, None]
    kj = jnp.arange(2 * blk)[None, :]
    m = blk + qi - kj
    for _, dil in ATT_GROUPS:
        in_specs.append(_const_spec((blk, 2 * blk)))
        args.append(_t5_bucket(jnp.clip(m, 0, blk) * dil).astype(jnp.int32))
    out_specs, out_shapes = [], []
    for (_, dil), nb in zip(ATT_GROUPS, nbs):
        for w, dt in ((gw, BF16), (LSE_W, F32)):
            if nb == steps:
                out_specs.append(pl.BlockSpec((None, G * blk, w), lambda b, i: (b, i, 0)))
                out_shapes.append(jax.ShapeDtypeStruct((B, S, w), dt))
            elif nb == 1:
                out_specs.append(pl.BlockSpec((None, G, blk, w), lambda b, i: (b, i, 0, 0)))
                out_shapes.append(jax.ShapeDtypeStruct((B, dil, S // dil, w), dt))
            else:
                out_specs.append(pl.BlockSpec((None, None, G * blk, w),
                                              lambda b, i: (*sub1(b, i), (i * G) % nb1 // G, 0)))
                out_shapes.append(jax.ShapeDtypeStruct((B, dil, S // dil, w), dt))
    heads = ATT_HEADS_PER_GROUP
    outs = pl.pallas_call(
        functools.partial(_attn_kernel, nb1=nb1, per_step=G),
        out_shape=out_shapes,
        grid=(B, steps // G),
        in_specs=in_specs,
        out_specs=out_specs,
        scratch_shapes=[pltpu.VMEM((2, heads, blk, 2 * blk), F32), pltpu.VMEM((2, heads, blk, 2 * blk), F32),
                        pltpu.VMEM((heads, blk, blk), F32),
                        pltpu.VMEM((G, heads, 2 * blk, 2 * ATT_HEAD_DIM), BF16),
                        pltpu.VMEM((G, heads, 2 * blk, 2 * ATT_HEAD_DIM), BF16),
                        pltpu.VMEM((G, heads, blk, 2 * ATT_HEAD_DIM), BF16)],
        compiler_params=_params(("arbitrary", "arbitrary")),
        name="attn",
    )(*args)
    return outs[0::2], outs[1::2]


def _merge_kernel(x_ref, g_ref, mod_ref, retg_ref, o0_ref, o1_ref, o2_ref, l0_ref, l1_ref, l2_ref,
                  wg_ref, wr_ref, wa_ref, wo_ref, out_ref, o1_scr, o2_scr, l1_scr, l2_scr):
    tm = x_ref.shape[0]
    for dil, o_ref, l_ref, o_scr, l_scr in ((ATT_GROUPS[1][1], o1_ref, l1_ref, o1_scr, l1_scr),
                                            (ATT_GROUPS[2][1], o2_ref, l2_ref, o2_scr, l2_scr)):
        n = tm // dil
        for r in range(dil):
            l_scr[pl.ds(r, n, stride=dil), :] = l_ref[0, r]
            for h in range(ATT_HEADS_PER_GROUP):
                hs = slice(h * ATT_HEAD_DIM, (h + 1) * ATT_HEAD_DIM)
                o_scr[h, pl.ds(r, n, stride=dil), :] = o_ref[0, r, :, hs].astype(F32)
    l0, l1, l2 = l0_ref[...], l1_scr[...], l2_scr[...]
    lm = jnp.maximum(jnp.maximum(l0, l1), l2)
    e0, e1, e2 = jnp.exp(l0 - lm), jnp.exp(l1 - lm), jnp.exp(l2 - lm)
    tot = e0 + e1 + e2
    w0, w1, w2 = e0 / tot, e1 / tot, e2 / tot
    parts = []
    for h in range(ATT_HEADS_PER_GROUP):
        hs = slice(h * ATT_HEAD_DIM, (h + 1) * ATT_HEAD_DIM)
        parts.append(w0[:, h:h + 1] * o0_ref[:, hs].astype(F32)
                     + w1[:, h:h + 1] * o1_scr[h]
                     + w2[:, h:h + 1] * o2_scr[h])
    att = jnp.concatenate(parts, axis=-1).astype(BF16)
    att_out = jnp.dot(att, wa_ref[...], preferred_element_type=F32)
    ret_out = jnp.dot(retg_ref[...], wr_ref[...], preferred_element_type=F32)
    x = x_ref[...]
    h = _modulated_norm(x, g_ref[...], mod_ref, 0).astype(BF16)
    gates = jax.nn.sigmoid(jnp.dot(h, wg_ref[...], preferred_element_type=F32))
    merged = gates[:, :D_MODEL] * ret_out + gates[:, D_MODEL:] * att_out
    y = jnp.dot(merged.astype(BF16), wo_ref[...], preferred_element_type=F32)
    out_ref[...] = x + mod_ref[0, :, 2 * D_MODEL:3 * D_MODEL] * y


def _merge(x2, norm_g, mod3, retg2, o, lse, w_gate, w_ret_out, w_att_out, w_o, batch, seq):
    tm = ROW_TM
    tps = seq // tm
    row = lambda w: pl.BlockSpec((tm, w), lambda i: (i, 0))
    sub = lambda d, w: pl.BlockSpec((1, d, tm // d, w), lambda i: (i // tps, 0, i % tps, 0))
    d1, d2 = ATT_GROUPS[1][1], ATT_GROUPS[2][1]
    return pl.pallas_call(
        _merge_kernel,
        out_shape=jax.ShapeDtypeStruct((batch * seq, D_MODEL), F32),
        grid=(batch * seq // tm,),
        in_specs=[row(D_MODEL),
                  _const_spec((1, D_MODEL)),
                  pl.BlockSpec((1, 1, 6 * D_MODEL), lambda i: (i // tps, 0, 0)),
                  row(RET_V_W),
                  row(ATT_GROUP_W), sub(d1, ATT_GROUP_W), sub(d2, ATT_GROUP_W),
                  row(LSE_W), sub(d1, LSE_W), sub(d2, LSE_W),
                  _const_spec((D_MODEL, 2 * D_MODEL)),
                  _const_spec((RET_V_W, D_MODEL)),
                  _const_spec((ATT_GROUP_W, D_MODEL)),
                  _const_spec((D_MODEL, D_MODEL))],
        out_specs=row(D_MODEL),
        scratch_shapes=[pltpu.VMEM((ATT_HEADS_PER_GROUP, tm, ATT_HEAD_DIM), F32)] * 2
                       + [pltpu.VMEM((tm, LSE_W), F32)] * 2,
        compiler_params=_params(("arbitrary",)),
        name="merge",
    )(x2, norm_g, mod3, retg2, *o, *lse, w_gate, w_ret_out, w_att_out, w_o)


def _mlp_kernel(x_ref, mod_ref, g2_ref, gf_ref, w1_ref, w2_ref, out_ref, *, final):
    x = x_ref[...]
    h = _modulated_norm(x, g2_ref[...], mod_ref, 3).astype(BF16)
    u = jnp.maximum(jnp.dot(h, w1_ref[...], preferred_element_type=F32), 0.0)
    y = jnp.dot((u * u).astype(BF16), w2_ref[...], preferred_element_type=F32)
    x = x + mod_ref[0, :, 5 * D_MODEL:6 * D_MODEL] * y
    out_ref[...] = _rms(x, gf_ref[...]) if final else x


def _mlp(x2, mod3, norm2_g, norm_f_g, w1, w2, seq, final):
    rows = x2.shape[0]
    tm = ROW_TM
    tps = seq // tm
    return pl.pallas_call(
        functools.partial(_mlp_kernel, final=final),
        out_shape=jax.ShapeDtypeStruct((rows, D_MODEL), F32),
        grid=(rows // tm,),
        in_specs=[pl.BlockSpec((tm, D_MODEL), lambda i: (i, 0)),
                  pl.BlockSpec((1, 1, 6 * D_MODEL), lambda i: (i // tps, 0, 0)),
                  _const_spec((1, D_MODEL)),
                  _const_spec((1, D_MODEL)),
                  _const_spec((D_MODEL, D_FF)),
                  _const_spec((D_FF, D_MODEL))],
        out_specs=pl.BlockSpec((tm, D_MODEL), lambda i: (i, 0)),
        compiler_params=_params(("arbitrary",)),
        name="mlp",
    )(x2, mod3, norm2_g, norm_f_g, w1, w2)


def kernel(x, c, w_ada, b_ada, norm1_g, w_in, rel_bias, ret_gn_g, ret_gn_b, w_ret_out, w_att_out,
           w_o, norm2_g, w_ff1, w_ff2, norm_f_g):
    B, S, D = x.shape
    depth = w_ada.shape[0]
    half = RET_DK // 2
    inv = ROPE_BASE ** (-jnp.arange(half, dtype=F32) / half)
    ang = jnp.arange(S).astype(F32)[:, None] * inv[None, :]
    cos, sin = jnp.cos(ang), jnp.sin(ang)
    x2 = x.reshape(B * S, D)
    for l in range(depth):
        g1 = norm1_g[l].reshape(1, D)
        mod3 = _mod(c, w_ada[l], b_ada[l]).reshape(B, 1, 6 * D)
        main2, att1, att2 = _proj(x2, g1, mod3, cos, sin, w_in[l, :, :COL_GATE].astype(BF16), B, S)
        main3 = main2.reshape(B, S, main2.shape[1])
        retg = _retention(main3, ret_gn_g[l], ret_gn_b[l])
        o, lse = _attention(main3, att1, att2, rel_bias)
        o = [o[0].reshape(B * S, ATT_GROUP_W), o[1], o[2]]
        lse = [lse[0].reshape(B * S, LSE_W), lse[1], lse[2]]
        x2 = _merge(x2, g1, mod3, retg.reshape(B * S, RET_V_W), o, lse,
                    w_in[l, :, COL_GATE:].astype(BF16), w_ret_out[l].astype(BF16),
                    w_att_out[l].astype(BF16), w_o[l].astype(BF16), B, S)
        x2 = _mlp(x2, mod3, norm2_g[l].reshape(1, D), norm_f_g.reshape(1, D),
                  w_ff1[l].astype(BF16), w_ff2[l].astype(BF16), S, final=l == depth - 1)
    return x2.reshape(B, S, D)
```

```python
import functools
import math

import jax
import jax.numpy as jnp
from jax import lax
from jax.experimental import pallas as pl
from jax.experimental.pallas import tpu as pltpu

F32 = jnp.float32
BF16 = jnp.bfloat16

D_MODEL = 1024
RET_HEADS = 4
RET_DK = 256
RET_DV = 512
RET_C = 256
RET_LOG_GAMMA = tuple(math.log1p(-(2.0 ** (-5.0 - h))) for h in range(RET_HEADS))
RET_QK_W = RET_HEADS * RET_DK
RET_V_W = RET_HEADS * RET_DV
ATT_GROUPS = ((128, 1), (512, 4), (2048, 16))
ATT_HEADS_PER_GROUP = 4
ATT_HEAD_DIM = 128
ATT_GROUP_W = ATT_HEADS_PER_GROUP * ATT_HEAD_DIM
ATT_BLK = 128
ATT_BLOCKS_PER_STEP = 4
LANES = 128
LSE_W = LANES
REL_BUCKETS = 32
REL_MAX_DIST = 2048
D_FF = 4 * D_MODEL
RMS_EPS = 1e-6
GN_EPS = 1e-5
ROPE_BASE = 10000.0
NEG = -1e30
ATT_QSCALE = ATT_HEAD_DIM ** -0.5

COL_ATT = 2 * RET_QK_W + 2 * RET_V_W
COL_GATE = COL_ATT + 9 * ATT_GROUP_W

VMEM_LIMIT = 56 * 1024 * 1024

PROJ_TM = 256
PROJ_TN = 3 * ATT_GROUP_W
PROJ_MAIN_TILES = (COL_ATT + PROJ_TN) // PROJ_TN
ROW_TM = 512
MERGE_ROW_PARTS = 2


def _params(sem):
    return pltpu.CompilerParams(dimension_semantics=sem, vmem_limit_bytes=VMEM_LIMIT)


def _const_spec(shape):
    zeros = (0,) * len(shape)
    return pl.BlockSpec(shape, lambda *_: zeros, pipeline_mode=pl.Buffered(1))


def _silu(t):
    return t * jax.nn.sigmoid(t)


def _rms(x, g):
    return x * lax.rsqrt(jnp.mean(x * x, axis=-1, keepdims=True) + RMS_EPS) * g


def _modulated_norm(x, g, mod_ref, k):
    shift = mod_ref[0, :, k * D_MODEL:(k + 1) * D_MODEL]
    scale = mod_ref[0, :, (k + 1) * D_MODEL:(k + 2) * D_MODEL]
    return _rms(x, g) * (1.0 + scale) + shift


def _mod_kernel(c_ref, w_ref, b_ref, o_ref):
    o_ref[...] = jnp.dot(_silu(c_ref[...]), w_ref[...], preferred_element_type=F32) + b_ref[...]


def _mod(c, w_ada, b_ada):
    B = c.shape[0]
    n = w_ada.shape[1]
    tn = D_MODEL
    return pl.pallas_call(
        _mod_kernel,
        out_shape=jax.ShapeDtypeStruct((B, n), F32),
        grid=(n // tn,),
        in_specs=[pl.BlockSpec((B, D_MODEL), lambda j: (0, 0)),
                  pl.BlockSpec((D_MODEL, tn), lambda j: (0, j)),
                  pl.BlockSpec((1, tn), lambda j: (0, j))],
        out_specs=pl.BlockSpec((B, tn), lambda j: (0, j)),
        compiler_params=_params(("arbitrary",)),
        name="mod",
    )(c, w_ada, b_ada.reshape(1, n))


def _proj_main_layout():
    spans = [(h * RET_DK, (h + 1) * RET_DK, "rot", (h % RET_HEADS, h >= RET_HEADS))
             for h in range(2 * RET_HEADS)]
    spans += [(2 * RET_QK_W, 2 * RET_QK_W + RET_V_W, "copy", None),
              (2 * RET_QK_W + RET_V_W, COL_ATT, "silu", None),
              (COL_ATT, COL_ATT + ATT_GROUP_W, "scale", ATT_QSCALE),
              (COL_ATT + ATT_GROUP_W, PROJ_MAIN_TILES * PROJ_TN, "copy", None)]
    tiles = [[] for _ in range(PROJ_MAIN_TILES)]
    for c0, c1, kind, arg in spans:
        for t in range(c0 // PROJ_TN, (c1 - 1) // PROJ_TN + 1):
            lo, hi = max(c0, t * PROJ_TN), min(c1, (t + 1) * PROJ_TN)
            assert kind != "rot" or (lo, hi) == (c0, c1)
            tiles[t].append((lo - t * PROJ_TN, hi - t * PROJ_TN, kind, arg))
    return tiles


PROJ_MAIN_LAYOUT = _proj_main_layout()


def _proj_kernel(x_ref, g_ref, mod_ref, cos_ref, sin_ref, w_ref, main_ref, a1_ref, a2_ref,
                 hf_scr, hd1_scr, h0_scr, h1_scr, h2_scr, *, tps):
    tm = x_ref.shape[0]
    perms = ((ATT_GROUPS[1][1], h1_scr, a1_ref), (ATT_GROUPS[2][1], h2_scr, a2_ref))

    hf = _modulated_norm(x_ref[...], g_ref[...], mod_ref, 0)
    h0_scr[...] = hf.astype(BF16)
    d1, d2 = ATT_GROUPS[1][1], ATT_GROUPS[2][1]
    n1, n2 = tm // d1, tm // d2
    for c in range(D_MODEL // LANES):
        cs = slice(c * LANES, (c + 1) * LANES)
        hf_scr[c] = hf[:, cs]
        for r in range(d1):
            part = hf_scr[c, pl.ds(r, n1, stride=d1), :]
            hd1_scr[c, r * n1:(r + 1) * n1, :] = part
            h1_scr[r * n1:(r + 1) * n1, cs] = part.astype(BF16)
        for r in range(d2):
            src = pl.ds((r % d1) * n1 + r // d1, n2, stride=d1)
            h2_scr[r * n2:(r + 1) * n2, cs] = hd1_scr[c, src, :].astype(BF16)

    def w_tile(t):
        return w_ref[:, t * PROJ_TN:(t + 1) * PROJ_TN]

    def main_tile(t, segments):
        acc = jnp.dot(h0_scr[...], w_tile(t), preferred_element_type=F32)
        if any(kind == "rot" for _, _, kind, _ in segments):
            cos, sin = cos_ref[...], sin_ref[...]
            row = lax.broadcasted_iota(jnp.int32, cos.shape, 0)
            pos = (((pl.program_id(0) % tps) * tm + row) % RET_C + 1).astype(F32)
        base = t * PROJ_TN
        for c0, c1, kind, arg in segments:
            seg = acc[:, c0:c1]
            if kind == "rot":
                head, is_key = arg
                rate = -RET_LOG_GAMMA[head] if is_key else RET_LOG_GAMMA[head]
                dec = jnp.exp(rate * pos) * ((RET_DK ** -0.5) if is_key else 1.0)
                cd, sd = cos * dec, sin * dec
                cm = (c0 + c1) // 2
                t1, t2 = acc[:, c0:cm], acc[:, cm:c1]
                main_ref[:, base + c0:base + cm] = (t1 * cd - t2 * sd).astype(BF16)
                main_ref[:, base + cm:base + c1] = (t1 * sd + t2 * cd).astype(BF16)
            elif kind == "silu":
                main_ref[:, base + c0:base + c1] = _silu(seg).astype(BF16)
            elif kind == "scale":
                main_ref[:, base + c0:base + c1] = (seg * arg).astype(BF16)
            else:
                main_ref[:, base + c0:base + c1] = seg.astype(BF16)

    for t, segments in enumerate(PROJ_MAIN_LAYOUT):
        main_tile(t, segments)

    for t, (dil, h_scr, a_ref) in enumerate(perms):
        n = tm // dil
        acc = jnp.dot(h_scr[...], w_tile(PROJ_MAIN_TILES + t), preferred_element_type=F32)
        gw = ATT_GROUP_W
        for r in range(dil):
            rows = slice(r * n, (r + 1) * n)
            a_ref[0, r, :, :gw] = (acc[rows, :gw] * ATT_QSCALE).astype(BF16)
            a_ref[0, r, :, gw:] = acc[rows, gw:].astype(BF16)


def _proj(x2, norm_g, mod3, cos, sin, w_in, batch, seq):
    tm, tn = PROJ_TM, PROJ_TN
    tps = seq // tm
    assert seq % tm == 0 and tm % RET_C == 0
    d1, d2 = ATT_GROUPS[1][1], ATT_GROUPS[2][1]
    assert COL_GATE == (PROJ_MAIN_TILES + 2) * tn
    sub = lambda d: pl.BlockSpec((1, d, tm // d, tn), lambda i: (i // tps, 0, i % tps, 0))
    return pl.pallas_call(
        functools.partial(_proj_kernel, tps=tps),
        out_shape=[jax.ShapeDtypeStruct((batch * seq, PROJ_MAIN_TILES * tn), BF16),
                   jax.ShapeDtypeStruct((batch, d1, seq // d1, tn), BF16),
                   jax.ShapeDtypeStruct((batch, d2, seq // d2, tn), BF16)],
        grid=(batch * seq // tm,),
        in_specs=[pl.BlockSpec((tm, D_MODEL), lambda i: (i, 0)),
                  _const_spec((1, D_MODEL)),
                  pl.BlockSpec((1, 1, 6 * D_MODEL), lambda i: (i // tps, 0, 0)),
                  pl.BlockSpec((tm, RET_DK // 2), lambda i: (i % tps, 0)),
                  pl.BlockSpec((tm, RET_DK // 2), lambda i: (i % tps, 0)),
                  pl.BlockSpec((D_MODEL, COL_GATE), lambda i: (0, 0), pipeline_mode=pl.Buffered(1))],
        out_specs=[pl.BlockSpec((tm, PROJ_MAIN_TILES * tn), lambda i: (i, 0)), sub(d1), sub(d2)],
        scratch_shapes=[pltpu.VMEM((D_MODEL // LANES, tm, LANES), F32)] * 2
                       + [pltpu.VMEM((tm, D_MODEL), BF16)] * 3,
        compiler_params=_params(("arbitrary",)),
        name="proj",
    )(x2, norm_g, mod3, cos, sin, w_in)


def _ret_kernel(q_ref, k_ref, v_ref, gate_ref, cdec_ref, gng_ref, gnb_ref, o_ref, state, st_in):
    C = RET_C
    nc = q_ref.shape[1] // C
    cdec = cdec_ref[0]
    gng = gng_ref[0]
    gnb = gnb_ref[0]
    chunk = lambda ci: pl.ds(pl.multiple_of(ci * C, C), C)

    state[...] = jnp.zeros_like(state)
    st_in[0] = jnp.zeros(st_in.shape[1:], BF16)

    def advance(ci, carry):
        rows = chunk(ci)
        kv = lax.dot_general(k_ref[0, rows, :], v_ref[0, rows, :], (((0,), (0,)), ((), ())),
                             preferred_element_type=F32)
        new = (state[...] + kv) * cdec
        state[...] = new
        st_in[ci + 1] = new.astype(BF16)
        return carry

    lax.fori_loop(0, nc - 1, advance, 0, unroll=True)

    qi = lax.broadcasted_iota(jnp.int32, (C, C), 0)
    kj = lax.broadcasted_iota(jnp.int32, (C, C), 1)
    causal = qi >= kj

    def emit(ci, carry):
        rows = chunk(ci)
        q = q_ref[0, rows, :]
        s = lax.dot_general(q, k_ref[0, rows, :], (((1,), (1,)), ((), ())), preferred_element_type=F32)
        s = jnp.where(causal, s, 0.0).astype(BF16)
        o = (jnp.dot(s, v_ref[0, rows, :], preferred_element_type=F32)
             + jnp.dot(q, st_in[ci], preferred_element_type=F32))
        mu = jnp.mean(o, axis=-1, keepdims=True)
        oc = o - mu
        var = jnp.mean(oc * oc, axis=-1, keepdims=True)
        on = oc * lax.rsqrt(var + GN_EPS) * gng + gnb
        o_ref[0, rows, :] = (gate_ref[0, rows, :].astype(F32) * on).astype(BF16)
        return carry

    lax.fori_loop(0, nc, emit, 0, unroll=True)


def _retention(main3, gn_g, gn_b):
    B, S, _ = main3.shape
    H, C = RET_HEADS, RET_C
    assert S % C == 0
    cdec = jnp.exp(jnp.asarray(RET_LOG_GAMMA, F32) * C)[:, None, None]
    kq = RET_QK_W // RET_DK
    kv = 2 * RET_QK_W // RET_DV
    kg = kv + RET_V_W // RET_DV
    return pl.pallas_call(
        _ret_kernel,
        out_shape=jax.ShapeDtypeStruct((B, S, RET_V_W), BF16),
        grid=(B, H),
        in_specs=[pl.BlockSpec((1, S, RET_DK), lambda b, h: (b, 0, h)),
                  pl.BlockSpec((1, S, RET_DK), lambda b, h: (b, 0, kq + h)),
                  pl.BlockSpec((1, S, RET_DV), lambda b, h: (b, 0, kv + h)),
                  pl.BlockSpec((1, S, RET_DV), lambda b, h: (b, 0, kg + h)),
                  pl.BlockSpec((1, 1, 1), lambda b, h: (h, 0, 0)),
                  pl.BlockSpec((1, 1, RET_DV), lambda b, h: (h, 0, 0)),
                  pl.BlockSpec((1, 1, RET_DV), lambda b, h: (h, 0, 0))],
        out_specs=pl.BlockSpec((1, S, RET_DV), lambda b, h: (b, 0, h)),
        scratch_shapes=[pltpu.VMEM((RET_DK, RET_DV), F32), pltpu.VMEM((S // C, RET_DK, RET_DV), BF16)],
        compiler_params=_params(("arbitrary", "arbitrary")),
        name="ret",
    )(main3, main3, main3, main3, cdec, gn_g.reshape(H, 1, RET_DV), gn_b.reshape(H, 1, RET_DV))


def _t5_bucket(dist):
    max_exact = REL_BUCKETS // 2
    d_f = jnp.maximum(dist, 1).astype(F32)
    large = max_exact + (jnp.log(d_f / max_exact) / math.log(REL_MAX_DIST / max_exact)
                         * (REL_BUCKETS - max_exact)).astype(jnp.int32)
    large = jnp.minimum(large, REL_BUCKETS - 1)
    return jnp.where(dist < max_exact, dist, large)


def _attn_unit(n, q_ref, k_ref, v_ref, bias_scr, vaug_scr, o_ref, lse_ref):
    blk, dh = ATT_BLK, ATT_HEAD_DIM
    windowed = k_ref.shape[0] > blk
    if windowed:
        win = pl.ds(pl.multiple_of(jnp.maximum(n - 1, 0) * blk, blk), 2 * blk)
        first = (n == 0).astype(jnp.int32)
    lane = lax.broadcasted_iota(jnp.int32, (blk, LSE_W), 1)
    lse_all = jnp.zeros((blk, LSE_W), F32)
    for h in range(ATT_HEADS_PER_GROUP):
        hs = slice(h * dh, (h + 1) * dh)
        if windowed:
            kb, vb, bias = k_ref[win, hs], v_ref[win, hs], bias_scr[first, h]
        else:
            kb, vb, bias = k_ref[:, hs], v_ref[:, hs], bias_scr[h]
        vaug_scr[h, :, :dh] = vb
        s = lax.dot_general(q_ref[:, hs], kb, (((1,), (1,)), ((), ())), preferred_element_type=F32) + bias
        mx = jnp.max(s, axis=-1, keepdims=True)
        e = jnp.exp(s - mx).astype(BF16)
        o_den = jnp.dot(e, vaug_scr[h], preferred_element_type=F32)
        den = o_den[:, dh:]
        o_ref[:, hs] = (o_den[:, :dh] * (1.0 / den)).astype(BF16)
        lse_all = jnp.where(lane == h, mx + jnp.log(den), lse_all)
    lse_ref[...] = lse_all


def _attn_kernel(tab_ref, q0, k0, v0, q1, k1, v1, q2, k2, v2, bk0, bk1, bk2,
                 o0, l0, o1, l1, o2, l2, bias0, bias1, bias2, vaug0, vaug1, vaug2, *, nb1, per_step):
    b = pl.program_id(0)
    i = pl.program_id(1)
    blk = ATT_BLK

    @pl.when((b == 0) & (i == 0))
    def _():
        for vaug in (vaug0, vaug1, vaug2):
            vaug[...] = jnp.ones(vaug.shape, BF16)
        qi = lax.broadcasted_iota(jnp.int32, (blk, 2 * blk), 0)
        kj = lax.broadcasted_iota(jnp.int32, (blk, 2 * blk), 1)
        m = blk + qi - kj
        band = (m >= 0) & (m <= blk)
        masked = jnp.full((blk, blk), NEG, F32)
        for gi, (bk_ref, bias_scr) in enumerate(((bk0, bias0), (bk1, bias1), (bk2, bias2))):
            bucket = bk_ref[...]
            for h in range(ATT_HEADS_PER_GROUP):
                col = gi * ATT_HEADS_PER_GROUP + h
                acc = jnp.zeros((blk, 2 * blk), F32)
                for t in range(REL_BUCKETS):
                    acc = jnp.where(bucket == t, tab_ref[t, col], acc)
                tile = jnp.where(band, acc, NEG)
                if len(bias_scr.shape) == 4:
                    bias_scr[0, h] = tile
                    bias_scr[1, h] = jnp.concatenate([tile[:, blk:], masked], axis=1)
                else:
                    bias_scr[h] = tile[:, blk:]

    for u in range(per_step):
        rows = pl.ds(u * blk, blk)
        n = i * per_step + u
        _attn_unit(n, q0.at[rows], k0, v0, bias0, vaug0.at[u], o0.at[rows], l0.at[rows])
        _attn_unit(n % nb1, q1.at[rows], k1, v1, bias1, vaug1.at[u], o1.at[rows], l1.at[rows])
        _attn_unit(0, q2.at[u], k2.at[u], v2.at[u], bias2, vaug2.at[u], o2.at[u], l2.at[u])


def _attention(main3, att1, att2, rel_bias):
    B, S, _ = main3.shape
    blk, gw = ATT_BLK, ATT_GROUP_W
    steps = S // blk
    lens = [S // dil for _, dil in ATT_GROUPS]
    nbs = [L // blk for L in lens]
    assert all(win // dil == blk for win, dil in ATT_GROUPS)
    assert nbs[0] == steps and nbs[1] >= 2 and nbs[2] == 1
    c0 = COL_ATT // gw
    nb1 = nbs[1]
    G = ATT_BLOCKS_PER_STEP
    assert nb1 % G == 0 and steps % G == 0
    sub1 = lambda b, i: (b, (i * G) // nb1)
    in_specs = [pl.BlockSpec(memory_space=pltpu.SMEM),
                pl.BlockSpec((None, G * blk, gw), lambda b, i: (b, i, c0)),
                pl.BlockSpec((None, S, gw), lambda b, i: (b, 0, c0 + 1)),
                pl.BlockSpec((None, S, gw), lambda b, i: (b, 0, c0 + 2)),
                pl.BlockSpec((None, None, G * blk, gw), lambda b, i: (*sub1(b, i), (i * G) % nb1 // G, 0)),
                pl.BlockSpec((None, None, lens[1], gw), lambda b, i: (*sub1(b, i), 0, 1)),
                pl.BlockSpec((None, None, lens[1], gw), lambda b, i: (*sub1(b, i), 0, 2)),
                pl.BlockSpec((None, G, blk, gw), lambda b, i: (b, i, 0, 0)),
                pl.BlockSpec((None, G, blk, gw), lambda b, i: (b, i, 0, 1)),
                pl.BlockSpec((None, G, blk, gw), lambda b, i: (b, i, 0, 2))]
    args = [rel_bias] + [main3] * 3 + [att1] * 3 + [att2] * 3
    qi = jnp.arange(blk)[:, None]
    kj = jnp.arange(2 * blk)[None, :]
    m = blk + qi - kj
    for _, dil in ATT_GROUPS:
        in_specs.append(_const_spec((blk, 2 * blk)))
        args.append(_t5_bucket(jnp.clip(m, 0, blk) * dil).astype(jnp.int32))
    out_specs, out_shapes = [], []
    for (_, dil), nb in zip(ATT_GROUPS, nbs):
        for w, dt in ((gw, BF16), (LSE_W, F32)):
            if nb == steps:
                out_specs.append(pl.BlockSpec((None, G * blk, w), lambda b, i: (b, i, 0)))
                out_shapes.append(jax.ShapeDtypeStruct((B, S, w), dt))
            elif nb == 1:
                out_specs.append(pl.BlockSpec((None, G, blk, w), lambda b, i: (b, i, 0, 0)))
                out_shapes.append(jax.ShapeDtypeStruct((B, dil, S // dil, w), dt))
            else:
                out_specs.append(pl.BlockSpec((None, None, G * blk, w),
                                              lambda b, i: (*sub1(b, i), (i * G) % nb1 // G, 0)))
                out_shapes.append(jax.ShapeDtypeStruct((B, dil, S // dil, w), dt))
    heads = ATT_HEADS_PER_GROUP
    outs = pl.pallas_call(
        functools.partial(_attn_kernel, nb1=nb1, per_step=G),
        out_shape=out_shapes,
        grid=(B, steps // G),
        in_specs=in_specs,
        out_specs=out_specs,
        scratch_shapes=[pltpu.VMEM((2, heads, blk, 2 * blk), F32), pltpu.VMEM((2, heads, blk, 2 * blk), F32),
                        pltpu.VMEM((heads, blk, blk), F32),
                        pltpu.VMEM((G, heads, 2 * blk, 2 * ATT_HEAD_DIM), BF16),
                        pltpu.VMEM((G, heads, 2 * blk, 2 * ATT_HEAD_DIM), BF16),
                        pltpu.VMEM((G, heads, blk, 2 * ATT_HEAD_DIM), BF16)],
        compiler_params=_params(("arbitrary", "arbitrary")),
        name="attn",
    )(*args)
    return outs[0::2], outs[1::2]


def _merge_kernel(x_ref, g_ref, mod_ref, retg_ref, o0_ref, o1_ref, o2_ref, l0_ref, l1_ref, l2_ref,
                  wg_ref, wr_ref, wa_ref, wo_ref, out_ref, o1_scr, o2_scr, l1_scr, l2_scr):
    tm = x_ref.shape[0]
    for dil, o_ref, l_ref, o_scr, l_scr in ((ATT_GROUPS[1][1], o1_ref, l1_ref, o1_scr, l1_scr),
                                            (ATT_GROUPS[2][1], o2_ref, l2_ref, o2_scr, l2_scr)):
        n = tm // dil
        for r in range(dil):
            l_scr[pl.ds(r, n, stride=dil), :] = l_ref[0, r]
            for h in range(ATT_HEADS_PER_GROUP):
                hs = slice(h * ATT_HEAD_DIM, (h + 1) * ATT_HEAD_DIM)
                o_scr[h, pl.ds(r, n, stride=dil), :] = o_ref[0, r, :, hs].astype(F32)
    part = tm // MERGE_ROW_PARTS
    for p in range(MERGE_ROW_PARTS):
        rows = slice(p * part, (p + 1) * part)
        l0, l1, l2 = l0_ref[rows, :], l1_scr[rows, :], l2_scr[rows, :]
        lm = jnp.maximum(jnp.maximum(l0, l1), l2)
        e0, e1, e2 = jnp.exp(l0 - lm), jnp.exp(l1 - lm), jnp.exp(l2 - lm)
        tot = e0 + e1 + e2
        w0, w1, w2 = e0 / tot, e1 / tot, e2 / tot
        parts = []
        for h in range(ATT_HEADS_PER_GROUP):
            hs = slice(h * ATT_HEAD_DIM, (h + 1) * ATT_HEAD_DIM)
            parts.append(w0[:, h:h + 1] * o0_ref[rows, hs].astype(F32)
                         + w1[:, h:h + 1] * o1_scr[h, rows, :]
                         + w2[:, h:h + 1] * o2_scr[h, rows, :])
        att = jnp.concatenate(parts, axis=-1).astype(BF16)
        att_out = jnp.dot(att, wa_ref[...], preferred_element_type=F32)
        ret_out = jnp.dot(retg_ref[rows, :], wr_ref[...], preferred_element_type=F32)
        x = x_ref[rows, :]
        h = _modulated_norm(x, g_ref[...], mod_ref, 0).astype(BF16)
        gates = jax.nn.sigmoid(jnp.dot(h, wg_ref[...], preferred_element_type=F32))
        merged = gates[:, :D_MODEL] * ret_out + gates[:, D_MODEL:] * att_out
        y = jnp.dot(merged.astype(BF16), wo_ref[...], preferred_element_type=F32)
        out_ref[rows, :] = x + mod_ref[0, :, 2 * D_MODEL:3 * D_MODEL] * y


def _merge(x2, norm_g, mod3, retg2, o, lse, w_gate, w_ret_out, w_att_out, w_o, batch, seq):
    tm = ROW_TM
    tps = seq // tm
    row = lambda w: pl.BlockSpec((tm, w), lambda i: (i, 0))
    sub = lambda d, w: pl.BlockSpec((1, d, tm // d, w), lambda i: (i // tps, 0, i % tps, 0))
    d1, d2 = ATT_GROUPS[1][1], ATT_GROUPS[2][1]
    return pl.pallas_call(
        _merge_kernel,
        out_shape=jax.ShapeDtypeStruct((batch * seq, D_MODEL), F32),
        grid=(batch * seq // tm,),
        in_specs=[row(D_MODEL),
                  _const_spec((1, D_MODEL)),
                  pl.BlockSpec((1, 1, 6 * D_MODEL), lambda i: (i // tps, 0, 0)),
                  row(RET_V_W),
                  row(ATT_GROUP_W), sub(d1, ATT_GROUP_W), sub(d2, ATT_GROUP_W),
                  row(LSE_W), sub(d1, LSE_W), sub(d2, LSE_W),
                  pl.BlockSpec((pl.Element(D_MODEL), pl.Element(2 * D_MODEL)), lambda i: (0, COL_GATE),
                               pipeline_mode=pl.Buffered(1)),
                  _const_spec((RET_V_W, D_MODEL)),
                  _const_spec((ATT_GROUP_W, D_MODEL)),
                  _const_spec((D_MODEL, D_MODEL))],
        out_specs=row(D_MODEL),
        scratch_shapes=[pltpu.VMEM((ATT_HEADS_PER_GROUP, tm, ATT_HEAD_DIM), F32)] * 2
                       + [pltpu.VMEM((tm, LSE_W), F32)] * 2,
        compiler_params=_params(("arbitrary",)),
        name="merge",
    )(x2, norm_g, mod3, retg2, *o, *lse, w_gate, w_ret_out, w_att_out, w_o)


def _mlp_kernel(x_ref, mod_ref, g2_ref, gf_ref, w1_ref, w2_ref, out_ref, *, final):
    x = x_ref[...]
    h = _modulated_norm(x, g2_ref[...], mod_ref, 3).astype(BF16)
    u = jnp.maximum(jnp.dot(h, w1_ref[...], preferred_element_type=F32), 0.0)
    y = jnp.dot((u * u).astype(BF16), w2_ref[...], preferred_element_type=F32)
    x = x + mod_ref[0, :, 5 * D_MODEL:6 * D_MODEL] * y
    out_ref[...] = _rms(x, gf_ref[...]) if final else x


def _mlp(x2, mod3, norm2_g, norm_f_g, w1, w2, seq, final):
    rows = x2.shape[0]
    tm = ROW_TM
    tps = seq // tm
    return pl.pallas_call(
        functools.partial(_mlp_kernel, final=final),
        out_shape=jax.ShapeDtypeStruct((rows, D_MODEL), F32),
        grid=(rows // tm,),
        in_specs=[pl.BlockSpec((tm, D_MODEL), lambda i: (i, 0)),
                  pl.BlockSpec((1, 1, 6 * D_MODEL), lambda i: (i // tps, 0, 0)),
                  _const_spec((1, D_MODEL)),
                  _const_spec((1, D_MODEL)),
                  _const_spec((D_MODEL, D_FF)),
                  _const_spec((D_FF, D_MODEL))],
        out_specs=pl.BlockSpec((tm, D_MODEL), lambda i: (i, 0)),
        compiler_params=_params(("arbitrary",)),
        name="mlp",
    )(x2, mod3, norm2_g, norm_f_g, w1, w2)


def kernel(x, c, w_ada, b_ada, norm1_g, w_in, rel_bias, ret_gn_g, ret_gn_b, w_ret_out, w_att_out,
           w_o, norm2_g, w_ff1, w_ff2, norm_f_g):
    B, S, D = x.shape
    depth = w_ada.shape[0]
    half = RET_DK // 2
    inv = ROPE_BASE ** (-jnp.arange(half, dtype=F32) / half)
    ang = jnp.arange(S).astype(F32)[:, None] * inv[None, :]
    cos, sin = jnp.cos(ang), jnp.sin(ang)
    x2 = x.reshape(B * S, D)
    for l in range(depth):
        g1 = norm1_g[l].reshape(1, D)
        mod3 = _mod(c, w_ada[l], b_ada[l]).reshape(B, 1, 6 * D)
        w_in_bf = w_in[l].astype(BF16)
        main2, att1, att2 = _proj(x2, g1, mod3, cos, sin, w_in_bf, B, S)
        main3 = main2.reshape(B, S, main2.shape[1])
        retg = _retention(main3, ret_gn_g[l], ret_gn_b[l])
        o, lse = _attention(main3, att1, att2, rel_bias)
        o = [o[0].reshape(B * S, ATT_GROUP_W), o[1], o[2]]
        lse = [lse[0].reshape(B * S, LSE_W), lse[1], lse[2]]
        x2 = _merge(x2, g1, mod3, retg.reshape(B * S, RET_V_W), o, lse,
                    w_in_bf, w_ret_out[l].astype(BF16),
                    w_att_out[l].astype(BF16), w_o[l].astype(BF16), B, S)
        x2 = _mlp(x2, mod3, norm2_g[l].reshape(1, D), norm_f_g.reshape(1, D),
                  w_ff1[l].astype(BF16), w_ff2[l].astype(BF16), S, final=l == depth - 1)
    return x2.reshape(B, S, D)
```

```python
import functools
import math

import jax
import jax.numpy as jnp
from jax import lax
from jax.experimental import pallas as pl
from jax.experimental.pallas import tpu as pltpu

F32 = jnp.float32
BF16 = jnp.bfloat16

D_MODEL = 1024
RET_HEADS = 4
RET_DK = 256
RET_DV = 512
RET_C = 256
RET_LOG_GAMMA = tuple(math.log1p(-(2.0 ** (-5.0 - h))) for h in range(RET_HEADS))
RET_QK_W = RET_HEADS * RET_DK
RET_V_W = RET_HEADS * RET_DV
ATT_GROUPS = ((128, 1), (512, 4), (2048, 16))
ATT_HEADS_PER_GROUP = 4
ATT_HEAD_DIM = 128
ATT_GROUP_W = ATT_HEADS_PER_GROUP * ATT_HEAD_DIM
ATT_BLK = 128
ATT_BLOCKS_PER_STEP = 4
LANES = 128
LSE_W = LANES
REL_BUCKETS = 32
REL_MAX_DIST = 2048
D_FF = 4 * D_MODEL
RMS_EPS = 1e-6
GN_EPS = 1e-5
ROPE_BASE = 10000.0
NEG = -1e30
ATT_QSCALE = ATT_HEAD_DIM ** -0.5

COL_ATT = 2 * RET_QK_W + 2 * RET_V_W
COL_GATE = COL_ATT + 9 * ATT_GROUP_W

VMEM_LIMIT = 56 * 1024 * 1024

PROJ_TM = 256
PROJ_TN = 3 * ATT_GROUP_W
PROJ_MAIN_TILES = (COL_ATT + PROJ_TN) // PROJ_TN
ROW_TM = 512
MERGE_ROW_PARTS = 2


def _params(sem):
    return pltpu.CompilerParams(dimension_semantics=sem, vmem_limit_bytes=VMEM_LIMIT)


def _const_spec(shape):
    zeros = (0,) * len(shape)
    return pl.BlockSpec(shape, lambda *_: zeros, pipeline_mode=pl.Buffered(1))


def _silu(t):
    return t * jax.nn.sigmoid(t)


def _rms(x, g):
    return x * lax.rsqrt(jnp.mean(x * x, axis=-1, keepdims=True) + RMS_EPS) * g


def _modulated_norm(x, g, mod_ref, k):
    shift = mod_ref[0, :, k * D_MODEL:(k + 1) * D_MODEL]
    scale = mod_ref[0, :, (k + 1) * D_MODEL:(k + 2) * D_MODEL]
    return _rms(x, g) * (1.0 + scale) + shift


def _mod_kernel(c_ref, w_ref, b_ref, o_ref):
    o_ref[...] = jnp.dot(_silu(c_ref[...]), w_ref[...], preferred_element_type=F32) + b_ref[...]


def _mod(c, w_ada, b_ada):
    B = c.shape[0]
    n = w_ada.shape[1]
    tn = D_MODEL
    return pl.pallas_call(
        _mod_kernel,
        out_shape=jax.ShapeDtypeStruct((B, n), F32),
        grid=(n // tn,),
        in_specs=[pl.BlockSpec((B, D_MODEL), lambda j: (0, 0)),
                  pl.BlockSpec((D_MODEL, tn), lambda j: (0, j)),
                  pl.BlockSpec((1, tn), lambda j: (0, j))],
        out_specs=pl.BlockSpec((B, tn), lambda j: (0, j)),
        compiler_params=_params(("arbitrary",)),
        name="mod",
    )(c, w_ada, b_ada.reshape(1, n))


def _proj_main_layout():
    spans = [(h * RET_DK, (h + 1) * RET_DK, "rot", (h % RET_HEADS, h >= RET_HEADS))
             for h in range(2 * RET_HEADS)]
    spans += [(2 * RET_QK_W, 2 * RET_QK_W + RET_V_W, "copy", None),
              (2 * RET_QK_W + RET_V_W, COL_ATT, "silu", None),
              (COL_ATT, COL_ATT + ATT_GROUP_W, "scale", ATT_QSCALE),
              (COL_ATT + ATT_GROUP_W, PROJ_MAIN_TILES * PROJ_TN, "copy", None)]
    tiles = [[] for _ in range(PROJ_MAIN_TILES)]
    for c0, c1, kind, arg in spans:
        for t in range(c0 // PROJ_TN, (c1 - 1) // PROJ_TN + 1):
            lo, hi = max(c0, t * PROJ_TN), min(c1, (t + 1) * PROJ_TN)
            assert kind != "rot" or (lo, hi) == (c0, c1)
            tiles[t].append((lo - t * PROJ_TN, hi - t * PROJ_TN, kind, arg))
    return tiles


PROJ_MAIN_LAYOUT = _proj_main_layout()


def _proj_kernel(x_ref, g_ref, mod_ref, cos_ref, sin_ref, w_ref, main_ref, a1_ref, a2_ref,
                 hf_scr, hd1_scr, h0_scr, h1_scr, h2_scr, *, tps):
    tm = x_ref.shape[0]
    perms = ((ATT_GROUPS[1][1], h1_scr, a1_ref), (ATT_GROUPS[2][1], h2_scr, a2_ref))

    hf = _modulated_norm(x_ref[...], g_ref[...], mod_ref, 0)
    h0_scr[...] = hf.astype(BF16)
    d1, d2 = ATT_GROUPS[1][1], ATT_GROUPS[2][1]
    n1, n2 = tm // d1, tm // d2
    for c in range(D_MODEL // LANES):
        cs = slice(c * LANES, (c + 1) * LANES)
        hf_scr[c] = hf[:, cs]
        for r in range(d1):
            part = hf_scr[c, pl.ds(r, n1, stride=d1), :]
            hd1_scr[c, r * n1:(r + 1) * n1, :] = part
            h1_scr[r * n1:(r + 1) * n1, cs] = part.astype(BF16)
        for r in range(d2):
            src = pl.ds((r % d1) * n1 + r // d1, n2, stride=d1)
            h2_scr[r * n2:(r + 1) * n2, cs] = hd1_scr[c, src, :].astype(BF16)

    def w_tile(t):
        return w_ref[:, t * PROJ_TN:(t + 1) * PROJ_TN]

    def main_tile(t, segments):
        acc = jnp.dot(h0_scr[...], w_tile(t), preferred_element_type=F32)
        if any(kind == "rot" for _, _, kind, _ in segments):
            cos, sin = cos_ref[...], sin_ref[...]
            row = lax.broadcasted_iota(jnp.int32, cos.shape, 0)
            pos = (((pl.program_id(0) % tps) * tm + row) % RET_C + 1).astype(F32)
        base = t * PROJ_TN
        for c0, c1, kind, arg in segments:
            seg = acc[:, c0:c1]
            if kind == "rot":
                head, is_key = arg
                rate = -RET_LOG_GAMMA[head] if is_key else RET_LOG_GAMMA[head]
                dec = jnp.exp(rate * pos) * ((RET_DK ** -0.5) if is_key else 1.0)
                cd, sd = cos * dec, sin * dec
                cm = (c0 + c1) // 2
                t1, t2 = acc[:, c0:cm], acc[:, cm:c1]
                main_ref[:, base + c0:base + cm] = (t1 * cd - t2 * sd).astype(BF16)
                main_ref[:, base + cm:base + c1] = (t1 * sd + t2 * cd).astype(BF16)
            elif kind == "silu":
                main_ref[:, base + c0:base + c1] = _silu(seg).astype(BF16)
            elif kind == "scale":
                main_ref[:, base + c0:base + c1] = (seg * arg).astype(BF16)
            else:
                main_ref[:, base + c0:base + c1] = seg.astype(BF16)

    for t, segments in enumerate(PROJ_MAIN_LAYOUT):
        main_tile(t, segments)

    for t, (dil, h_scr, a_ref) in enumerate(perms):
        n = tm // dil
        acc = jnp.dot(h_scr[...], w_tile(PROJ_MAIN_TILES + t), preferred_element_type=F32)
        gw = ATT_GROUP_W
        for r in range(dil):
            rows = slice(r * n, (r + 1) * n)
            a_ref[0, r, :, :gw] = (acc[rows, :gw] * ATT_QSCALE).astype(BF16)
            a_ref[0, r, :, gw:] = acc[rows, gw:].astype(BF16)


def _proj(x2, norm_g, mod3, cos, sin, w_in, batch, seq):
    tm, tn = PROJ_TM, PROJ_TN
    tps = seq // tm
    assert seq % tm == 0 and tm % RET_C == 0
    d1, d2 = ATT_GROUPS[1][1], ATT_GROUPS[2][1]
    assert COL_GATE == (PROJ_MAIN_TILES + 2) * tn
    sub = lambda d: pl.BlockSpec((1, d, tm // d, tn), lambda i: (i // tps, 0, i % tps, 0))
    return pl.pallas_call(
        functools.partial(_proj_kernel, tps=tps),
        out_shape=[jax.ShapeDtypeStruct((batch * seq, PROJ_MAIN_TILES * tn), BF16),
                   jax.ShapeDtypeStruct((batch, d1, seq // d1, tn), BF16),
                   jax.ShapeDtypeStruct((batch, d2, seq // d2, tn), BF16)],
        grid=(batch * seq // tm,),
        in_specs=[pl.BlockSpec((tm, D_MODEL), lambda i: (i, 0)),
                  _const_spec((1, D_MODEL)),
                  pl.BlockSpec((1, 1, 6 * D_MODEL), lambda i: (i // tps, 0, 0)),
                  pl.BlockSpec((tm, RET_DK // 2), lambda i: (i % tps, 0)),
                  pl.BlockSpec((tm, RET_DK // 2), lambda i: (i % tps, 0)),
                  pl.BlockSpec((D_MODEL, COL_GATE), lambda i: (0, 0), pipeline_mode=pl.Buffered(1))],
        out_specs=[pl.BlockSpec((tm, PROJ_MAIN_TILES * tn), lambda i: (i, 0)), sub(d1), sub(d2)],
        scratch_shapes=[pltpu.VMEM((D_MODEL // LANES, tm, LANES), F32)] * 2
                       + [pltpu.VMEM((tm, D_MODEL), BF16)] * 3,
        compiler_params=_params(("arbitrary",)),
        name="proj",
    )(x2, norm_g, mod3, cos, sin, w_in)


def _ret_kernel(q_ref, k_ref, v_ref, gate_ref, cdec_ref, gng_ref, gnb_ref, o_ref, state, st_in):
    C = RET_C
    nc = q_ref.shape[1] // C
    cdec = cdec_ref[0]
    gng = gng_ref[0]
    gnb = gnb_ref[0]
    chunk = lambda ci: pl.ds(pl.multiple_of(ci * C, C), C)

    state[...] = jnp.zeros_like(state)
    st_in[0] = jnp.zeros(st_in.shape[1:], BF16)

    def advance(ci, carry):
        rows = chunk(ci)
        kv = lax.dot_general(k_ref[0, rows, :], v_ref[0, rows, :], (((0,), (0,)), ((), ())),
                             preferred_element_type=F32)
        new = (state[...] + kv) * cdec
        state[...] = new
        st_in[ci + 1] = new.astype(BF16)
        return carry

    lax.fori_loop(0, nc - 1, advance, 0, unroll=True)

    qi = lax.broadcasted_iota(jnp.int32, (C, C), 0)
    kj = lax.broadcasted_iota(jnp.int32, (C, C), 1)
    causal = qi >= kj

    def emit(ci, carry):
        rows = chunk(ci)
        q = q_ref[0, rows, :]
        s = lax.dot_general(q, k_ref[0, rows, :], (((1,), (1,)), ((), ())), preferred_element_type=F32)
        s = jnp.where(causal, s, 0.0).astype(BF16)
        o = (jnp.dot(s, v_ref[0, rows, :], preferred_element_type=F32)
             + jnp.dot(q, st_in[ci], preferred_element_type=F32))
        mu = jnp.mean(o, axis=-1, keepdims=True)
        oc = o - mu
        var = jnp.mean(oc * oc, axis=-1, keepdims=True)
        on = oc * lax.rsqrt(var + GN_EPS) * gng + gnb
        o_ref[0, rows, :] = (gate_ref[0, rows, :].astype(F32) * on).astype(BF16)
        return carry

    lax.fori_loop(0, nc, emit, 0, unroll=True)


def _retention(main3, gn_g, gn_b):
    B, S, _ = main3.shape
    H, C = RET_HEADS, RET_C
    assert S % C == 0
    cdec = jnp.exp(jnp.asarray(RET_LOG_GAMMA, F32) * C)[:, None, None]
    kq = RET_QK_W // RET_DK
    kv = 2 * RET_QK_W // RET_DV
    kg = kv + RET_V_W // RET_DV
    return pl.pallas_call(
        _ret_kernel,
        out_shape=jax.ShapeDtypeStruct((B, S, RET_V_W), BF16),
        grid=(B, H),
        in_specs=[pl.BlockSpec((1, S, RET_DK), lambda b, h: (b, 0, h)),
                  pl.BlockSpec((1, S, RET_DK), lambda b, h: (b, 0, kq + h)),
                  pl.BlockSpec((1, S, RET_DV), lambda b, h: (b, 0, kv + h)),
                  pl.BlockSpec((1, S, RET_DV), lambda b, h: (b, 0, kg + h)),
                  pl.BlockSpec((1, 1, 1), lambda b, h: (h, 0, 0)),
                  pl.BlockSpec((1, 1, RET_DV), lambda b, h: (h, 0, 0)),
                  pl.BlockSpec((1, 1, RET_DV), lambda b, h: (h, 0, 0))],
        out_specs=pl.BlockSpec((1, S, RET_DV), lambda b, h: (b, 0, h)),
        scratch_shapes=[pltpu.VMEM((RET_DK, RET_DV), F32), pltpu.VMEM((S // C, RET_DK, RET_DV), BF16)],
        compiler_params=_params(("arbitrary", "arbitrary")),
        name="ret",
    )(main3, main3, main3, main3, cdec, gn_g.reshape(H, 1, RET_DV), gn_b.reshape(H, 1, RET_DV))


def _t5_bucket(dist):
    max_exact = REL_BUCKETS // 2
    d_f = jnp.maximum(dist, 1).astype(F32)
    large = max_exact + (jnp.log(d_f / max_exact) / math.log(REL_MAX_DIST / max_exact)
                         * (REL_BUCKETS - max_exact)).astype(jnp.int32)
    large = jnp.minimum(large, REL_BUCKETS - 1)
    return jnp.where(dist < max_exact, dist, large)


def _attn_unit(n, q_ref, k_ref, v_ref, bias_scr, vaug_scr, o_ref, stats_ref):
    blk, dh = ATT_BLK, ATT_HEAD_DIM
    windowed = k_ref.shape[0] > blk
    if windowed:
        win = pl.ds(pl.multiple_of(jnp.maximum(n - 1, 0) * blk, blk), 2 * blk)
        first = (n == 0).astype(jnp.int32)
    heads = ATT_HEADS_PER_GROUP
    lane = lax.broadcasted_iota(jnp.int32, (blk, LSE_W), 1)
    stats = jnp.ones((blk, LSE_W), F32)
    for h in range(heads):
        hs = slice(h * dh, (h + 1) * dh)
        if windowed:
            kb, vb, bias = k_ref[win, hs], v_ref[win, hs], bias_scr[first, h]
        else:
            kb, vb, bias = k_ref[:, hs], v_ref[:, hs], bias_scr[h]
        vaug_scr[h, :, :dh] = vb
        s = lax.dot_general(q_ref[:, hs], kb, (((1,), (1,)), ((), ())), preferred_element_type=F32) + bias
        mx = jnp.max(s, axis=-1, keepdims=True)
        e = jnp.exp(s - mx).astype(BF16)
        o_den = jnp.dot(e, vaug_scr[h], preferred_element_type=F32)
        o_ref[:, hs] = o_den[:, :dh].astype(BF16)
        stats = jnp.where(lane == h, mx, jnp.where(lane == heads + h, o_den[:, dh:], stats))
    stats_ref[...] = stats


def _attn_kernel(tab_ref, q0, k0, v0, q1, k1, v1, q2, k2, v2, bk0, bk1, bk2,
                 o0, l0, o1, l1, o2, l2, bias0, bias1, bias2, vaug0, vaug1, vaug2, *, nb1, per_step):
    b = pl.program_id(0)
    i = pl.program_id(1)
    blk = ATT_BLK

    @pl.when((b == 0) & (i == 0))
    def _():
        for vaug in (vaug0, vaug1, vaug2):
            vaug[...] = jnp.ones(vaug.shape, BF16)
        qi = lax.broadcasted_iota(jnp.int32, (blk, 2 * blk), 0)
        kj = lax.broadcasted_iota(jnp.int32, (blk, 2 * blk), 1)
        m = blk + qi - kj
        band = (m >= 0) & (m <= blk)
        masked = jnp.full((blk, blk), NEG, F32)
        for gi, (bk_ref, bias_scr) in enumerate(((bk0, bias0), (bk1, bias1), (bk2, bias2))):
            bucket = bk_ref[...]
            for h in range(ATT_HEADS_PER_GROUP):
                col = gi * ATT_HEADS_PER_GROUP + h
                acc = jnp.zeros((blk, 2 * blk), F32)
                for t in range(REL_BUCKETS):
                    acc = jnp.where(bucket == t, tab_ref[t, col], acc)
                tile = jnp.where(band, acc, NEG)
                if len(bias_scr.shape) == 4:
                    bias_scr[0, h] = tile
                    bias_scr[1, h] = jnp.concatenate([tile[:, blk:], masked], axis=1)
                else:
                    bias_scr[h] = tile[:, blk:]

    for u in range(per_step):
        rows = pl.ds(u * blk, blk)
        n = i * per_step + u
        _attn_unit(n, q0.at[rows], k0, v0, bias0, vaug0.at[u], o0.at[rows], l0.at[rows])
        _attn_unit(n % nb1, q1.at[rows], k1, v1, bias1, vaug1.at[u], o1.at[rows], l1.at[rows])
        _attn_unit(0, q2.at[u], k2.at[u], v2.at[u], bias2, vaug2.at[u], o2.at[u], l2.at[u])


def _attention(main3, att1, att2, rel_bias):
    B, S, _ = main3.shape
    blk, gw = ATT_BLK, ATT_GROUP_W
    steps = S // blk
    lens = [S // dil for _, dil in ATT_GROUPS]
    nbs = [L // blk for L in lens]
    assert all(win // dil == blk for win, dil in ATT_GROUPS)
    assert nbs[0] == steps and nbs[1] >= 2 and nbs[2] == 1
    c0 = COL_ATT // gw
    nb1 = nbs[1]
    G = ATT_BLOCKS_PER_STEP
    assert nb1 % G == 0 and steps % G == 0
    sub1 = lambda b, i: (b, (i * G) // nb1)
    in_specs = [pl.BlockSpec(memory_space=pltpu.SMEM),
                pl.BlockSpec((None, G * blk, gw), lambda b, i: (b, i, c0)),
                pl.BlockSpec((None, S, gw), lambda b, i: (b, 0, c0 + 1)),
                pl.BlockSpec((None, S, gw), lambda b, i: (b, 0, c0 + 2)),
                pl.BlockSpec((None, None, G * blk, gw), lambda b, i: (*sub1(b, i), (i * G) % nb1 // G, 0)),
                pl.BlockSpec((None, None, lens[1], gw), lambda b, i: (*sub1(b, i), 0, 1)),
                pl.BlockSpec((None, None, lens[1], gw), lambda b, i: (*sub1(b, i), 0, 2)),
                pl.BlockSpec((None, G, blk, gw), lambda b, i: (b, i, 0, 0)),
                pl.BlockSpec((None, G, blk, gw), lambda b, i: (b, i, 0, 1)),
                pl.BlockSpec((None, G, blk, gw), lambda b, i: (b, i, 0, 2))]
    args = [rel_bias] + [main3] * 3 + [att1] * 3 + [att2] * 3
    qi = jnp.arange(blk)[:, None]
    kj = jnp.arange(2 * blk)[None, :]
    m = blk + qi - kj
    for _, dil in ATT_GROUPS:
        in_specs.append(_const_spec((blk, 2 * blk)))
        args.append(_t5_bucket(jnp.clip(m, 0, blk) * dil).astype(jnp.int32))
    out_specs, out_shapes = [], []
    for (_, dil), nb in zip(ATT_GROUPS, nbs):
        for w, dt in ((gw, BF16), (LSE_W, F32)):
            if nb == steps:
                out_specs.append(pl.BlockSpec((None, G * blk, w), lambda b, i: (b, i, 0)))
                out_shapes.append(jax.ShapeDtypeStruct((B, S, w), dt))
            elif nb == 1:
                out_specs.append(pl.BlockSpec((None, G, blk, w), lambda b, i: (b, i, 0, 0)))
                out_shapes.append(jax.ShapeDtypeStruct((B, dil, S // dil, w), dt))
            else:
                out_specs.append(pl.BlockSpec((None, None, G * blk, w),
                                              lambda b, i: (*sub1(b, i), (i * G) % nb1 // G, 0)))
                out_shapes.append(jax.ShapeDtypeStruct((B, dil, S // dil, w), dt))
    heads = ATT_HEADS_PER_GROUP
    outs = pl.pallas_call(
        functools.partial(_attn_kernel, nb1=nb1, per_step=G),
        out_shape=out_shapes,
        grid=(B, steps // G),
        in_specs=in_specs,
        out_specs=out_specs,
        scratch_shapes=[pltpu.VMEM((2, heads, blk, 2 * blk), F32), pltpu.VMEM((2, heads, blk, 2 * blk), F32),
                        pltpu.VMEM((heads, blk, blk), F32),
                        pltpu.VMEM((G, heads, 2 * blk, 2 * ATT_HEAD_DIM), BF16),
                        pltpu.VMEM((G, heads, 2 * blk, 2 * ATT_HEAD_DIM), BF16),
                        pltpu.VMEM((G, heads, blk, 2 * ATT_HEAD_DIM), BF16)],
        compiler_params=_params(("arbitrary", "arbitrary")),
        name="attn",
    )(*args)
    return outs[0::2], outs[1::2]


def _merge_kernel(x_ref, g_ref, mod_ref, retg_ref, o0_ref, o1_ref, o2_ref, s0_ref, s1_ref, s2_ref,
                  wg_ref, wr_ref, wa_ref, wo_ref, out_ref, o1_scr, o2_scr, s1_scr, s2_scr):
    tm = x_ref.shape[0]
    for dil, o_ref, s_ref, o_scr, s_scr in ((ATT_GROUPS[1][1], o1_ref, s1_ref, o1_scr, s1_scr),
                                            (ATT_GROUPS[2][1], o2_ref, s2_ref, o2_scr, s2_scr)):
        n = tm // dil
        for r in range(dil):
            s_scr[pl.ds(r, n, stride=dil), :] = s_ref[0, r]
            for h in range(ATT_HEADS_PER_GROUP):
                hs = slice(h * ATT_HEAD_DIM, (h + 1) * ATT_HEAD_DIM)
                o_scr[h, pl.ds(r, n, stride=dil), :] = o_ref[0, r, :, hs].astype(F32)
    part = tm // MERGE_ROW_PARTS
    for p in range(MERGE_ROW_PARTS):
        rows = slice(p * part, (p + 1) * part)
        ret_out = jnp.dot(retg_ref[rows, :], wr_ref[...], preferred_element_type=F32)
        x = x_ref[rows, :]
        h = _modulated_norm(x, g_ref[...], mod_ref, 0).astype(BF16)
        gates = jax.nn.sigmoid(jnp.dot(h, wg_ref[...], preferred_element_type=F32))
        stats = (s0_ref[rows, :], s1_scr[rows, :], s2_scr[rows, :])
        top = jnp.maximum(jnp.maximum(stats[0], stats[1]), stats[2])
        a = [jnp.exp(st - top) for st in stats]
        lane = lax.broadcasted_iota(jnp.int32, top.shape, 1)
        heads = ATT_HEADS_PER_GROUP
        den = sum(ag * pltpu.roll(st, LSE_W - heads, axis=1) for ag, st in zip(a, stats))
        inv = 1.0 / jnp.where(lane < heads, den, 1.0)
        w0, w1, w2 = (ag * inv for ag in a)
        parts = []
        for hd in range(heads):
            hs = slice(hd * ATT_HEAD_DIM, (hd + 1) * ATT_HEAD_DIM)
            parts.append(w0[:, hd:hd + 1] * o0_ref[rows, hs].astype(F32)
                         + w1[:, hd:hd + 1] * o1_scr[hd, rows, :]
                         + w2[:, hd:hd + 1] * o2_scr[hd, rows, :])
        att = jnp.concatenate(parts, axis=-1).astype(BF16)
        att_out = jnp.dot(att, wa_ref[...], preferred_element_type=F32)
        merged = gates[:, :D_MODEL] * ret_out + gates[:, D_MODEL:] * att_out
        y = jnp.dot(merged.astype(BF16), wo_ref[...], preferred_element_type=F32)
        out_ref[rows, :] = x + mod_ref[0, :, 2 * D_MODEL:3 * D_MODEL] * y


def _merge(x2, norm_g, mod3, retg2, o, lse, w_gate, w_ret_out, w_att_out, w_o, batch, seq):
    tm = ROW_TM
    tps = seq // tm
    row = lambda w: pl.BlockSpec((tm, w), lambda i: (i, 0))
    sub = lambda d, w: pl.BlockSpec((1, d, tm // d, w), lambda i: (i // tps, 0, i % tps, 0))
    d1, d2 = ATT_GROUPS[1][1], ATT_GROUPS[2][1]
    return pl.pallas_call(
        _merge_kernel,
        out_shape=jax.ShapeDtypeStruct((batch * seq, D_MODEL), F32),
        grid=(batch * seq // tm,),
        in_specs=[row(D_MODEL),
                  _const_spec((1, D_MODEL)),
                  pl.BlockSpec((1, 1, 6 * D_MODEL), lambda i: (i // tps, 0, 0)),
                  row(RET_V_W),
                  row(ATT_GROUP_W), sub(d1, ATT_GROUP_W), sub(d2, ATT_GROUP_W),
                  row(LSE_W), sub(d1, LSE_W), sub(d2, LSE_W),
                  pl.BlockSpec((pl.Element(D_MODEL), pl.Element(2 * D_MODEL)), lambda i: (0, COL_GATE),
                               pipeline_mode=pl.Buffered(1)),
                  _const_spec((RET_V_W, D_MODEL)),
                  _const_spec((ATT_GROUP_W, D_MODEL)),
                  _const_spec((D_MODEL, D_MODEL))],
        out_specs=row(D_MODEL),
        scratch_shapes=[pltpu.VMEM((ATT_HEADS_PER_GROUP, tm, ATT_HEAD_DIM), F32)] * 2
                       + [pltpu.VMEM((tm, LSE_W), F32)] * 2,
        compiler_params=_params(("arbitrary",)),
        name="merge",
    )(x2, norm_g, mod3, retg2, *o, *lse, w_gate, w_ret_out, w_att_out, w_o)


def _mlp_kernel(x_ref, mod_ref, g2_ref, gf_ref, w1_ref, w2_ref, out_ref, *, final):
    x = x_ref[...]
    h = _modulated_norm(x, g2_ref[...], mod_ref, 3).astype(BF16)
    u = jnp.maximum(jnp.dot(h, w1_ref[...], preferred_element_type=F32), 0.0)
    y = jnp.dot((u * u).astype(BF16), w2_ref[...], preferred_element_type=F32)
    x = x + mod_ref[0, :, 5 * D_MODEL:6 * D_MODEL] * y
    out_ref[...] = _rms(x, gf_ref[...]) if final else x


def _mlp(x2, mod3, norm2_g, norm_f_g, w1, w2, seq, final):
    rows = x2.shape[0]
    tm = ROW_TM
    tps = seq // tm
    return pl.pallas_call(
        functools.partial(_mlp_kernel, final=final),
        out_shape=jax.ShapeDtypeStruct((rows, D_MODEL), F32),
        grid=(rows // tm,),
        in_specs=[pl.BlockSpec((tm, D_MODEL), lambda i: (i, 0)),
                  pl.BlockSpec((1, 1, 6 * D_MODEL), lambda i: (i // tps, 0, 0)),
                  _const_spec((1, D_MODEL)),
                  _const_spec((1, D_MODEL)),
                  _const_spec((D_MODEL, D_FF)),
                  _const_spec((D_FF, D_MODEL))],
        out_specs=pl.BlockSpec((tm, D_MODEL), lambda i: (i, 0)),
        compiler_params=_params(("arbitrary",)),
        name="mlp",
    )(x2, mod3, norm2_g, norm_f_g, w1, w2)


def kernel(x, c, w_ada, b_ada, norm1_g, w_in, rel_bias, ret_gn_g, ret_gn_b, w_ret_out, w_att_out,
           w_o, norm2_g, w_ff1, w_ff2, norm_f_g):
    B, S, D = x.shape
    depth = w_ada.shape[0]
    half = RET_DK // 2
    inv = ROPE_BASE ** (-jnp.arange(half, dtype=F32) / half)
    ang = jnp.arange(S).astype(F32)[:, None] * inv[None, :]
    cos, sin = jnp.cos(ang), jnp.sin(ang)
    x2 = x.reshape(B * S, D)
    for l in range(depth):
        g1 = norm1_g[l].reshape(1, D)
        mod3 = _mod(c, w_ada[l], b_ada[l]).reshape(B, 1, 6 * D)
        w_in_bf = w_in[l].astype(BF16)
        main2, att1, att2 = _proj(x2, g1, mod3, cos, sin, w_in_bf, B, S)
        main3 = main2.reshape(B, S, main2.shape[1])
        retg = _retention(main3, ret_gn_g[l], ret_gn_b[l])
        o, lse = _attention(main3, att1, att2, rel_bias)
        o = [o[0].reshape(B * S, ATT_GROUP_W), o[1], o[2]]
        lse = [lse[0].reshape(B * S, LSE_W), lse[1], lse[2]]
        x2 = _merge(x2, g1, mod3, retg.reshape(B * S, RET_V_W), o, lse,
                    w_in_bf, w_ret_out[l].astype(BF16),
                    w_att_out[l].astype(BF16), w_o[l].astype(BF16), B, S)
        x2 = _mlp(x2, mod3, norm2_g[l].reshape(1, D), norm_f_g.reshape(1, D),
                  w_ff1[l].astype(BF16), w_ff2[l].astype(BF16), S, final=l == depth - 1)
    return x2.reshape(B, S, D)
```

```python
import functools
import math

import jax
import jax.numpy as jnp
import numpy as np
from jax import lax
from jax.experimental import pallas as pl
from jax.experimental.pallas import tpu as pltpu

F32 = jnp.float32
BF16 = jnp.bfloat16

D_MODEL = 1024
RET_HEADS = 4
RET_DK = 256
RET_DV = 512
RET_C = 256
RET_LOG_GAMMA = tuple(math.log1p(-(2.0 ** (-5.0 - h))) for h in range(RET_HEADS))
RET_QK_W = RET_HEADS * RET_DK
RET_V_W = RET_HEADS * RET_DV
ATT_GROUPS = ((128, 1), (512, 4), (2048, 16))
ATT_HEADS_PER_GROUP = 4
ATT_HEAD_DIM = 128
ATT_GROUP_W = ATT_HEADS_PER_GROUP * ATT_HEAD_DIM
ATT_BLK = 128
ATT_BLOCKS_PER_STEP = 4
LANES = 128
BF16_SUBLANES = 16
LSE_W = LANES
REL_BUCKETS = 32
REL_MAX_DIST = 2048
D_FF = 4 * D_MODEL
RMS_EPS = 1e-6
GN_EPS = 1e-5
ROPE_BASE = 10000.0
NEG = -1e30
ATT_QSCALE = ATT_HEAD_DIM ** -0.5

COL_ATT = 2 * RET_QK_W + 2 * RET_V_W
COL_GATE = COL_ATT + 9 * ATT_GROUP_W

VMEM_LIMIT = 56 * 1024 * 1024

PROJ_TM = 256
PROJ_TN = 3 * ATT_GROUP_W
PROJ_MAIN_TILES = (COL_ATT + PROJ_TN) // PROJ_TN
PROJ_LAST_PIECES = 3
ROW_TM = 512
MERGE_ROW_PARTS = 2


def _params(sem):
    return pltpu.CompilerParams(dimension_semantics=sem, vmem_limit_bytes=VMEM_LIMIT)


def _const_spec(shape):
    zeros = (0,) * len(shape)
    return pl.BlockSpec(shape, lambda *_: zeros, pipeline_mode=pl.Buffered(1))


def _silu(t):
    return t * jax.nn.sigmoid(t)


def _rms(x, g):
    return x * lax.rsqrt(jnp.mean(x * x, axis=-1, keepdims=True) + RMS_EPS) * g


def _modulated_norm(x, g, mod_ref, k):
    shift = mod_ref[0, :, k * D_MODEL:(k + 1) * D_MODEL]
    scale = mod_ref[0, :, (k + 1) * D_MODEL:(k + 2) * D_MODEL]
    return _rms(x, g) * (1.0 + scale) + shift


def _mod_kernel(c_ref, w_ref, b_ref, o_ref):
    o_ref[...] = jnp.dot(_silu(c_ref[...]), w_ref[...], preferred_element_type=F32) + b_ref[...]


def _mod(c, w_ada, b_ada):
    B = c.shape[0]
    n = w_ada.shape[1]
    tn = D_MODEL
    return pl.pallas_call(
        _mod_kernel,
        out_shape=jax.ShapeDtypeStruct((B, n), F32),
        grid=(n // tn,),
        in_specs=[pl.BlockSpec((B, D_MODEL), lambda j: (0, 0)),
                  pl.BlockSpec((D_MODEL, tn), lambda j: (0, j)),
                  pl.BlockSpec((1, tn), lambda j: (0, j))],
        out_specs=pl.BlockSpec((B, tn), lambda j: (0, j)),
        compiler_params=_params(("arbitrary",)),
        name="mod",
    )(c, w_ada, b_ada.reshape(1, n))


def _proj_main_layout():
    spans = [(h * RET_DK, (h + 1) * RET_DK, "rot", (h % RET_HEADS, h >= RET_HEADS))
             for h in range(2 * RET_HEADS)]
    spans += [(2 * RET_QK_W, 2 * RET_QK_W + RET_V_W, "copy", None),
              (2 * RET_QK_W + RET_V_W, COL_ATT, "silu", None),
              (COL_ATT, COL_ATT + ATT_GROUP_W, "scale", ATT_QSCALE),
              (COL_ATT + ATT_GROUP_W, PROJ_MAIN_TILES * PROJ_TN, "copy", None)]
    tiles = [[] for _ in range(PROJ_MAIN_TILES)]
    for c0, c1, kind, arg in spans:
        for t in range(c0 // PROJ_TN, (c1 - 1) // PROJ_TN + 1):
            lo, hi = max(c0, t * PROJ_TN), min(c1, (t + 1) * PROJ_TN)
            assert kind != "rot" or (lo, hi) == (c0, c1)
            tiles[t].append((lo - t * PROJ_TN, hi - t * PROJ_TN, kind, arg))
    return tiles


PROJ_MAIN_LAYOUT = _proj_main_layout()


def _proj_kernel(x_ref, mod_ref, g_ref, cos_ref, sin_ref, w_ref, *refs, tps, n_cast):
    cast_in, (main_ref, a1_ref, a2_ref) = refs[:n_cast], refs[n_cast:n_cast + 3]
    cast_out = refs[n_cast + 3:2 * n_cast + 3]
    hf_scr, hd1_scr = refs[2 * n_cast + 3:2 * n_cast + 5]
    lhs = refs[2 * n_cast + 5:2 * n_cast + 8]
    tm = x_ref.shape[0]
    step = pl.program_id(0)

    for src, dst in zip(cast_in, cast_out):
        dst[...] = src[...].astype(BF16)

    def prepare(x_ref, mod_ref, lhs):
        h0_scr, h1_scr, h2_scr = lhs
        hf = _modulated_norm(x_ref[...], g_ref[...], mod_ref, 0)
        h0_scr[...] = hf.astype(BF16)
        d1, d2 = ATT_GROUPS[1][1], ATT_GROUPS[2][1]
        n1, n2 = tm // d1, tm // d2
        for c in range(D_MODEL // LANES):
            cs = slice(c * LANES, (c + 1) * LANES)
            hf_scr[c] = hf[:, cs]
            for r in range(d1):
                part = hf_scr[c, pl.ds(r, n1, stride=d1), :]
                hd1_scr[c, r * n1:(r + 1) * n1, :] = part
                h1_scr[r * n1:(r + 1) * n1, cs] = part.astype(BF16)
            for r in range(d2):
                src = pl.ds((r % d1) * n1 + r // d1, n2, stride=d1)
                h2_scr[r * n2:(r + 1) * n2, cs] = hd1_scr[c, src, :].astype(BF16)

    def w_tile(t):
        return w_ref[:, t * PROJ_TN:(t + 1) * PROJ_TN]

    def main_tile(t, segments, h0_scr, pieces=1):
        if pieces > 1:
            width = PROJ_TN // pieces
            for p in range(pieces):
                p0, p1 = p * width, (p + 1) * width
                acc = jnp.dot(h0_scr[...], w_ref[:, t * PROJ_TN + p0:t * PROJ_TN + p1],
                              preferred_element_type=F32)
                for c0, c1, kind, arg in segments:
                    lo, hi = max(c0, p0), min(c1, p1)
                    if lo < hi:
                        assert kind in ("scale", "copy")
                        seg = acc[:, lo - p0:hi - p0]
                        seg = seg * arg if kind == "scale" else seg
                        main_ref[:, t * PROJ_TN + lo:t * PROJ_TN + hi] = seg.astype(BF16)
            return
        acc = jnp.dot(h0_scr[...], w_tile(t), preferred_element_type=F32)
        if any(kind == "rot" for _, _, kind, _ in segments):
            cos, sin = cos_ref[...], sin_ref[...]
            row = lax.broadcasted_iota(jnp.int32, cos.shape, 0)
            pos = (((step % tps) * tm + row) % RET_C + 1).astype(F32)
        base = t * PROJ_TN
        for c0, c1, kind, arg in segments:
            seg = acc[:, c0:c1]
            if kind == "rot":
                head, is_key = arg
                rate = -RET_LOG_GAMMA[head] if is_key else RET_LOG_GAMMA[head]
                dec = jnp.exp(rate * pos) * ((RET_DK ** -0.5) if is_key else 1.0)
                cd, sd = cos * dec, sin * dec
                cm = (c0 + c1) // 2
                t1, t2 = acc[:, c0:cm], acc[:, cm:c1]
                main_ref[:, base + c0:base + cm] = (t1 * cd - t2 * sd).astype(BF16)
                main_ref[:, base + cm:base + c1] = (t1 * sd + t2 * cd).astype(BF16)
            elif kind == "silu":
                main_ref[:, base + c0:base + c1] = _silu(seg).astype(BF16)
            elif kind == "scale":
                main_ref[:, base + c0:base + c1] = (seg * arg).astype(BF16)
            else:
                main_ref[:, base + c0:base + c1] = seg.astype(BF16)

    def multiply(lhs):
        h0_scr, h1_scr, h2_scr = lhs
        for t, (dil, h_scr, a_ref) in enumerate(((ATT_GROUPS[1][1], h1_scr, a1_ref),
                                                 (ATT_GROUPS[2][1], h2_scr, a2_ref))):
            n = tm // dil
            acc = jnp.dot(h_scr[...], w_tile(PROJ_MAIN_TILES + t), preferred_element_type=F32)
            gw = ATT_GROUP_W
            for r in range(dil):
                rows = slice(r * n, (r + 1) * n)
                a_ref[0, r, :, :gw] = (acc[rows, :gw] * ATT_QSCALE).astype(BF16)
                a_ref[0, r, :, gw:] = acc[rows, gw:].astype(BF16)
        last = len(PROJ_MAIN_LAYOUT) - 1
        for t, segments in enumerate(PROJ_MAIN_LAYOUT):
            main_tile(t, segments, h0_scr, pieces=PROJ_LAST_PIECES if t == last else 1)

    prepare(x_ref, mod_ref, lhs)
    multiply(lhs)


def _proj(x2, norm_g, mod3, cos, sin, w_in, cast_weights, batch, seq):
    tm, tn = PROJ_TM, PROJ_TN
    tps = seq // tm
    steps = batch * seq // tm
    assert seq % tm == 0 and tm % RET_C == 0
    d1, d2 = ATT_GROUPS[1][1], ATT_GROUPS[2][1]
    assert COL_GATE == (PROJ_MAIN_TILES + 2) * tn
    sub = lambda d: pl.BlockSpec((1, d, tm // d, tn), lambda i: (i // tps, 0, i % tps, 0))
    slabs = []
    for w in cast_weights:
        assert w.shape[0] % (steps * BF16_SUBLANES) == 0
        slabs.append(pl.BlockSpec((w.shape[0] // steps, w.shape[1]), lambda i: (i, 0)))
    outs = pl.pallas_call(
        functools.partial(_proj_kernel, tps=tps, n_cast=len(cast_weights)),
        out_shape=[jax.ShapeDtypeStruct((batch * seq, PROJ_MAIN_TILES * tn), BF16),
                   jax.ShapeDtypeStruct((batch, d1, seq // d1, tn), BF16),
                   jax.ShapeDtypeStruct((batch, d2, seq // d2, tn), BF16)]
                  + [jax.ShapeDtypeStruct(w.shape, BF16) for w in cast_weights],
        grid=(steps,),
        in_specs=[pl.BlockSpec((tm, D_MODEL), lambda i: (i, 0)),
                  pl.BlockSpec((1, 1, 6 * D_MODEL), lambda i: (i // tps, 0, 0)),
                  _const_spec((1, D_MODEL)),
                  pl.BlockSpec((tm, RET_DK // 2), lambda i: (i % tps, 0)),
                  pl.BlockSpec((tm, RET_DK // 2), lambda i: (i % tps, 0)),
                  pl.BlockSpec((D_MODEL, COL_GATE), lambda i: (0, 0), pipeline_mode=pl.Buffered(1))]
                 + slabs,
        out_specs=[pl.BlockSpec((tm, PROJ_MAIN_TILES * tn), lambda i: (i, 0)), sub(d1), sub(d2)] + slabs,
        scratch_shapes=[pltpu.VMEM((D_MODEL // LANES, tm, LANES), F32)] * 2
                       + [pltpu.VMEM((tm, D_MODEL), BF16)] * 3,
        compiler_params=_params(("arbitrary",)),
        name="proj",
    )(x2, mod3, norm_g, cos, sin, w_in, *cast_weights)
    return outs[0], outs[1], outs[2], outs[3:]


def _ret_kernel(q_ref, k_ref, v_ref, gate_ref, cdec_ref, gng_ref, gnb_ref, o_ref, state, st_in):
    C = RET_C
    nc = q_ref.shape[1] // C
    cdec = cdec_ref[0]
    gng = gng_ref[0]
    gnb = gnb_ref[0]
    chunk = lambda ci: pl.ds(pl.multiple_of(ci * C, C), C)

    state[...] = jnp.zeros_like(state)
    st_in[0] = jnp.zeros(st_in.shape[1:], BF16)

    def advance(ci, carry):
        rows = chunk(ci)
        kv = lax.dot_general(k_ref[0, rows, :], v_ref[0, rows, :], (((0,), (0,)), ((), ())),
                             preferred_element_type=F32)
        new = (state[...] + kv) * cdec
        state[...] = new
        st_in[ci + 1] = new.astype(BF16)
        return carry

    lax.fori_loop(0, nc - 1, advance, 0, unroll=True)

    qi = lax.broadcasted_iota(jnp.int32, (C, C), 0)
    kj = lax.broadcasted_iota(jnp.int32, (C, C), 1)
    causal = qi >= kj

    def emit(ci, carry):
        rows = chunk(ci)
        q = q_ref[0, rows, :]
        s = lax.dot_general(q, k_ref[0, rows, :], (((1,), (1,)), ((), ())), preferred_element_type=F32)
        s = jnp.where(causal, s, 0.0).astype(BF16)
        o = (jnp.dot(s, v_ref[0, rows, :], preferred_element_type=F32)
             + jnp.dot(q, st_in[ci], preferred_element_type=F32))
        mu = jnp.mean(o, axis=-1, keepdims=True)
        oc = o - mu
        var = jnp.mean(oc * oc, axis=-1, keepdims=True)
        on = oc * lax.rsqrt(var + GN_EPS) * gng + gnb
        o_ref[0, rows, :] = (gate_ref[0, rows, :].astype(F32) * on).astype(BF16)
        return carry

    lax.fori_loop(0, nc, emit, 0, unroll=True)


def _retention(main3, gn_g, gn_b):
    B, S, _ = main3.shape
    H, C = RET_HEADS, RET_C
    assert S % C == 0
    cdec = jnp.exp(jnp.asarray(RET_LOG_GAMMA, F32) * C)[:, None, None]
    kq = RET_QK_W // RET_DK
    kv = 2 * RET_QK_W // RET_DV
    kg = kv + RET_V_W // RET_DV
    return pl.pallas_call(
        _ret_kernel,
        out_shape=jax.ShapeDtypeStruct((B, S, RET_V_W), BF16),
        grid=(B, H),
        in_specs=[pl.BlockSpec((1, S, RET_DK), lambda b, h: (b, 0, h)),
                  pl.BlockSpec((1, S, RET_DK), lambda b, h: (b, 0, kq + h)),
                  pl.BlockSpec((1, S, RET_DV), lambda b, h: (b, 0, kv + h)),
                  pl.BlockSpec((1, S, RET_DV), lambda b, h: (b, 0, kg + h)),
                  pl.BlockSpec((1, 1, 1), lambda b, h: (h, 0, 0)),
                  pl.BlockSpec((1, 1, RET_DV), lambda b, h: (h, 0, 0)),
                  pl.BlockSpec((1, 1, RET_DV), lambda b, h: (h, 0, 0))],
        out_specs=pl.BlockSpec((1, S, RET_DV), lambda b, h: (b, 0, h)),
        scratch_shapes=[pltpu.VMEM((RET_DK, RET_DV), F32), pltpu.VMEM((S // C, RET_DK, RET_DV), BF16)],
        compiler_params=_params(("arbitrary", "arbitrary")),
        name="ret",
    )(main3, main3, main3, main3, cdec, gn_g.reshape(H, 1, RET_DV), gn_b.reshape(H, 1, RET_DV))


def _t5_bucket(dist):
    dist = np.asarray(dist)
    max_exact = REL_BUCKETS // 2
    d_f = np.maximum(dist, 1).astype(np.float32)
    large = max_exact + (np.log(d_f / np.float32(max_exact)) / np.float32(math.log(REL_MAX_DIST / max_exact))
                         * np.float32(REL_BUCKETS - max_exact)).astype(np.int32)
    large = np.minimum(large, REL_BUCKETS - 1)
    return np.where(dist < max_exact, dist, large).astype(np.int32)


def _attn_unit(n, q_ref, k_ref, v_ref, bias_scr, vaug_scr, o_ref, stats_ref):
    blk, dh = ATT_BLK, ATT_HEAD_DIM
    windowed = k_ref.shape[0] > blk
    if windowed:
        win = pl.ds(pl.multiple_of(jnp.maximum(n - 1, 0) * blk, blk), 2 * blk)
        first = (n == 0).astype(jnp.int32)
    heads = ATT_HEADS_PER_GROUP
    lane = lax.broadcasted_iota(jnp.int32, (blk, LSE_W), 1)
    stats = jnp.ones((blk, LSE_W), F32)
    for h in range(heads):
        hs = slice(h * dh, (h + 1) * dh)
        if windowed:
            kb, vb, bias = k_ref[win, hs], v_ref[win, hs], bias_scr[first, h]
        else:
            kb, vb, bias = k_ref[:, hs], v_ref[:, hs], bias_scr[h]
        vaug_scr[h, :, :dh] = vb
        s = lax.dot_general(q_ref[:, hs], kb, (((1,), (1,)), ((), ())), preferred_element_type=F32) + bias
        mx = jnp.max(s, axis=-1, keepdims=True)
        e = jnp.exp(s - mx).astype(BF16)
        o_den = jnp.dot(e, vaug_scr[h], preferred_element_type=F32)
        o_ref[:, hs] = o_den[:, :dh].astype(BF16)
        stats = jnp.where(lane == h, mx, jnp.where(lane == heads + h, o_den[:, dh:], stats))
    stats_ref[...] = stats


def _attn_kernel(tab_ref, q0, k0, v0, q1, k1, v1, q2, k2, v2, bk0, bk1, bk2,
                 o0, l0, o1, l1, o2, l2, bias0, bias1, bias2, vaug0, vaug1, vaug2, *, nb1, per_step):
    b = pl.program_id(0)
    i = pl.program_id(1)
    blk = ATT_BLK

    @pl.when((b == 0) & (i == 0))
    def _():
        for vaug in (vaug0, vaug1, vaug2):
            vaug[...] = jnp.ones(vaug.shape, BF16)
        qi = lax.broadcasted_iota(jnp.int32, (blk, 2 * blk), 0)
        kj = lax.broadcasted_iota(jnp.int32, (blk, 2 * blk), 1)
        m = blk + qi - kj
        band = (m >= 0) & (m <= blk)
        masked = jnp.full((blk, blk), NEG, F32)
        for gi, (bk_ref, bias_scr) in enumerate(((bk0, bias0), (bk1, bias1), (bk2, bias2))):
            bucket = bk_ref[...]
            for h in range(ATT_HEADS_PER_GROUP):
                col = gi * ATT_HEADS_PER_GROUP + h
                acc = jnp.zeros((blk, 2 * blk), F32)
                for t in range(REL_BUCKETS):
                    acc = jnp.where(bucket == t, tab_ref[t, col], acc)
                tile = jnp.where(band, acc, NEG)
                if len(bias_scr.shape) == 4:
                    bias_scr[0, h] = tile
                    bias_scr[1, h] = jnp.concatenate([tile[:, blk:], masked], axis=1)
                else:
                    bias_scr[h] = tile[:, blk:]

    for u in range(per_step):
        rows = pl.ds(u * blk, blk)
        n = i * per_step + u
        _attn_unit(n, q0.at[rows], k0, v0, bias0, vaug0.at[u], o0.at[rows], l0.at[rows])
        _attn_unit(n % nb1, q1.at[rows], k1, v1, bias1, vaug1.at[u], o1.at[rows], l1.at[rows])
        _attn_unit(0, q2.at[u], k2.at[u], v2.at[u], bias2, vaug2.at[u], o2.at[u], l2.at[u])


def _attention(main3, att1, att2, rel_bias):
    B, S, _ = main3.shape
    blk, gw = ATT_BLK, ATT_GROUP_W
    steps = S // blk
    lens = [S // dil for _, dil in ATT_GROUPS]
    nbs = [L // blk for L in lens]
    assert all(win // dil == blk for win, dil in ATT_GROUPS)
    assert nbs[0] == steps and nbs[1] >= 2 and nbs[2] == 1
    c0 = COL_ATT // gw
    nb1 = nbs[1]
    G = ATT_BLOCKS_PER_STEP
    assert nb1 % G == 0 and steps % G == 0
    sub1 = lambda b, i: (b, (i * G) // nb1)
    in_specs = [pl.BlockSpec(memory_space=pltpu.SMEM),
                pl.BlockSpec((None, G * blk, gw), lambda b, i: (b, i, c0)),
                pl.BlockSpec((None, S, gw), lambda b, i: (b, 0, c0 + 1)),
                pl.BlockSpec((None, S, gw), lambda b, i: (b, 0, c0 + 2)),
                pl.BlockSpec((None, None, G * blk, gw), lambda b, i: (*sub1(b, i), (i * G) % nb1 // G, 0)),
                pl.BlockSpec((None, None, lens[1], gw), lambda b, i: (*sub1(b, i), 0, 1)),
                pl.BlockSpec((None, None, lens[1], gw), lambda b, i: (*sub1(b, i), 0, 2)),
                pl.BlockSpec((None, G, blk, gw), lambda b, i: (b, i, 0, 0)),
                pl.BlockSpec((None, G, blk, gw), lambda b, i: (b, i, 0, 1)),
                pl.BlockSpec((None, G, blk, gw), lambda b, i: (b, i, 0, 2))]
    args = [rel_bias] + [main3] * 3 + [att1] * 3 + [att2] * 3
    qi = np.arange(blk)[:, None]
    kj = np.arange(2 * blk)[None, :]
    m = blk + qi - kj
    for _, dil in ATT_GROUPS:
        in_specs.append(_const_spec((blk, 2 * blk)))
        args.append(jnp.asarray(_t5_bucket(np.clip(m, 0, blk) * dil)))
    out_specs, out_shapes = [], []
    for (_, dil), nb in zip(ATT_GROUPS, nbs):
        for w, dt in ((gw, BF16), (LSE_W, F32)):
            if nb == steps:
                out_specs.append(pl.BlockSpec((None, G * blk, w), lambda b, i: (b, i, 0)))
                out_shapes.append(jax.ShapeDtypeStruct((B, S, w), dt))
            elif nb == 1:
                out_specs.append(pl.BlockSpec((None, G, blk, w), lambda b, i: (b, i, 0, 0)))
                out_shapes.append(jax.ShapeDtypeStruct((B, dil, S // dil, w), dt))
            else:
                out_specs.append(pl.BlockSpec((None, None, G * blk, w),
                                              lambda b, i: (*sub1(b, i), (i * G) % nb1 // G, 0)))
                out_shapes.append(jax.ShapeDtypeStruct((B, dil, S // dil, w), dt))
    heads = ATT_HEADS_PER_GROUP
    outs = pl.pallas_call(
        functools.partial(_attn_kernel, nb1=nb1, per_step=G),
        out_shape=out_shapes,
        grid=(B, steps // G),
        in_specs=in_specs,
        out_specs=out_specs,
        scratch_shapes=[pltpu.VMEM((2, heads, blk, 2 * blk), F32), pltpu.VMEM((2, heads, blk, 2 * blk), F32),
                        pltpu.VMEM((heads, blk, blk), F32),
                        pltpu.VMEM((G, heads, 2 * blk, 2 * ATT_HEAD_DIM), BF16),
                        pltpu.VMEM((G, heads, 2 * blk, 2 * ATT_HEAD_DIM), BF16),
                        pltpu.VMEM((G, heads, blk, 2 * ATT_HEAD_DIM), BF16)],
        compiler_params=_params(("arbitrary", "arbitrary")),
        name="attn",
    )(*args)
    return outs[0::2], outs[1::2]


def _merge_kernel(x_ref, g_ref, mod_ref, retg_ref, o0_ref, o1_ref, o2_ref, s0_ref, s1_ref, s2_ref,
                  wg_ref, wr_ref, wa_ref, wo_ref, out_ref, o1_scr, o2_scr, s1_scr, s2_scr):
    tm = x_ref.shape[0]
    for dil, o_ref, s_ref, o_scr, s_scr in ((ATT_GROUPS[1][1], o1_ref, s1_ref, o1_scr, s1_scr),
                                            (ATT_GROUPS[2][1], o2_ref, s2_ref, o2_scr, s2_scr)):
        n = tm // dil
        for r in range(dil):
            s_scr[pl.ds(r, n, stride=dil), :] = s_ref[0, r]
            for h in range(ATT_HEADS_PER_GROUP):
                hs = slice(h * ATT_HEAD_DIM, (h + 1) * ATT_HEAD_DIM)
                o_scr[h, pl.ds(r, n, stride=dil), :] = o_ref[0, r, :, hs].astype(F32)
    part = tm // MERGE_ROW_PARTS
    for p in range(MERGE_ROW_PARTS):
        rows = slice(p * part, (p + 1) * part)
        ret_out = jnp.dot(retg_ref[rows, :], wr_ref[...], preferred_element_type=F32)
        x = x_ref[rows, :]
        h = _modulated_norm(x, g_ref[...], mod_ref, 0).astype(BF16)
        gates = jax.nn.sigmoid(jnp.dot(h, wg_ref[...], preferred_element_type=F32))
        stats = (s0_ref[rows, :], s1_scr[rows, :], s2_scr[rows, :])
        top = jnp.maximum(jnp.maximum(stats[0], stats[1]), stats[2])
        a = [jnp.exp(st - top) for st in stats]
        lane = lax.broadcasted_iota(jnp.int32, top.shape, 1)
        heads = ATT_HEADS_PER_GROUP
        den = sum(ag * pltpu.roll(st, LSE_W - heads, axis=1) for ag, st in zip(a, stats))
        inv = 1.0 / jnp.where(lane < heads, den, 1.0)
        w0, w1, w2 = (ag * inv for ag in a)
        parts = []
        for hd in range(heads):
            hs = slice(hd * ATT_HEAD_DIM, (hd + 1) * ATT_HEAD_DIM)
            parts.append(w0[:, hd:hd + 1] * o0_ref[rows, hs].astype(F32)
                         + w1[:, hd:hd + 1] * o1_scr[hd, rows, :]
                         + w2[:, hd:hd + 1] * o2_scr[hd, rows, :])
        att = jnp.concatenate(parts, axis=-1).astype(BF16)
        att_out = jnp.dot(att, wa_ref[...], preferred_element_type=F32)
        merged = gates[:, :D_MODEL] * ret_out + gates[:, D_MODEL:] * att_out
        y = jnp.dot(merged.astype(BF16), wo_ref[...], preferred_element_type=F32)
        out_ref[rows, :] = x + mod_ref[0, :, 2 * D_MODEL:3 * D_MODEL] * y


def _merge(x2, norm_g, mod3, retg2, o, lse, w_gate, w_ret_out, w_att_out, w_o, batch, seq):
    tm = ROW_TM
    tps = seq // tm
    row = lambda w: pl.BlockSpec((tm, w), lambda i: (i, 0))
    sub = lambda d, w: pl.BlockSpec((1, d, tm // d, w), lambda i: (i // tps, 0, i % tps, 0))
    d1, d2 = ATT_GROUPS[1][1], ATT_GROUPS[2][1]
    return pl.pallas_call(
        _merge_kernel,
        out_shape=jax.ShapeDtypeStruct((batch * seq, D_MODEL), F32),
        grid=(batch * seq // tm,),
        in_specs=[row(D_MODEL),
                  _const_spec((1, D_MODEL)),
                  pl.BlockSpec((1, 1, 6 * D_MODEL), lambda i: (i // tps, 0, 0)),
                  row(RET_V_W),
                  row(ATT_GROUP_W), sub(d1, ATT_GROUP_W), sub(d2, ATT_GROUP_W),
                  row(LSE_W), sub(d1, LSE_W), sub(d2, LSE_W),
                  pl.BlockSpec((pl.Element(D_MODEL), pl.Element(2 * D_MODEL)), lambda i: (0, COL_GATE),
                               pipeline_mode=pl.Buffered(1)),
                  _const_spec((RET_V_W, D_MODEL)),
                  _const_spec((ATT_GROUP_W, D_MODEL)),
                  _const_spec((D_MODEL, D_MODEL))],
        out_specs=row(D_MODEL),
        scratch_shapes=[pltpu.VMEM((ATT_HEADS_PER_GROUP, tm, ATT_HEAD_DIM), F32)] * 2
                       + [pltpu.VMEM((tm, LSE_W), F32)] * 2,
        compiler_params=_params(("arbitrary",)),
        name="merge",
    )(x2, norm_g, mod3, retg2, *o, *lse, w_gate, w_ret_out, w_att_out, w_o)


def _mlp_kernel(x_ref, mod_ref, g2_ref, gf_ref, w1_ref, w2_ref, out_ref, *, final):
    x = x_ref[...]
    h = _modulated_norm(x, g2_ref[...], mod_ref, 3).astype(BF16)
    u = jnp.maximum(jnp.dot(h, w1_ref[...], preferred_element_type=F32), 0.0)
    y = jnp.dot((u * u).astype(BF16), w2_ref[...], preferred_element_type=F32)
    x = x + mod_ref[0, :, 5 * D_MODEL:6 * D_MODEL] * y
    out_ref[...] = _rms(x, gf_ref[...]) if final else x


def _mlp(x2, mod3, norm2_g, norm_f_g, w1, w2, seq, final):
    rows = x2.shape[0]
    tm = ROW_TM
    tps = seq // tm
    return pl.pallas_call(
        functools.partial(_mlp_kernel, final=final),
        out_shape=jax.ShapeDtypeStruct((rows, D_MODEL), F32),
        grid=(rows // tm,),
        in_specs=[pl.BlockSpec((tm, D_MODEL), lambda i: (i, 0)),
                  pl.BlockSpec((1, 1, 6 * D_MODEL), lambda i: (i // tps, 0, 0)),
                  _const_spec((1, D_MODEL)),
                  _const_spec((1, D_MODEL)),
                  _const_spec((D_MODEL, D_FF)),
                  _const_spec((D_FF, D_MODEL))],
        out_specs=pl.BlockSpec((tm, D_MODEL), lambda i: (i, 0)),
        compiler_params=_params(("arbitrary",)),
        name="mlp",
    )(x2, mod3, norm2_g, norm_f_g, w1, w2)


def kernel(x, c, w_ada, b_ada, norm1_g, w_in, rel_bias, ret_gn_g, ret_gn_b, w_ret_out, w_att_out,
           w_o, norm2_g, w_ff1, w_ff2, norm_f_g):
    B, S, D = x.shape
    depth = w_ada.shape[0]
    half = RET_DK // 2
    inv = ROPE_BASE ** (-jnp.arange(half, dtype=F32) / half)
    ang = jnp.arange(S).astype(F32)[:, None] * inv[None, :]
    cos, sin = jnp.cos(ang), jnp.sin(ang)
    x2 = x.reshape(B * S, D)
    for l in range(depth):
        g1 = norm1_g[l].reshape(1, D)
        mod3 = _mod(c, w_ada[l], b_ada[l]).reshape(B, 1, 6 * D)
        w_in_bf = w_in[l].astype(BF16)
        main2, att1, att2, (w_ret_bf, w_o_bf, w_ff1_bf, w_ff2_bf) = _proj(
            x2, g1, mod3, cos, sin, w_in_bf, [w_ret_out[l], w_o[l], w_ff1[l], w_ff2[l]], B, S)
        main3 = main2.reshape(B, S, main2.shape[1])
        retg = _retention(main3, ret_gn_g[l], ret_gn_b[l])
        o, stats = _attention(main3, att1, att2, rel_bias)
        o = [o[0].reshape(B * S, ATT_GROUP_W), o[1], o[2]]
        stats = [stats[0].reshape(B * S, LSE_W), stats[1], stats[2]]
        x2 = _merge(x2, g1, mod3, retg.reshape(B * S, RET_V_W), o, stats,
                    w_in_bf, w_ret_bf, w_att_out[l].astype(BF16), w_o_bf, B, S)
        x2 = _mlp(x2, mod3, norm2_g[l].reshape(1, D), norm_f_g.reshape(1, D),
                  w_ff1_bf, w_ff2_bf, S, final=l == depth - 1)
    return x2.reshape(B, S, D)
```

```python
import functools
import math
from typing import NamedTuple

import jax
import jax.numpy as jnp
import numpy as np
from jax import lax
from jax.experimental import pallas as pl
from jax.experimental.pallas import tpu as pltpu

F32 = jnp.float32
BF16 = jnp.bfloat16

D_MODEL = 1024
RET_HEADS = 4
RET_DK = 256
RET_DV = 512
RET_C = 256
RET_LOG_GAMMA = tuple(math.log1p(-(2.0 ** (-5.0 - h))) for h in range(RET_HEADS))
RET_QK_W = RET_HEADS * RET_DK
RET_V_W = RET_HEADS * RET_DV
ATT_GROUPS = ((128, 1), (512, 4), (2048, 16))
ATT_HEADS_PER_GROUP = 4
ATT_HEAD_DIM = 128
ATT_GROUP_W = ATT_HEADS_PER_GROUP * ATT_HEAD_DIM
ATT_BLK = 128
ATT_BLOCKS_PER_STEP = 4
LANES = 128
BF16_SUBLANES = 16
LSE_W = LANES
REL_BUCKETS = 32
REL_MAX_DIST = 2048
D_FF = 4 * D_MODEL
RMS_EPS = 1e-6
GN_EPS = 1e-5
ROPE_BASE = 10000.0
NEG = -1e30
ATT_QSCALE = ATT_HEAD_DIM ** -0.5

COL_ATT = 2 * RET_QK_W + 2 * RET_V_W
COL_GATE = COL_ATT + 9 * ATT_GROUP_W

VMEM_LIMIT = 56 * 1024 * 1024

PROJ_TM = 256
PROJ_TN = 3 * ATT_GROUP_W
PROJ_MAIN_TILES = (COL_ATT + PROJ_TN) // PROJ_TN
PROJ_LAST_PIECES = 3
ROW_TM = 512
MERGE_ROW_PARTS = 2
MLP_TM = 1024
MLP_ROW_PARTS = 2


def _params(sem):
    return pltpu.CompilerParams(dimension_semantics=sem, vmem_limit_bytes=VMEM_LIMIT)


def _const_spec(shape):
    zeros = (0,) * len(shape)
    return pl.BlockSpec(shape, lambda *_: zeros, pipeline_mode=pl.Buffered(1))


def _silu(t):
    return t * jax.nn.sigmoid(t)


def _rms(x, g):
    return x * lax.rsqrt(jnp.mean(x * x, axis=-1, keepdims=True) + RMS_EPS) * g


def _modulated_norm(x, g, mod_ref, k):
    shift = mod_ref[0, :, k * D_MODEL:(k + 1) * D_MODEL]
    scale = mod_ref[0, :, (k + 1) * D_MODEL:(k + 2) * D_MODEL]
    return _rms(x, g) * (1.0 + scale) + shift


def _mod_kernel(c_ref, w_ref, b_ref, o_ref):
    o_ref[...] = jnp.dot(_silu(c_ref[...]), w_ref[...], preferred_element_type=F32) + b_ref[...]


def _mod(c, w_ada, b_ada):
    B = c.shape[0]
    n = w_ada.shape[1]
    tn = D_MODEL
    return pl.pallas_call(
        _mod_kernel,
        out_shape=jax.ShapeDtypeStruct((B, n), F32),
        grid=(n // tn,),
        in_specs=[pl.BlockSpec((B, D_MODEL), lambda j: (0, 0)),
                  pl.BlockSpec((D_MODEL, tn), lambda j: (0, j)),
                  pl.BlockSpec((1, tn), lambda j: (0, j))],
        out_specs=pl.BlockSpec((B, tn), lambda j: (0, j)),
        compiler_params=_params(("arbitrary",)),
        name="mod",
    )(c, w_ada, b_ada.reshape(1, n))


def _proj_main_layout():
    spans = [(h * RET_DK, (h + 1) * RET_DK, "rot", (h % RET_HEADS, h >= RET_HEADS))
             for h in range(2 * RET_HEADS)]
    spans += [(2 * RET_QK_W, 2 * RET_QK_W + RET_V_W, "copy", None),
              (2 * RET_QK_W + RET_V_W, COL_ATT, "silu", None),
              (COL_ATT, COL_ATT + ATT_GROUP_W, "scale", ATT_QSCALE),
              (COL_ATT + ATT_GROUP_W, PROJ_MAIN_TILES * PROJ_TN, "copy", None)]
    tiles = [[] for _ in range(PROJ_MAIN_TILES)]
    for c0, c1, kind, arg in spans:
        for t in range(c0 // PROJ_TN, (c1 - 1) // PROJ_TN + 1):
            lo, hi = max(c0, t * PROJ_TN), min(c1, (t + 1) * PROJ_TN)
            assert kind != "rot" or (lo, hi) == (c0, c1)
            tiles[t].append((lo - t * PROJ_TN, hi - t * PROJ_TN, kind, arg))
    return tiles


PROJ_MAIN_LAYOUT = _proj_main_layout()


def _proj_kernel(x_ref, mod_ref, g_ref, cos_ref, sin_ref, w_ref, *refs, tps, n_cast):
    cast_in, (main_ref, a1_ref, a2_ref) = refs[:n_cast], refs[n_cast:n_cast + 3]
    cast_out = refs[n_cast + 3:2 * n_cast + 3]
    hf_scr, hd1_scr = refs[2 * n_cast + 3:2 * n_cast + 5]
    lhs = refs[2 * n_cast + 5:2 * n_cast + 8]
    tm = x_ref.shape[0]
    step = pl.program_id(0)

    for src, dst in zip(cast_in, cast_out):
        dst[...] = src[...].astype(BF16)

    def prepare(x_ref, mod_ref, lhs):
        h0_scr, h1_scr, h2_scr = lhs
        hf = _modulated_norm(x_ref[...], g_ref[...], mod_ref, 0)
        h0_scr[...] = hf.astype(BF16)
        d1, d2 = ATT_GROUPS[1][1], ATT_GROUPS[2][1]
        n1, n2 = tm // d1, tm // d2
        for c in range(D_MODEL // LANES):
            cs = slice(c * LANES, (c + 1) * LANES)
            hf_scr[c] = hf[:, cs]
            for r in range(d1):
                part = hf_scr[c, pl.ds(r, n1, stride=d1), :]
                hd1_scr[c, r * n1:(r + 1) * n1, :] = part
                h1_scr[r * n1:(r + 1) * n1, cs] = part.astype(BF16)
            for r in range(d2):
                src = pl.ds((r % d1) * n1 + r // d1, n2, stride=d1)
                h2_scr[r * n2:(r + 1) * n2, cs] = hd1_scr[c, src, :].astype(BF16)

    def w_tile(t):
        return w_ref[:, t * PROJ_TN:(t + 1) * PROJ_TN]

    def main_tile(t, segments, h0_scr, pieces=1):
        if pieces > 1:
            width = PROJ_TN // pieces
            for p in range(pieces):
                p0, p1 = p * width, (p + 1) * width
                acc = jnp.dot(h0_scr[...], w_ref[:, t * PROJ_TN + p0:t * PROJ_TN + p1],
                              preferred_element_type=F32)
                for c0, c1, kind, arg in segments:
                    lo, hi = max(c0, p0), min(c1, p1)
                    if lo < hi:
                        assert kind in ("scale", "copy")
                        seg = acc[:, lo - p0:hi - p0]
                        seg = seg * arg if kind == "scale" else seg
                        main_ref[:, t * PROJ_TN + lo:t * PROJ_TN + hi] = seg.astype(BF16)
            return
        acc = jnp.dot(h0_scr[...], w_tile(t), preferred_element_type=F32)
        if any(kind == "rot" for _, _, kind, _ in segments):
            cos, sin = cos_ref[...], sin_ref[...]
            row = lax.broadcasted_iota(jnp.int32, cos.shape, 0)
            pos = (((step % tps) * tm + row) % RET_C + 1).astype(F32)
        base = t * PROJ_TN
        for c0, c1, kind, arg in segments:
            seg = acc[:, c0:c1]
            if kind == "rot":
                head, is_key = arg
                rate = -RET_LOG_GAMMA[head] if is_key else RET_LOG_GAMMA[head]
                dec = jnp.exp(rate * pos) * ((RET_DK ** -0.5) if is_key else 1.0)
                cd, sd = cos * dec, sin * dec
                cm = (c0 + c1) // 2
                t1, t2 = acc[:, c0:cm], acc[:, cm:c1]
                main_ref[:, base + c0:base + cm] = (t1 * cd - t2 * sd).astype(BF16)
                main_ref[:, base + cm:base + c1] = (t1 * sd + t2 * cd).astype(BF16)
            elif kind == "silu":
                main_ref[:, base + c0:base + c1] = _silu(seg).astype(BF16)
            elif kind == "scale":
                main_ref[:, base + c0:base + c1] = (seg * arg).astype(BF16)
            else:
                main_ref[:, base + c0:base + c1] = seg.astype(BF16)

    def multiply(lhs):
        h0_scr, h1_scr, h2_scr = lhs
        for t, (dil, h_scr, a_ref) in enumerate(((ATT_GROUPS[1][1], h1_scr, a1_ref),
                                                 (ATT_GROUPS[2][1], h2_scr, a2_ref))):
            n = tm // dil
            acc = jnp.dot(h_scr[...], w_tile(PROJ_MAIN_TILES + t), preferred_element_type=F32)
            gw = ATT_GROUP_W
            for r in range(dil):
                rows = slice(r * n, (r + 1) * n)
                a_ref[0, r, :, :gw] = (acc[rows, :gw] * ATT_QSCALE).astype(BF16)
                a_ref[0, r, :, gw:] = acc[rows, gw:].astype(BF16)
        last = len(PROJ_MAIN_LAYOUT) - 1
        for t, segments in enumerate(PROJ_MAIN_LAYOUT):
            main_tile(t, segments, h0_scr, pieces=PROJ_LAST_PIECES if t == last else 1)

    prepare(x_ref, mod_ref, lhs)
    multiply(lhs)


def _proj(x2, norm_g, mod3, cos, sin, w_in, cast_weights, batch, seq):
    tm, tn = PROJ_TM, PROJ_TN
    tps = seq // tm
    steps = batch * seq // tm
    assert seq % tm == 0 and tm % RET_C == 0
    d1, d2 = ATT_GROUPS[1][1], ATT_GROUPS[2][1]
    assert COL_GATE == (PROJ_MAIN_TILES + 2) * tn
    sub = lambda d: pl.BlockSpec((1, d, tm // d, tn), lambda i: (i // tps, 0, i % tps, 0))
    slabs = []
    for w in cast_weights:
        assert w.shape[0] % (steps * BF16_SUBLANES) == 0
        slabs.append(pl.BlockSpec((w.shape[0] // steps, w.shape[1]), lambda i: (i, 0)))
    outs = pl.pallas_call(
        functools.partial(_proj_kernel, tps=tps, n_cast=len(cast_weights)),
        out_shape=[jax.ShapeDtypeStruct((batch * seq, PROJ_MAIN_TILES * tn), BF16),
                   jax.ShapeDtypeStruct((batch, d1, seq // d1, tn), BF16),
                   jax.ShapeDtypeStruct((batch, d2, seq // d2, tn), BF16)]
                  + [jax.ShapeDtypeStruct(w.shape, BF16) for w in cast_weights],
        grid=(steps,),
        in_specs=[pl.BlockSpec((tm, D_MODEL), lambda i: (i, 0)),
                  pl.BlockSpec((1, 1, 6 * D_MODEL), lambda i: (i // tps, 0, 0)),
                  _const_spec((1, D_MODEL)),
                  pl.BlockSpec((tm, RET_DK // 2), lambda i: (i % tps, 0)),
                  pl.BlockSpec((tm, RET_DK // 2), lambda i: (i % tps, 0)),
                  pl.BlockSpec((D_MODEL, COL_GATE), lambda i: (0, 0), pipeline_mode=pl.Buffered(1))]
                 + slabs,
        out_specs=[pl.BlockSpec((tm, PROJ_MAIN_TILES * tn), lambda i: (i, 0)), sub(d1), sub(d2)] + slabs,
        scratch_shapes=[pltpu.VMEM((D_MODEL // LANES, tm, LANES), F32)] * 2
                       + [pltpu.VMEM((tm, D_MODEL), BF16)] * 3,
        compiler_params=_params(("arbitrary",)),
        name="proj",
    )(x2, mod3, norm_g, cos, sin, w_in, *cast_weights)
    return outs[0], outs[1], outs[2], outs[3:]


def _ret_kernel(q_ref, k_ref, v_ref, gate_ref, cdec_ref, gng_ref, gnb_ref, o_ref, state, st_in):
    C = RET_C
    nc = q_ref.shape[1] // C
    cdec = cdec_ref[0]
    gng = gng_ref[0]
    gnb = gnb_ref[0]
    chunk = lambda ci: pl.ds(pl.multiple_of(ci * C, C), C)

    state[...] = jnp.zeros_like(state)
    st_in[0] = jnp.zeros(st_in.shape[1:], BF16)

    def advance(ci, carry):
        rows = chunk(ci)
        kv = lax.dot_general(k_ref[0, rows, :], v_ref[0, rows, :], (((0,), (0,)), ((), ())),
                             preferred_element_type=F32)
        new = (state[...] + kv) * cdec
        state[...] = new
        st_in[ci + 1] = new.astype(BF16)
        return carry

    lax.fori_loop(0, nc - 1, advance, 0, unroll=True)

    qi = lax.broadcasted_iota(jnp.int32, (C, C), 0)
    kj = lax.broadcasted_iota(jnp.int32, (C, C), 1)
    causal = qi >= kj

    def emit(ci, carry):
        rows = chunk(ci)
        q = q_ref[0, rows, :]
        s = lax.dot_general(q, k_ref[0, rows, :], (((1,), (1,)), ((), ())), preferred_element_type=F32)
        s = jnp.where(causal, s, 0.0).astype(BF16)
        o = (jnp.dot(s, v_ref[0, rows, :], preferred_element_type=F32)
             + jnp.dot(q, st_in[ci], preferred_element_type=F32))
        mu = jnp.mean(o, axis=-1, keepdims=True)
        oc = o - mu
        var = jnp.mean(oc * oc, axis=-1, keepdims=True)
        on = oc * lax.rsqrt(var + GN_EPS) * gng + gnb
        o_ref[0, rows, :] = (gate_ref[0, rows, :].astype(F32) * on).astype(BF16)
        return carry

    lax.fori_loop(0, nc, emit, 0, unroll=True)


def _retention_parts(main3, gn_g, gn_b):
    B, S, _ = main3.shape
    H, C = RET_HEADS, RET_C
    assert S % C == 0
    cdec = jnp.exp(jnp.asarray(RET_LOG_GAMMA, F32) * C)[:, None, None]
    kq = RET_QK_W // RET_DK
    kv = 2 * RET_QK_W // RET_DV
    kg = kv + RET_V_W // RET_DV
    in_specs = [pl.BlockSpec((1, S, RET_DK), lambda b, h: (b, 0, h)),
                pl.BlockSpec((1, S, RET_DK), lambda b, h: (b, 0, kq + h)),
                pl.BlockSpec((1, S, RET_DV), lambda b, h: (b, 0, kv + h)),
                pl.BlockSpec((1, S, RET_DV), lambda b, h: (b, 0, kg + h)),
                pl.BlockSpec((1, 1, 1), lambda b, h: (h, 0, 0)),
                pl.BlockSpec((1, 1, RET_DV), lambda b, h: (h, 0, 0)),
                pl.BlockSpec((1, 1, RET_DV), lambda b, h: (h, 0, 0))]
    args = [main3, main3, main3, main3, cdec, gn_g.reshape(H, 1, RET_DV), gn_b.reshape(H, 1, RET_DV)]
    return _CallParts(None, _ret_kernel, (B, H), in_specs, args,
                      [pl.BlockSpec((1, S, RET_DV), lambda b, h: (b, 0, h))],
                      [jax.ShapeDtypeStruct((B, S, RET_V_W), BF16)],
                      [pltpu.VMEM((RET_DK, RET_DV), F32), pltpu.VMEM((S // C, RET_DK, RET_DV), BF16)])


def _t5_bucket(dist):
    dist = np.asarray(dist)
    max_exact = REL_BUCKETS // 2
    d_f = np.maximum(dist, 1).astype(np.float32)
    large = max_exact + (np.log(d_f / np.float32(max_exact)) / np.float32(math.log(REL_MAX_DIST / max_exact))
                         * np.float32(REL_BUCKETS - max_exact)).astype(np.int32)
    large = np.minimum(large, REL_BUCKETS - 1)
    return np.where(dist < max_exact, dist, large).astype(np.int32)


def _attn_unit(n, q_ref, k_ref, v_ref, bias_scr, vaug_scr, o_ref, stats_ref):
    blk, dh = ATT_BLK, ATT_HEAD_DIM
    windowed = k_ref.shape[0] > blk
    if windowed:
        win = pl.ds(pl.multiple_of(jnp.maximum(n - 1, 0) * blk, blk), 2 * blk)
        first = (n == 0).astype(jnp.int32)
    heads = ATT_HEADS_PER_GROUP
    lane = lax.broadcasted_iota(jnp.int32, (blk, LSE_W), 1)
    stats = jnp.ones((blk, LSE_W), F32)
    for h in range(heads):
        hs = slice(h * dh, (h + 1) * dh)
        if windowed:
            kb, vb, bias = k_ref[win, hs], v_ref[win, hs], bias_scr[first, h]
        else:
            kb, vb, bias = k_ref[:, hs], v_ref[:, hs], bias_scr[h]
        vaug_scr[h, :, :dh] = vb
        s = lax.dot_general(q_ref[:, hs], kb, (((1,), (1,)), ((), ())), preferred_element_type=F32) + bias
        mx = jnp.max(s, axis=-1, keepdims=True)
        e = jnp.exp(s - mx).astype(BF16)
        o_den = jnp.dot(e, vaug_scr[h], preferred_element_type=F32)
        o_ref[:, hs] = o_den[:, :dh].astype(BF16)
        stats = jnp.where(lane == h, mx, jnp.where(lane == heads + h, o_den[:, dh:], stats))
    stats_ref[...] = stats


def _attn_init(tab_ref, q0, k0, v0, q1, k1, v1, q2, k2, v2, bk0, bk1, bk2,
               o0, l0, o1, l1, o2, l2, bias0, bias1, bias2, vaug0, vaug1, vaug2):
    blk = ATT_BLK

    @pl.when((pl.program_id(0) == 0) & (pl.program_id(1) == 0))
    def _():
        for vaug in (vaug0, vaug1, vaug2):
            vaug[...] = jnp.ones(vaug.shape, BF16)
        qi = lax.broadcasted_iota(jnp.int32, (blk, 2 * blk), 0)
        kj = lax.broadcasted_iota(jnp.int32, (blk, 2 * blk), 1)
        m = blk + qi - kj
        band = (m >= 0) & (m <= blk)
        masked = jnp.full((blk, blk), NEG, F32)
        for gi, (bk_ref, bias_scr) in enumerate(((bk0, bias0), (bk1, bias1), (bk2, bias2))):
            bucket = bk_ref[...]
            for h in range(ATT_HEADS_PER_GROUP):
                col = gi * ATT_HEADS_PER_GROUP + h
                acc = jnp.zeros((blk, 2 * blk), F32)
                for t in range(REL_BUCKETS):
                    acc = jnp.where(bucket == t, tab_ref[t, col], acc)
                tile = jnp.where(band, acc, NEG)
                if len(bias_scr.shape) == 4:
                    bias_scr[0, h] = tile
                    bias_scr[1, h] = jnp.concatenate([tile[:, blk:], masked], axis=1)
                else:
                    bias_scr[h] = tile[:, blk:]


def _attn_kernel(tab_ref, q0, k0, v0, q1, k1, v1, q2, k2, v2, bk0, bk1, bk2,
                 o0, l0, o1, l1, o2, l2, bias0, bias1, bias2, vaug0, vaug1, vaug2, *, nb1, per_step):
    i = pl.program_id(1)
    blk = ATT_BLK
    for u in range(per_step):
        rows = pl.ds(u * blk, blk)
        n = i * per_step + u
        _attn_unit(n, q0.at[rows], k0, v0, bias0, vaug0.at[u], o0.at[rows], l0.at[rows])
        _attn_unit(n % nb1, q1.at[rows], k1, v1, bias1, vaug1.at[u], o1.at[rows], l1.at[rows])
        _attn_unit(0, q2.at[u], k2.at[u], v2.at[u], bias2, vaug2.at[u], o2.at[u], l2.at[u])


def _attention_parts(main3, att1, att2, rel_bias):
    B, S, _ = main3.shape
    blk, gw = ATT_BLK, ATT_GROUP_W
    steps = S // blk
    lens = [S // dil for _, dil in ATT_GROUPS]
    nbs = [L // blk for L in lens]
    assert all(win // dil == blk for win, dil in ATT_GROUPS)
    assert nbs[0] == steps and nbs[1] >= 2 and nbs[2] == 1
    c0 = COL_ATT // gw
    nb1 = nbs[1]
    G = ATT_BLOCKS_PER_STEP
    assert nb1 % G == 0 and steps % G == 0
    sub1 = lambda b, i: (b, (i * G) // nb1)
    in_specs = [pl.BlockSpec(memory_space=pltpu.SMEM),
                pl.BlockSpec((None, G * blk, gw), lambda b, i: (b, i, c0)),
                pl.BlockSpec((None, S, gw), lambda b, i: (b, 0, c0 + 1)),
                pl.BlockSpec((None, S, gw), lambda b, i: (b, 0, c0 + 2)),
                pl.BlockSpec((None, None, G * blk, gw), lambda b, i: (*sub1(b, i), (i * G) % nb1 // G, 0)),
                pl.BlockSpec((None, None, lens[1], gw), lambda b, i: (*sub1(b, i), 0, 1)),
                pl.BlockSpec((None, None, lens[1], gw), lambda b, i: (*sub1(b, i), 0, 2)),
                pl.BlockSpec((None, G, blk, gw), lambda b, i: (b, i, 0, 0)),
                pl.BlockSpec((None, G, blk, gw), lambda b, i: (b, i, 0, 1)),
                pl.BlockSpec((None, G, blk, gw), lambda b, i: (b, i, 0, 2))]
    args = [rel_bias] + [main3] * 3 + [att1] * 3 + [att2] * 3
    qi = np.arange(blk)[:, None]
    kj = np.arange(2 * blk)[None, :]
    m = blk + qi - kj
    for _, dil in ATT_GROUPS:
        in_specs.append(_const_spec((blk, 2 * blk)))
        args.append(jnp.asarray(_t5_bucket(np.clip(m, 0, blk) * dil)))
    out_specs, out_shapes = [], []
    for (_, dil), nb in zip(ATT_GROUPS, nbs):
        for w, dt in ((gw, BF16), (LSE_W, F32)):
            if nb == steps:
                out_specs.append(pl.BlockSpec((None, G * blk, w), lambda b, i: (b, i, 0)))
                out_shapes.append(jax.ShapeDtypeStruct((B, S, w), dt))
            elif nb == 1:
                out_specs.append(pl.BlockSpec((None, G, blk, w), lambda b, i: (b, i, 0, 0)))
                out_shapes.append(jax.ShapeDtypeStruct((B, dil, S // dil, w), dt))
            else:
                out_specs.append(pl.BlockSpec((None, None, G * blk, w),
                                              lambda b, i: (*sub1(b, i), (i * G) % nb1 // G, 0)))
                out_shapes.append(jax.ShapeDtypeStruct((B, dil, S // dil, w), dt))
    heads = ATT_HEADS_PER_GROUP
    scratch = [pltpu.VMEM((2, heads, blk, 2 * blk), F32), pltpu.VMEM((2, heads, blk, 2 * blk), F32),
               pltpu.VMEM((heads, blk, blk), F32),
               pltpu.VMEM((G, heads, 2 * blk, 2 * ATT_HEAD_DIM), BF16),
               pltpu.VMEM((G, heads, 2 * blk, 2 * ATT_HEAD_DIM), BF16),
               pltpu.VMEM((G, heads, blk, 2 * ATT_HEAD_DIM), BF16)]
    return _CallParts(_attn_init, functools.partial(_attn_kernel, nb1=nb1, per_step=G), (B, steps // G),
                      in_specs, args, out_specs, out_shapes, scratch)


class _CallParts(NamedTuple):
    init: object
    body: object
    grid: tuple
    in_specs: list
    args: list
    out_specs: list
    out_shapes: list
    scratch: list


def _call(parts, name):
    def body(*refs):
        if parts.init is not None:
            parts.init(*refs)
        parts.body(*refs)

    return pl.pallas_call(
        body,
        out_shape=parts.out_shapes,
        grid=parts.grid,
        in_specs=parts.in_specs,
        out_specs=parts.out_specs,
        scratch_shapes=parts.scratch,
        compiler_params=_params(("arbitrary",) * len(parts.grid)),
        name=name,
    )(*parts.args)


def _mixers(main3, att1, att2, rel_bias, gn_g, gn_b):
    retg, = _call(_retention_parts(main3, gn_g, gn_b), "ret")
    att = _call(_attention_parts(main3, att1, att2, rel_bias), "attn")
    return retg, att[0::2], att[1::2]


def _merge_kernel(x_ref, g_ref, mod_ref, retg_ref, o0_ref, o1_ref, o2_ref, s0_ref, s1_ref, s2_ref,
                  wg_ref, wr_ref, wa_ref, wo_ref, out_ref, o1_scr, o2_scr, s1_scr, s2_scr):
    tm = x_ref.shape[0]
    for dil, o_ref, s_ref, o_scr, s_scr in ((ATT_GROUPS[1][1], o1_ref, s1_ref, o1_scr, s1_scr),
                                            (ATT_GROUPS[2][1], o2_ref, s2_ref, o2_scr, s2_scr)):
        n = tm // dil
        for r in range(dil):
            s_scr[pl.ds(r, n, stride=dil), :] = s_ref[0, r]
            for h in range(ATT_HEADS_PER_GROUP):
                hs = slice(h * ATT_HEAD_DIM, (h + 1) * ATT_HEAD_DIM)
                o_scr[h, pl.ds(r, n, stride=dil), :] = o_ref[0, r, :, hs].astype(F32)
    part = tm // MERGE_ROW_PARTS
    for p in range(MERGE_ROW_PARTS):
        rows = slice(p * part, (p + 1) * part)
        ret_out = jnp.dot(retg_ref[rows, :], wr_ref[...], preferred_element_type=F32)
        x = x_ref[rows, :]
        h = _modulated_norm(x, g_ref[...], mod_ref, 0).astype(BF16)
        gates = jax.nn.sigmoid(jnp.dot(h, wg_ref[...], preferred_element_type=F32))
        stats = (s0_ref[rows, :], s1_scr[rows, :], s2_scr[rows, :])
        top = jnp.maximum(jnp.maximum(stats[0], stats[1]), stats[2])
        a = [jnp.exp(st - top) for st in stats]
        lane = lax.broadcasted_iota(jnp.int32, top.shape, 1)
        heads = ATT_HEADS_PER_GROUP
        den = sum(ag * pltpu.roll(st, LSE_W - heads, axis=1) for ag, st in zip(a, stats))
        inv = 1.0 / jnp.where(lane < heads, den, 1.0)
        w0, w1, w2 = (ag * inv for ag in a)
        parts = []
        for hd in range(heads):
            hs = slice(hd * ATT_HEAD_DIM, (hd + 1) * ATT_HEAD_DIM)
            parts.append(w0[:, hd:hd + 1] * o0_ref[rows, hs].astype(F32)
                         + w1[:, hd:hd + 1] * o1_scr[hd, rows, :]
                         + w2[:, hd:hd + 1] * o2_scr[hd, rows, :])
        att = jnp.concatenate(parts, axis=-1).astype(BF16)
        att_out = jnp.dot(att, wa_ref[...], preferred_element_type=F32)
        merged = gates[:, :D_MODEL] * ret_out + gates[:, D_MODEL:] * att_out
        y = jnp.dot(merged.astype(BF16), wo_ref[...], preferred_element_type=F32)
        out_ref[rows, :] = x + mod_ref[0, :, 2 * D_MODEL:3 * D_MODEL] * y


def _merge(x2, norm_g, mod3, retg2, o, lse, w_gate, w_ret_out, w_att_out, w_o, batch, seq):
    tm = ROW_TM
    tps = seq // tm
    row = lambda w: pl.BlockSpec((tm, w), lambda i: (i, 0))
    sub = lambda d, w: pl.BlockSpec((1, d, tm // d, w), lambda i: (i // tps, 0, i % tps, 0))
    d1, d2 = ATT_GROUPS[1][1], ATT_GROUPS[2][1]
    return pl.pallas_call(
        _merge_kernel,
        out_shape=jax.ShapeDtypeStruct((batch * seq, D_MODEL), F32),
        grid=(batch * seq // tm,),
        in_specs=[row(D_MODEL),
                  _const_spec((1, D_MODEL)),
                  pl.BlockSpec((1, 1, 6 * D_MODEL), lambda i: (i // tps, 0, 0)),
                  row(RET_V_W),
                  row(ATT_GROUP_W), sub(d1, ATT_GROUP_W), sub(d2, ATT_GROUP_W),
                  row(LSE_W), sub(d1, LSE_W), sub(d2, LSE_W),
                  pl.BlockSpec((pl.Element(D_MODEL), pl.Element(2 * D_MODEL)), lambda i: (0, COL_GATE),
                               pipeline_mode=pl.Buffered(1)),
                  _const_spec((RET_V_W, D_MODEL)),
                  _const_spec((ATT_GROUP_W, D_MODEL)),
                  _const_spec((D_MODEL, D_MODEL))],
        out_specs=row(D_MODEL),
        scratch_shapes=[pltpu.VMEM((ATT_HEADS_PER_GROUP, tm, ATT_HEAD_DIM), F32)] * 2
                       + [pltpu.VMEM((tm, LSE_W), F32)] * 2,
        compiler_params=_params(("arbitrary",)),
        name="merge",
    )(x2, norm_g, mod3, retg2, *o, *lse, w_gate, w_ret_out, w_att_out, w_o)


def _mlp_kernel(x_ref, mod_ref, g2_ref, gf_ref, w1_ref, w2_ref, out_ref, *, final):
    part = x_ref.shape[0] // MLP_ROW_PARTS
    for p in range(MLP_ROW_PARTS):
        rows = slice(p * part, (p + 1) * part)
        x = x_ref[rows, :]
        h = _modulated_norm(x, g2_ref[...], mod_ref, 3).astype(BF16)
        u = jnp.maximum(jnp.dot(h, w1_ref[...], preferred_element_type=F32), 0.0)
        y = jnp.dot((u * u).astype(BF16), w2_ref[...], preferred_element_type=F32)
        x = x + mod_ref[0, :, 5 * D_MODEL:6 * D_MODEL] * y
        out_ref[rows, :] = _rms(x, gf_ref[...]) if final else x


def _mlp(x2, mod3, norm2_g, norm_f_g, w1, w2, seq, final):
    rows = x2.shape[0]
    tm = MLP_TM
    tps = seq // tm
    return pl.pallas_call(
        functools.partial(_mlp_kernel, final=final),
        out_shape=jax.ShapeDtypeStruct((rows, D_MODEL), F32),
        grid=(rows // tm,),
        in_specs=[pl.BlockSpec((tm, D_MODEL), lambda i: (i, 0)),
                  pl.BlockSpec((1, 1, 6 * D_MODEL), lambda i: (i // tps, 0, 0)),
                  _const_spec((1, D_MODEL)),
                  _const_spec((1, D_MODEL)),
                  _const_spec((D_MODEL, D_FF)),
                  _const_spec((D_FF, D_MODEL))],
        out_specs=pl.BlockSpec((tm, D_MODEL), lambda i: (i, 0)),
        compiler_params=_params(("arbitrary",)),
        name="mlp",
    )(x2, mod3, norm2_g, norm_f_g, w1, w2)


def kernel(x, c, w_ada, b_ada, norm1_g, w_in, rel_bias, ret_gn_g, ret_gn_b, w_ret_out, w_att_out,
           w_o, norm2_g, w_ff1, w_ff2, norm_f_g):
    B, S, D = x.shape
    depth = w_ada.shape[0]
    half = RET_DK // 2
    inv = ROPE_BASE ** (-jnp.arange(half, dtype=F32) / half)
    ang = jnp.arange(S).astype(F32)[:, None] * inv[None, :]
    cos, sin = jnp.cos(ang), jnp.sin(ang)
    x2 = x.reshape(B * S, D)
    for l in range(depth):
        g1 = norm1_g[l].reshape(1, D)
        mod3 = _mod(c, w_ada[l], b_ada[l]).reshape(B, 1, 6 * D)
        w_in_bf = w_in[l].astype(BF16)
        main2, att1, att2, (w_ret_bf, w_o_bf, w_ff1_bf, w_ff2_bf) = _proj(
            x2, g1, mod3, cos, sin, w_in_bf, [w_ret_out[l], w_o[l], w_ff1[l], w_ff2[l]], B, S)
        main3 = main2.reshape(B, S, main2.shape[1])
        retg, o, stats = _mixers(main3, att1, att2, rel_bias, ret_gn_g[l], ret_gn_b[l])
        o = [o[0].reshape(B * S, ATT_GROUP_W), o[1], o[2]]
        stats = [stats[0].reshape(B * S, LSE_W), stats[1], stats[2]]
        x2 = _merge(x2, g1, mod3, retg.reshape(B * S, RET_V_W), o, stats,
                    w_in_bf, w_ret_bf, w_att_out[l].astype(BF16), w_o_bf, B, S)
        x2 = _mlp(x2, mod3, norm2_g[l].reshape(1, D), norm_f_g.reshape(1, D),
                  w_ff1_bf, w_ff2_bf, S, final=l == depth - 1)
    return x2.reshape(B, S, D)
```

```python
import functools
import math
from typing import NamedTuple

import jax
import jax.numpy as jnp
import numpy as np
from jax import lax
from jax.experimental import pallas as pl
from jax.experimental.pallas import tpu as pltpu

F32 = jnp.float32
BF16 = jnp.bfloat16

D_MODEL = 1024
RET_HEADS = 4
RET_DK = 256
RET_DV = 512
RET_C = 256
RET_LOG_GAMMA = tuple(math.log1p(-(2.0 ** (-5.0 - h))) for h in range(RET_HEADS))
RET_QK_W = RET_HEADS * RET_DK
RET_V_W = RET_HEADS * RET_DV
ATT_GROUPS = ((128, 1), (512, 4), (2048, 16))
ATT_HEADS_PER_GROUP = 4
ATT_HEAD_DIM = 128
ATT_GROUP_W = ATT_HEADS_PER_GROUP * ATT_HEAD_DIM
ATT_BLK = 128
ATT_BLOCKS_PER_STEP = 4
LANES = 128
BF16_SUBLANES = 16
LSE_W = LANES
REL_BUCKETS = 32
REL_MAX_DIST = 2048
D_FF = 4 * D_MODEL
RMS_EPS = 1e-6
GN_EPS = 1e-5
ROPE_BASE = 10000.0
NEG = -1e30
ATT_QSCALE = ATT_HEAD_DIM ** -0.5

COL_ATT = 2 * RET_QK_W + 2 * RET_V_W
COL_GATE = COL_ATT + 9 * ATT_GROUP_W

VMEM_LIMIT = 56 * 1024 * 1024

PROJ_TM = 256
PROJ_TN = 3 * ATT_GROUP_W
PROJ_MAIN_TILES = (COL_ATT + PROJ_TN) // PROJ_TN
PROJ_LAST_PIECES = 3
ROW_TM = 512
MERGE_ROW_PARTS = 2
MLP_TM = 1024
MLP_ROW_PARTS = 2


def _params(sem):
    return pltpu.CompilerParams(dimension_semantics=sem, vmem_limit_bytes=VMEM_LIMIT)


def _const_spec(shape):
    zeros = (0,) * len(shape)
    return pl.BlockSpec(shape, lambda *_: zeros, pipeline_mode=pl.Buffered(1))


def _silu(t):
    return t * jax.nn.sigmoid(t)


def _rms(x, g):
    return x * lax.rsqrt(jnp.mean(x * x, axis=-1, keepdims=True) + RMS_EPS) * g


def _modulated_norm(x, g, mod_ref, k):
    shift = mod_ref[0, :, k * D_MODEL:(k + 1) * D_MODEL]
    scale = mod_ref[0, :, (k + 1) * D_MODEL:(k + 2) * D_MODEL]
    return _rms(x, g) * (1.0 + scale) + shift


def _mod_kernel(c_ref, w_ref, b_ref, o_ref):
    o_ref[...] = jnp.dot(_silu(c_ref[...]), w_ref[...], preferred_element_type=F32) + b_ref[...]


def _mod(c, w_ada, b_ada):
    B = c.shape[0]
    n = w_ada.shape[1]
    tn = D_MODEL
    return pl.pallas_call(
        _mod_kernel,
        out_shape=jax.ShapeDtypeStruct((B, n), F32),
        grid=(n // tn,),
        in_specs=[pl.BlockSpec((B, D_MODEL), lambda j: (0, 0)),
                  pl.BlockSpec((D_MODEL, tn), lambda j: (0, j)),
                  pl.BlockSpec((1, tn), lambda j: (0, j))],
        out_specs=pl.BlockSpec((B, tn), lambda j: (0, j)),
        compiler_params=_params(("arbitrary",)),
        name="mod",
    )(c, w_ada, b_ada.reshape(1, n))


def _proj_main_layout():
    spans = [(h * RET_DK, (h + 1) * RET_DK, "rot", (h % RET_HEADS, h >= RET_HEADS))
             for h in range(2 * RET_HEADS)]
    spans += [(2 * RET_QK_W, 2 * RET_QK_W + RET_V_W, "copy", None),
              (2 * RET_QK_W + RET_V_W, COL_ATT, "silu", None),
              (COL_ATT, COL_ATT + ATT_GROUP_W, "scale", ATT_QSCALE),
              (COL_ATT + ATT_GROUP_W, PROJ_MAIN_TILES * PROJ_TN, "copy", None)]
    tiles = [[] for _ in range(PROJ_MAIN_TILES)]
    for c0, c1, kind, arg in spans:
        for t in range(c0 // PROJ_TN, (c1 - 1) // PROJ_TN + 1):
            lo, hi = max(c0, t * PROJ_TN), min(c1, (t + 1) * PROJ_TN)
            assert kind != "rot" or (lo, hi) == (c0, c1)
            tiles[t].append((lo - t * PROJ_TN, hi - t * PROJ_TN, kind, arg))
    return tiles


PROJ_MAIN_LAYOUT = _proj_main_layout()


def _proj_kernel(x_ref, mod_ref, g_ref, cos_ref, sin_ref, w_ref, *refs, tps, n_cast):
    cast_in, (main_ref, a1_ref, a2_ref) = refs[:n_cast], refs[n_cast:n_cast + 3]
    cast_out = refs[n_cast + 3:2 * n_cast + 3]
    hf_scr, hd1_scr = refs[2 * n_cast + 3:2 * n_cast + 5]
    lhs = refs[2 * n_cast + 5:2 * n_cast + 8]
    tm = x_ref.shape[0]
    step = pl.program_id(0)

    for src, dst in zip(cast_in, cast_out):
        dst[...] = src[...].astype(BF16)

    def prepare(x_ref, mod_ref, lhs):
        h0_scr, h1_scr, h2_scr = lhs
        hf = _modulated_norm(x_ref[...], g_ref[...], mod_ref, 0)
        h0_scr[...] = hf.astype(BF16)
        d1, d2 = ATT_GROUPS[1][1], ATT_GROUPS[2][1]
        n1, n2 = tm // d1, tm // d2
        for c in range(D_MODEL // LANES):
            cs = slice(c * LANES, (c + 1) * LANES)
            hf_scr[c] = hf[:, cs]
            for r in range(d1):
                part = hf_scr[c, pl.ds(r, n1, stride=d1), :]
                hd1_scr[c, r * n1:(r + 1) * n1, :] = part
                h1_scr[r * n1:(r + 1) * n1, cs] = part.astype(BF16)
            for r in range(d2):
                src = pl.ds((r % d1) * n1 + r // d1, n2, stride=d1)
                h2_scr[r * n2:(r + 1) * n2, cs] = hd1_scr[c, src, :].astype(BF16)

    def w_tile(t):
        return w_ref[:, t * PROJ_TN:(t + 1) * PROJ_TN]

    def main_tile(t, segments, h0_scr, pieces=1):
        if pieces > 1:
            width = PROJ_TN // pieces
            for p in range(pieces):
                p0, p1 = p * width, (p + 1) * width
                acc = jnp.dot(h0_scr[...], w_ref[:, t * PROJ_TN + p0:t * PROJ_TN + p1],
                              preferred_element_type=F32)
                for c0, c1, kind, arg in segments:
                    lo, hi = max(c0, p0), min(c1, p1)
                    if lo < hi:
                        assert kind in ("scale", "copy")
                        seg = acc[:, lo - p0:hi - p0]
                        seg = seg * arg if kind == "scale" else seg
                        main_ref[:, t * PROJ_TN + lo:t * PROJ_TN + hi] = seg.astype(BF16)
            return
        acc = jnp.dot(h0_scr[...], w_tile(t), preferred_element_type=F32)
        if any(kind == "rot" for _, _, kind, _ in segments):
            cos, sin = cos_ref[...], sin_ref[...]
            row = lax.broadcasted_iota(jnp.int32, cos.shape, 0)
            pos = (((step % tps) * tm + row) % RET_C + 1).astype(F32)
        base = t * PROJ_TN
        for c0, c1, kind, arg in segments:
            seg = acc[:, c0:c1]
            if kind == "rot":
                head, is_key = arg
                rate = -RET_LOG_GAMMA[head] if is_key else RET_LOG_GAMMA[head]
                dec = jnp.exp(rate * pos) * ((RET_DK ** -0.5) if is_key else 1.0)
                cd, sd = cos * dec, sin * dec
                cm = (c0 + c1) // 2
                t1, t2 = acc[:, c0:cm], acc[:, cm:c1]
                main_ref[:, base + c0:base + cm] = (t1 * cd - t2 * sd).astype(BF16)
                main_ref[:, base + cm:base + c1] = (t1 * sd + t2 * cd).astype(BF16)
            elif kind == "silu":
                main_ref[:, base + c0:base + c1] = _silu(seg).astype(BF16)
            elif kind == "scale":
                main_ref[:, base + c0:base + c1] = (seg * arg).astype(BF16)
            else:
                main_ref[:, base + c0:base + c1] = seg.astype(BF16)

    def multiply(lhs):
        h0_scr, h1_scr, h2_scr = lhs
        for t, (dil, h_scr, a_ref) in enumerate(((ATT_GROUPS[1][1], h1_scr, a1_ref),
                                                 (ATT_GROUPS[2][1], h2_scr, a2_ref))):
            n = tm // dil
            acc = jnp.dot(h_scr[...], w_tile(PROJ_MAIN_TILES + t), preferred_element_type=F32)
            gw = ATT_GROUP_W
            for r in range(dil):
                rows = slice(r * n, (r + 1) * n)
                a_ref[0, r, :, :gw] = (acc[rows, :gw] * ATT_QSCALE).astype(BF16)
                a_ref[0, r, :, gw:] = acc[rows, gw:].astype(BF16)
        last = len(PROJ_MAIN_LAYOUT) - 1
        for t, segments in enumerate(PROJ_MAIN_LAYOUT):
            main_tile(t, segments, h0_scr, pieces=PROJ_LAST_PIECES if t == last else 1)

    prepare(x_ref, mod_ref, lhs)
    multiply(lhs)


def _proj(x2, norm_g, mod3, cos, sin, w_in, cast_weights, batch, seq):
    tm, tn = PROJ_TM, PROJ_TN
    tps = seq // tm
    steps = batch * seq // tm
    assert seq % tm == 0 and tm % RET_C == 0
    d1, d2 = ATT_GROUPS[1][1], ATT_GROUPS[2][1]
    assert COL_GATE == (PROJ_MAIN_TILES + 2) * tn
    sub = lambda d: pl.BlockSpec((1, d, tm // d, tn), lambda i: (i // tps, 0, i % tps, 0))
    slabs = []
    for w in cast_weights:
        assert w.shape[0] % (steps * BF16_SUBLANES) == 0
        slabs.append(pl.BlockSpec((w.shape[0] // steps, w.shape[1]), lambda i: (i, 0)))
    outs = pl.pallas_call(
        functools.partial(_proj_kernel, tps=tps, n_cast=len(cast_weights)),
        out_shape=[jax.ShapeDtypeStruct((batch * seq, PROJ_MAIN_TILES * tn), BF16),
                   jax.ShapeDtypeStruct((batch, d1, seq // d1, tn), BF16),
                   jax.ShapeDtypeStruct((batch, d2, seq // d2, tn), BF16)]
                  + [jax.ShapeDtypeStruct(w.shape, BF16) for w in cast_weights],
        grid=(steps,),
        in_specs=[pl.BlockSpec((tm, D_MODEL), lambda i: (i, 0)),
                  pl.BlockSpec((1, 1, 6 * D_MODEL), lambda i: (i // tps, 0, 0)),
                  _const_spec((1, D_MODEL)),
                  pl.BlockSpec((tm, RET_DK // 2), lambda i: (i % tps, 0)),
                  pl.BlockSpec((tm, RET_DK // 2), lambda i: (i % tps, 0)),
                  pl.BlockSpec((D_MODEL, COL_GATE), lambda i: (0, 0), pipeline_mode=pl.Buffered(1))]
                 + slabs,
        out_specs=[pl.BlockSpec((tm, PROJ_MAIN_TILES * tn), lambda i: (i, 0)), sub(d1), sub(d2)] + slabs,
        scratch_shapes=[pltpu.VMEM((D_MODEL // LANES, tm, LANES), F32)] * 2
                       + [pltpu.VMEM((tm, D_MODEL), BF16)] * 3,
        compiler_params=_params(("arbitrary",)),
        name="proj",
    )(x2, mod3, norm_g, cos, sin, w_in, *cast_weights)
    return outs[0], outs[1], outs[2], outs[3:]


def _ret_kernel(q_ref, k_ref, v_ref, cdec_ref, o_ref, state, st_in):
    C = RET_C
    nc = q_ref.shape[1] // C
    cdec = cdec_ref[0]
    chunk = lambda ci: pl.ds(pl.multiple_of(ci * C, C), C)

    state[...] = jnp.zeros_like(state)
    st_in[0] = jnp.zeros(st_in.shape[1:], BF16)

    def advance(ci, carry):
        rows = chunk(ci)
        kv = lax.dot_general(k_ref[0, rows, :], v_ref[0, rows, :], (((0,), (0,)), ((), ())),
                             preferred_element_type=F32)
        new = (state[...] + kv) * cdec
        state[...] = new
        st_in[ci + 1] = new.astype(BF16)
        return carry

    lax.fori_loop(0, nc - 1, advance, 0, unroll=True)

    qi = lax.broadcasted_iota(jnp.int32, (C, C), 0)
    kj = lax.broadcasted_iota(jnp.int32, (C, C), 1)
    causal = qi >= kj

    def emit(ci, carry):
        rows = chunk(ci)
        q = q_ref[0, rows, :]
        s = lax.dot_general(q, k_ref[0, rows, :], (((1,), (1,)), ((), ())), preferred_element_type=F32)
        s = jnp.where(causal, s, 0.0).astype(BF16)
        o = (jnp.dot(s, v_ref[0, rows, :], preferred_element_type=F32)
             + jnp.dot(q, st_in[ci], preferred_element_type=F32))
        o_ref[0, rows, :] = (o - jnp.mean(o, axis=-1, keepdims=True)).astype(BF16)
        return carry

    lax.fori_loop(0, nc, emit, 0, unroll=True)


def _retention_parts(main3):
    B, S, _ = main3.shape
    H, C = RET_HEADS, RET_C
    assert S % C == 0
    cdec = jnp.exp(jnp.asarray(RET_LOG_GAMMA, F32) * C)[:, None, None]
    kq = RET_QK_W // RET_DK
    kv = 2 * RET_QK_W // RET_DV
    in_specs = [pl.BlockSpec((1, S, RET_DK), lambda b, h: (b, 0, h)),
                pl.BlockSpec((1, S, RET_DK), lambda b, h: (b, 0, kq + h)),
                pl.BlockSpec((1, S, RET_DV), lambda b, h: (b, 0, kv + h)),
                pl.BlockSpec((1, 1, 1), lambda b, h: (h, 0, 0))]
    args = [main3, main3, main3, cdec]
    return _CallParts(None, _ret_kernel, (B, H), in_specs, args,
                      [pl.BlockSpec((1, S, RET_DV), lambda b, h: (b, 0, h))],
                      [jax.ShapeDtypeStruct((B, S, RET_V_W), BF16)],
                      [pltpu.VMEM((RET_DK, RET_DV), F32), pltpu.VMEM((S // C, RET_DK, RET_DV), BF16)])


def _t5_bucket(dist):
    dist = np.asarray(dist)
    max_exact = REL_BUCKETS // 2
    d_f = np.maximum(dist, 1).astype(np.float32)
    large = max_exact + (np.log(d_f / np.float32(max_exact)) / np.float32(math.log(REL_MAX_DIST / max_exact))
                         * np.float32(REL_BUCKETS - max_exact)).astype(np.int32)
    large = np.minimum(large, REL_BUCKETS - 1)
    return np.where(dist < max_exact, dist, large).astype(np.int32)


def _attn_unit(n, q_ref, k_ref, v_ref, bias_scr, vaug_scr, o_ref, stats_ref):
    blk, dh = ATT_BLK, ATT_HEAD_DIM
    windowed = k_ref.shape[0] > blk
    if windowed:
        win = pl.ds(pl.multiple_of(jnp.maximum(n - 1, 0) * blk, blk), 2 * blk)
        first = (n == 0).astype(jnp.int32)
    heads = ATT_HEADS_PER_GROUP
    lane = lax.broadcasted_iota(jnp.int32, (blk, LSE_W), 1)
    stats = jnp.ones((blk, LSE_W), F32)
    for h in range(heads):
        hs = slice(h * dh, (h + 1) * dh)
        if windowed:
            kb, vb, bias = k_ref[win, hs], v_ref[win, hs], bias_scr[first, h]
        else:
            kb, vb, bias = k_ref[:, hs], v_ref[:, hs], bias_scr[h]
        vaug_scr[h, :, :dh] = vb
        s = lax.dot_general(q_ref[:, hs], kb, (((1,), (1,)), ((), ())), preferred_element_type=F32) + bias
        mx = jnp.max(s, axis=-1, keepdims=True)
        e = jnp.exp(s - mx).astype(BF16)
        o_den = jnp.dot(e, vaug_scr[h], preferred_element_type=F32)
        o_ref[:, hs] = o_den[:, :dh].astype(BF16)
        stats = jnp.where(lane == h, mx, jnp.where(lane == heads + h, o_den[:, dh:], stats))
    stats_ref[...] = stats


def _attn_init(tab_ref, q0, k0, v0, q1, k1, v1, q2, k2, v2, bk0, bk1, bk2,
               o0, l0, o1, l1, o2, l2, bias0, bias1, bias2, vaug0, vaug1, vaug2):
    blk = ATT_BLK

    @pl.when((pl.program_id(0) == 0) & (pl.program_id(1) == 0))
    def _():
        for vaug in (vaug0, vaug1, vaug2):
            vaug[...] = jnp.ones(vaug.shape, BF16)
        qi = lax.broadcasted_iota(jnp.int32, (blk, 2 * blk), 0)
        kj = lax.broadcasted_iota(jnp.int32, (blk, 2 * blk), 1)
        m = blk + qi - kj
        band = (m >= 0) & (m <= blk)
        masked = jnp.full((blk, blk), NEG, F32)
        for gi, (bk_ref, bias_scr) in enumerate(((bk0, bias0), (bk1, bias1), (bk2, bias2))):
            bucket = bk_ref[...]
            for h in range(ATT_HEADS_PER_GROUP):
                col = gi * ATT_HEADS_PER_GROUP + h
                acc = jnp.zeros((blk, 2 * blk), F32)
                for t in range(REL_BUCKETS):
                    acc = jnp.where(bucket == t, tab_ref[t, col], acc)
                tile = jnp.where(band, acc, NEG)
                if len(bias_scr.shape) == 4:
                    bias_scr[0, h] = tile
                    bias_scr[1, h] = jnp.concatenate([tile[:, blk:], masked], axis=1)
                else:
                    bias_scr[h] = tile[:, blk:]


def _attn_kernel(tab_ref, q0, k0, v0, q1, k1, v1, q2, k2, v2, bk0, bk1, bk2,
                 o0, l0, o1, l1, o2, l2, bias0, bias1, bias2, vaug0, vaug1, vaug2, *, nb1, per_step):
    i = pl.program_id(1)
    blk = ATT_BLK
    for u in range(per_step):
        rows = pl.ds(u * blk, blk)
        n = i * per_step + u
        _attn_unit(n, q0.at[rows], k0, v0, bias0, vaug0.at[u], o0.at[rows], l0.at[rows])
        _attn_unit(n % nb1, q1.at[rows], k1, v1, bias1, vaug1.at[u], o1.at[rows], l1.at[rows])
        _attn_unit(0, q2.at[u], k2.at[u], v2.at[u], bias2, vaug2.at[u], o2.at[u], l2.at[u])


def _attention_parts(main3, att1, att2, rel_bias):
    B, S, _ = main3.shape
    blk, gw = ATT_BLK, ATT_GROUP_W
    steps = S // blk
    lens = [S // dil for _, dil in ATT_GROUPS]
    nbs = [L // blk for L in lens]
    assert all(win // dil == blk for win, dil in ATT_GROUPS)
    assert nbs[0] == steps and nbs[1] >= 2 and nbs[2] == 1
    c0 = COL_ATT // gw
    nb1 = nbs[1]
    G = ATT_BLOCKS_PER_STEP
    assert nb1 % G == 0 and steps % G == 0
    sub1 = lambda b, i: (b, (i * G) // nb1)
    in_specs = [pl.BlockSpec(memory_space=pltpu.SMEM),
                pl.BlockSpec((None, G * blk, gw), lambda b, i: (b, i, c0)),
                pl.BlockSpec((None, S, gw), lambda b, i: (b, 0, c0 + 1)),
                pl.BlockSpec((None, S, gw), lambda b, i: (b, 0, c0 + 2)),
                pl.BlockSpec((None, None, G * blk, gw), lambda b, i: (*sub1(b, i), (i * G) % nb1 // G, 0)),
                pl.BlockSpec((None, None, lens[1], gw), lambda b, i: (*sub1(b, i), 0, 1)),
                pl.BlockSpec((None, None, lens[1], gw), lambda b, i: (*sub1(b, i), 0, 2)),
                pl.BlockSpec((None, G, blk, gw), lambda b, i: (b, i, 0, 0)),
                pl.BlockSpec((None, G, blk, gw), lambda b, i: (b, i, 0, 1)),
                pl.BlockSpec((None, G, blk, gw), lambda b, i: (b, i, 0, 2))]
    args = [rel_bias] + [main3] * 3 + [att1] * 3 + [att2] * 3
    qi = np.arange(blk)[:, None]
    kj = np.arange(2 * blk)[None, :]
    m = blk + qi - kj
    for _, dil in ATT_GROUPS:
        in_specs.append(_const_spec((blk, 2 * blk)))
        args.append(jnp.asarray(_t5_bucket(np.clip(m, 0, blk) * dil)))
    out_specs, out_shapes = [], []
    for (_, dil), nb in zip(ATT_GROUPS, nbs):
        for w, dt in ((gw, BF16), (LSE_W, F32)):
            if nb == steps:
                out_specs.append(pl.BlockSpec((None, G * blk, w), lambda b, i: (b, i, 0)))
                out_shapes.append(jax.ShapeDtypeStruct((B, S, w), dt))
            elif nb == 1:
                out_specs.append(pl.BlockSpec((None, G, blk, w), lambda b, i: (b, i, 0, 0)))
                out_shapes.append(jax.ShapeDtypeStruct((B, dil, S // dil, w), dt))
            else:
                out_specs.append(pl.BlockSpec((None, None, G * blk, w),
                                              lambda b, i: (*sub1(b, i), (i * G) % nb1 // G, 0)))
                out_shapes.append(jax.ShapeDtypeStruct((B, dil, S // dil, w), dt))
    heads = ATT_HEADS_PER_GROUP
    scratch = [pltpu.VMEM((2, heads, blk, 2 * blk), F32), pltpu.VMEM((2, heads, blk, 2 * blk), F32),
               pltpu.VMEM((heads, blk, blk), F32),
               pltpu.VMEM((G, heads, 2 * blk, 2 * ATT_HEAD_DIM), BF16),
               pltpu.VMEM((G, heads, 2 * blk, 2 * ATT_HEAD_DIM), BF16),
               pltpu.VMEM((G, heads, blk, 2 * ATT_HEAD_DIM), BF16)]
    return _CallParts(_attn_init, functools.partial(_attn_kernel, nb1=nb1, per_step=G), (B, steps // G),
                      in_specs, args, out_specs, out_shapes, scratch)


class _CallParts(NamedTuple):
    init: object
    body: object
    grid: tuple
    in_specs: list
    args: list
    out_specs: list
    out_shapes: list
    scratch: list


def _call(parts, name):
    def body(*refs):
        if parts.init is not None:
            parts.init(*refs)
        parts.body(*refs)

    return pl.pallas_call(
        body,
        out_shape=parts.out_shapes,
        grid=parts.grid,
        in_specs=parts.in_specs,
        out_specs=parts.out_specs,
        scratch_shapes=parts.scratch,
        compiler_params=_params(("arbitrary",) * len(parts.grid)),
        name=name,
    )(*parts.args)


def _mixers(main3, att1, att2, rel_bias):
    ret_centred, = _call(_retention_parts(main3), "ret")
    att = _call(_attention_parts(main3, att1, att2, rel_bias), "attn")
    return ret_centred, att[0::2], att[1::2]


def _merge_kernel(x_ref, g_ref, mod_ref, oc_ref, rgate_ref, gng_ref, gnb_ref,
                  o0_ref, o1_ref, o2_ref, s0_ref, s1_ref, s2_ref,
                  wg_ref, wr_ref, wa_ref, wo_ref, out_ref, o1_scr, o2_scr, s1_scr, s2_scr):
    tm = x_ref.shape[0]
    for dil, o_ref, s_ref, o_scr, s_scr in ((ATT_GROUPS[1][1], o1_ref, s1_ref, o1_scr, s1_scr),
                                            (ATT_GROUPS[2][1], o2_ref, s2_ref, o2_scr, s2_scr)):
        n = tm // dil
        for r in range(dil):
            s_scr[pl.ds(r, n, stride=dil), :] = s_ref[0, r]
            for h in range(ATT_HEADS_PER_GROUP):
                hs = slice(h * ATT_HEAD_DIM, (h + 1) * ATT_HEAD_DIM)
                o_scr[h, pl.ds(r, n, stride=dil), :] = o_ref[0, r, :, hs].astype(F32)
    part = tm // MERGE_ROW_PARTS
    for p in range(MERGE_ROW_PARTS):
        rows = slice(p * part, (p + 1) * part)
        x = x_ref[rows, :]
        h = _modulated_norm(x, g_ref[...], mod_ref, 0).astype(BF16)
        gates = jax.nn.sigmoid(jnp.dot(h, wg_ref[...], preferred_element_type=F32))
        retg = []
        for hd in range(RET_HEADS):
            hs = slice(hd * RET_DV, (hd + 1) * RET_DV)
            oc = oc_ref[rows, hs].astype(F32)
            var = jnp.mean(oc * oc, axis=-1, keepdims=True)
            on = oc * lax.rsqrt(var + GN_EPS) * gng_ref[:, hs] + gnb_ref[:, hs]
            retg.append((rgate_ref[rows, hs].astype(F32) * on).astype(BF16))
        ret_out = jnp.dot(jnp.concatenate(retg, axis=-1), wr_ref[...], preferred_element_type=F32)
        stats = (s0_ref[rows, :], s1_scr[rows, :], s2_scr[rows, :])
        top = jnp.maximum(jnp.maximum(stats[0], stats[1]), stats[2])
        a = [jnp.exp(st - top) for st in stats]
        lane = lax.broadcasted_iota(jnp.int32, top.shape, 1)
        heads = ATT_HEADS_PER_GROUP
        den = sum(ag * pltpu.roll(st, LSE_W - heads, axis=1) for ag, st in zip(a, stats))
        inv = 1.0 / jnp.where(lane < heads, den, 1.0)
        w0, w1, w2 = (ag * inv for ag in a)
        parts = []
        for hd in range(heads):
            hs = slice(hd * ATT_HEAD_DIM, (hd + 1) * ATT_HEAD_DIM)
            parts.append(w0[:, hd:hd + 1] * o0_ref[rows, hs].astype(F32)
                         + w1[:, hd:hd + 1] * o1_scr[hd, rows, :]
                         + w2[:, hd:hd + 1] * o2_scr[hd, rows, :])
        att = jnp.concatenate(parts, axis=-1).astype(BF16)
        att_out = jnp.dot(att, wa_ref[...], preferred_element_type=F32)
        merged = gates[:, :D_MODEL] * ret_out + gates[:, D_MODEL:] * att_out
        y = jnp.dot(merged.astype(BF16), wo_ref[...], preferred_element_type=F32)
        out_ref[rows, :] = x + mod_ref[0, :, 2 * D_MODEL:3 * D_MODEL] * y


def _merge(x2, norm_g, mod3, oc2, main2, gn_g, gn_b, o, lse, w_gate, w_ret_out, w_att_out, w_o, batch, seq):
    tm = ROW_TM
    gate_col = (2 * RET_QK_W + RET_V_W) // RET_V_W
    assert gate_col * RET_V_W == 2 * RET_QK_W + RET_V_W
    tps = seq // tm
    row = lambda w: pl.BlockSpec((tm, w), lambda i: (i, 0))
    sub = lambda d, w: pl.BlockSpec((1, d, tm // d, w), lambda i: (i // tps, 0, i % tps, 0))
    d1, d2 = ATT_GROUPS[1][1], ATT_GROUPS[2][1]
    return pl.pallas_call(
        _merge_kernel,
        out_shape=jax.ShapeDtypeStruct((batch * seq, D_MODEL), F32),
        grid=(batch * seq // tm,),
        in_specs=[row(D_MODEL),
                  _const_spec((1, D_MODEL)),
                  pl.BlockSpec((1, 1, 6 * D_MODEL), lambda i: (i // tps, 0, 0)),
                  row(RET_V_W),
                  pl.BlockSpec((tm, RET_V_W), lambda i: (i, gate_col)),
                  _const_spec((1, RET_V_W)),
                  _const_spec((1, RET_V_W)),
                  row(ATT_GROUP_W), sub(d1, ATT_GROUP_W), sub(d2, ATT_GROUP_W),
                  row(LSE_W), sub(d1, LSE_W), sub(d2, LSE_W),
                  pl.BlockSpec((pl.Element(D_MODEL), pl.Element(2 * D_MODEL)), lambda i: (0, COL_GATE),
                               pipeline_mode=pl.Buffered(1)),
                  _const_spec((RET_V_W, D_MODEL)),
                  _const_spec((ATT_GROUP_W, D_MODEL)),
                  _const_spec((D_MODEL, D_MODEL))],
        out_specs=row(D_MODEL),
        scratch_shapes=[pltpu.VMEM((ATT_HEADS_PER_GROUP, tm, ATT_HEAD_DIM), F32)] * 2
                       + [pltpu.VMEM((tm, LSE_W), F32)] * 2,
        compiler_params=_params(("arbitrary",)),
        name="merge",
    )(x2, norm_g, mod3, oc2, main2, gn_g.reshape(1, RET_V_W), gn_b.reshape(1, RET_V_W), *o, *lse,
      w_gate, w_ret_out, w_att_out, w_o)


def _mlp_kernel(x_ref, mod_ref, g2_ref, gf_ref, w1_ref, w2_ref, out_ref, *, final):
    part = x_ref.shape[0] // MLP_ROW_PARTS
    for p in range(MLP_ROW_PARTS):
        rows = slice(p * part, (p + 1) * part)
        x = x_ref[rows, :]
        h = _modulated_norm(x, g2_ref[...], mod_ref, 3).astype(BF16)
        u = jnp.maximum(jnp.dot(h, w1_ref[...], preferred_element_type=F32), 0.0)
        y = jnp.dot((u * u).astype(BF16), w2_ref[...], preferred_element_type=F32)
        x = x + mod_ref[0, :, 5 * D_MODEL:6 * D_MODEL] * y
        out_ref[rows, :] = _rms(x, gf_ref[...]) if final else x


def _mlp(x2, mod3, norm2_g, norm_f_g, w1, w2, seq, final):
    rows = x2.shape[0]
    tm = MLP_TM
    tps = seq // tm
    return pl.pallas_call(
        functools.partial(_mlp_kernel, final=final),
        out_shape=jax.ShapeDtypeStruct((rows, D_MODEL), F32),
        grid=(rows // tm,),
        in_specs=[pl.BlockSpec((tm, D_MODEL), lambda i: (i, 0)),
                  pl.BlockSpec((1, 1, 6 * D_MODEL), lambda i: (i // tps, 0, 0)),
                  _const_spec((1, D_MODEL)),
                  _const_spec((1, D_MODEL)),
                  _const_spec((D_MODEL, D_FF)),
                  _const_spec((D_FF, D_MODEL))],
        out_specs=pl.BlockSpec((tm, D_MODEL), lambda i: (i, 0)),
        compiler_params=_params(("arbitrary",)),
        name="mlp",
    )(x2, mod3, norm2_g, norm_f_g, w1, w2)


def kernel(x, c, w_ada, b_ada, norm1_g, w_in, rel_bias, ret_gn_g, ret_gn_b, w_ret_out, w_att_out,
           w_o, norm2_g, w_ff1, w_ff2, norm_f_g):
    B, S, D = x.shape
    depth = w_ada.shape[0]
    half = RET_DK // 2
    ang = np.arange(S, dtype=np.float64)[:, None] * ROPE_BASE ** (-np.arange(half, dtype=np.float64) / half)
    cos, sin = jnp.asarray(np.cos(ang), F32), jnp.asarray(np.sin(ang), F32)
    x2 = x.reshape(B * S, D)
    for l in range(depth):
        g1 = norm1_g[l].reshape(1, D)
        mod3 = _mod(c, w_ada[l], b_ada[l]).reshape(B, 1, 6 * D)
        w_in_bf = w_in[l].astype(BF16)
        main2, att1, att2, (w_ret_bf, w_o_bf, w_ff1_bf, w_ff2_bf) = _proj(
            x2, g1, mod3, cos, sin, w_in_bf, [w_ret_out[l], w_o[l], w_ff1[l], w_ff2[l]], B, S)
        main3 = main2.reshape(B, S, main2.shape[1])
        ret_centred, o, stats = _mixers(main3, att1, att2, rel_bias)
        o = [o[0].reshape(B * S, ATT_GROUP_W), o[1], o[2]]
        stats = [stats[0].reshape(B * S, LSE_W), stats[1], stats[2]]
        x2 = _merge(x2, g1, mod3, ret_centred.reshape(B * S, RET_V_W), main2, ret_gn_g[l], ret_gn_b[l],
                    o, stats, w_in_bf, w_ret_bf, w_att_out[l].astype(BF16), w_o_bf, B, S)
        x2 = _mlp(x2, mod3, norm2_g[l].reshape(1, D), norm_f_g.reshape(1, D),
                  w_ff1_bf, w_ff2_bf, S, final=l == depth - 1)
    return x2.reshape(B, S, D)
```

```python
import functools
import math
from typing import NamedTuple

import jax
import jax.numpy as jnp
import numpy as np
from jax import lax
from jax.experimental import pallas as pl
from jax.experimental.pallas import tpu as pltpu

F32 = jnp.float32
BF16 = jnp.bfloat16

D_MODEL = 1024
RET_HEADS = 4
RET_DK = 256
RET_DV = 512
RET_C = 256
RET_LOG_GAMMA = tuple(math.log1p(-(2.0 ** (-5.0 - h))) for h in range(RET_HEADS))
RET_QK_W = RET_HEADS * RET_DK
RET_V_W = RET_HEADS * RET_DV
ATT_GROUPS = ((128, 1), (512, 4), (2048, 16))
ATT_HEADS_PER_GROUP = 4
ATT_HEAD_DIM = 128
ATT_GROUP_W = ATT_HEADS_PER_GROUP * ATT_HEAD_DIM
ATT_BLK = 128
ATT_BLOCKS_PER_STEP = 4
LANES = 128
BF16_SUBLANES = 16
LSE_W = LANES
REL_BUCKETS = 32
REL_MAX_DIST = 2048
D_FF = 4 * D_MODEL
RMS_EPS = 1e-6
GN_EPS = 1e-5
ROPE_BASE = 10000.0
NEG = -1e30
ATT_QSCALE = ATT_HEAD_DIM ** -0.5

COL_ATT = 2 * RET_QK_W + 2 * RET_V_W
COL_GATE = COL_ATT + 9 * ATT_GROUP_W

VMEM_LIMIT = 56 * 1024 * 1024

PROJ_TM = 256
PROJ_TN = 3 * ATT_GROUP_W
PROJ_MAIN_TILES = (COL_ATT + PROJ_TN) // PROJ_TN
PROJ_LAST_PIECES = 3
PROJ_W_STAGE_ROWS = 64
ROW_TM = 512
MERGE_ROW_PARTS = 2
MLP_TM = 1024
MLP_ROW_PARTS = 2


def _params(sem):
    return pltpu.CompilerParams(dimension_semantics=sem, vmem_limit_bytes=VMEM_LIMIT)


def _const_spec(shape):
    zeros = (0,) * len(shape)
    return pl.BlockSpec(shape, lambda *_: zeros, pipeline_mode=pl.Buffered(1))


def _silu(t):
    return t * jax.nn.sigmoid(t)


def _rms(x, g):
    return x * lax.rsqrt(jnp.mean(x * x, axis=-1, keepdims=True) + RMS_EPS) * g


def _modulated_norm(x, g, mod_ref, k):
    shift = mod_ref[0, :, k * D_MODEL:(k + 1) * D_MODEL]
    scale = mod_ref[0, :, (k + 1) * D_MODEL:(k + 2) * D_MODEL]
    return _rms(x, g) * (1.0 + scale) + shift


def _mod_kernel(c_ref, w_ref, b_ref, o_ref):
    o_ref[...] = jnp.dot(_silu(c_ref[...]), w_ref[...], preferred_element_type=F32) + b_ref[...]


def _mod(c, w_ada, b_ada):
    B = c.shape[0]
    n = w_ada.shape[1]
    tn = D_MODEL
    return pl.pallas_call(
        _mod_kernel,
        out_shape=jax.ShapeDtypeStruct((B, n), F32),
        grid=(n // tn,),
        in_specs=[pl.BlockSpec((B, D_MODEL), lambda j: (0, 0)),
                  pl.BlockSpec((D_MODEL, tn), lambda j: (0, j)),
                  pl.BlockSpec((1, tn), lambda j: (0, j))],
        out_specs=pl.BlockSpec((B, tn), lambda j: (0, j)),
        compiler_params=_params(("arbitrary",)),
        name="mod",
    )(c, w_ada, b_ada.reshape(1, n))


def _proj_main_layout():
    spans = [(h * RET_DK, (h + 1) * RET_DK, "rot", (h % RET_HEADS, h >= RET_HEADS))
             for h in range(2 * RET_HEADS)]
    spans += [(2 * RET_QK_W, 2 * RET_QK_W + RET_V_W, "copy", None),
              (2 * RET_QK_W + RET_V_W, COL_ATT, "silu", None),
              (COL_ATT, COL_ATT + ATT_GROUP_W, "scale", ATT_QSCALE),
              (COL_ATT + ATT_GROUP_W, PROJ_MAIN_TILES * PROJ_TN, "copy", None)]
    tiles = [[] for _ in range(PROJ_MAIN_TILES)]
    for c0, c1, kind, arg in spans:
        for t in range(c0 // PROJ_TN, (c1 - 1) // PROJ_TN + 1):
            lo, hi = max(c0, t * PROJ_TN), min(c1, (t + 1) * PROJ_TN)
            assert kind != "rot" or (lo, hi) == (c0, c1)
            tiles[t].append((lo - t * PROJ_TN, hi - t * PROJ_TN, kind, arg))
    return tiles


PROJ_MAIN_LAYOUT = _proj_main_layout()


def _proj_kernel(x_ref, mod_ref, g_ref, cos_ref, sin_ref, w_hbm, *refs, tps, n_cast):
    cast_in, (main_ref, a1_ref, a2_ref) = refs[:n_cast], refs[n_cast:n_cast + 3]
    cast_out = refs[n_cast + 3:2 * n_cast + 3]
    hf_scr, hd1_scr = refs[2 * n_cast + 3:2 * n_cast + 5]
    lhs = refs[2 * n_cast + 5:2 * n_cast + 8]
    w_ref, stage, sems = refs[2 * n_cast + 8:2 * n_cast + 11]
    tm = x_ref.shape[0]
    step = pl.program_id(0)

    @pl.when(step == 0)
    def _():
        rows = stage.shape[1]
        chunks = w_ref.shape[0] // rows

        def fetch(c):
            return pltpu.make_async_copy(w_hbm.at[pl.ds(c * rows, rows), pl.ds(0, w_ref.shape[1])],
                                         stage.at[c % 2], sems.at[c % 2])

        fetch(0).start()
        for c in range(chunks):
            if c + 1 < chunks:
                fetch(c + 1).start()
            fetch(c).wait()
            w_ref[c * rows:(c + 1) * rows, :] = stage[c % 2].astype(BF16)

    for src, dst in zip(cast_in, cast_out):
        dst[...] = src[...].astype(BF16)

    def prepare(x_ref, mod_ref, lhs):
        h0_scr, h1_scr, h2_scr = lhs
        hf = _modulated_norm(x_ref[...], g_ref[...], mod_ref, 0)
        h0_scr[...] = hf.astype(BF16)
        d1, d2 = ATT_GROUPS[1][1], ATT_GROUPS[2][1]
        n1, n2 = tm // d1, tm // d2
        for c in range(D_MODEL // LANES):
            cs = slice(c * LANES, (c + 1) * LANES)
            hf_scr[c] = hf[:, cs]
            for r in range(d1):
                part = hf_scr[c, pl.ds(r, n1, stride=d1), :]
                hd1_scr[c, r * n1:(r + 1) * n1, :] = part
                h1_scr[r * n1:(r + 1) * n1, cs] = part.astype(BF16)
            for r in range(d2):
                src = pl.ds((r % d1) * n1 + r // d1, n2, stride=d1)
                h2_scr[r * n2:(r + 1) * n2, cs] = hd1_scr[c, src, :].astype(BF16)

    def w_tile(t):
        return w_ref[:, t * PROJ_TN:(t + 1) * PROJ_TN]

    def main_tile(t, segments, h0_scr, pieces=1):
        if pieces > 1:
            width = PROJ_TN // pieces
            for p in range(pieces):
                p0, p1 = p * width, (p + 1) * width
                acc = jnp.dot(h0_scr[...], w_ref[:, t * PROJ_TN + p0:t * PROJ_TN + p1],
                              preferred_element_type=F32)
                for c0, c1, kind, arg in segments:
                    lo, hi = max(c0, p0), min(c1, p1)
                    if lo < hi:
                        assert kind in ("scale", "copy")
                        seg = acc[:, lo - p0:hi - p0]
                        seg = seg * arg if kind == "scale" else seg
                        main_ref[:, t * PROJ_TN + lo:t * PROJ_TN + hi] = seg.astype(BF16)
            return
        acc = jnp.dot(h0_scr[...], w_tile(t), preferred_element_type=F32)
        if any(kind == "rot" for _, _, kind, _ in segments):
            cos, sin = cos_ref[...], sin_ref[...]
            row = lax.broadcasted_iota(jnp.int32, cos.shape, 0)
            pos = (((step % tps) * tm + row) % RET_C + 1).astype(F32)
        base = t * PROJ_TN
        for c0, c1, kind, arg in segments:
            seg = acc[:, c0:c1]
            if kind == "rot":
                head, is_key = arg
                rate = -RET_LOG_GAMMA[head] if is_key else RET_LOG_GAMMA[head]
                dec = jnp.exp(rate * pos) * ((RET_DK ** -0.5) if is_key else 1.0)
                cd, sd = cos * dec, sin * dec
                cm = (c0 + c1) // 2
                t1, t2 = acc[:, c0:cm], acc[:, cm:c1]
                main_ref[:, base + c0:base + cm] = (t1 * cd - t2 * sd).astype(BF16)
                main_ref[:, base + cm:base + c1] = (t1 * sd + t2 * cd).astype(BF16)
            elif kind == "silu":
                main_ref[:, base + c0:base + c1] = _silu(seg).astype(BF16)
            elif kind == "scale":
                main_ref[:, base + c0:base + c1] = (seg * arg).astype(BF16)
            else:
                main_ref[:, base + c0:base + c1] = seg.astype(BF16)

    def multiply(lhs):
        h0_scr, h1_scr, h2_scr = lhs
        for t, (dil, h_scr, a_ref) in enumerate(((ATT_GROUPS[1][1], h1_scr, a1_ref),
                                                 (ATT_GROUPS[2][1], h2_scr, a2_ref))):
            n = tm // dil
            acc = jnp.dot(h_scr[...], w_tile(PROJ_MAIN_TILES + t), preferred_element_type=F32)
            gw = ATT_GROUP_W
            for r in range(dil):
                rows = slice(r * n, (r + 1) * n)
                a_ref[0, r, :, :gw] = (acc[rows, :gw] * ATT_QSCALE).astype(BF16)
                a_ref[0, r, :, gw:] = acc[rows, gw:].astype(BF16)
        last = len(PROJ_MAIN_LAYOUT) - 1
        for t, segments in enumerate(PROJ_MAIN_LAYOUT):
            main_tile(t, segments, h0_scr, pieces=PROJ_LAST_PIECES if t == last else 1)

    prepare(x_ref, mod_ref, lhs)
    multiply(lhs)


def _proj(x2, norm_g, mod3, cos, sin, w_in, cast_weights, batch, seq):
    tm, tn = PROJ_TM, PROJ_TN
    tps = seq // tm
    steps = batch * seq // tm
    assert seq % tm == 0 and tm % RET_C == 0
    d1, d2 = ATT_GROUPS[1][1], ATT_GROUPS[2][1]
    assert COL_GATE == (PROJ_MAIN_TILES + 2) * tn
    sub = lambda d: pl.BlockSpec((1, d, tm // d, tn), lambda i: (i // tps, 0, i % tps, 0))
    gate_w = w_in.shape[1] - COL_GATE
    gate_rows = D_MODEL // steps
    assert D_MODEL % (steps * BF16_SUBLANES) == 0
    slab_in = [pl.BlockSpec((pl.Element(gate_rows), pl.Element(gate_w)), lambda i: (i * gate_rows, COL_GATE))]
    slab_out = [pl.BlockSpec((gate_rows, gate_w), lambda i: (i, 0))]
    slab_shapes = [jax.ShapeDtypeStruct((D_MODEL, gate_w), BF16)]
    for w in cast_weights:
        assert w.shape[0] % (steps * BF16_SUBLANES) == 0
        slab_in.append(pl.BlockSpec((w.shape[0] // steps, w.shape[1]), lambda i: (i, 0)))
        slab_out.append(slab_in[-1])
        slab_shapes.append(jax.ShapeDtypeStruct(w.shape, BF16))
    outs = pl.pallas_call(
        functools.partial(_proj_kernel, tps=tps, n_cast=len(slab_in)),
        out_shape=[jax.ShapeDtypeStruct((batch * seq, PROJ_MAIN_TILES * tn), BF16),
                   jax.ShapeDtypeStruct((batch, d1, seq // d1, tn), BF16),
                   jax.ShapeDtypeStruct((batch, d2, seq // d2, tn), BF16)] + slab_shapes,
        grid=(steps,),
        in_specs=[pl.BlockSpec((tm, D_MODEL), lambda i: (i, 0)),
                  pl.BlockSpec((1, 1, 6 * D_MODEL), lambda i: (i // tps, 0, 0)),
                  _const_spec((1, D_MODEL)),
                  pl.BlockSpec((tm, RET_DK // 2), lambda i: (i % tps, 0)),
                  pl.BlockSpec((tm, RET_DK // 2), lambda i: (i % tps, 0)),
                  pl.BlockSpec(memory_space=pl.ANY)] + slab_in,
        out_specs=[pl.BlockSpec((tm, PROJ_MAIN_TILES * tn), lambda i: (i, 0)), sub(d1), sub(d2)] + slab_out,
        scratch_shapes=[pltpu.VMEM((D_MODEL // LANES, tm, LANES), F32)] * 2
                       + [pltpu.VMEM((tm, D_MODEL), BF16)] * 3
                       + [pltpu.VMEM((D_MODEL, COL_GATE), BF16),
                          pltpu.VMEM((2, PROJ_W_STAGE_ROWS, COL_GATE), F32),
                          pltpu.SemaphoreType.DMA((2,))],
        compiler_params=_params(("arbitrary",)),
        name="proj",
    )(x2, mod3, norm_g, cos, sin, w_in, w_in, *cast_weights)
    return outs[0], outs[1], outs[2], outs[3:]


def _ret_kernel(q_ref, k_ref, v_ref, gate_ref, cdec_ref, gng_ref, gnb_ref, o_ref, state, st_in):
    C = RET_C
    nc = q_ref.shape[1] // C
    cdec = cdec_ref[0]
    gng = gng_ref[0]
    gnb = gnb_ref[0]
    chunk = lambda ci: pl.ds(pl.multiple_of(ci * C, C), C)

    state[...] = jnp.zeros_like(state)
    st_in[0] = jnp.zeros(st_in.shape[1:], BF16)

    def advance(ci, carry):
        rows = chunk(ci)
        kv = lax.dot_general(k_ref[0, rows, :], v_ref[0, rows, :], (((0,), (0,)), ((), ())),
                             preferred_element_type=F32)
        new = (state[...] + kv) * cdec
        state[...] = new
        st_in[ci + 1] = new.astype(BF16)
        return carry

    lax.fori_loop(0, nc - 1, advance, 0, unroll=True)

    qi = lax.broadcasted_iota(jnp.int32, (C, C), 0)
    kj = lax.broadcasted_iota(jnp.int32, (C, C), 1)
    causal = qi >= kj

    def emit(ci, carry):
        rows = chunk(ci)
        q = q_ref[0, rows, :]
        s = lax.dot_general(q, k_ref[0, rows, :], (((1,), (1,)), ((), ())), preferred_element_type=F32)
        s = jnp.where(causal, s, 0.0).astype(BF16)
        o = (jnp.dot(s, v_ref[0, rows, :], preferred_element_type=F32)
             + jnp.dot(q, st_in[ci], preferred_element_type=F32))
        mu = jnp.mean(o, axis=-1, keepdims=True)
        oc = o - mu
        var = jnp.mean(oc * oc, axis=-1, keepdims=True)
        on = oc * lax.rsqrt(var + GN_EPS) * gng + gnb
        o_ref[0, rows, :] = (gate_ref[0, rows, :].astype(F32) * on).astype(BF16)
        return carry

    lax.fori_loop(0, nc, emit, 0, unroll=True)


def _retention_parts(main3, gn_g, gn_b):
    B, S, _ = main3.shape
    H, C = RET_HEADS, RET_C
    assert S % C == 0
    cdec = jnp.exp(jnp.asarray(RET_LOG_GAMMA, F32) * C)[:, None, None]
    kq = RET_QK_W // RET_DK
    kv = 2 * RET_QK_W // RET_DV
    kg = kv + RET_V_W // RET_DV
    in_specs = [pl.BlockSpec((1, S, RET_DK), lambda b, h: (b, 0, h)),
                pl.BlockSpec((1, S, RET_DK), lambda b, h: (b, 0, kq + h)),
                pl.BlockSpec((1, S, RET_DV), lambda b, h: (b, 0, kv + h)),
                pl.BlockSpec((1, S, RET_DV), lambda b, h: (b, 0, kg + h)),
                pl.BlockSpec((1, 1, 1), lambda b, h: (h, 0, 0)),
                pl.BlockSpec((1, 1, RET_DV), lambda b, h: (h, 0, 0)),
                pl.BlockSpec((1, 1, RET_DV), lambda b, h: (h, 0, 0))]
    args = [main3, main3, main3, main3, cdec, gn_g.reshape(H, 1, RET_DV), gn_b.reshape(H, 1, RET_DV)]
    return _CallParts(None, _ret_kernel, (B, H), in_specs, args,
                      [pl.BlockSpec((1, S, RET_DV), lambda b, h: (b, 0, h))],
                      [jax.ShapeDtypeStruct((B, S, RET_V_W), BF16)],
                      [pltpu.VMEM((RET_DK, RET_DV), F32), pltpu.VMEM((S // C, RET_DK, RET_DV), BF16)])


def _t5_bucket(dist):
    dist = np.asarray(dist)
    max_exact = REL_BUCKETS // 2
    d_f = np.maximum(dist, 1).astype(np.float32)
    large = max_exact + (np.log(d_f / np.float32(max_exact)) / np.float32(math.log(REL_MAX_DIST / max_exact))
                         * np.float32(REL_BUCKETS - max_exact)).astype(np.int32)
    large = np.minimum(large, REL_BUCKETS - 1)
    return np.where(dist < max_exact, dist, large).astype(np.int32)


def _attn_unit(n, q_ref, k_ref, v_ref, bias_scr, vaug_scr, o_ref, stats_ref):
    blk, dh = ATT_BLK, ATT_HEAD_DIM
    windowed = k_ref.shape[0] > blk
    if windowed:
        win = pl.ds(pl.multiple_of(jnp.maximum(n - 1, 0) * blk, blk), 2 * blk)
        first = (n == 0).astype(jnp.int32)
    heads = ATT_HEADS_PER_GROUP
    lane = lax.broadcasted_iota(jnp.int32, (blk, LSE_W), 1)
    stats = jnp.ones((blk, LSE_W), F32)
    for h in range(heads):
        hs = slice(h * dh, (h + 1) * dh)
        if windowed:
            kb, vb, bias = k_ref[win, hs], v_ref[win, hs], bias_scr[first, h]
        else:
            kb, vb, bias = k_ref[:, hs], v_ref[:, hs], bias_scr[h]
        vaug_scr[h, :, :dh] = vb
        s = lax.dot_general(q_ref[:, hs], kb, (((1,), (1,)), ((), ())), preferred_element_type=F32) + bias
        mx = jnp.max(s, axis=-1, keepdims=True)
        e = jnp.exp(s - mx).astype(BF16)
        o_den = jnp.dot(e, vaug_scr[h], preferred_element_type=F32)
        o_ref[:, hs] = o_den[:, :dh].astype(BF16)
        stats = jnp.where(lane == h, mx, jnp.where(lane == heads + h, o_den[:, dh:], stats))
    stats_ref[...] = stats


def _attn_init(tab_ref, q0, k0, v0, q1, k1, v1, q2, k2, v2, bk0, bk1, bk2,
               o0, l0, o1, l1, o2, l2, bias0, bias1, bias2, vaug0, vaug1, vaug2):
    blk = ATT_BLK

    @pl.when((pl.program_id(0) == 0) & (pl.program_id(1) == 0))
    def _():
        for vaug in (vaug0, vaug1, vaug2):
            vaug[...] = jnp.ones(vaug.shape, BF16)
        qi = lax.broadcasted_iota(jnp.int32, (blk, 2 * blk), 0)
        kj = lax.broadcasted_iota(jnp.int32, (blk, 2 * blk), 1)
        m = blk + qi - kj
        band = (m >= 0) & (m <= blk)
        masked = jnp.full((blk, blk), NEG, F32)
        for gi, (bk_ref, bias_scr) in enumerate(((bk0, bias0), (bk1, bias1), (bk2, bias2))):
            bucket = bk_ref[...]
            for h in range(ATT_HEADS_PER_GROUP):
                col = gi * ATT_HEADS_PER_GROUP + h
                acc = jnp.zeros((blk, 2 * blk), F32)
                for t in range(REL_BUCKETS):
                    acc = jnp.where(bucket == t, tab_ref[t, col], acc)
                tile = jnp.where(band, acc, NEG)
                if len(bias_scr.shape) == 4:
                    bias_scr[0, h] = tile
                    bias_scr[1, h] = jnp.concatenate([tile[:, blk:], masked], axis=1)
                else:
                    bias_scr[h] = tile[:, blk:]


def _attn_kernel(tab_ref, q0, k0, v0, q1, k1, v1, q2, k2, v2, bk0, bk1, bk2,
                 o0, l0, o1, l1, o2, l2, bias0, bias1, bias2, vaug0, vaug1, vaug2, *, nb1, per_step):
    i = pl.program_id(1)
    blk = ATT_BLK
    for u in range(per_step):
        rows = pl.ds(u * blk, blk)
        n = i * per_step + u
        _attn_unit(n, q0.at[rows], k0, v0, bias0, vaug0.at[u], o0.at[rows], l0.at[rows])
        _attn_unit(n % nb1, q1.at[rows], k1, v1, bias1, vaug1.at[u], o1.at[rows], l1.at[rows])
        _attn_unit(0, q2.at[u], k2.at[u], v2.at[u], bias2, vaug2.at[u], o2.at[u], l2.at[u])


def _attention_parts(main3, att1, att2, rel_bias):
    B, S, _ = main3.shape
    blk, gw = ATT_BLK, ATT_GROUP_W
    steps = S // blk
    lens = [S // dil for _, dil in ATT_GROUPS]
    nbs = [L // blk for L in lens]
    assert all(win // dil == blk for win, dil in ATT_GROUPS)
    assert nbs[0] == steps and nbs[1] >= 2 and nbs[2] == 1
    c0 = COL_ATT // gw
    nb1 = nbs[1]
    G = ATT_BLOCKS_PER_STEP
    assert nb1 % G == 0 and steps % G == 0
    sub1 = lambda b, i: (b, (i * G) // nb1)
    in_specs = [pl.BlockSpec(memory_space=pltpu.SMEM),
                pl.BlockSpec((None, G * blk, gw), lambda b, i: (b, i, c0)),
                pl.BlockSpec((None, S, gw), lambda b, i: (b, 0, c0 + 1)),
                pl.BlockSpec((None, S, gw), lambda b, i: (b, 0, c0 + 2)),
                pl.BlockSpec((None, None, G * blk, gw), lambda b, i: (*sub1(b, i), (i * G) % nb1 // G, 0)),
                pl.BlockSpec((None, None, lens[1], gw), lambda b, i: (*sub1(b, i), 0, 1)),
                pl.BlockSpec((None, None, lens[1], gw), lambda b, i: (*sub1(b, i), 0, 2)),
                pl.BlockSpec((None, G, blk, gw), lambda b, i: (b, i, 0, 0)),
                pl.BlockSpec((None, G, blk, gw), lambda b, i: (b, i, 0, 1)),
                pl.BlockSpec((None, G, blk, gw), lambda b, i: (b, i, 0, 2))]
    args = [rel_bias] + [main3] * 3 + [att1] * 3 + [att2] * 3
    qi = np.arange(blk)[:, None]
    kj = np.arange(2 * blk)[None, :]
    m = blk + qi - kj
    for _, dil in ATT_GROUPS:
        in_specs.append(_const_spec((blk, 2 * blk)))
        args.append(jnp.asarray(_t5_bucket(np.clip(m, 0, blk) * dil)))
    out_specs, out_shapes = [], []
    for (_, dil), nb in zip(ATT_GROUPS, nbs):
        for w, dt in ((gw, BF16), (LSE_W, F32)):
            if nb == steps:
                out_specs.append(pl.BlockSpec((None, G * blk, w), lambda b, i: (b, i, 0)))
                out_shapes.append(jax.ShapeDtypeStruct((B, S, w), dt))
            elif nb == 1:
                out_specs.append(pl.BlockSpec((None, G, blk, w), lambda b, i: (b, i, 0, 0)))
                out_shapes.append(jax.ShapeDtypeStruct((B, dil, S // dil, w), dt))
            else:
                out_specs.append(pl.BlockSpec((None, None, G * blk, w),
                                              lambda b, i: (*sub1(b, i), (i * G) % nb1 // G, 0)))
                out_shapes.append(jax.ShapeDtypeStruct((B, dil, S // dil, w), dt))
    heads = ATT_HEADS_PER_GROUP
    scratch = [pltpu.VMEM((2, heads, blk, 2 * blk), F32), pltpu.VMEM((2, heads, blk, 2 * blk), F32),
               pltpu.VMEM((heads, blk, blk), F32),
               pltpu.VMEM((G, heads, 2 * blk, 2 * ATT_HEAD_DIM), BF16),
               pltpu.VMEM((G, heads, 2 * blk, 2 * ATT_HEAD_DIM), BF16),
               pltpu.VMEM((G, heads, blk, 2 * ATT_HEAD_DIM), BF16)]
    return _CallParts(_attn_init, functools.partial(_attn_kernel, nb1=nb1, per_step=G), (B, steps // G),
                      in_specs, args, out_specs, out_shapes, scratch)


class _CallParts(NamedTuple):
    init: object
    body: object
    grid: tuple
    in_specs: list
    args: list
    out_specs: list
    out_shapes: list
    scratch: list


def _call(parts, name):
    def body(*refs):
        if parts.init is not None:
            parts.init(*refs)
        parts.body(*refs)

    return pl.pallas_call(
        body,
        out_shape=parts.out_shapes,
        grid=parts.grid,
        in_specs=parts.in_specs,
        out_specs=parts.out_specs,
        scratch_shapes=parts.scratch,
        compiler_params=_params(("arbitrary",) * len(parts.grid)),
        name=name,
    )(*parts.args)


def _mixers(main3, att1, att2, rel_bias, gn_g, gn_b):
    retg, = _call(_retention_parts(main3, gn_g, gn_b), "ret")
    att = _call(_attention_parts(main3, att1, att2, rel_bias), "attn")
    return retg, att[0::2], att[1::2]


def _merge_kernel(x_ref, g_ref, mod_ref, retg_ref, o0_ref, o1_ref, o2_ref, s0_ref, s1_ref, s2_ref,
                  wg_ref, wr_ref, wa_ref, wo_ref, out_ref, o1_scr, o2_scr, s1_scr, s2_scr):
    tm = x_ref.shape[0]
    for dil, o_ref, s_ref, o_scr, s_scr in ((ATT_GROUPS[1][1], o1_ref, s1_ref, o1_scr, s1_scr),
                                            (ATT_GROUPS[2][1], o2_ref, s2_ref, o2_scr, s2_scr)):
        n = tm // dil
        for r in range(dil):
            s_scr[pl.ds(r, n, stride=dil), :] = s_ref[0, r]
            for h in range(ATT_HEADS_PER_GROUP):
                hs = slice(h * ATT_HEAD_DIM, (h + 1) * ATT_HEAD_DIM)
                o_scr[h, pl.ds(r, n, stride=dil), :] = o_ref[0, r, :, hs].astype(F32)
    part = tm // MERGE_ROW_PARTS
    for p in range(MERGE_ROW_PARTS):
        rows = slice(p * part, (p + 1) * part)
        ret_out = jnp.dot(retg_ref[rows, :], wr_ref[...], preferred_element_type=F32)
        x = x_ref[rows, :]
        h = _modulated_norm(x, g_ref[...], mod_ref, 0).astype(BF16)
        gates = jax.nn.sigmoid(jnp.dot(h, wg_ref[...], preferred_element_type=F32))
        stats = (s0_ref[rows, :], s1_scr[rows, :], s2_scr[rows, :])
        top = jnp.maximum(jnp.maximum(stats[0], stats[1]), stats[2])
        a = [jnp.exp(st - top) for st in stats]
        lane = lax.broadcasted_iota(jnp.int32, top.shape, 1)
        heads = ATT_HEADS_PER_GROUP
        den = sum(ag * pltpu.roll(st, LSE_W - heads, axis=1) for ag, st in zip(a, stats))
        inv = 1.0 / jnp.where(lane < heads, den, 1.0)
        w0, w1, w2 = (ag * inv for ag in a)
        parts = []
        for hd in range(heads):
            hs = slice(hd * ATT_HEAD_DIM, (hd + 1) * ATT_HEAD_DIM)
            parts.append(w0[:, hd:hd + 1] * o0_ref[rows, hs].astype(F32)
                         + w1[:, hd:hd + 1] * o1_scr[hd, rows, :]
                         + w2[:, hd:hd + 1] * o2_scr[hd, rows, :])
        att = jnp.concatenate(parts, axis=-1).astype(BF16)
        att_out = jnp.dot(att, wa_ref[...], preferred_element_type=F32)
        merged = gates[:, :D_MODEL] * ret_out + gates[:, D_MODEL:] * att_out
        y = jnp.dot(merged.astype(BF16), wo_ref[...], preferred_element_type=F32)
        out_ref[rows, :] = x + mod_ref[0, :, 2 * D_MODEL:3 * D_MODEL] * y


def _merge(x2, norm_g, mod3, retg2, o, lse, w_gate, w_ret_out, w_att_out, w_o, batch, seq):
    tm = ROW_TM
    tps = seq // tm
    row = lambda w: pl.BlockSpec((tm, w), lambda i: (i, 0))
    sub = lambda d, w: pl.BlockSpec((1, d, tm // d, w), lambda i: (i // tps, 0, i % tps, 0))
    d1, d2 = ATT_GROUPS[1][1], ATT_GROUPS[2][1]
    return pl.pallas_call(
        _merge_kernel,
        out_shape=jax.ShapeDtypeStruct((batch * seq, D_MODEL), F32),
        grid=(batch * seq // tm,),
        in_specs=[row(D_MODEL),
                  _const_spec((1, D_MODEL)),
                  pl.BlockSpec((1, 1, 6 * D_MODEL), lambda i: (i // tps, 0, 0)),
                  row(RET_V_W),
                  row(ATT_GROUP_W), sub(d1, ATT_GROUP_W), sub(d2, ATT_GROUP_W),
                  row(LSE_W), sub(d1, LSE_W), sub(d2, LSE_W),
                  _const_spec((D_MODEL, 2 * D_MODEL)),
                  _const_spec((RET_V_W, D_MODEL)),
                  _const_spec((ATT_GROUP_W, D_MODEL)),
                  _const_spec((D_MODEL, D_MODEL))],
        out_specs=row(D_MODEL),
        scratch_shapes=[pltpu.VMEM((ATT_HEADS_PER_GROUP, tm, ATT_HEAD_DIM), F32)] * 2
                       + [pltpu.VMEM((tm, LSE_W), F32)] * 2,
        compiler_params=_params(("arbitrary",)),
        name="merge",
    )(x2, norm_g, mod3, retg2, *o, *lse, w_gate, w_ret_out, w_att_out, w_o)


def _mlp_kernel(x_ref, mod_ref, g2_ref, gf_ref, w1_ref, w2_ref, out_ref, *, final):
    part = x_ref.shape[0] // MLP_ROW_PARTS
    for p in range(MLP_ROW_PARTS):
        rows = slice(p * part, (p + 1) * part)
        x = x_ref[rows, :]
        h = _modulated_norm(x, g2_ref[...], mod_ref, 3).astype(BF16)
        u = jnp.maximum(jnp.dot(h, w1_ref[...], preferred_element_type=F32), 0.0)
        y = jnp.dot((u * u).astype(BF16), w2_ref[...], preferred_element_type=F32)
        x = x + mod_ref[0, :, 5 * D_MODEL:6 * D_MODEL] * y
        out_ref[rows, :] = _rms(x, gf_ref[...]) if final else x


def _mlp(x2, mod3, norm2_g, norm_f_g, w1, w2, seq, final):
    rows = x2.shape[0]
    tm = MLP_TM
    tps = seq // tm
    return pl.pallas_call(
        functools.partial(_mlp_kernel, final=final),
        out_shape=jax.ShapeDtypeStruct((rows, D_MODEL), F32),
        grid=(rows // tm,),
        in_specs=[pl.BlockSpec((tm, D_MODEL), lambda i: (i, 0)),
                  pl.BlockSpec((1, 1, 6 * D_MODEL), lambda i: (i // tps, 0, 0)),
                  _const_spec((1, D_MODEL)),
                  _const_spec((1, D_MODEL)),
                  _const_spec((D_MODEL, D_FF)),
                  _const_spec((D_FF, D_MODEL))],
        out_specs=pl.BlockSpec((tm, D_MODEL), lambda i: (i, 0)),
        compiler_params=_params(("arbitrary",)),
        name="mlp",
    )(x2, mod3, norm2_g, norm_f_g, w1, w2)


def kernel(x, c, w_ada, b_ada, norm1_g, w_in, rel_bias, ret_gn_g, ret_gn_b, w_ret_out, w_att_out,
           w_o, norm2_g, w_ff1, w_ff2, norm_f_g):
    B, S, D = x.shape
    depth = w_ada.shape[0]
    half = RET_DK // 2
    ang = np.arange(S, dtype=np.float64)[:, None] * ROPE_BASE ** (-np.arange(half, dtype=np.float64) / half)
    cos, sin = jnp.asarray(np.cos(ang), F32), jnp.asarray(np.sin(ang), F32)
    x2 = x.reshape(B * S, D)
    for l in range(depth):
        g1 = norm1_g[l].reshape(1, D)
        mod3 = _mod(c, w_ada[l], b_ada[l]).reshape(B, 1, 6 * D)
        main2, att1, att2, (w_gate_bf, w_ret_bf, w_o_bf, w_ff1_bf, w_ff2_bf) = _proj(
            x2, g1, mod3, cos, sin, w_in[l], [w_ret_out[l], w_o[l], w_ff1[l], w_ff2[l]], B, S)
        main3 = main2.reshape(B, S, main2.shape[1])
        retg, o, stats = _mixers(main3, att1, att2, rel_bias, ret_gn_g[l], ret_gn_b[l])
        o = [o[0].reshape(B * S, ATT_GROUP_W), o[1], o[2]]
        stats = [stats[0].reshape(B * S, LSE_W), stats[1], stats[2]]
        x2 = _merge(x2, g1, mod3, retg.reshape(B * S, RET_V_W), o, stats,
                    w_gate_bf, w_ret_bf, w_att_out[l].astype(BF16), w_o_bf, B, S)
        x2 = _mlp(x2, mod3, norm2_g[l].reshape(1, D), norm_f_g.reshape(1, D),
                  w_ff1_bf, w_ff2_bf, S, final=l == depth - 1)
    return x2.reshape(B, S, D)
```

```python
import functools
import math
from typing import NamedTuple

import jax
import jax.numpy as jnp
import numpy as np
from jax import lax
from jax.experimental import pallas as pl
from jax.experimental.pallas import tpu as pltpu

F32 = jnp.float32
BF16 = jnp.bfloat16

D_MODEL = 1024
RET_HEADS = 4
RET_DK = 256
RET_DV = 512
RET_C = 256
RET_LOG_GAMMA = tuple(math.log1p(-(2.0 ** (-5.0 - h))) for h in range(RET_HEADS))
RET_QK_W = RET_HEADS * RET_DK
RET_V_W = RET_HEADS * RET_DV
ATT_GROUPS = ((128, 1), (512, 4), (2048, 16))
ATT_HEADS_PER_GROUP = 4
ATT_HEAD_DIM = 128
ATT_GROUP_W = ATT_HEADS_PER_GROUP * ATT_HEAD_DIM
ATT_BLK = 128
ATT_BLOCKS_PER_STEP = 8
LANES = 128
BF16_SUBLANES = 16
LSE_W = LANES
REL_BUCKETS = 32
REL_MAX_DIST = 2048
D_FF = 4 * D_MODEL
RMS_EPS = 1e-6
GN_EPS = 1e-5
ROPE_BASE = 10000.0
NEG = -1e30
ATT_QSCALE = ATT_HEAD_DIM ** -0.5

COL_ATT = 2 * RET_QK_W + 2 * RET_V_W
COL_GATE = COL_ATT + 9 * ATT_GROUP_W

VMEM_LIMIT = 56 * 1024 * 1024

PROJ_TM = 256
PROJ_TN = 3 * ATT_GROUP_W
PROJ_MAIN_TILES = (COL_ATT + PROJ_TN) // PROJ_TN
PROJ_LAST_PIECES = 3
PROJ_W_STAGE_ROWS = 64
ROW_TM = 512
MERGE_ROW_PARTS = 2
MLP_TM = 1024
MLP_ROW_PARTS = 2


def _params(sem):
    return pltpu.CompilerParams(dimension_semantics=sem, vmem_limit_bytes=VMEM_LIMIT)


def _const_spec(shape):
    zeros = (0,) * len(shape)
    return pl.BlockSpec(shape, lambda *_: zeros, pipeline_mode=pl.Buffered(1))


def _silu(t):
    return t * jax.nn.sigmoid(t)


def _rms(x, g):
    return x * lax.rsqrt(jnp.mean(x * x, axis=-1, keepdims=True) + RMS_EPS) * g


def _modulated_norm(x, g, mod_ref, k):
    shift = mod_ref[0, :, k * D_MODEL:(k + 1) * D_MODEL]
    scale = mod_ref[0, :, (k + 1) * D_MODEL:(k + 2) * D_MODEL]
    return _rms(x, g) * (1.0 + scale) + shift


def _mod_kernel(c_ref, w_ref, b_ref, o_ref):
    o_ref[...] = jnp.dot(_silu(c_ref[...]), w_ref[...], preferred_element_type=F32) + b_ref[...]


def _mod(c, w_ada, b_ada):
    B = c.shape[0]
    n = w_ada.shape[1]
    tn = n // 2
    return pl.pallas_call(
        _mod_kernel,
        out_shape=jax.ShapeDtypeStruct((B, n), F32),
        grid=(n // tn,),
        in_specs=[pl.BlockSpec((B, D_MODEL), lambda j: (0, 0)),
                  pl.BlockSpec((D_MODEL, tn), lambda j: (0, j)),
                  pl.BlockSpec((1, tn), lambda j: (0, j))],
        out_specs=pl.BlockSpec((B, tn), lambda j: (0, j)),
        compiler_params=_params(("arbitrary",)),
        name="mod",
    )(c, w_ada, b_ada.reshape(1, n))


def _proj_main_layout():
    spans = [(h * RET_DK, (h + 1) * RET_DK, "rot", (h % RET_HEADS, h >= RET_HEADS))
             for h in range(2 * RET_HEADS)]
    spans += [(2 * RET_QK_W, 2 * RET_QK_W + RET_V_W, "copy", None),
              (2 * RET_QK_W + RET_V_W, COL_ATT, "silu", None),
              (COL_ATT, COL_ATT + ATT_GROUP_W, "scale", ATT_QSCALE),
              (COL_ATT + ATT_GROUP_W, PROJ_MAIN_TILES * PROJ_TN, "copy", None)]
    tiles = [[] for _ in range(PROJ_MAIN_TILES)]
    for c0, c1, kind, arg in spans:
        for t in range(c0 // PROJ_TN, (c1 - 1) // PROJ_TN + 1):
            lo, hi = max(c0, t * PROJ_TN), min(c1, (t + 1) * PROJ_TN)
            assert kind != "rot" or (lo, hi) == (c0, c1)
            tiles[t].append((lo - t * PROJ_TN, hi - t * PROJ_TN, kind, arg))
    return tiles


PROJ_MAIN_LAYOUT = _proj_main_layout()


def _proj_kernel(x_ref, mod_ref, g_ref, cos_ref, sin_ref, w_hbm, *refs, tps, n_cast):
    cast_in, (main_ref, a1_ref, a2_ref) = refs[:n_cast], refs[n_cast:n_cast + 3]
    cast_out = refs[n_cast + 3:2 * n_cast + 3]
    hf_scr, hd1_scr = refs[2 * n_cast + 3:2 * n_cast + 5]
    lhs = refs[2 * n_cast + 5:2 * n_cast + 8]
    w_ref, stage, sems = refs[2 * n_cast + 8:2 * n_cast + 11]
    tm = x_ref.shape[0]
    step = pl.program_id(0)

    @pl.when(step == 0)
    def _():
        rows = stage.shape[1]
        chunks = w_ref.shape[0] // rows

        def fetch(c):
            return pltpu.make_async_copy(w_hbm.at[pl.ds(c * rows, rows), pl.ds(0, w_ref.shape[1])],
                                         stage.at[c % 2], sems.at[c % 2])

        fetch(0).start()
        for c in range(chunks):
            if c + 1 < chunks:
                fetch(c + 1).start()
            fetch(c).wait()
            w_ref[c * rows:(c + 1) * rows, :] = stage[c % 2].astype(BF16)

    for src, dst in zip(cast_in, cast_out):
        dst[...] = src[...].astype(BF16)

    def prepare(x_ref, mod_ref, lhs):
        h0_scr, h1_scr, h2_scr = lhs
        hf = _modulated_norm(x_ref[...], g_ref[...], mod_ref, 0)
        h0_scr[...] = hf.astype(BF16)
        d1, d2 = ATT_GROUPS[1][1], ATT_GROUPS[2][1]
        n1, n2 = tm // d1, tm // d2
        for c in range(D_MODEL // LANES):
            cs = slice(c * LANES, (c + 1) * LANES)
            hf_scr[c] = hf[:, cs]
            for r in range(d1):
                part = hf_scr[c, pl.ds(r, n1, stride=d1), :]
                hd1_scr[c, r * n1:(r + 1) * n1, :] = part
                h1_scr[r * n1:(r + 1) * n1, cs] = part.astype(BF16)
            for r in range(d2):
                src = pl.ds((r % d1) * n1 + r // d1, n2, stride=d1)
                h2_scr[r * n2:(r + 1) * n2, cs] = hd1_scr[c, src, :].astype(BF16)

    def w_tile(t):
        return w_ref[:, t * PROJ_TN:(t + 1) * PROJ_TN]

    def main_tile(t, segments, h0_scr, pieces=1):
        if pieces > 1:
            width = PROJ_TN // pieces
            for p in range(pieces):
                p0, p1 = p * width, (p + 1) * width
                acc = jnp.dot(h0_scr[...], w_ref[:, t * PROJ_TN + p0:t * PROJ_TN + p1],
                              preferred_element_type=F32)
                for c0, c1, kind, arg in segments:
                    lo, hi = max(c0, p0), min(c1, p1)
                    if lo < hi:
                        assert kind in ("scale", "copy")
                        seg = acc[:, lo - p0:hi - p0]
                        seg = seg * arg if kind == "scale" else seg
                        main_ref[:, t * PROJ_TN + lo:t * PROJ_TN + hi] = seg.astype(BF16)
            return
        acc = jnp.dot(h0_scr[...], w_tile(t), preferred_element_type=F32)
        if any(kind == "rot" for _, _, kind, _ in segments):
            cos, sin = cos_ref[...], sin_ref[...]
            row = lax.broadcasted_iota(jnp.int32, cos.shape, 0)
            pos = (((step % tps) * tm + row) % RET_C + 1).astype(F32)
        base = t * PROJ_TN
        for c0, c1, kind, arg in segments:
            seg = acc[:, c0:c1]
            if kind == "rot":
                head, is_key = arg
                rate = -RET_LOG_GAMMA[head] if is_key else RET_LOG_GAMMA[head]
                dec = jnp.exp(rate * pos) * ((RET_DK ** -0.5) if is_key else 1.0)
                cd, sd = cos * dec, sin * dec
                cm = (c0 + c1) // 2
                t1, t2 = acc[:, c0:cm], acc[:, cm:c1]
                main_ref[:, base + c0:base + cm] = (t1 * cd - t2 * sd).astype(BF16)
                main_ref[:, base + cm:base + c1] = (t1 * sd + t2 * cd).astype(BF16)
            elif kind == "silu":
                main_ref[:, base + c0:base + c1] = _silu(seg).astype(BF16)
            elif kind == "scale":
                main_ref[:, base + c0:base + c1] = (seg * arg).astype(BF16)
            else:
                main_ref[:, base + c0:base + c1] = seg.astype(BF16)

    def multiply(lhs):
        h0_scr, h1_scr, h2_scr = lhs
        for t, (dil, h_scr, a_ref) in enumerate(((ATT_GROUPS[1][1], h1_scr, a1_ref),
                                                 (ATT_GROUPS[2][1], h2_scr, a2_ref))):
            n = tm // dil
            acc = jnp.dot(h_scr[...], w_tile(PROJ_MAIN_TILES + t), preferred_element_type=F32)
            gw = ATT_GROUP_W
            for r in range(dil):
                rows = slice(r * n, (r + 1) * n)
                a_ref[0, r, :, :gw] = (acc[rows, :gw] * ATT_QSCALE).astype(BF16)
                a_ref[0, r, :, gw:] = acc[rows, gw:].astype(BF16)
        last = len(PROJ_MAIN_LAYOUT) - 1
        for t, segments in enumerate(PROJ_MAIN_LAYOUT):
            main_tile(t, segments, h0_scr, pieces=PROJ_LAST_PIECES if t == last else 1)

    prepare(x_ref, mod_ref, lhs)
    multiply(lhs)


def _proj(x2, norm_g, mod3, cos, sin, w_in, cast_weights, batch, seq):
    tm, tn = PROJ_TM, PROJ_TN
    tps = seq // tm
    steps = batch * seq // tm
    assert seq % tm == 0 and tm % RET_C == 0
    d1, d2 = ATT_GROUPS[1][1], ATT_GROUPS[2][1]
    assert COL_GATE == (PROJ_MAIN_TILES + 2) * tn
    sub = lambda d: pl.BlockSpec((1, d, tm // d, tn), lambda i: (i // tps, 0, i % tps, 0))
    gate_w = w_in.shape[1] - COL_GATE
    gate_rows = D_MODEL // steps
    assert D_MODEL % (steps * BF16_SUBLANES) == 0
    slab_in = [pl.BlockSpec((pl.Element(gate_rows), pl.Element(gate_w)), lambda i: (i * gate_rows, COL_GATE))]
    slab_out = [pl.BlockSpec((gate_rows, gate_w), lambda i: (i, 0))]
    slab_shapes = [jax.ShapeDtypeStruct((D_MODEL, gate_w), BF16)]
    for w in cast_weights:
        assert w.shape[0] % (steps * BF16_SUBLANES) == 0
        slab_in.append(pl.BlockSpec((w.shape[0] // steps, w.shape[1]), lambda i: (i, 0)))
        slab_out.append(slab_in[-1])
        slab_shapes.append(jax.ShapeDtypeStruct(w.shape, BF16))
    outs = pl.pallas_call(
        functools.partial(_proj_kernel, tps=tps, n_cast=len(slab_in)),
        out_shape=[jax.ShapeDtypeStruct((batch * seq, PROJ_MAIN_TILES * tn), BF16),
                   jax.ShapeDtypeStruct((batch, d1, seq // d1, tn), BF16),
                   jax.ShapeDtypeStruct((batch, d2, seq // d2, tn), BF16)] + slab_shapes,
        grid=(steps,),
        in_specs=[pl.BlockSpec((tm, D_MODEL), lambda i: (i, 0)),
                  pl.BlockSpec((1, 1, 6 * D_MODEL), lambda i: (i // tps, 0, 0)),
                  _const_spec((1, D_MODEL)),
                  pl.BlockSpec((tm, RET_DK // 2), lambda i: (i % tps, 0)),
                  pl.BlockSpec((tm, RET_DK // 2), lambda i: (i % tps, 0)),
                  pl.BlockSpec(memory_space=pl.ANY)] + slab_in,
        out_specs=[pl.BlockSpec((tm, PROJ_MAIN_TILES * tn), lambda i: (i, 0)), sub(d1), sub(d2)] + slab_out,
        scratch_shapes=[pltpu.VMEM((D_MODEL // LANES, tm, LANES), F32)] * 2
                       + [pltpu.VMEM((tm, D_MODEL), BF16)] * 3
                       + [pltpu.VMEM((D_MODEL, COL_GATE), BF16),
                          pltpu.VMEM((2, PROJ_W_STAGE_ROWS, COL_GATE), F32),
                          pltpu.SemaphoreType.DMA((2,))],
        compiler_params=_params(("arbitrary",)),
        name="proj",
    )(x2, mod3, norm_g, cos, sin, w_in, w_in, *cast_weights)
    return outs[0], outs[1], outs[2], outs[3:]


def _ret_kernel(q_ref, k_ref, v_ref, gate_ref, cdec_ref, gng_ref, gnb_ref, o_ref, state, st_in):
    C = RET_C
    nc = q_ref.shape[1] // C
    cdec = cdec_ref[0]
    gng = gng_ref[0]
    gnb = gnb_ref[0]
    chunk = lambda ci: pl.ds(pl.multiple_of(ci * C, C), C)

    state[...] = jnp.zeros_like(state)
    st_in[0] = jnp.zeros(st_in.shape[1:], BF16)

    def advance(ci, carry):
        rows = chunk(ci)
        kv = lax.dot_general(k_ref[0, rows, :], v_ref[0, rows, :], (((0,), (0,)), ((), ())),
                             preferred_element_type=F32)
        new = (state[...] + kv) * cdec
        state[...] = new
        st_in[ci + 1] = new.astype(BF16)
        return carry

    lax.fori_loop(0, nc - 1, advance, 0, unroll=True)

    qi = lax.broadcasted_iota(jnp.int32, (C, C), 0)
    kj = lax.broadcasted_iota(jnp.int32, (C, C), 1)
    causal = qi >= kj

    def emit(ci, carry):
        rows = chunk(ci)
        q = q_ref[0, rows, :]
        s = lax.dot_general(q, k_ref[0, rows, :], (((1,), (1,)), ((), ())), preferred_element_type=F32)
        s = jnp.where(causal, s, 0.0).astype(BF16)
        o = (jnp.dot(s, v_ref[0, rows, :], preferred_element_type=F32)
             + jnp.dot(q, st_in[ci], preferred_element_type=F32))
        mu = jnp.mean(o, axis=-1, keepdims=True)
        oc = o - mu
        var = jnp.mean(oc * oc, axis=-1, keepdims=True)
        on = oc * lax.rsqrt(var + GN_EPS) * gng + gnb
        o_ref[0, rows, :] = (gate_ref[0, rows, :].astype(F32) * on).astype(BF16)
        return carry

    lax.fori_loop(0, nc, emit, 0, unroll=True)


def _retention_parts(main3, gn_g, gn_b):
    B, S, _ = main3.shape
    H, C = RET_HEADS, RET_C
    assert S % C == 0
    cdec = jnp.exp(jnp.asarray(RET_LOG_GAMMA, F32) * C)[:, None, None]
    kq = RET_QK_W // RET_DK
    kv = 2 * RET_QK_W // RET_DV
    kg = kv + RET_V_W // RET_DV
    in_specs = [pl.BlockSpec((1, S, RET_DK), lambda b, h: (b, 0, h)),
                pl.BlockSpec((1, S, RET_DK), lambda b, h: (b, 0, kq + h)),
                pl.BlockSpec((1, S, RET_DV), lambda b, h: (b, 0, kv + h)),
                pl.BlockSpec((1, S, RET_DV), lambda b, h: (b, 0, kg + h)),
                pl.BlockSpec((1, 1, 1), lambda b, h: (h, 0, 0)),
                pl.BlockSpec((1, 1, RET_DV), lambda b, h: (h, 0, 0)),
                pl.BlockSpec((1, 1, RET_DV), lambda b, h: (h, 0, 0))]
    args = [main3, main3, main3, main3, cdec, gn_g.reshape(H, 1, RET_DV), gn_b.reshape(H, 1, RET_DV)]
    return _CallParts(None, _ret_kernel, (B, H), in_specs, args,
                      [pl.BlockSpec((1, S, RET_DV), lambda b, h: (b, 0, h))],
                      [jax.ShapeDtypeStruct((B, S, RET_V_W), BF16)],
                      [pltpu.VMEM((RET_DK, RET_DV), F32), pltpu.VMEM((S // C, RET_DK, RET_DV), BF16)])


def _t5_bucket(dist):
    dist = np.asarray(dist)
    max_exact = REL_BUCKETS // 2
    d_f = np.maximum(dist, 1).astype(np.float32)
    large = max_exact + (np.log(d_f / np.float32(max_exact)) / np.float32(math.log(REL_MAX_DIST / max_exact))
                         * np.float32(REL_BUCKETS - max_exact)).astype(np.int32)
    large = np.minimum(large, REL_BUCKETS - 1)
    return np.where(dist < max_exact, dist, large).astype(np.int32)


def _attn_unit(n, q_ref, k_ref, v_ref, bias_scr, vaug_scr, o_ref, stats_ref):
    blk, dh = ATT_BLK, ATT_HEAD_DIM
    windowed = k_ref.shape[0] > blk
    if windowed and isinstance(n, int):
        win, first = pl.ds(max(n - 1, 0) * blk, 2 * blk), int(n == 0)
    elif windowed:
        win = pl.ds(pl.multiple_of(jnp.maximum(n - 1, 0) * blk, blk), 2 * blk)
        first = (n == 0).astype(jnp.int32)
    heads = ATT_HEADS_PER_GROUP
    lane = lax.broadcasted_iota(jnp.int32, (blk, LSE_W), 1)
    stats = jnp.ones((blk, LSE_W), F32)
    for h in range(heads):
        hs = slice(h * dh, (h + 1) * dh)
        if windowed:
            kb, vb, bias = k_ref[win, hs], v_ref[win, hs], bias_scr[first, h]
        else:
            kb, vb, bias = k_ref[:, hs], v_ref[:, hs], bias_scr[h]
        vaug_scr[h, :, :dh] = vb
        s = lax.dot_general(q_ref[:, hs], kb, (((1,), (1,)), ((), ())), preferred_element_type=F32) + bias
        mx = jnp.max(s, axis=-1, keepdims=True)
        e = jnp.exp(s - mx).astype(BF16)
        o_den = jnp.dot(e, vaug_scr[h], preferred_element_type=F32)
        o_ref[:, hs] = o_den[:, :dh].astype(BF16)
        stats = jnp.where(lane == h, mx, jnp.where(lane == heads + h, o_den[:, dh:], stats))
    stats_ref[...] = stats


def _attn_init(tab_ref, q0, k0, v0, q1, k1, v1, q2, k2, v2, bk0, bk1, bk2,
               o0, l0, o1, l1, o2, l2, bias0, bias1, bias2, vaug0, vaug1, vaug2):
    blk = ATT_BLK

    @pl.when((pl.program_id(0) == 0) & (pl.program_id(1) == 0))
    def _():
        for vaug in (vaug0, vaug1, vaug2):
            vaug[...] = jnp.ones(vaug.shape, BF16)
        qi = lax.broadcasted_iota(jnp.int32, (blk, 2 * blk), 0)
        kj = lax.broadcasted_iota(jnp.int32, (blk, 2 * blk), 1)
        m = blk + qi - kj
        band = (m >= 0) & (m <= blk)
        masked = jnp.full((blk, blk), NEG, F32)
        for gi, (bk_ref, bias_scr) in enumerate(((bk0, bias0), (bk1, bias1), (bk2, bias2))):
            bucket = bk_ref[...]
            for h in range(ATT_HEADS_PER_GROUP):
                col = gi * ATT_HEADS_PER_GROUP + h
                acc = jnp.zeros((blk, 2 * blk), F32)
                for t in range(REL_BUCKETS):
                    acc = jnp.where(bucket == t, tab_ref[t, col], acc)
                tile = jnp.where(band, acc, NEG)
                if len(bias_scr.shape) == 4:
                    bias_scr[0, h] = tile
                    bias_scr[1, h] = jnp.concatenate([tile[:, blk:], masked], axis=1)
                else:
                    bias_scr[h] = tile[:, blk:]


def _attn_kernel(tab_ref, q0, k0, v0, q1, k1, v1, q2, k2, v2, bk0, bk1, bk2,
                 o0, l0, o1, l1, o2, l2, bias0, bias1, bias2, vaug0, vaug1, vaug2, *, nb1, per_step):
    i = pl.program_id(1)
    blk = ATT_BLK
    for u in range(per_step):
        rows = pl.ds(u * blk, blk)
        _attn_unit(i * per_step + u, q0.at[rows], k0, v0, bias0, vaug0.at[u], o0.at[rows], l0.at[rows])
        sub, n1 = divmod(u, nb1)
        rows1 = pl.ds(n1 * blk, blk)
        _attn_unit(n1, q1.at[sub, rows1], k1.at[sub], v1.at[sub], bias1, vaug1.at[u],
                   o1.at[sub, rows1], l1.at[sub, rows1])
        _attn_unit(0, q2.at[u], k2.at[u], v2.at[u], bias2, vaug2.at[u], o2.at[u], l2.at[u])


def _attention_parts(main3, att1, att2, rel_bias):
    B, S, _ = main3.shape
    blk, gw = ATT_BLK, ATT_GROUP_W
    steps = S // blk
    lens = [S // dil for _, dil in ATT_GROUPS]
    nbs = [L // blk for L in lens]
    assert all(win // dil == blk for win, dil in ATT_GROUPS)
    assert nbs[0] == steps and nbs[1] >= 2 and nbs[2] == 1
    c0 = COL_ATT // gw
    nb1 = nbs[1]
    G = ATT_BLOCKS_PER_STEP
    assert G % nb1 == 0 and steps % G == 0
    ns1 = G // nb1
    in_specs = [pl.BlockSpec(memory_space=pltpu.SMEM),
                pl.BlockSpec((None, G * blk, gw), lambda b, i: (b, i, c0)),
                pl.BlockSpec((None, S, gw), lambda b, i: (b, 0, c0 + 1)),
                pl.BlockSpec((None, S, gw), lambda b, i: (b, 0, c0 + 2)),
                pl.BlockSpec((None, ns1, lens[1], gw), lambda b, i: (b, i, 0, 0)),
                pl.BlockSpec((None, ns1, lens[1], gw), lambda b, i: (b, i, 0, 1)),
                pl.BlockSpec((None, ns1, lens[1], gw), lambda b, i: (b, i, 0, 2)),
                pl.BlockSpec((None, G, blk, gw), lambda b, i: (b, i, 0, 0)),
                pl.BlockSpec((None, G, blk, gw), lambda b, i: (b, i, 0, 1)),
                pl.BlockSpec((None, G, blk, gw), lambda b, i: (b, i, 0, 2))]
    args = [rel_bias] + [main3] * 3 + [att1] * 3 + [att2] * 3
    qi = np.arange(blk)[:, None]
    kj = np.arange(2 * blk)[None, :]
    m = blk + qi - kj
    for _, dil in ATT_GROUPS:
        in_specs.append(_const_spec((blk, 2 * blk)))
        args.append(jnp.asarray(_t5_bucket(np.clip(m, 0, blk) * dil)))
    out_specs, out_shapes = [], []
    for (_, dil), nb in zip(ATT_GROUPS, nbs):
        for w, dt in ((gw, BF16), (LSE_W, F32)):
            if nb == steps:
                out_specs.append(pl.BlockSpec((None, G * blk, w), lambda b, i: (b, i, 0)))
                out_shapes.append(jax.ShapeDtypeStruct((B, S, w), dt))
            elif nb == 1:
                out_specs.append(pl.BlockSpec((None, G, blk, w), lambda b, i: (b, i, 0, 0)))
                out_shapes.append(jax.ShapeDtypeStruct((B, dil, S // dil, w), dt))
            else:
                out_specs.append(pl.BlockSpec((None, ns1, lens[1], w), lambda b, i: (b, i, 0, 0)))
                out_shapes.append(jax.ShapeDtypeStruct((B, dil, S // dil, w), dt))
    heads = ATT_HEADS_PER_GROUP
    scratch = [pltpu.VMEM((2, heads, blk, 2 * blk), F32), pltpu.VMEM((2, heads, blk, 2 * blk), F32),
               pltpu.VMEM((heads, blk, blk), F32),
               pltpu.VMEM((G, heads, 2 * blk, 2 * ATT_HEAD_DIM), BF16),
               pltpu.VMEM((G, heads, 2 * blk, 2 * ATT_HEAD_DIM), BF16),
               pltpu.VMEM((G, heads, blk, 2 * ATT_HEAD_DIM), BF16)]
    return _CallParts(_attn_init, functools.partial(_attn_kernel, nb1=nb1, per_step=G), (B, steps // G),
                      in_specs, args, out_specs, out_shapes, scratch)


class _CallParts(NamedTuple):
    init: object
    body: object
    grid: tuple
    in_specs: list
    args: list
    out_specs: list
    out_shapes: list
    scratch: list


def _call(parts, name):
    def body(*refs):
        if parts.init is not None:
            parts.init(*refs)
        parts.body(*refs)

    return pl.pallas_call(
        body,
        out_shape=parts.out_shapes,
        grid=parts.grid,
        in_specs=parts.in_specs,
        out_specs=parts.out_specs,
        scratch_shapes=parts.scratch,
        compiler_params=_params(("arbitrary",) * len(parts.grid)),
        name=name,
    )(*parts.args)


def _mixers(main3, att1, att2, rel_bias, gn_g, gn_b):
    retg, = _call(_retention_parts(main3, gn_g, gn_b), "ret")
    att = _call(_attention_parts(main3, att1, att2, rel_bias), "attn")
    return retg, att[0::2], att[1::2]


def _merge_kernel(x_ref, g_ref, mod_ref, retg_ref, o0_ref, o1_ref, o2_ref, s0_ref, s1_ref, s2_ref,
                  wg_ref, wr_ref, wa_ref, wo_ref, out_ref, o1_scr, o2_scr, s1_scr, s2_scr):
    tm = x_ref.shape[0]
    for dil, o_ref, s_ref, o_scr, s_scr in ((ATT_GROUPS[1][1], o1_ref, s1_ref, o1_scr, s1_scr),
                                            (ATT_GROUPS[2][1], o2_ref, s2_ref, o2_scr, s2_scr)):
        n = tm // dil
        for r in range(dil):
            s_scr[pl.ds(r, n, stride=dil), :] = s_ref[0, r]
            for h in range(ATT_HEADS_PER_GROUP):
                hs = slice(h * ATT_HEAD_DIM, (h + 1) * ATT_HEAD_DIM)
                o_scr[h, pl.ds(r, n, stride=dil), :] = o_ref[0, r, :, hs].astype(F32)
    part = tm // MERGE_ROW_PARTS
    for p in range(MERGE_ROW_PARTS):
        rows = slice(p * part, (p + 1) * part)
        ret_out = jnp.dot(retg_ref[rows, :], wr_ref[...], preferred_element_type=F32)
        x = x_ref[rows, :]
        h = _modulated_norm(x, g_ref[...], mod_ref, 0).astype(BF16)
        gates = jax.nn.sigmoid(jnp.dot(h, wg_ref[...], preferred_element_type=F32))
        stats = (s0_ref[rows, :], s1_scr[rows, :], s2_scr[rows, :])
        top = jnp.maximum(jnp.maximum(stats[0], stats[1]), stats[2])
        a = [jnp.exp(st - top) for st in stats]
        lane = lax.broadcasted_iota(jnp.int32, top.shape, 1)
        heads = ATT_HEADS_PER_GROUP
        den = sum(ag * pltpu.roll(st, LSE_W - heads, axis=1) for ag, st in zip(a, stats))
        inv = 1.0 / jnp.where(lane < heads, den, 1.0)
        w0, w1, w2 = (ag * inv for ag in a)
        parts = []
        for hd in range(heads):
            hs = slice(hd * ATT_HEAD_DIM, (hd + 1) * ATT_HEAD_DIM)
            parts.append(w0[:, hd:hd + 1] * o0_ref[rows, hs].astype(F32)
                         + w1[:, hd:hd + 1] * o1_scr[hd, rows, :]
                         + w2[:, hd:hd + 1] * o2_scr[hd, rows, :])
        att = jnp.concatenate(parts, axis=-1).astype(BF16)
        att_out = jnp.dot(att, wa_ref[...], preferred_element_type=F32)
        merged = gates[:, :D_MODEL] * ret_out + gates[:, D_MODEL:] * att_out
        y = jnp.dot(merged.astype(BF16), wo_ref[...], preferred_element_type=F32)
        out_ref[rows, :] = x + mod_ref[0, :, 2 * D_MODEL:3 * D_MODEL] * y


def _merge(x2, norm_g, mod3, retg2, o, lse, w_gate, w_ret_out, w_att_out, w_o, batch, seq):
    tm = ROW_TM
    tps = seq // tm
    row = lambda w: pl.BlockSpec((tm, w), lambda i: (i, 0))
    sub = lambda d, w: pl.BlockSpec((1, d, tm // d, w), lambda i: (i // tps, 0, i % tps, 0))
    d1, d2 = ATT_GROUPS[1][1], ATT_GROUPS[2][1]
    return pl.pallas_call(
        _merge_kernel,
        out_shape=jax.ShapeDtypeStruct((batch * seq, D_MODEL), F32),
        grid=(batch * seq // tm,),
        in_specs=[row(D_MODEL),
                  _const_spec((1, D_MODEL)),
                  pl.BlockSpec((1, 1, 6 * D_MODEL), lambda i: (i // tps, 0, 0)),
                  row(RET_V_W),
                  row(ATT_GROUP_W), sub(d1, ATT_GROUP_W), sub(d2, ATT_GROUP_W),
                  row(LSE_W), sub(d1, LSE_W), sub(d2, LSE_W),
                  _const_spec((D_MODEL, 2 * D_MODEL)),
                  _const_spec((RET_V_W, D_MODEL)),
                  _const_spec((ATT_GROUP_W, D_MODEL)),
                  _const_spec((D_MODEL, D_MODEL))],
        out_specs=row(D_MODEL),
        scratch_shapes=[pltpu.VMEM((ATT_HEADS_PER_GROUP, tm, ATT_HEAD_DIM), F32)] * 2
                       + [pltpu.VMEM((tm, LSE_W), F32)] * 2,
        compiler_params=_params(("arbitrary",)),
        name="merge",
    )(x2, norm_g, mod3, retg2, *o, *lse, w_gate, w_ret_out, w_att_out, w_o)


def _mlp_kernel(x_ref, mod_ref, g2_ref, gf_ref, w1_ref, w2_ref, out_ref, *, final):
    part = x_ref.shape[0] // MLP_ROW_PARTS
    for p in range(MLP_ROW_PARTS):
        rows = slice(p * part, (p + 1) * part)
        x = x_ref[rows, :]
        h = _modulated_norm(x, g2_ref[...], mod_ref, 3).astype(BF16)
        u = jnp.maximum(jnp.dot(h, w1_ref[...], preferred_element_type=F32), 0.0)
        y = jnp.dot((u * u).astype(BF16), w2_ref[...], preferred_element_type=F32)
        x = x + mod_ref[0, :, 5 * D_MODEL:6 * D_MODEL] * y
        out_ref[rows, :] = _rms(x, gf_ref[...]) if final else x


def _mlp(x2, mod3, norm2_g, norm_f_g, w1, w2, seq, final):
    rows = x2.shape[0]
    tm = MLP_TM
    tps = seq // tm
    return pl.pallas_call(
        functools.partial(_mlp_kernel, final=final),
        out_shape=jax.ShapeDtypeStruct((rows, D_MODEL), F32),
        grid=(rows // tm,),
        in_specs=[pl.BlockSpec((tm, D_MODEL), lambda i: (i, 0)),
                  pl.BlockSpec((1, 1, 6 * D_MODEL), lambda i: (i // tps, 0, 0)),
                  _const_spec((1, D_MODEL)),
                  _const_spec((1, D_MODEL)),
                  _const_spec((D_MODEL, D_FF)),
                  _const_spec((D_FF, D_MODEL))],
        out_specs=pl.BlockSpec((tm, D_MODEL), lambda i: (i, 0)),
        compiler_params=_params(("arbitrary",)),
        name="mlp",
    )(x2, mod3, norm2_g, norm_f_g, w1, w2)


def kernel(x, c, w_ada, b_ada, norm1_g, w_in, rel_bias, ret_gn_g, ret_gn_b, w_ret_out, w_att_out,
           w_o, norm2_g, w_ff1, w_ff2, norm_f_g):
    B, S, D = x.shape
    depth = w_ada.shape[0]
    half = RET_DK // 2
    ang = np.arange(S, dtype=np.float64)[:, None] * ROPE_BASE ** (-np.arange(half, dtype=np.float64) / half)
    cos, sin = jnp.asarray(np.cos(ang), F32), jnp.asarray(np.sin(ang), F32)
    x2 = x.reshape(B * S, D)
    for l in range(depth):
        g1 = norm1_g[l].reshape(1, D)
        mod3 = _mod(c, w_ada[l], b_ada[l]).reshape(B, 1, 6 * D)
        main2, att1, att2, (w_gate_bf, w_ret_bf, w_o_bf, w_ff1_bf, w_ff2_bf) = _proj(
            x2, g1, mod3, cos, sin, w_in[l], [w_ret_out[l], w_o[l], w_ff1[l], w_ff2[l]], B, S)
        main3 = main2.reshape(B, S, main2.shape[1])
        retg, o, stats = _mixers(main3, att1, att2, rel_bias, ret_gn_g[l], ret_gn_b[l])
        o = [o[0].reshape(B * S, ATT_GROUP_W), o[1], o[2]]
        stats = [stats[0].reshape(B * S, LSE_W), stats[1], stats[2]]
        x2 = _merge(x2, g1, mod3, retg.reshape(B * S, RET_V_W), o, stats,
                    w_gate_bf, w_ret_bf, w_att_out[l].astype(BF16), w_o_bf, B, S)
        x2 = _mlp(x2, mod3, norm2_g[l].reshape(1, D), norm_f_g.reshape(1, D),
                  w_ff1_bf, w_ff2_bf, S, final=l == depth - 1)
    return x2.reshape(B, S, D)
```

```python
import functools
import math
from typing import NamedTuple

import jax
import jax.numpy as jnp
import numpy as np
from jax import lax
from jax.experimental import pallas as pl
from jax.experimental.pallas import tpu as pltpu

F32 = jnp.float32
BF16 = jnp.bfloat16

D_MODEL = 1024
RET_HEADS = 4
RET_DK = 256
RET_DV = 512
RET_C = 256
RET_LOG_GAMMA = tuple(math.log1p(-(2.0 ** (-5.0 - h))) for h in range(RET_HEADS))
RET_QK_W = RET_HEADS * RET_DK
RET_V_W = RET_HEADS * RET_DV
ATT_GROUPS = ((128, 1), (512, 4), (2048, 16))
ATT_HEADS_PER_GROUP = 4
ATT_HEAD_DIM = 128
ATT_GROUP_W = ATT_HEADS_PER_GROUP * ATT_HEAD_DIM
ATT_BLK = 128
ATT_BLOCKS_PER_STEP = 8
LANES = 128
BF16_SUBLANES = 16
LSE_W = LANES
REL_BUCKETS = 32
REL_MAX_DIST = 2048
D_FF = 4 * D_MODEL
RMS_EPS = 1e-6
GN_EPS = 1e-5
ROPE_BASE = 10000.0
NEG = -1e30
ATT_QSCALE = ATT_HEAD_DIM ** -0.5

COL_ATT = 2 * RET_QK_W + 2 * RET_V_W
COL_GATE = COL_ATT + 9 * ATT_GROUP_W

VMEM_LIMIT = 56 * 1024 * 1024

PROJ_TM = 256
PROJ_TN = 3 * ATT_GROUP_W
PROJ_MAIN_TILES = (COL_ATT + PROJ_TN) // PROJ_TN
PROJ_LAST_PIECES = 3
PROJ_W_STAGE_ROWS = 64
ROW_TM = 512
MERGE_ROW_PARTS = 2
MLP_TM = 1024
MLP_ROW_PARTS = 2


def _params(sem):
    return pltpu.CompilerParams(dimension_semantics=sem, vmem_limit_bytes=VMEM_LIMIT)


def _const_spec(shape):
    zeros = (0,) * len(shape)
    return pl.BlockSpec(shape, lambda *_: zeros, pipeline_mode=pl.Buffered(1))


def _silu(t):
    return t * jax.nn.sigmoid(t)


def _rms(x, g):
    return x * lax.rsqrt(jnp.mean(x * x, axis=-1, keepdims=True) + RMS_EPS) * g


def _modulated_norm(x, g, mod_ref, k):
    shift = mod_ref[0, :, k * D_MODEL:(k + 1) * D_MODEL]
    scale = mod_ref[0, :, (k + 1) * D_MODEL:(k + 2) * D_MODEL]
    return _rms(x, g) * (1.0 + scale) + shift


def _mod_kernel(c_ref, w_ref, b_ref, o_ref):
    o_ref[...] = jnp.dot(_silu(c_ref[...]), w_ref[...], preferred_element_type=F32) + b_ref[...]


def _mod(c, w_ada, b_ada):
    B = c.shape[0]
    n = w_ada.shape[1]
    tn = n // 2
    return pl.pallas_call(
        _mod_kernel,
        out_shape=jax.ShapeDtypeStruct((B, n), F32),
        grid=(n // tn,),
        in_specs=[pl.BlockSpec((B, D_MODEL), lambda j: (0, 0)),
                  pl.BlockSpec((D_MODEL, tn), lambda j: (0, j)),
                  pl.BlockSpec((1, tn), lambda j: (0, j))],
        out_specs=pl.BlockSpec((B, tn), lambda j: (0, j)),
        compiler_params=_params(("arbitrary",)),
        name="mod",
    )(c, w_ada, b_ada.reshape(1, n))


def _proj_main_layout():
    spans = [(h * RET_DK, (h + 1) * RET_DK, "rot", (h % RET_HEADS, h >= RET_HEADS))
             for h in range(2 * RET_HEADS)]
    spans += [(2 * RET_QK_W, 2 * RET_QK_W + RET_V_W, "copy", None),
              (2 * RET_QK_W + RET_V_W, COL_ATT, "silu", None),
              (COL_ATT, COL_ATT + ATT_GROUP_W, "scale", ATT_QSCALE),
              (COL_ATT + ATT_GROUP_W, PROJ_MAIN_TILES * PROJ_TN, "copy", None)]
    tiles = [[] for _ in range(PROJ_MAIN_TILES)]
    for c0, c1, kind, arg in spans:
        for t in range(c0 // PROJ_TN, (c1 - 1) // PROJ_TN + 1):
            lo, hi = max(c0, t * PROJ_TN), min(c1, (t + 1) * PROJ_TN)
            assert kind != "rot" or (lo, hi) == (c0, c1)
            tiles[t].append((lo - t * PROJ_TN, hi - t * PROJ_TN, kind, arg))
    return tiles


PROJ_MAIN_LAYOUT = _proj_main_layout()


def _proj_kernel(x_ref, mod_ref, g_ref, cos_ref, sin_ref, w_hbm, *refs, tps, n_cast):
    cast_in, (main_ref, a1_ref, a2_ref) = refs[:n_cast], refs[n_cast:n_cast + 3]
    cast_out = refs[n_cast + 3:2 * n_cast + 3]
    hf_scr, hd1_scr = refs[2 * n_cast + 3:2 * n_cast + 5]
    lhs = refs[2 * n_cast + 5:2 * n_cast + 8]
    w_ref, stage, sems = refs[2 * n_cast + 8:2 * n_cast + 11]
    tm = x_ref.shape[0]
    step = pl.program_id(0)

    @pl.when(step == 0)
    def _():
        rows = stage.shape[1]
        chunks = w_ref.shape[0] // rows

        def fetch(c):
            return pltpu.make_async_copy(w_hbm.at[pl.ds(c * rows, rows), pl.ds(0, w_ref.shape[1])],
                                         stage.at[c % 2], sems.at[c % 2])

        fetch(0).start()
        for c in range(chunks):
            if c + 1 < chunks:
                fetch(c + 1).start()
            fetch(c).wait()
            w_ref[c * rows:(c + 1) * rows, :] = stage[c % 2].astype(BF16)

    for src, dst in zip(cast_in, cast_out):
        dst[...] = src[...].astype(BF16)

    def prepare(x_ref, mod_ref, lhs):
        h0_scr, h1_scr, h2_scr = lhs
        hf = _modulated_norm(x_ref[...], g_ref[...], mod_ref, 0)
        h0_scr[...] = hf.astype(BF16)
        d1, d2 = ATT_GROUPS[1][1], ATT_GROUPS[2][1]
        n1, n2 = tm // d1, tm // d2
        for c in range(D_MODEL // LANES):
            cs = slice(c * LANES, (c + 1) * LANES)
            hf_scr[c] = hf[:, cs]
            for r in range(d1):
                part = hf_scr[c, pl.ds(r, n1, stride=d1), :]
                hd1_scr[c, r * n1:(r + 1) * n1, :] = part
                h1_scr[r * n1:(r + 1) * n1, cs] = part.astype(BF16)
            for r in range(d2):
                src = pl.ds((r % d1) * n1 + r // d1, n2, stride=d1)
                h2_scr[r * n2:(r + 1) * n2, cs] = hd1_scr[c, src, :].astype(BF16)

    def w_tile(t):
        return w_ref[:, t * PROJ_TN:(t + 1) * PROJ_TN]

    def main_tile(t, segments, h0_scr, pieces=1):
        if pieces > 1:
            width = PROJ_TN // pieces
            for p in range(pieces):
                p0, p1 = p * width, (p + 1) * width
                acc = jnp.dot(h0_scr[...], w_ref[:, t * PROJ_TN + p0:t * PROJ_TN + p1],
                              preferred_element_type=F32)
                for c0, c1, kind, arg in segments:
                    lo, hi = max(c0, p0), min(c1, p1)
                    if lo < hi:
                        assert kind in ("scale", "copy")
                        seg = acc[:, lo - p0:hi - p0]
                        seg = seg * arg if kind == "scale" else seg
                        main_ref[:, t * PROJ_TN + lo:t * PROJ_TN + hi] = seg.astype(BF16)
            return
        acc = jnp.dot(h0_scr[...], w_tile(t), preferred_element_type=F32)
        if any(kind == "rot" for _, _, kind, _ in segments):
            cos, sin = cos_ref[...], sin_ref[...]
            row = lax.broadcasted_iota(jnp.int32, cos.shape, 0)
            pos = (((step % tps) * tm + row) % RET_C + 1).astype(F32)
        base = t * PROJ_TN
        for c0, c1, kind, arg in segments:
            seg = acc[:, c0:c1]
            if kind == "rot":
                head, is_key = arg
                rate = -RET_LOG_GAMMA[head] if is_key else RET_LOG_GAMMA[head]
                dec = jnp.exp(rate * pos) * ((RET_DK ** -0.5) if is_key else 1.0)
                cd, sd = cos * dec, sin * dec
                cm = (c0 + c1) // 2
                t1, t2 = acc[:, c0:cm], acc[:, cm:c1]
                main_ref[:, base + c0:base + cm] = (t1 * cd - t2 * sd).astype(BF16)
                main_ref[:, base + cm:base + c1] = (t1 * sd + t2 * cd).astype(BF16)
            elif kind == "silu":
                main_ref[:, base + c0:base + c1] = _silu(seg).astype(BF16)
            elif kind == "scale":
                main_ref[:, base + c0:base + c1] = (seg * arg).astype(BF16)
            else:
                main_ref[:, base + c0:base + c1] = seg.astype(BF16)

    def multiply(lhs):
        h0_scr, h1_scr, h2_scr = lhs
        for t, (dil, h_scr, a_ref) in enumerate(((ATT_GROUPS[1][1], h1_scr, a1_ref),
                                                 (ATT_GROUPS[2][1], h2_scr, a2_ref))):
            n = tm // dil
            acc = jnp.dot(h_scr[...], w_tile(PROJ_MAIN_TILES + t), preferred_element_type=F32)
            gw = ATT_GROUP_W
            for r in range(dil):
                rows = slice(r * n, (r + 1) * n)
                a_ref[0, r, :, :gw] = (acc[rows, :gw] * ATT_QSCALE).astype(BF16)
                a_ref[0, r, :, gw:] = acc[rows, gw:].astype(BF16)
        last = len(PROJ_MAIN_LAYOUT) - 1
        for t, segments in enumerate(PROJ_MAIN_LAYOUT):
            main_tile(t, segments, h0_scr, pieces=PROJ_LAST_PIECES if t == last else 1)

    prepare(x_ref, mod_ref, lhs)
    multiply(lhs)


def _proj(x2, norm_g, mod3, cos, sin, w_in, cast_weights, batch, seq):
    tm, tn = PROJ_TM, PROJ_TN
    tps = seq // tm
    steps = batch * seq // tm
    assert seq % tm == 0 and tm % RET_C == 0
    d1, d2 = ATT_GROUPS[1][1], ATT_GROUPS[2][1]
    assert COL_GATE == (PROJ_MAIN_TILES + 2) * tn
    sub = lambda d: pl.BlockSpec((1, d, tm // d, tn), lambda i: (i // tps, 0, i % tps, 0))
    gate_w = w_in.shape[1] - COL_GATE
    gate_rows = D_MODEL // steps
    assert D_MODEL % (steps * BF16_SUBLANES) == 0
    slab_in = [pl.BlockSpec((pl.Element(gate_rows), pl.Element(gate_w)), lambda i: (i * gate_rows, COL_GATE))]
    slab_out = [pl.BlockSpec((gate_rows, gate_w), lambda i: (i, 0))]
    slab_shapes = [jax.ShapeDtypeStruct((D_MODEL, gate_w), BF16)]
    for w in cast_weights:
        assert w.shape[0] % (steps * BF16_SUBLANES) == 0
        slab_in.append(pl.BlockSpec((w.shape[0] // steps, w.shape[1]), lambda i: (i, 0)))
        slab_out.append(slab_in[-1])
        slab_shapes.append(jax.ShapeDtypeStruct(w.shape, BF16))
    outs = pl.pallas_call(
        functools.partial(_proj_kernel, tps=tps, n_cast=len(slab_in)),
        out_shape=[jax.ShapeDtypeStruct((batch * seq, PROJ_MAIN_TILES * tn), BF16),
                   jax.ShapeDtypeStruct((batch, d1, seq // d1, tn), BF16),
                   jax.ShapeDtypeStruct((batch, d2, seq // d2, tn), BF16)] + slab_shapes,
        grid=(steps,),
        in_specs=[pl.BlockSpec((tm, D_MODEL), lambda i: (i, 0)),
                  pl.BlockSpec((1, 1, 6 * D_MODEL), lambda i: (i // tps, 0, 0)),
                  _const_spec((1, D_MODEL)),
                  pl.BlockSpec((tm, RET_DK // 2), lambda i: (i % tps, 0)),
                  pl.BlockSpec((tm, RET_DK // 2), lambda i: (i % tps, 0)),
                  pl.BlockSpec(memory_space=pl.ANY)] + slab_in,
        out_specs=[pl.BlockSpec((tm, PROJ_MAIN_TILES * tn), lambda i: (i, 0)), sub(d1), sub(d2)] + slab_out,
        scratch_shapes=[pltpu.VMEM((D_MODEL // LANES, tm, LANES), F32)] * 2
                       + [pltpu.VMEM((tm, D_MODEL), BF16)] * 3
                       + [pltpu.VMEM((D_MODEL, COL_GATE), BF16),
                          pltpu.VMEM((2, PROJ_W_STAGE_ROWS, COL_GATE), F32),
                          pltpu.SemaphoreType.DMA((2,))],
        compiler_params=_params(("arbitrary",)),
        name="proj",
    )(x2, mod3, norm_g, cos, sin, w_in, w_in, *cast_weights)
    return outs[0], outs[1], outs[2], outs[3:]


RET_RING = 3


def _ret_kernel(main_hbm, cdec_ref, gng_ref, gnb_ref, o_ref, state, st_in, q_buf, k_buf, v_buf, g_buf, sems):
    n_heads = pl.num_programs(1)
    step = pl.program_id(0) * n_heads + pl.program_id(1)
    last = pl.num_programs(0) * n_heads - 1
    bufs = (q_buf, k_buf, v_buf, g_buf)
    col0 = (0, RET_QK_W, 2 * RET_QK_W, 2 * RET_QK_W + RET_V_W)

    def copies(s, slot):
        b, h = s // n_heads, s % n_heads
        out = []
        for j, buf in enumerate(bufs):
            width = buf.shape[2]
            cols = pl.ds(pl.multiple_of(col0[j] + h * width, LANES), width)
            out.append(pltpu.make_async_copy(main_hbm.at[b, :, cols], buf.at[slot], sems.at[slot, j]))
        return out

    @pl.when(step == 0)
    def _():
        for s in range(RET_RING - 1):
            for cp in copies(s, s):
                cp.start()

    for slot in range(RET_RING):
        @pl.when(step % RET_RING == slot)
        def _(slot=slot):
            ahead = step + (RET_RING - 1)

            @pl.when(ahead <= last)
            def _():
                for cp in copies(ahead, (slot + RET_RING - 1) % RET_RING):
                    cp.start()

            for cp in copies(step, slot):
                cp.wait()
            _ret_body(q_buf.at[slot], k_buf.at[slot], v_buf.at[slot], g_buf.at[slot],
                      cdec_ref, gng_ref, gnb_ref, o_ref, state, st_in)


def _ret_body(q_ref, k_ref, v_ref, gate_ref, cdec_ref, gng_ref, gnb_ref, o_ref, state, st_in):
    C = RET_C
    nc = q_ref.shape[0] // C
    cdec = cdec_ref[0]
    gng = gng_ref[0]
    gnb = gnb_ref[0]
    chunk = lambda ci: pl.ds(pl.multiple_of(ci * C, C), C)

    state[...] = jnp.zeros_like(state)
    st_in[0] = jnp.zeros(st_in.shape[1:], BF16)

    def advance(ci, carry):
        rows = chunk(ci)
        kv = lax.dot_general(k_ref[rows, :], v_ref[rows, :], (((0,), (0,)), ((), ())),
                             preferred_element_type=F32)
        new = (state[...] + kv) * cdec
        state[...] = new
        st_in[ci + 1] = new.astype(BF16)
        return carry

    lax.fori_loop(0, nc - 1, advance, 0, unroll=True)

    qi = lax.broadcasted_iota(jnp.int32, (C, C), 0)
    kj = lax.broadcasted_iota(jnp.int32, (C, C), 1)
    causal = qi >= kj

    def emit(ci, carry):
        rows = chunk(ci)
        q = q_ref[rows, :]
        s = lax.dot_general(q, k_ref[rows, :], (((1,), (1,)), ((), ())), preferred_element_type=F32)
        s = jnp.where(causal, s, 0.0).astype(BF16)
        o = (jnp.dot(s, v_ref[rows, :], preferred_element_type=F32)
             + jnp.dot(q, st_in[ci], preferred_element_type=F32))
        mu = jnp.mean(o, axis=-1, keepdims=True)
        oc = o - mu
        var = jnp.mean(oc * oc, axis=-1, keepdims=True)
        on = oc * lax.rsqrt(var + GN_EPS) * gng + gnb
        o_ref[0, rows, :] = (gate_ref[rows, :].astype(F32) * on).astype(BF16)
        return carry

    lax.fori_loop(0, nc, emit, 0, unroll=True)


def _retention_parts(main3, gn_g, gn_b):
    B, S, _ = main3.shape
    H, C = RET_HEADS, RET_C
    assert S % C == 0
    assert B * H >= RET_RING
    cdec = jnp.exp(jnp.asarray(RET_LOG_GAMMA, F32) * C)[:, None, None]
    in_specs = [pl.BlockSpec(memory_space=pl.ANY),
                pl.BlockSpec((1, 1, 1), lambda b, h: (h, 0, 0)),
                pl.BlockSpec((1, 1, RET_DV), lambda b, h: (h, 0, 0)),
                pl.BlockSpec((1, 1, RET_DV), lambda b, h: (h, 0, 0))]
    args = [main3, cdec, gn_g.reshape(H, 1, RET_DV), gn_b.reshape(H, 1, RET_DV)]
    ring = lambda w: pltpu.VMEM((RET_RING, S, w), BF16)
    return _CallParts(None, _ret_kernel, (B, H), in_specs, args,
                      [pl.BlockSpec((1, S, RET_DV), lambda b, h: (b, 0, h))],
                      [jax.ShapeDtypeStruct((B, S, RET_V_W), BF16)],
                      [pltpu.VMEM((RET_DK, RET_DV), F32), pltpu.VMEM((S // C, RET_DK, RET_DV), BF16),
                       ring(RET_DK), ring(RET_DK), ring(RET_DV), ring(RET_DV),
                       pltpu.SemaphoreType.DMA((RET_RING, 4))])


def _t5_bucket(dist):
    dist = np.asarray(dist)
    max_exact = REL_BUCKETS // 2
    d_f = np.maximum(dist, 1).astype(np.float32)
    large = max_exact + (np.log(d_f / np.float32(max_exact)) / np.float32(math.log(REL_MAX_DIST / max_exact))
                         * np.float32(REL_BUCKETS - max_exact)).astype(np.int32)
    large = np.minimum(large, REL_BUCKETS - 1)
    return np.where(dist < max_exact, dist, large).astype(np.int32)


def _attn_unit(n, q_ref, k_ref, v_ref, bias_scr, vaug_scr, o_ref, stats_ref):
    blk, dh = ATT_BLK, ATT_HEAD_DIM
    windowed = k_ref.shape[0] > blk
    if windowed and isinstance(n, int):
        win, first = pl.ds(max(n - 1, 0) * blk, 2 * blk), int(n == 0)
    elif windowed:
        win = pl.ds(pl.multiple_of(jnp.maximum(n - 1, 0) * blk, blk), 2 * blk)
        first = (n == 0).astype(jnp.int32)
    heads = ATT_HEADS_PER_GROUP
    lane = lax.broadcasted_iota(jnp.int32, (blk, LSE_W), 1)
    stats = jnp.ones((blk, LSE_W), F32)
    for h in range(heads):
        hs = slice(h * dh, (h + 1) * dh)
        if windowed:
            kb, vb, bias = k_ref[win, hs], v_ref[win, hs], bias_scr[first, h]
        else:
            kb, vb, bias = k_ref[:, hs], v_ref[:, hs], bias_scr[h]
        vaug_scr[h, :, :dh] = vb
        s = lax.dot_general(q_ref[:, hs], kb, (((1,), (1,)), ((), ())), preferred_element_type=F32) + bias
        mx = jnp.max(s, axis=-1, keepdims=True)
        e = jnp.exp(s - mx).astype(BF16)
        o_den = jnp.dot(e, vaug_scr[h], preferred_element_type=F32)
        o_ref[:, hs] = o_den[:, :dh].astype(BF16)
        stats = jnp.where(lane == h, mx, jnp.where(lane == heads + h, o_den[:, dh:], stats))
    stats_ref[...] = stats


def _attn_init(tab_ref, q0, k0, v0, q1, k1, v1, q2, k2, v2, bk0, bk1, bk2,
               o0, l0, o1, l1, o2, l2, bias0, bias1, bias2, vaug0, vaug1, vaug2):
    blk = ATT_BLK

    @pl.when((pl.program_id(0) == 0) & (pl.program_id(1) == 0))
    def _():
        for vaug in (vaug0, vaug1, vaug2):
            vaug[...] = jnp.ones(vaug.shape, BF16)
        qi = lax.broadcasted_iota(jnp.int32, (blk, 2 * blk), 0)
        kj = lax.broadcasted_iota(jnp.int32, (blk, 2 * blk), 1)
        m = blk + qi - kj
        band = (m >= 0) & (m <= blk)
        masked = jnp.full((blk, blk), NEG, F32)
        for gi, (bk_ref, bias_scr) in enumerate(((bk0, bias0), (bk1, bias1), (bk2, bias2))):
            bucket = bk_ref[...]
            for h in range(ATT_HEADS_PER_GROUP):
                col = gi * ATT_HEADS_PER_GROUP + h
                acc = jnp.zeros((blk, 2 * blk), F32)
                for t in range(REL_BUCKETS):
                    acc = jnp.where(bucket == t, tab_ref[t, col], acc)
                tile = jnp.where(band, acc, NEG)
                if len(bias_scr.shape) == 4:
                    bias_scr[0, h] = tile
                    bias_scr[1, h] = jnp.concatenate([tile[:, blk:], masked], axis=1)
                else:
                    bias_scr[h] = tile[:, blk:]


def _attn_kernel(tab_ref, q0, k0, v0, q1, k1, v1, q2, k2, v2, bk0, bk1, bk2,
                 o0, l0, o1, l1, o2, l2, bias0, bias1, bias2, vaug0, vaug1, vaug2, *, nb1, per_step):
    i = pl.program_id(1)
    blk = ATT_BLK
    for u in range(per_step):
        rows = pl.ds(u * blk, blk)
        _attn_unit(i * per_step + u, q0.at[rows], k0, v0, bias0, vaug0.at[u], o0.at[rows], l0.at[rows])
        sub, n1 = divmod(u, nb1)
        rows1 = pl.ds(n1 * blk, blk)
        _attn_unit(n1, q1.at[sub, rows1], k1.at[sub], v1.at[sub], bias1, vaug1.at[u],
                   o1.at[sub, rows1], l1.at[sub, rows1])
        _attn_unit(0, q2.at[u], k2.at[u], v2.at[u], bias2, vaug2.at[u], o2.at[u], l2.at[u])


def _attention_parts(main3, att1, att2, rel_bias):
    B, S, _ = main3.shape
    blk, gw = ATT_BLK, ATT_GROUP_W
    steps = S // blk
    lens = [S // dil for _, dil in ATT_GROUPS]
    nbs = [L // blk for L in lens]
    assert all(win // dil == blk for win, dil in ATT_GROUPS)
    assert nbs[0] == steps and nbs[1] >= 2 and nbs[2] == 1
    c0 = COL_ATT // gw
    nb1 = nbs[1]
    G = ATT_BLOCKS_PER_STEP
    assert G % nb1 == 0 and steps % G == 0
    ns1 = G // nb1
    in_specs = [pl.BlockSpec(memory_space=pltpu.SMEM),
                pl.BlockSpec((None, G * blk, gw), lambda b, i: (b, i, c0)),
                pl.BlockSpec((None, S, gw), lambda b, i: (b, 0, c0 + 1)),
                pl.BlockSpec((None, S, gw), lambda b, i: (b, 0, c0 + 2)),
                pl.BlockSpec((None, ns1, lens[1], gw), lambda b, i: (b, i, 0, 0)),
                pl.BlockSpec((None, ns1, lens[1], gw), lambda b, i: (b, i, 0, 1)),
                pl.BlockSpec((None, ns1, lens[1], gw), lambda b, i: (b, i, 0, 2)),
                pl.BlockSpec((None, G, blk, gw), lambda b, i: (b, i, 0, 0)),
                pl.BlockSpec((None, G, blk, gw), lambda b, i: (b, i, 0, 1)),
                pl.BlockSpec((None, G, blk, gw), lambda b, i: (b, i, 0, 2))]
    args = [rel_bias] + [main3] * 3 + [att1] * 3 + [att2] * 3
    qi = np.arange(blk)[:, None]
    kj = np.arange(2 * blk)[None, :]
    m = blk + qi - kj
    for _, dil in ATT_GROUPS:
        in_specs.append(_const_spec((blk, 2 * blk)))
        args.append(jnp.asarray(_t5_bucket(np.clip(m, 0, blk) * dil)))
    out_specs, out_shapes = [], []
    for (_, dil), nb in zip(ATT_GROUPS, nbs):
        for w, dt in ((gw, BF16), (LSE_W, F32)):
            if nb == steps:
                out_specs.append(pl.BlockSpec((None, G * blk, w), lambda b, i: (b, i, 0)))
                out_shapes.append(jax.ShapeDtypeStruct((B, S, w), dt))
            elif nb == 1:
                out_specs.append(pl.BlockSpec((None, G, blk, w), lambda b, i: (b, i, 0, 0)))
                out_shapes.append(jax.ShapeDtypeStruct((B, dil, S // dil, w), dt))
            else:
                out_specs.append(pl.BlockSpec((None, ns1, lens[1], w), lambda b, i: (b, i, 0, 0)))
                out_shapes.append(jax.ShapeDtypeStruct((B, dil, S // dil, w), dt))
    heads = ATT_HEADS_PER_GROUP
    scratch = [pltpu.VMEM((2, heads, blk, 2 * blk), F32), pltpu.VMEM((2, heads, blk, 2 * blk), F32),
               pltpu.VMEM((heads, blk, blk), F32),
               pltpu.VMEM((G, heads, 2 * blk, 2 * ATT_HEAD_DIM), BF16),
               pltpu.VMEM((G, heads, 2 * blk, 2 * ATT_HEAD_DIM), BF16),
               pltpu.VMEM((G, heads, blk, 2 * ATT_HEAD_DIM), BF16)]
    return _CallParts(_attn_init, functools.partial(_attn_kernel, nb1=nb1, per_step=G), (B, steps // G),
                      in_specs, args, out_specs, out_shapes, scratch)


class _CallParts(NamedTuple):
    init: object
    body: object
    grid: tuple
    in_specs: list
    args: list
    out_specs: list
    out_shapes: list
    scratch: list


def _call(parts, name):
    def body(*refs):
        if parts.init is not None:
            parts.init(*refs)
        parts.body(*refs)

    return pl.pallas_call(
        body,
        out_shape=parts.out_shapes,
        grid=parts.grid,
        in_specs=parts.in_specs,
        out_specs=parts.out_specs,
        scratch_shapes=parts.scratch,
        compiler_params=_params(("arbitrary",) * len(parts.grid)),
        name=name,
    )(*parts.args)


def _mixers(main3, att1, att2, rel_bias, gn_g, gn_b):
    retg, = _call(_retention_parts(main3, gn_g, gn_b), "ret")
    att = _call(_attention_parts(main3, att1, att2, rel_bias), "attn")
    return retg, att[0::2], att[1::2]


def _merge_kernel(x_ref, g_ref, mod_ref, retg_ref, o0_ref, o1_ref, o2_ref, s0_ref, s1_ref, s2_ref,
                  wg_ref, wr_ref, wa_ref, wo_ref, out_ref, o1_scr, o2_scr, s1_scr, s2_scr):
    tm = x_ref.shape[0]
    for dil, o_ref, s_ref, o_scr, s_scr in ((ATT_GROUPS[1][1], o1_ref, s1_ref, o1_scr, s1_scr),
                                            (ATT_GROUPS[2][1], o2_ref, s2_ref, o2_scr, s2_scr)):
        n = tm // dil
        for r in range(dil):
            s_scr[pl.ds(r, n, stride=dil), :] = s_ref[0, r]
            for h in range(ATT_HEADS_PER_GROUP):
                hs = slice(h * ATT_HEAD_DIM, (h + 1) * ATT_HEAD_DIM)
                o_scr[h, pl.ds(r, n, stride=dil), :] = o_ref[0, r, :, hs].astype(F32)
    part = tm // MERGE_ROW_PARTS
    for p in range(MERGE_ROW_PARTS):
        rows = slice(p * part, (p + 1) * part)
        ret_out = jnp.dot(retg_ref[rows, :], wr_ref[...], preferred_element_type=F32)
        x = x_ref[rows, :]
        h = _modulated_norm(x, g_ref[...], mod_ref, 0).astype(BF16)
        gates = jax.nn.sigmoid(jnp.dot(h, wg_ref[...], preferred_element_type=F32))
        stats = (s0_ref[rows, :], s1_scr[rows, :], s2_scr[rows, :])
        top = jnp.maximum(jnp.maximum(stats[0], stats[1]), stats[2])
        a = [jnp.exp(st - top) for st in stats]
        lane = lax.broadcasted_iota(jnp.int32, top.shape, 1)
        heads = ATT_HEADS_PER_GROUP
        den = sum(ag * pltpu.roll(st, LSE_W - heads, axis=1) for ag, st in zip(a, stats))
        inv = 1.0 / jnp.where(lane < heads, den, 1.0)
        w0, w1, w2 = (ag * inv for ag in a)
        parts = []
        for hd in range(heads):
            hs = slice(hd * ATT_HEAD_DIM, (hd + 1) * ATT_HEAD_DIM)
            parts.append(w0[:, hd:hd + 1] * o0_ref[rows, hs].astype(F32)
                         + w1[:, hd:hd + 1] * o1_scr[hd, rows, :]
                         + w2[:, hd:hd + 1] * o2_scr[hd, rows, :])
        att = jnp.concatenate(parts, axis=-1).astype(BF16)
        att_out = jnp.dot(att, wa_ref[...], preferred_element_type=F32)
        merged = gates[:, :D_MODEL] * ret_out + gates[:, D_MODEL:] * att_out
        y = jnp.dot(merged.astype(BF16), wo_ref[...], preferred_element_type=F32)
        out_ref[rows, :] = x + mod_ref[0, :, 2 * D_MODEL:3 * D_MODEL] * y


def _merge(x2, norm_g, mod3, retg2, o, lse, w_gate, w_ret_out, w_att_out, w_o, batch, seq):
    tm = ROW_TM
    tps = seq // tm
    row = lambda w: pl.BlockSpec((tm, w), lambda i: (i, 0))
    sub = lambda d, w: pl.BlockSpec((1, d, tm // d, w), lambda i: (i // tps, 0, i % tps, 0))
    d1, d2 = ATT_GROUPS[1][1], ATT_GROUPS[2][1]
    return pl.pallas_call(
        _merge_kernel,
        out_shape=jax.ShapeDtypeStruct((batch * seq, D_MODEL), F32),
        grid=(batch * seq // tm,),
        in_specs=[row(D_MODEL),
                  _const_spec((1, D_MODEL)),
                  pl.BlockSpec((1, 1, 6 * D_MODEL), lambda i: (i // tps, 0, 0)),
                  row(RET_V_W),
                  row(ATT_GROUP_W), sub(d1, ATT_GROUP_W), sub(d2, ATT_GROUP_W),
                  row(LSE_W), sub(d1, LSE_W), sub(d2, LSE_W),
                  _const_spec((D_MODEL, 2 * D_MODEL)),
                  _const_spec((RET_V_W, D_MODEL)),
                  _const_spec((ATT_GROUP_W, D_MODEL)),
                  _const_spec((D_MODEL, D_MODEL))],
        out_specs=row(D_MODEL),
        scratch_shapes=[pltpu.VMEM((ATT_HEADS_PER_GROUP, tm, ATT_HEAD_DIM), F32)] * 2
                       + [pltpu.VMEM((tm, LSE_W), F32)] * 2,
        compiler_params=_params(("arbitrary",)),
        name="merge",
    )(x2, norm_g, mod3, retg2, *o, *lse, w_gate, w_ret_out, w_att_out, w_o)


def _mlp_kernel(x_ref, mod_ref, g2_ref, gf_ref, w1_ref, w2_ref, out_ref, *, final):
    part = x_ref.shape[0] // MLP_ROW_PARTS
    for p in range(MLP_ROW_PARTS):
        rows = slice(p * part, (p + 1) * part)
        x = x_ref[rows, :]
        h = _modulated_norm(x, g2_ref[...], mod_ref, 3).astype(BF16)
        u = jnp.maximum(jnp.dot(h, w1_ref[...], preferred_element_type=F32), 0.0)
        y = jnp.dot((u * u).astype(BF16), w2_ref[...], preferred_element_type=F32)
        x = x + mod_ref[0, :, 5 * D_MODEL:6 * D_MODEL] * y
        out_ref[rows, :] = _rms(x, gf_ref[...]) if final else x


def _mlp(x2, mod3, norm2_g, norm_f_g, w1, w2, seq, final):
    rows = x2.shape[0]
    tm = MLP_TM
    tps = seq // tm
    return pl.pallas_call(
        functools.partial(_mlp_kernel, final=final),
        out_shape=jax.ShapeDtypeStruct((rows, D_MODEL), F32),
        grid=(rows // tm,),
        in_specs=[pl.BlockSpec((tm, D_MODEL), lambda i: (i, 0)),
                  pl.BlockSpec((1, 1, 6 * D_MODEL), lambda i: (i // tps, 0, 0)),
                  _const_spec((1, D_MODEL)),
                  _const_spec((1, D_MODEL)),
                  _const_spec((D_MODEL, D_FF)),
                  _const_spec((D_FF, D_MODEL))],
        out_specs=pl.BlockSpec((tm, D_MODEL), lambda i: (i, 0)),
        compiler_params=_params(("arbitrary",)),
        name="mlp",
    )(x2, mod3, norm2_g, norm_f_g, w1, w2)


def kernel(x, c, w_ada, b_ada, norm1_g, w_in, rel_bias, ret_gn_g, ret_gn_b, w_ret_out, w_att_out,
           w_o, norm2_g, w_ff1, w_ff2, norm_f_g):
    B, S, D = x.shape
    depth = w_ada.shape[0]
    half = RET_DK // 2
    ang = np.arange(S, dtype=np.float64)[:, None] * ROPE_BASE ** (-np.arange(half, dtype=np.float64) / half)
    cos, sin = jnp.asarray(np.cos(ang), F32), jnp.asarray(np.sin(ang), F32)
    x2 = x.reshape(B * S, D)
    for l in range(depth):
        g1 = norm1_g[l].reshape(1, D)
        mod3 = _mod(c, w_ada[l], b_ada[l]).reshape(B, 1, 6 * D)
        main2, att1, att2, (w_gate_bf, w_ret_bf, w_o_bf, w_ff1_bf, w_ff2_bf) = _proj(
            x2, g1, mod3, cos, sin, w_in[l], [w_ret_out[l], w_o[l], w_ff1[l], w_ff2[l]], B, S)
        main3 = main2.reshape(B, S, main2.shape[1])
        retg, o, stats = _mixers(main3, att1, att2, rel_bias, ret_gn_g[l], ret_gn_b[l])
        o = [o[0].reshape(B * S, ATT_GROUP_W), o[1], o[2]]
        stats = [stats[0].reshape(B * S, LSE_W), stats[1], stats[2]]
        x2 = _merge(x2, g1, mod3, retg.reshape(B * S, RET_V_W), o, stats,
                    w_gate_bf, w_ret_bf, w_att_out[l].astype(BF16), w_o_bf, B, S)
        x2 = _mlp(x2, mod3, norm2_g[l].reshape(1, D), norm_f_g.reshape(1, D),
                  w_ff1_bf, w_ff2_bf, S, final=l == depth - 1)
    return x2.reshape(B, S, D)
```

```python
import functools
import math
from typing import NamedTuple

import jax
import jax.numpy as jnp
import numpy as np
from jax import lax
from jax.experimental import pallas as pl
from jax.experimental.pallas import tpu as pltpu

F32 = jnp.float32
BF16 = jnp.bfloat16

D_MODEL = 1024
RET_HEADS = 4
RET_DK = 256
RET_DV = 512
RET_C = 256
RET_LOG_GAMMA = tuple(math.log1p(-(2.0 ** (-5.0 - h))) for h in range(RET_HEADS))
RET_QK_W = RET_HEADS * RET_DK
RET_V_W = RET_HEADS * RET_DV
ATT_GROUPS = ((128, 1), (512, 4), (2048, 16))
ATT_HEADS_PER_GROUP = 4
ATT_HEAD_DIM = 128
ATT_GROUP_W = ATT_HEADS_PER_GROUP * ATT_HEAD_DIM
ATT_BLK = 128
ATT_BLOCKS_PER_STEP = 8
LANES = 128
BF16_SUBLANES = 16
LSE_W = LANES
REL_BUCKETS = 32
REL_MAX_DIST = 2048
D_FF = 4 * D_MODEL
RMS_EPS = 1e-6
GN_EPS = 1e-5
ROPE_BASE = 10000.0
NEG = -1e30
ATT_QSCALE = ATT_HEAD_DIM ** -0.5

COL_ATT = 2 * RET_QK_W + 2 * RET_V_W
COL_GATE = COL_ATT + 9 * ATT_GROUP_W

VMEM_LIMIT = 56 * 1024 * 1024

PROJ_TM = 256
PROJ_TN = 3 * ATT_GROUP_W
PROJ_MAIN_TILES = (COL_ATT + PROJ_TN) // PROJ_TN
PROJ_LAST_PIECES = 3
PROJ_W_STAGE_ROWS = 64
ROW_TM = 512
MERGE_ROW_PARTS = 2
MLP_TM = 1024
MLP_ROW_PARTS = 2


def _params(sem):
    return pltpu.CompilerParams(dimension_semantics=sem, vmem_limit_bytes=VMEM_LIMIT)


def _const_spec(shape):
    zeros = (0,) * len(shape)
    return pl.BlockSpec(shape, lambda *_: zeros, pipeline_mode=pl.Buffered(1))


def _silu(t):
    return t * jax.nn.sigmoid(t)


def _rms(x, g):
    return x * lax.rsqrt(jnp.mean(x * x, axis=-1, keepdims=True) + RMS_EPS) * g


def _modulated_norm(x, g, mod_ref, k):
    shift = mod_ref[0, :, k * D_MODEL:(k + 1) * D_MODEL]
    scale = mod_ref[0, :, (k + 1) * D_MODEL:(k + 2) * D_MODEL]
    return _rms(x, g) * (1.0 + scale) + shift


def _mod_kernel(c_ref, w_ref, b_ref, o_ref):
    o_ref[...] = jnp.dot(_silu(c_ref[...]), w_ref[...], preferred_element_type=F32) + b_ref[...]


def _mod(c, w_ada, b_ada):
    B = c.shape[0]
    n = w_ada.shape[1]
    tn = n // 2
    return pl.pallas_call(
        _mod_kernel,
        out_shape=jax.ShapeDtypeStruct((B, n), F32),
        grid=(n // tn,),
        in_specs=[pl.BlockSpec((B, D_MODEL), lambda j: (0, 0)),
                  pl.BlockSpec((D_MODEL, tn), lambda j: (0, j)),
                  pl.BlockSpec((1, tn), lambda j: (0, j))],
        out_specs=pl.BlockSpec((B, tn), lambda j: (0, j)),
        compiler_params=_params(("arbitrary",)),
        name="mod",
    )(c, w_ada, b_ada.reshape(1, n))


def _proj_main_layout():
    spans = [(h * RET_DK, (h + 1) * RET_DK, "rot", (h % RET_HEADS, h >= RET_HEADS))
             for h in range(2 * RET_HEADS)]
    spans += [(2 * RET_QK_W, 2 * RET_QK_W + RET_V_W, "copy", None),
              (2 * RET_QK_W + RET_V_W, COL_ATT, "silu", None),
              (COL_ATT, COL_ATT + ATT_GROUP_W, "scale", ATT_QSCALE),
              (COL_ATT + ATT_GROUP_W, PROJ_MAIN_TILES * PROJ_TN, "copy", None)]
    tiles = [[] for _ in range(PROJ_MAIN_TILES)]
    for c0, c1, kind, arg in spans:
        for t in range(c0 // PROJ_TN, (c1 - 1) // PROJ_TN + 1):
            lo, hi = max(c0, t * PROJ_TN), min(c1, (t + 1) * PROJ_TN)
            assert kind != "rot" or (lo, hi) == (c0, c1)
            tiles[t].append((lo - t * PROJ_TN, hi - t * PROJ_TN, kind, arg))
    return tiles


PROJ_MAIN_LAYOUT = _proj_main_layout()


def _proj_kernel(x_ref, mod_ref, g_ref, cos_ref, sin_ref, w_hbm, *refs, tps, n_cast):
    cast_in, (main_ref, a1_ref, a2_ref) = refs[:n_cast], refs[n_cast:n_cast + 3]
    cast_out = refs[n_cast + 3:2 * n_cast + 3]
    hf_scr, hd1_scr = refs[2 * n_cast + 3:2 * n_cast + 5]
    lhs = refs[2 * n_cast + 5:2 * n_cast + 8]
    w_ref, stage, sems = refs[2 * n_cast + 8:2 * n_cast + 11]
    tm = x_ref.shape[0]
    step = pl.program_id(0)

    @pl.when(step == 0)
    def _():
        rows = stage.shape[1]
        chunks = w_ref.shape[0] // rows

        def fetch(c):
            return pltpu.make_async_copy(w_hbm.at[pl.ds(c * rows, rows), pl.ds(0, w_ref.shape[1])],
                                         stage.at[c % 2], sems.at[c % 2])

        fetch(0).start()
        for c in range(chunks):
            if c + 1 < chunks:
                fetch(c + 1).start()
            fetch(c).wait()
            w_ref[c * rows:(c + 1) * rows, :] = stage[c % 2].astype(BF16)

    for src, dst in zip(cast_in, cast_out):
        dst[...] = src[...].astype(BF16)

    def prepare(x_ref, mod_ref, lhs):
        h0_scr, h1_scr, h2_scr = lhs
        hf = _modulated_norm(x_ref[...], g_ref[...], mod_ref, 0)
        h0_scr[...] = hf.astype(BF16)
        d1, d2 = ATT_GROUPS[1][1], ATT_GROUPS[2][1]
        n1, n2 = tm // d1, tm // d2
        for c in range(D_MODEL // LANES):
            cs = slice(c * LANES, (c + 1) * LANES)
            hf_scr[c] = hf[:, cs]
            for r in range(d1):
                part = hf_scr[c, pl.ds(r, n1, stride=d1), :]
                hd1_scr[c, r * n1:(r + 1) * n1, :] = part
                h1_scr[r * n1:(r + 1) * n1, cs] = part.astype(BF16)
            for r in range(d2):
                src = pl.ds((r % d1) * n1 + r // d1, n2, stride=d1)
                h2_scr[r * n2:(r + 1) * n2, cs] = hd1_scr[c, src, :].astype(BF16)

    def w_tile(t):
        return w_ref[:, t * PROJ_TN:(t + 1) * PROJ_TN]

    def main_tile(t, segments, h0_scr, pieces=1):
        if pieces > 1:
            width = PROJ_TN // pieces
            for p in range(pieces):
                p0, p1 = p * width, (p + 1) * width
                acc = jnp.dot(h0_scr[...], w_ref[:, t * PROJ_TN + p0:t * PROJ_TN + p1],
                              preferred_element_type=F32)
                for c0, c1, kind, arg in segments:
                    lo, hi = max(c0, p0), min(c1, p1)
                    if lo < hi:
                        assert kind in ("scale", "copy")
                        seg = acc[:, lo - p0:hi - p0]
                        seg = seg * arg if kind == "scale" else seg
                        main_ref[:, t * PROJ_TN + lo:t * PROJ_TN + hi] = seg.astype(BF16)
            return
        acc = jnp.dot(h0_scr[...], w_tile(t), preferred_element_type=F32)
        if any(kind == "rot" for _, _, kind, _ in segments):
            cos, sin = cos_ref[...], sin_ref[...]
            row = lax.broadcasted_iota(jnp.int32, cos.shape, 0)
            pos = (((step % tps) * tm + row) % RET_C + 1).astype(F32)
        base = t * PROJ_TN
        for c0, c1, kind, arg in segments:
            seg = acc[:, c0:c1]
            if kind == "rot":
                head, is_key = arg
                rate = -RET_LOG_GAMMA[head] if is_key else RET_LOG_GAMMA[head]
                dec = jnp.exp(rate * pos) * ((RET_DK ** -0.5) if is_key else 1.0)
                cd, sd = cos * dec, sin * dec
                cm = (c0 + c1) // 2
                t1, t2 = acc[:, c0:cm], acc[:, cm:c1]
                main_ref[:, base + c0:base + cm] = (t1 * cd - t2 * sd).astype(BF16)
                main_ref[:, base + cm:base + c1] = (t1 * sd + t2 * cd).astype(BF16)
            elif kind == "silu":
                main_ref[:, base + c0:base + c1] = _silu(seg).astype(BF16)
            elif kind == "scale":
                main_ref[:, base + c0:base + c1] = (seg * arg).astype(BF16)
            else:
                main_ref[:, base + c0:base + c1] = seg.astype(BF16)

    def multiply(lhs):
        h0_scr, h1_scr, h2_scr = lhs
        for t, (dil, h_scr, a_ref) in enumerate(((ATT_GROUPS[1][1], h1_scr, a1_ref),
                                                 (ATT_GROUPS[2][1], h2_scr, a2_ref))):
            n = tm // dil
            acc = jnp.dot(h_scr[...], w_tile(PROJ_MAIN_TILES + t), preferred_element_type=F32)
            gw = ATT_GROUP_W
            for r in range(dil):
                rows = slice(r * n, (r + 1) * n)
                a_ref[0, r, :, :gw] = (acc[rows, :gw] * ATT_QSCALE).astype(BF16)
                a_ref[0, r, :, gw:] = acc[rows, gw:].astype(BF16)
        last = len(PROJ_MAIN_LAYOUT) - 1
        for t, segments in enumerate(PROJ_MAIN_LAYOUT):
            main_tile(t, segments, h0_scr, pieces=PROJ_LAST_PIECES if t == last else 1)

    prepare(x_ref, mod_ref, lhs)
    multiply(lhs)


def _proj(x2, norm_g, mod3, cos, sin, w_in, cast_weights, batch, seq):
    tm, tn = PROJ_TM, PROJ_TN
    tps = seq // tm
    steps = batch * seq // tm
    assert seq % tm == 0 and tm % RET_C == 0
    d1, d2 = ATT_GROUPS[1][1], ATT_GROUPS[2][1]
    assert COL_GATE == (PROJ_MAIN_TILES + 2) * tn
    sub = lambda d: pl.BlockSpec((1, d, tm // d, tn), lambda i: (i // tps, 0, i % tps, 0))
    gate_w = w_in.shape[1] - COL_GATE
    gate_rows = D_MODEL // steps
    assert D_MODEL % (steps * BF16_SUBLANES) == 0
    slab_in = [pl.BlockSpec((pl.Element(gate_rows), pl.Element(gate_w)), lambda i: (i * gate_rows, COL_GATE))]
    slab_out = [pl.BlockSpec((gate_rows, gate_w), lambda i: (i, 0))]
    slab_shapes = [jax.ShapeDtypeStruct((D_MODEL, gate_w), BF16)]
    for w in cast_weights:
        assert w.shape[0] % (steps * BF16_SUBLANES) == 0
        slab_in.append(pl.BlockSpec((w.shape[0] // steps, w.shape[1]), lambda i: (i, 0)))
        slab_out.append(slab_in[-1])
        slab_shapes.append(jax.ShapeDtypeStruct(w.shape, BF16))
    outs = pl.pallas_call(
        functools.partial(_proj_kernel, tps=tps, n_cast=len(slab_in)),
        out_shape=[jax.ShapeDtypeStruct((batch * seq, PROJ_MAIN_TILES * tn), BF16),
                   jax.ShapeDtypeStruct((batch, d1, seq // d1, tn), BF16),
                   jax.ShapeDtypeStruct((batch, d2, seq // d2, tn), BF16)] + slab_shapes,
        grid=(steps,),
        in_specs=[pl.BlockSpec((tm, D_MODEL), lambda i: (i, 0)),
                  pl.BlockSpec((1, 1, 6 * D_MODEL), lambda i: (i // tps, 0, 0)),
                  _const_spec((1, D_MODEL)),
                  pl.BlockSpec((tm, RET_DK // 2), lambda i: (i % tps, 0)),
                  pl.BlockSpec((tm, RET_DK // 2), lambda i: (i % tps, 0)),
                  pl.BlockSpec(memory_space=pl.ANY)] + slab_in,
        out_specs=[pl.BlockSpec((tm, PROJ_MAIN_TILES * tn), lambda i: (i, 0)), sub(d1), sub(d2)] + slab_out,
        scratch_shapes=[pltpu.VMEM((D_MODEL // LANES, tm, LANES), F32)] * 2
                       + [pltpu.VMEM((tm, D_MODEL), BF16)] * 3
                       + [pltpu.VMEM((D_MODEL, COL_GATE), BF16),
                          pltpu.VMEM((2, PROJ_W_STAGE_ROWS, COL_GATE), F32),
                          pltpu.SemaphoreType.DMA((2,))],
        compiler_params=_params(("arbitrary",)),
        name="proj",
    )(x2, mod3, norm_g, cos, sin, w_in, w_in, *cast_weights)
    return outs[0], outs[1], outs[2], outs[3:]


RET_RING = 3


def _ret_kernel(main_hbm, cdec_ref, gng_ref, gnb_ref, o_ref, state, st_in, q_buf, k_buf, v_buf, g_buf, sems):
    n_heads = pl.num_programs(1)
    step = pl.program_id(0) * n_heads + pl.program_id(1)
    last = pl.num_programs(0) * n_heads - 1
    bufs = (q_buf, k_buf, v_buf, g_buf)
    col0 = (0, RET_QK_W, 2 * RET_QK_W, 2 * RET_QK_W + RET_V_W)

    def copies(s, slot):
        b, h = s // n_heads, s % n_heads
        out = []
        for j, buf in enumerate(bufs):
            width = buf.shape[2]
            cols = pl.ds(pl.multiple_of(col0[j] + h * width, LANES), width)
            out.append(pltpu.make_async_copy(main_hbm.at[b, :, cols], buf.at[slot], sems.at[slot, j]))
        return out

    @pl.when(step == 0)
    def _():
        for s in range(RET_RING - 1):
            for cp in copies(s, s):
                cp.start()

    for slot in range(RET_RING):
        @pl.when(step % RET_RING == slot)
        def _(slot=slot):
            ahead = step + (RET_RING - 1)

            @pl.when(ahead <= last)
            def _():
                for cp in copies(ahead, (slot + RET_RING - 1) % RET_RING):
                    cp.start()

            for cp in copies(step, slot):
                cp.wait()
            _ret_body(q_buf.at[slot], k_buf.at[slot], v_buf.at[slot], g_buf.at[slot],
                      cdec_ref, gng_ref, gnb_ref, o_ref, state, st_in)


def _ret_body(q_ref, k_ref, v_ref, gate_ref, cdec_ref, gng_ref, gnb_ref, o_ref, state, st_in):
    C = RET_C
    nc = q_ref.shape[0] // C
    cdec = cdec_ref[0]
    gng = gng_ref[0]
    gnb = gnb_ref[0]
    chunk = lambda ci: pl.ds(pl.multiple_of(ci * C, C), C)

    state[...] = jnp.zeros_like(state)
    st_in[0] = jnp.zeros(st_in.shape[1:], BF16)

    def advance(ci, carry):
        rows = chunk(ci)
        kv = lax.dot_general(k_ref[rows, :], v_ref[rows, :], (((0,), (0,)), ((), ())),
                             preferred_element_type=F32)
        new = (state[...] + kv) * cdec
        state[...] = new
        st_in[ci + 1] = new.astype(BF16)
        return carry

    lax.fori_loop(0, nc - 1, advance, 0, unroll=True)

    qi = lax.broadcasted_iota(jnp.int32, (C, C), 0)
    kj = lax.broadcasted_iota(jnp.int32, (C, C), 1)
    causal = qi >= kj

    def emit(ci, carry):
        rows = chunk(ci)
        q = q_ref[rows, :]
        s = lax.dot_general(q, k_ref[rows, :], (((1,), (1,)), ((), ())), preferred_element_type=F32)
        s = jnp.where(causal, s, 0.0).astype(BF16)
        o = (jnp.dot(s, v_ref[rows, :], preferred_element_type=F32)
             + jnp.dot(q, st_in[ci], preferred_element_type=F32))
        mu = jnp.mean(o, axis=-1, keepdims=True)
        oc = o - mu
        var = jnp.mean(oc * oc, axis=-1, keepdims=True)
        on = oc * lax.rsqrt(var + GN_EPS) * gng + gnb
        o_ref[0, rows, :] = (gate_ref[rows, :].astype(F32) * on).astype(BF16)
        return carry

    lax.fori_loop(0, nc, emit, 0, unroll=True)


def _retention_parts(main3, gn_g, gn_b):
    B, S, _ = main3.shape
    H, C = RET_HEADS, RET_C
    assert S % C == 0
    assert B * H >= RET_RING
    cdec = jnp.exp(jnp.asarray(RET_LOG_GAMMA, F32) * C)[:, None, None]
    in_specs = [pl.BlockSpec(memory_space=pl.ANY),
                pl.BlockSpec((1, 1, 1), lambda b, h: (h, 0, 0)),
                pl.BlockSpec((1, 1, RET_DV), lambda b, h: (h, 0, 0)),
                pl.BlockSpec((1, 1, RET_DV), lambda b, h: (h, 0, 0))]
    args = [main3, cdec, gn_g.reshape(H, 1, RET_DV), gn_b.reshape(H, 1, RET_DV)]
    ring = lambda w: pltpu.VMEM((RET_RING, S, w), BF16)
    return _CallParts(None, _ret_kernel, (B, H), in_specs, args,
                      [pl.BlockSpec((1, S, RET_DV), lambda b, h: (b, 0, h))],
                      [jax.ShapeDtypeStruct((B, S, RET_V_W), BF16)],
                      [pltpu.VMEM((RET_DK, RET_DV), F32), pltpu.VMEM((S // C, RET_DK, RET_DV), BF16),
                       ring(RET_DK), ring(RET_DK), ring(RET_DV), ring(RET_DV),
                       pltpu.SemaphoreType.DMA((RET_RING, 4))])


def _t5_bucket(dist):
    dist = np.asarray(dist)
    max_exact = REL_BUCKETS // 2
    d_f = np.maximum(dist, 1).astype(np.float32)
    large = max_exact + (np.log(d_f / np.float32(max_exact)) / np.float32(math.log(REL_MAX_DIST / max_exact))
                         * np.float32(REL_BUCKETS - max_exact)).astype(np.int32)
    large = np.minimum(large, REL_BUCKETS - 1)
    return np.where(dist < max_exact, dist, large).astype(np.int32)


def _attn_unit(n, q_ref, k_ref, v_ref, bias_scr, vaug_scr, o_ref, stats_ref):
    blk, dh = ATT_BLK, ATT_HEAD_DIM
    windowed = k_ref.shape[0] > blk
    if windowed and isinstance(n, int):
        win, first = pl.ds(max(n - 1, 0) * blk, 2 * blk), int(n == 0)
    elif windowed:
        win = pl.ds(pl.multiple_of(jnp.maximum(n - 1, 0) * blk, blk), 2 * blk)
        first = (n == 0).astype(jnp.int32)
    heads = ATT_HEADS_PER_GROUP
    lane = lax.broadcasted_iota(jnp.int32, (blk, LSE_W), 1)
    stats = jnp.ones((blk, LSE_W), F32)
    for h in range(heads):
        hs = slice(h * dh, (h + 1) * dh)
        if windowed:
            kb, vb, bias = k_ref[win, hs], v_ref[win, hs], bias_scr[first, h]
        else:
            kb, vb, bias = k_ref[:, hs], v_ref[:, hs], bias_scr[h]
        vaug_scr[h, :, :dh] = vb
        s = lax.dot_general(q_ref[:, hs], kb, (((1,), (1,)), ((), ())), preferred_element_type=F32) + bias
        mx = jnp.max(s, axis=-1, keepdims=True)
        e = jnp.exp(s - mx).astype(BF16)
        o_den = jnp.dot(e, vaug_scr[h], preferred_element_type=F32)
        o_ref[:, hs] = o_den[:, :dh].astype(BF16)
        stats = jnp.where(lane == h, mx, jnp.where(lane == heads + h, o_den[:, dh:], stats))
    stats_ref[...] = stats


def _attn_init(tab_ref, bk0, bk1, bk2, bias0, bias1, bias2, vaug0, vaug1, vaug2):
    blk = ATT_BLK

    @pl.when((pl.program_id(0) == 0) & (pl.program_id(1) == 0))
    def _():
        for vaug in (vaug0, vaug1, vaug2):
            vaug[...] = jnp.ones(vaug.shape, BF16)
        qi = lax.broadcasted_iota(jnp.int32, (blk, 2 * blk), 0)
        kj = lax.broadcasted_iota(jnp.int32, (blk, 2 * blk), 1)
        m = blk + qi - kj
        band = (m >= 0) & (m <= blk)
        masked = jnp.full((blk, blk), NEG, F32)
        for gi, (bk_ref, bias_scr) in enumerate(((bk0, bias0), (bk1, bias1), (bk2, bias2))):
            bucket = bk_ref[...]
            for h in range(ATT_HEADS_PER_GROUP):
                col = gi * ATT_HEADS_PER_GROUP + h
                acc = jnp.zeros((blk, 2 * blk), F32)
                for t in range(REL_BUCKETS):
                    acc = jnp.where(bucket == t, tab_ref[t, col], acc)
                tile = jnp.where(band, acc, NEG)
                if len(bias_scr.shape) == 4:
                    bias_scr[0, h] = tile
                    bias_scr[1, h] = jnp.concatenate([tile[:, blk:], masked], axis=1)
                else:
                    bias_scr[h] = tile[:, blk:]


ATT_RING = 3


def _attn_kernel(tab_ref, main_hbm, k0, v0, att1_hbm, att2_hbm, bk0, bk1, bk2,
                 o0, l0, o1, l1, o2, l2, bias0, bias1, bias2, vaug0, vaug1, vaug2,
                 q0_buf, q1_buf, k1_buf, v1_buf, q2_buf, k2_buf, v2_buf, sems, *, nb1, per_step):
    n_steps = pl.num_programs(1)
    step = pl.program_id(0) * n_steps + pl.program_id(1)
    last = pl.num_programs(0) * n_steps - 1
    gw = ATT_GROUP_W
    ns1 = per_step // nb1

    def copies(s, slot):
        b, i = s // n_steps, s % n_steps
        rows0 = pl.ds(pl.multiple_of(i * (per_step * ATT_BLK), ATT_BLK), per_step * ATT_BLK)
        srcs = [main_hbm.at[b, rows0, pl.ds(COL_ATT, gw)]]
        srcs += [att1_hbm.at[b, pl.ds(i * ns1, ns1), :, pl.ds(part * gw, gw)] for part in range(3)]
        srcs += [att2_hbm.at[b, pl.ds(i * per_step, per_step), :, pl.ds(part * gw, gw)] for part in range(3)]
        bufs = (q0_buf, q1_buf, k1_buf, v1_buf, q2_buf, k2_buf, v2_buf)
        return [pltpu.make_async_copy(src, buf.at[slot], sems.at[slot, j])
                for j, (src, buf) in enumerate(zip(srcs, bufs))]

    slot = step % ATT_RING

    @pl.when(step == 0)
    def _():
        for s in range(ATT_RING - 1):
            for cp in copies(s, s):
                cp.start()

    ahead = step + (ATT_RING - 1)

    @pl.when(ahead <= last)
    def _():
        for cp in copies(ahead, ahead % ATT_RING):
            cp.start()

    _attn_init(tab_ref, bk0, bk1, bk2, bias0, bias1, bias2, vaug0, vaug1, vaug2)
    for cp in copies(step, slot):
        cp.wait()
    _attn_blocks(pl.program_id(1), q0_buf.at[slot], k0, v0, q1_buf.at[slot], k1_buf.at[slot], v1_buf.at[slot],
                 q2_buf.at[slot], k2_buf.at[slot], v2_buf.at[slot], o0, l0, o1, l1, o2, l2,
                 bias0, bias1, bias2, vaug0, vaug1, vaug2, nb1=nb1, per_step=per_step)


def _attn_blocks(i, q0, k0, v0, q1, k1, v1, q2, k2, v2, o0, l0, o1, l1, o2, l2,
                 bias0, bias1, bias2, vaug0, vaug1, vaug2, *, nb1, per_step):
    blk = ATT_BLK
    for u in range(per_step):
        rows = pl.ds(u * blk, blk)
        _attn_unit(i * per_step + u, q0.at[rows], k0, v0, bias0, vaug0.at[u], o0.at[rows], l0.at[rows])
        sub, n1 = divmod(u, nb1)
        rows1 = pl.ds(n1 * blk, blk)
        _attn_unit(n1, q1.at[sub, rows1], k1.at[sub], v1.at[sub], bias1, vaug1.at[u],
                   o1.at[sub, rows1], l1.at[sub, rows1])
        _attn_unit(0, q2.at[u], k2.at[u], v2.at[u], bias2, vaug2.at[u], o2.at[u], l2.at[u])


def _attention_parts(main3, att1, att2, rel_bias):
    B, S, _ = main3.shape
    blk, gw = ATT_BLK, ATT_GROUP_W
    steps = S // blk
    lens = [S // dil for _, dil in ATT_GROUPS]
    nbs = [L // blk for L in lens]
    assert all(win // dil == blk for win, dil in ATT_GROUPS)
    assert nbs[0] == steps and nbs[1] >= 2 and nbs[2] == 1
    c0 = COL_ATT // gw
    nb1 = nbs[1]
    G = ATT_BLOCKS_PER_STEP
    assert G % nb1 == 0 and steps % G == 0
    ns1 = G // nb1
    assert B * (steps // G) >= ATT_RING
    in_specs = [pl.BlockSpec(memory_space=pltpu.SMEM),
                pl.BlockSpec(memory_space=pl.ANY),
                pl.BlockSpec((None, S, gw), lambda b, i: (b, 0, c0 + 1)),
                pl.BlockSpec((None, S, gw), lambda b, i: (b, 0, c0 + 2)),
                pl.BlockSpec(memory_space=pl.ANY),
                pl.BlockSpec(memory_space=pl.ANY)]
    args = [rel_bias, main3, main3, main3, att1, att2]
    qi = np.arange(blk)[:, None]
    kj = np.arange(2 * blk)[None, :]
    m = blk + qi - kj
    for _, dil in ATT_GROUPS:
        in_specs.append(_const_spec((blk, 2 * blk)))
        args.append(jnp.asarray(_t5_bucket(np.clip(m, 0, blk) * dil)))
    out_specs, out_shapes = [], []
    for (_, dil), nb in zip(ATT_GROUPS, nbs):
        for w, dt in ((gw, BF16), (LSE_W, F32)):
            if nb == steps:
                out_specs.append(pl.BlockSpec((None, G * blk, w), lambda b, i: (b, i, 0)))
                out_shapes.append(jax.ShapeDtypeStruct((B, S, w), dt))
            elif nb == 1:
                out_specs.append(pl.BlockSpec((None, G, blk, w), lambda b, i: (b, i, 0, 0)))
                out_shapes.append(jax.ShapeDtypeStruct((B, dil, S // dil, w), dt))
            else:
                out_specs.append(pl.BlockSpec((None, ns1, lens[1], w), lambda b, i: (b, i, 0, 0)))
                out_shapes.append(jax.ShapeDtypeStruct((B, dil, S // dil, w), dt))
    heads = ATT_HEADS_PER_GROUP
    scratch = [pltpu.VMEM((2, heads, blk, 2 * blk), F32), pltpu.VMEM((2, heads, blk, 2 * blk), F32),
               pltpu.VMEM((heads, blk, blk), F32),
               pltpu.VMEM((G, heads, 2 * blk, 2 * ATT_HEAD_DIM), BF16),
               pltpu.VMEM((G, heads, 2 * blk, 2 * ATT_HEAD_DIM), BF16),
               pltpu.VMEM((G, heads, blk, 2 * ATT_HEAD_DIM), BF16)]
    scratch += [pltpu.VMEM((ATT_RING, G * blk, gw), BF16)]
    scratch += [pltpu.VMEM((ATT_RING, ns1, lens[1], gw), BF16)] * 3
    scratch += [pltpu.VMEM((ATT_RING, G, blk, gw), BF16)] * 3
    scratch += [pltpu.SemaphoreType.DMA((ATT_RING, 7))]
    return _CallParts(None, functools.partial(_attn_kernel, nb1=nb1, per_step=G), (B, steps // G),
                      in_specs, args, out_specs, out_shapes, scratch)


class _CallParts(NamedTuple):
    init: object
    body: object
    grid: tuple
    in_specs: list
    args: list
    out_specs: list
    out_shapes: list
    scratch: list


def _call(parts, name):
    def body(*refs):
        if parts.init is not None:
            parts.init(*refs)
        parts.body(*refs)

    return pl.pallas_call(
        body,
        out_shape=parts.out_shapes,
        grid=parts.grid,
        in_specs=parts.in_specs,
        out_specs=parts.out_specs,
        scratch_shapes=parts.scratch,
        compiler_params=_params(("arbitrary",) * len(parts.grid)),
        name=name,
    )(*parts.args)


def _mixers(main3, att1, att2, rel_bias, gn_g, gn_b):
    retg, = _call(_retention_parts(main3, gn_g, gn_b), "ret")
    att = _call(_attention_parts(main3, att1, att2, rel_bias), "attn")
    return retg, att[0::2], att[1::2]


def _merge_kernel(x_ref, g_ref, mod_ref, retg_ref, o0_ref, o1_ref, o2_ref, s0_ref, s1_ref, s2_ref,
                  wg_ref, wr_ref, wa_ref, wo_ref, out_ref, o1_scr, o2_scr, s1_scr, s2_scr):
    tm = x_ref.shape[0]
    for dil, o_ref, s_ref, o_scr, s_scr in ((ATT_GROUPS[1][1], o1_ref, s1_ref, o1_scr, s1_scr),
                                            (ATT_GROUPS[2][1], o2_ref, s2_ref, o2_scr, s2_scr)):
        n = tm // dil
        for r in range(dil):
            s_scr[pl.ds(r, n, stride=dil), :] = s_ref[0, r]
            for h in range(ATT_HEADS_PER_GROUP):
                hs = slice(h * ATT_HEAD_DIM, (h + 1) * ATT_HEAD_DIM)
                o_scr[h, pl.ds(r, n, stride=dil), :] = o_ref[0, r, :, hs].astype(F32)
    part = tm // MERGE_ROW_PARTS
    for p in range(MERGE_ROW_PARTS):
        rows = slice(p * part, (p + 1) * part)
        ret_out = jnp.dot(retg_ref[rows, :], wr_ref[...], preferred_element_type=F32)
        x = x_ref[rows, :]
        h = _modulated_norm(x, g_ref[...], mod_ref, 0).astype(BF16)
        gates = jax.nn.sigmoid(jnp.dot(h, wg_ref[...], preferred_element_type=F32))
        stats = (s0_ref[rows, :], s1_scr[rows, :], s2_scr[rows, :])
        top = jnp.maximum(jnp.maximum(stats[0], stats[1]), stats[2])
        a = [jnp.exp(st - top) for st in stats]
        lane = lax.broadcasted_iota(jnp.int32, top.shape, 1)
        heads = ATT_HEADS_PER_GROUP
        den = sum(ag * pltpu.roll(st, LSE_W - heads, axis=1) for ag, st in zip(a, stats))
        inv = 1.0 / jnp.where(lane < heads, den, 1.0)
        w0, w1, w2 = (ag * inv for ag in a)
        parts = []
        for hd in range(heads):
            hs = slice(hd * ATT_HEAD_DIM, (hd + 1) * ATT_HEAD_DIM)
            parts.append(w0[:, hd:hd + 1] * o0_ref[rows, hs].astype(F32)
                         + w1[:, hd:hd + 1] * o1_scr[hd, rows, :]
                         + w2[:, hd:hd + 1] * o2_scr[hd, rows, :])
        att = jnp.concatenate(parts, axis=-1).astype(BF16)
        att_out = jnp.dot(att, wa_ref[...], preferred_element_type=F32)
        merged = gates[:, :D_MODEL] * ret_out + gates[:, D_MODEL:] * att_out
        y = jnp.dot(merged.astype(BF16), wo_ref[...], preferred_element_type=F32)
        out_ref[rows, :] = x + mod_ref[0, :, 2 * D_MODEL:3 * D_MODEL] * y


def _merge(x2, norm_g, mod3, retg2, o, lse, w_gate, w_ret_out, w_att_out, w_o, batch, seq):
    tm = ROW_TM
    tps = seq // tm
    row = lambda w: pl.BlockSpec((tm, w), lambda i: (i, 0))
    sub = lambda d, w: pl.BlockSpec((1, d, tm // d, w), lambda i: (i // tps, 0, i % tps, 0))
    d1, d2 = ATT_GROUPS[1][1], ATT_GROUPS[2][1]
    return pl.pallas_call(
        _merge_kernel,
        out_shape=jax.ShapeDtypeStruct((batch * seq, D_MODEL), F32),
        grid=(batch * seq // tm,),
        in_specs=[row(D_MODEL),
                  _const_spec((1, D_MODEL)),
                  pl.BlockSpec((1, 1, 6 * D_MODEL), lambda i: (i // tps, 0, 0)),
                  row(RET_V_W),
                  row(ATT_GROUP_W), sub(d1, ATT_GROUP_W), sub(d2, ATT_GROUP_W),
                  row(LSE_W), sub(d1, LSE_W), sub(d2, LSE_W),
                  _const_spec((D_MODEL, 2 * D_MODEL)),
                  _const_spec((RET_V_W, D_MODEL)),
                  _const_spec((ATT_GROUP_W, D_MODEL)),
                  _const_spec((D_MODEL, D_MODEL))],
        out_specs=row(D_MODEL),
        scratch_shapes=[pltpu.VMEM((ATT_HEADS_PER_GROUP, tm, ATT_HEAD_DIM), F32)] * 2
                       + [pltpu.VMEM((tm, LSE_W), F32)] * 2,
        compiler_params=_params(("arbitrary",)),
        name="merge",
    )(x2, norm_g, mod3, retg2, *o, *lse, w_gate, w_ret_out, w_att_out, w_o)


def _mlp_kernel(x_ref, mod_ref, g2_ref, gf_ref, w1_ref, w2_ref, out_ref, *, final):
    part = x_ref.shape[0] // MLP_ROW_PARTS
    for p in range(MLP_ROW_PARTS):
        rows = slice(p * part, (p + 1) * part)
        x = x_ref[rows, :]
        h = _modulated_norm(x, g2_ref[...], mod_ref, 3).astype(BF16)
        u = jnp.maximum(jnp.dot(h, w1_ref[...], preferred_element_type=F32), 0.0)
        y = jnp.dot((u * u).astype(BF16), w2_ref[...], preferred_element_type=F32)
        x = x + mod_ref[0, :, 5 * D_MODEL:6 * D_MODEL] * y
        out_ref[rows, :] = _rms(x, gf_ref[...]) if final else x


def _mlp(x2, mod3, norm2_g, norm_f_g, w1, w2, seq, final):
    rows = x2.shape[0]
    tm = MLP_TM
    tps = seq // tm
    return pl.pallas_call(
        functools.partial(_mlp_kernel, final=final),
        out_shape=jax.ShapeDtypeStruct((rows, D_MODEL), F32),
        grid=(rows // tm,),
        in_specs=[pl.BlockSpec((tm, D_MODEL), lambda i: (i, 0)),
                  pl.BlockSpec((1, 1, 6 * D_MODEL), lambda i: (i // tps, 0, 0)),
                  _const_spec((1, D_MODEL)),
                  _const_spec((1, D_MODEL)),
                  _const_spec((D_MODEL, D_FF)),
                  _const_spec((D_FF, D_MODEL))],
        out_specs=pl.BlockSpec((tm, D_MODEL), lambda i: (i, 0)),
        compiler_params=_params(("arbitrary",)),
        name="mlp",
    )(x2, mod3, norm2_g, norm_f_g, w1, w2)


def kernel(x, c, w_ada, b_ada, norm1_g, w_in, rel_bias, ret_gn_g, ret_gn_b, w_ret_out, w_att_out,
           w_o, norm2_g, w_ff1, w_ff2, norm_f_g):
    B, S, D = x.shape
    depth = w_ada.shape[0]
    half = RET_DK // 2
    ang = np.arange(S, dtype=np.float64)[:, None] * ROPE_BASE ** (-np.arange(half, dtype=np.float64) / half)
    cos, sin = jnp.asarray(np.cos(ang), F32), jnp.asarray(np.sin(ang), F32)
    x2 = x.reshape(B * S, D)
    for l in range(depth):
        g1 = norm1_g[l].reshape(1, D)
        mod3 = _mod(c, w_ada[l], b_ada[l]).reshape(B, 1, 6 * D)
        main2, att1, att2, (w_gate_bf, w_ret_bf, w_o_bf, w_ff1_bf, w_ff2_bf) = _proj(
            x2, g1, mod3, cos, sin, w_in[l], [w_ret_out[l], w_o[l], w_ff1[l], w_ff2[l]], B, S)
        main3 = main2.reshape(B, S, main2.shape[1])
        retg, o, stats = _mixers(main3, att1, att2, rel_bias, ret_gn_g[l], ret_gn_b[l])
        o = [o[0].reshape(B * S, ATT_GROUP_W), o[1], o[2]]
        stats = [stats[0].reshape(B * S, LSE_W), stats[1], stats[2]]
        x2 = _merge(x2, g1, mod3, retg.reshape(B * S, RET_V_W), o, stats,
                    w_gate_bf, w_ret_bf, w_att_out[l].astype(BF16), w_o_bf, B, S)
        x2 = _mlp(x2, mod3, norm2_g[l].reshape(1, D), norm_f_g.reshape(1, D),
                  w_ff1_bf, w_ff2_bf, S, final=l == depth - 1)
    return x2.reshape(B, S, D)
```

```python
import functools
import math
from typing import NamedTuple

import jax
import jax.numpy as jnp
import numpy as np
from jax import lax
from jax.experimental import pallas as pl
from jax.experimental.pallas import tpu as pltpu

F32 = jnp.float32
BF16 = jnp.bfloat16

D_MODEL = 1024
RET_HEADS = 4
RET_DK = 256
RET_DV = 512
RET_C = 256
RET_LOG_GAMMA = tuple(math.log1p(-(2.0 ** (-5.0 - h))) for h in range(RET_HEADS))
RET_QK_W = RET_HEADS * RET_DK
RET_V_W = RET_HEADS * RET_DV
ATT_GROUPS = ((128, 1), (512, 4), (2048, 16))
ATT_HEADS_PER_GROUP = 4
ATT_HEAD_DIM = 128
ATT_GROUP_W = ATT_HEADS_PER_GROUP * ATT_HEAD_DIM
ATT_BLK = 128
ATT_BLOCKS_PER_STEP = 8
LANES = 128
BF16_SUBLANES = 16
LSE_W = LANES
REL_BUCKETS = 32
REL_MAX_DIST = 2048
D_FF = 4 * D_MODEL
RMS_EPS = 1e-6
GN_EPS = 1e-5
ROPE_BASE = 10000.0
NEG = -1e30
ATT_QSCALE = ATT_HEAD_DIM ** -0.5

COL_ATT = 2 * RET_QK_W + 2 * RET_V_W
COL_GATE = COL_ATT + 9 * ATT_GROUP_W

VMEM_LIMIT = 56 * 1024 * 1024

PROJ_TM = 256
PROJ_TN = 3 * ATT_GROUP_W
PROJ_MAIN_TILES = (COL_ATT + PROJ_TN) // PROJ_TN
PROJ_LAST_PIECES = 3
PROJ_W_STAGE_ROWS = 64
ROW_TM = 512
MERGE_ROW_PARTS = 2
MLP_TM = 1024
MLP_ROW_PARTS = 2


def _params(sem):
    return pltpu.CompilerParams(dimension_semantics=sem, vmem_limit_bytes=VMEM_LIMIT)


def _const_spec(shape):
    zeros = (0,) * len(shape)
    return pl.BlockSpec(shape, lambda *_: zeros, pipeline_mode=pl.Buffered(1))


def _silu(t):
    return t * jax.nn.sigmoid(t)


def _rms(x, g):
    return x * lax.rsqrt(jnp.mean(x * x, axis=-1, keepdims=True) + RMS_EPS) * g


def _modulated_norm(x, g, mod_ref, k):
    shift = mod_ref[0, :, k * D_MODEL:(k + 1) * D_MODEL]
    scale = mod_ref[0, :, (k + 1) * D_MODEL:(k + 2) * D_MODEL]
    return _rms(x, g) * (1.0 + scale) + shift


def _mod_kernel(c_ref, w_ref, b_ref, o_ref):
    o_ref[...] = jnp.dot(_silu(c_ref[...]), w_ref[...], preferred_element_type=F32) + b_ref[...]


def _mod(c, w_ada, b_ada):
    B = c.shape[0]
    n = w_ada.shape[1]
    tn = n // 2
    return pl.pallas_call(
        _mod_kernel,
        out_shape=jax.ShapeDtypeStruct((B, n), F32),
        grid=(n // tn,),
        in_specs=[pl.BlockSpec((B, D_MODEL), lambda j: (0, 0)),
                  pl.BlockSpec((D_MODEL, tn), lambda j: (0, j)),
                  pl.BlockSpec((1, tn), lambda j: (0, j))],
        out_specs=pl.BlockSpec((B, tn), lambda j: (0, j)),
        compiler_params=_params(("arbitrary",)),
        name="mod",
    )(c, w_ada, b_ada.reshape(1, n))


def _proj_main_layout():
    spans = [(h * RET_DK, (h + 1) * RET_DK, "rot", (h % RET_HEADS, h >= RET_HEADS))
             for h in range(2 * RET_HEADS)]
    spans += [(2 * RET_QK_W, 2 * RET_QK_W + RET_V_W, "copy", None),
              (2 * RET_QK_W + RET_V_W, COL_ATT, "silu", None),
              (COL_ATT, COL_ATT + ATT_GROUP_W, "scale", ATT_QSCALE),
              (COL_ATT + ATT_GROUP_W, PROJ_MAIN_TILES * PROJ_TN, "copy", None)]
    tiles = [[] for _ in range(PROJ_MAIN_TILES)]
    for c0, c1, kind, arg in spans:
        for t in range(c0 // PROJ_TN, (c1 - 1) // PROJ_TN + 1):
            lo, hi = max(c0, t * PROJ_TN), min(c1, (t + 1) * PROJ_TN)
            assert kind != "rot" or (lo, hi) == (c0, c1)
            tiles[t].append((lo - t * PROJ_TN, hi - t * PROJ_TN, kind, arg))
    return tiles


PROJ_MAIN_LAYOUT = _proj_main_layout()


def _proj_kernel(x_ref, mod_ref, g_ref, cos_ref, sin_ref, w_hbm, *refs, tps, n_cast):
    cast_in, (main_ref, a1_ref, a2_ref) = refs[:n_cast], refs[n_cast:n_cast + 3]
    cast_out = refs[n_cast + 3:2 * n_cast + 3]
    hf_scr, hd1_scr = refs[2 * n_cast + 3:2 * n_cast + 5]
    lhs = refs[2 * n_cast + 5:2 * n_cast + 8]
    w_ref, stage, sems = refs[2 * n_cast + 8:2 * n_cast + 11]
    tm = x_ref.shape[0]
    step = pl.program_id(0)

    @pl.when(step == 0)
    def _():
        rows = stage.shape[1]
        chunks = w_ref.shape[0] // rows

        def fetch(c):
            return pltpu.make_async_copy(w_hbm.at[pl.ds(c * rows, rows), pl.ds(0, w_ref.shape[1])],
                                         stage.at[c % 2], sems.at[c % 2])

        fetch(0).start()
        for c in range(chunks):
            if c + 1 < chunks:
                fetch(c + 1).start()
            fetch(c).wait()
            w_ref[c * rows:(c + 1) * rows, :] = stage[c % 2].astype(BF16)

    for src, dst in zip(cast_in, cast_out):
        dst[...] = src[...].astype(BF16)

    def prepare(x_ref, mod_ref, lhs):
        h0_scr, h1_scr, h2_scr = lhs
        hf = _modulated_norm(x_ref[...], g_ref[...], mod_ref, 0)
        h0_scr[...] = hf.astype(BF16)
        d1, d2 = ATT_GROUPS[1][1], ATT_GROUPS[2][1]
        n1, n2 = tm // d1, tm // d2
        for c in range(D_MODEL // LANES):
            cs = slice(c * LANES, (c + 1) * LANES)
            hf_scr[c] = hf[:, cs]
            for r in range(d1):
                part = hf_scr[c, pl.ds(r, n1, stride=d1), :]
                hd1_scr[c, r * n1:(r + 1) * n1, :] = part
                h1_scr[r * n1:(r + 1) * n1, cs] = part.astype(BF16)
            for r in range(d2):
                src = pl.ds((r % d1) * n1 + r // d1, n2, stride=d1)
                h2_scr[r * n2:(r + 1) * n2, cs] = hd1_scr[c, src, :].astype(BF16)

    def w_tile(t):
        return w_ref[:, t * PROJ_TN:(t + 1) * PROJ_TN]

    def main_tile(t, segments, h0_scr, pieces=1):
        if pieces > 1:
            width = PROJ_TN // pieces
            for p in range(pieces):
                p0, p1 = p * width, (p + 1) * width
                acc = jnp.dot(h0_scr[...], w_ref[:, t * PROJ_TN + p0:t * PROJ_TN + p1],
                              preferred_element_type=F32)
                for c0, c1, kind, arg in segments:
                    lo, hi = max(c0, p0), min(c1, p1)
                    if lo < hi:
                        assert kind in ("scale", "copy")
                        seg = acc[:, lo - p0:hi - p0]
                        seg = seg * arg if kind == "scale" else seg
                        main_ref[:, t * PROJ_TN + lo:t * PROJ_TN + hi] = seg.astype(BF16)
            return
        acc = jnp.dot(h0_scr[...], w_tile(t), preferred_element_type=F32)
        if any(kind == "rot" for _, _, kind, _ in segments):
            cos, sin = cos_ref[...], sin_ref[...]
            row = lax.broadcasted_iota(jnp.int32, cos.shape, 0)
            pos = (((step % tps) * tm + row) % RET_C + 1).astype(F32)
        base = t * PROJ_TN
        for c0, c1, kind, arg in segments:
            seg = acc[:, c0:c1]
            if kind == "rot":
                head, is_key = arg
                rate = -RET_LOG_GAMMA[head] if is_key else RET_LOG_GAMMA[head]
                dec = jnp.exp(rate * pos) * ((RET_DK ** -0.5) if is_key else 1.0)
                cd, sd = cos * dec, sin * dec
                cm = (c0 + c1) // 2
                t1, t2 = acc[:, c0:cm], acc[:, cm:c1]
                main_ref[:, base + c0:base + cm] = (t1 * cd - t2 * sd).astype(BF16)
                main_ref[:, base + cm:base + c1] = (t1 * sd + t2 * cd).astype(BF16)
            elif kind == "silu":
                main_ref[:, base + c0:base + c1] = _silu(seg).astype(BF16)
            elif kind == "scale":
                main_ref[:, base + c0:base + c1] = (seg * arg).astype(BF16)
            else:
                main_ref[:, base + c0:base + c1] = seg.astype(BF16)

    def multiply(lhs):
        h0_scr, h1_scr, h2_scr = lhs
        for t, (dil, h_scr, a_ref) in enumerate(((ATT_GROUPS[1][1], h1_scr, a1_ref),
                                                 (ATT_GROUPS[2][1], h2_scr, a2_ref))):
            n = tm // dil
            acc = jnp.dot(h_scr[...], w_tile(PROJ_MAIN_TILES + t), preferred_element_type=F32)
            gw = ATT_GROUP_W
            for r in range(dil):
                rows = slice(r * n, (r + 1) * n)
                a_ref[0, r, :, :gw] = (acc[rows, :gw] * ATT_QSCALE).astype(BF16)
                a_ref[0, r, :, gw:] = acc[rows, gw:].astype(BF16)
        last = len(PROJ_MAIN_LAYOUT) - 1
        for t, segments in enumerate(PROJ_MAIN_LAYOUT):
            main_tile(t, segments, h0_scr, pieces=PROJ_LAST_PIECES if t == last else 1)

    prepare(x_ref, mod_ref, lhs)
    multiply(lhs)


def _proj(x2, norm_g, mod3, cos, sin, w_in, cast_weights, batch, seq):
    tm, tn = PROJ_TM, PROJ_TN
    tps = seq // tm
    steps = batch * seq // tm
    assert seq % tm == 0 and tm % RET_C == 0
    d1, d2 = ATT_GROUPS[1][1], ATT_GROUPS[2][1]
    assert COL_GATE == (PROJ_MAIN_TILES + 2) * tn
    sub = lambda d: pl.BlockSpec((1, d, tm // d, tn), lambda i: (i // tps, 0, i % tps, 0))
    gate_w = w_in.shape[1] - COL_GATE
    gate_rows = D_MODEL // steps
    assert D_MODEL % (steps * BF16_SUBLANES) == 0
    slab_in = [pl.BlockSpec((pl.Element(gate_rows), pl.Element(gate_w)), lambda i: (i * gate_rows, COL_GATE))]
    slab_out = [pl.BlockSpec((gate_rows, gate_w), lambda i: (i, 0))]
    slab_shapes = [jax.ShapeDtypeStruct((D_MODEL, gate_w), BF16)]
    for w in cast_weights:
        assert w.shape[0] % (steps * BF16_SUBLANES) == 0
        slab_in.append(pl.BlockSpec((w.shape[0] // steps, w.shape[1]), lambda i: (i, 0)))
        slab_out.append(slab_in[-1])
        slab_shapes.append(jax.ShapeDtypeStruct(w.shape, BF16))
    outs = pl.pallas_call(
        functools.partial(_proj_kernel, tps=tps, n_cast=len(slab_in)),
        out_shape=[jax.ShapeDtypeStruct((batch * seq, PROJ_MAIN_TILES * tn), BF16),
                   jax.ShapeDtypeStruct((batch, d1, seq // d1, tn), BF16),
                   jax.ShapeDtypeStruct((batch, d2, seq // d2, tn), BF16)] + slab_shapes,
        grid=(steps,),
        in_specs=[pl.BlockSpec((tm, D_MODEL), lambda i: (i, 0)),
                  pl.BlockSpec((1, 1, 6 * D_MODEL), lambda i: (i // tps, 0, 0)),
                  _const_spec((1, D_MODEL)),
                  pl.BlockSpec((tm, RET_DK // 2), lambda i: (i % tps, 0)),
                  pl.BlockSpec((tm, RET_DK // 2), lambda i: (i % tps, 0)),
                  pl.BlockSpec(memory_space=pl.ANY)] + slab_in,
        out_specs=[pl.BlockSpec((tm, PROJ_MAIN_TILES * tn), lambda i: (i, 0)), sub(d1), sub(d2)] + slab_out,
        scratch_shapes=[pltpu.VMEM((D_MODEL // LANES, tm, LANES), F32)] * 2
                       + [pltpu.VMEM((tm, D_MODEL), BF16)] * 3
                       + [pltpu.VMEM((D_MODEL, COL_GATE), BF16),
                          pltpu.VMEM((2, PROJ_W_STAGE_ROWS, COL_GATE), F32),
                          pltpu.SemaphoreType.DMA((2,))],
        compiler_params=_params(("arbitrary",)),
        name="proj",
    )(x2, mod3, norm_g, cos, sin, w_in, w_in, *cast_weights)
    return outs[0], outs[1], outs[2], outs[3:]


RET_RING = 3


def _ret_kernel(main_hbm, cdec_ref, gng_ref, gnb_ref, o_ref, state, st_in, q_buf, k_buf, v_buf, g_buf, sems):
    n_heads = pl.num_programs(1)
    step = pl.program_id(0) * n_heads + pl.program_id(1)
    last = pl.num_programs(0) * n_heads - 1
    bufs = (q_buf, k_buf, v_buf, g_buf)
    col0 = (0, RET_QK_W, 2 * RET_QK_W, 2 * RET_QK_W + RET_V_W)

    def copies(s, slot):
        b, h = s // n_heads, s % n_heads
        out = []
        for j, buf in enumerate(bufs):
            width = buf.shape[2]
            cols = pl.ds(pl.multiple_of(col0[j] + h * width, LANES), width)
            out.append(pltpu.make_async_copy(main_hbm.at[b, :, cols], buf.at[slot], sems.at[slot, j]))
        return out

    @pl.when(step == 0)
    def _():
        for s in range(RET_RING - 1):
            for cp in copies(s, s):
                cp.start()

    for slot in range(RET_RING):
        @pl.when(step % RET_RING == slot)
        def _(slot=slot):
            ahead = step + (RET_RING - 1)

            @pl.when(ahead <= last)
            def _():
                for cp in copies(ahead, (slot + RET_RING - 1) % RET_RING):
                    cp.start()

            for cp in copies(step, slot):
                cp.wait()
            _ret_body(q_buf.at[slot], k_buf.at[slot], v_buf.at[slot], g_buf.at[slot],
                      cdec_ref, gng_ref, gnb_ref, o_ref, state, st_in)


def _ret_body(q_ref, k_ref, v_ref, gate_ref, cdec_ref, gng_ref, gnb_ref, o_ref, state, st_in):
    C = RET_C
    nc = q_ref.shape[0] // C
    cdec = cdec_ref[0]
    gng = gng_ref[0]
    gnb = gnb_ref[0]
    chunk = lambda ci: pl.ds(pl.multiple_of(ci * C, C), C)

    state[...] = jnp.zeros_like(state)
    st_in[0] = jnp.zeros(st_in.shape[1:], BF16)

    def advance(ci, carry):
        rows = chunk(ci)
        kv = lax.dot_general(k_ref[rows, :], v_ref[rows, :], (((0,), (0,)), ((), ())),
                             preferred_element_type=F32)
        new = (state[...] + kv) * cdec
        state[...] = new
        st_in[ci + 1] = new.astype(BF16)
        return carry

    lax.fori_loop(0, nc - 1, advance, 0, unroll=True)

    qi = lax.broadcasted_iota(jnp.int32, (C, C), 0)
    kj = lax.broadcasted_iota(jnp.int32, (C, C), 1)
    causal = qi >= kj

    def emit(ci, carry):
        rows = chunk(ci)
        q = q_ref[rows, :]
        s = lax.dot_general(q, k_ref[rows, :], (((1,), (1,)), ((), ())), preferred_element_type=F32)
        s = jnp.where(causal, s, 0.0).astype(BF16)
        o = (jnp.dot(s, v_ref[rows, :], preferred_element_type=F32)
             + jnp.dot(q, st_in[ci], preferred_element_type=F32))
        mu = jnp.mean(o, axis=-1, keepdims=True)
        oc = o - mu
        var = jnp.mean(oc * oc, axis=-1, keepdims=True)
        on = oc * lax.rsqrt(var + GN_EPS) * gng + gnb
        o_ref[0, rows, :] = (gate_ref[rows, :].astype(F32) * on).astype(BF16)
        return carry

    lax.fori_loop(0, nc, emit, 0, unroll=True)


def _retention_parts(main3, gn_g, gn_b):
    B, S, _ = main3.shape
    H, C = RET_HEADS, RET_C
    assert S % C == 0
    assert B * H >= RET_RING
    cdec = jnp.exp(jnp.asarray(RET_LOG_GAMMA, F32) * C)[:, None, None]
    in_specs = [pl.BlockSpec(memory_space=pl.ANY),
                pl.BlockSpec((1, 1, 1), lambda b, h: (h, 0, 0)),
                pl.BlockSpec((1, 1, RET_DV), lambda b, h: (h, 0, 0)),
                pl.BlockSpec((1, 1, RET_DV), lambda b, h: (h, 0, 0))]
    args = [main3, cdec, gn_g.reshape(H, 1, RET_DV), gn_b.reshape(H, 1, RET_DV)]
    ring = lambda w: pltpu.VMEM((RET_RING, S, w), BF16)
    return _CallParts(_ret_kernel, (B, H), in_specs, args,
                      [pl.BlockSpec((1, S, RET_DV), lambda b, h: (b, 0, h))],
                      [jax.ShapeDtypeStruct((B, S, RET_V_W), BF16)],
                      [pltpu.VMEM((RET_DK, RET_DV), F32), pltpu.VMEM((S // C, RET_DK, RET_DV), BF16),
                       ring(RET_DK), ring(RET_DK), ring(RET_DV), ring(RET_DV),
                       pltpu.SemaphoreType.DMA((RET_RING, 4))])


def _t5_bucket(dist):
    dist = np.asarray(dist)
    max_exact = REL_BUCKETS // 2
    d_f = np.maximum(dist, 1).astype(np.float32)
    large = max_exact + (np.log(d_f / np.float32(max_exact)) / np.float32(math.log(REL_MAX_DIST / max_exact))
                         * np.float32(REL_BUCKETS - max_exact)).astype(np.int32)
    large = np.minimum(large, REL_BUCKETS - 1)
    return np.where(dist < max_exact, dist, large).astype(np.int32)


def _attn_unit(n, q_ref, k_ref, v_ref, bias_scr, vaug_scr, o_ref, stats_ref):
    blk, dh = ATT_BLK, ATT_HEAD_DIM
    windowed = k_ref.shape[0] > blk
    if windowed and isinstance(n, int):
        win, first = pl.ds(max(n - 1, 0) * blk, 2 * blk), int(n == 0)
    elif windowed:
        win = pl.ds(pl.multiple_of(jnp.maximum(n - 1, 0) * blk, blk), 2 * blk)
        first = (n == 0).astype(jnp.int32)
    heads = ATT_HEADS_PER_GROUP
    lane = lax.broadcasted_iota(jnp.int32, (blk, LSE_W), 1)
    stats = jnp.ones((blk, LSE_W), F32)
    for h in range(heads):
        hs = slice(h * dh, (h + 1) * dh)
        if windowed:
            kb, vb, bias = k_ref[win, hs], v_ref[win, hs], bias_scr[first, h]
        else:
            kb, vb, bias = k_ref[:, hs], v_ref[:, hs], bias_scr[h]
        vaug_scr[h, :, :dh] = vb
        s = lax.dot_general(q_ref[:, hs], kb, (((1,), (1,)), ((), ())), preferred_element_type=F32) + bias
        mx = jnp.max(s, axis=-1, keepdims=True)
        e = jnp.exp(s - mx).astype(BF16)
        o_den = jnp.dot(e, vaug_scr[h], preferred_element_type=F32)
        o_ref[:, hs] = o_den[:, :dh].astype(BF16)
        stats = jnp.where(lane == h, mx, jnp.where(lane == heads + h, o_den[:, dh:], stats))
    stats_ref[...] = stats


def _attn_init(tab_ref, bk0, bk1, bk2, bias0, bias1, bias2, vaug0, vaug1, vaug2):
    blk = ATT_BLK

    @pl.when((pl.program_id(0) == 0) & (pl.program_id(1) == 0))
    def _():
        for vaug in (vaug0, vaug1, vaug2):
            vaug[...] = jnp.ones(vaug.shape, BF16)
        qi = lax.broadcasted_iota(jnp.int32, (blk, 2 * blk), 0)
        kj = lax.broadcasted_iota(jnp.int32, (blk, 2 * blk), 1)
        m = blk + qi - kj
        band = (m >= 0) & (m <= blk)
        masked = jnp.full((blk, blk), NEG, F32)
        for gi, (bk_ref, bias_scr) in enumerate(((bk0, bias0), (bk1, bias1), (bk2, bias2))):
            bucket = bk_ref[...]
            for h in range(ATT_HEADS_PER_GROUP):
                col = gi * ATT_HEADS_PER_GROUP + h
                acc = jnp.zeros((blk, 2 * blk), F32)
                for t in range(REL_BUCKETS):
                    acc = jnp.where(bucket == t, tab_ref[t, col], acc)
                tile = jnp.where(band, acc, NEG)
                if len(bias_scr.shape) == 4:
                    bias_scr[0, h] = tile
                    bias_scr[1, h] = jnp.concatenate([tile[:, blk:], masked], axis=1)
                else:
                    bias_scr[h] = tile[:, blk:]


ATT_RING = 3


def _attn_kernel(tab_ref, main_hbm, k0, v0, att1_hbm, att2_hbm, bk0, bk1, bk2,
                 o0, l0, o1, l1, o2, l2, bias0, bias1, bias2, vaug0, vaug1, vaug2,
                 q0_buf, q1_buf, k1_buf, v1_buf, q2_buf, k2_buf, v2_buf, sems, *, nb1, per_step):
    n_steps = pl.num_programs(1)
    step = pl.program_id(0) * n_steps + pl.program_id(1)
    last = pl.num_programs(0) * n_steps - 1
    gw = ATT_GROUP_W
    ns1 = per_step // nb1

    def copies(s, slot):
        b, i = s // n_steps, s % n_steps
        rows0 = pl.ds(pl.multiple_of(i * (per_step * ATT_BLK), ATT_BLK), per_step * ATT_BLK)
        srcs = [main_hbm.at[b, rows0, pl.ds(COL_ATT, gw)]]
        srcs += [att1_hbm.at[b, pl.ds(i * ns1, ns1), :, pl.ds(part * gw, gw)] for part in range(3)]
        srcs += [att2_hbm.at[b, pl.ds(i * per_step, per_step), :, pl.ds(part * gw, gw)] for part in range(3)]
        bufs = (q0_buf, q1_buf, k1_buf, v1_buf, q2_buf, k2_buf, v2_buf)
        return [pltpu.make_async_copy(src, buf.at[slot], sems.at[slot, j])
                for j, (src, buf) in enumerate(zip(srcs, bufs))]

    slot = step % ATT_RING

    @pl.when(step == 0)
    def _():
        for s in range(ATT_RING - 1):
            for cp in copies(s, s):
                cp.start()

    ahead = step + (ATT_RING - 1)

    @pl.when(ahead <= last)
    def _():
        for cp in copies(ahead, ahead % ATT_RING):
            cp.start()

    _attn_init(tab_ref, bk0, bk1, bk2, bias0, bias1, bias2, vaug0, vaug1, vaug2)
    for cp in copies(step, slot):
        cp.wait()
    _attn_blocks(pl.program_id(1), q0_buf.at[slot], k0, v0, q1_buf.at[slot], k1_buf.at[slot], v1_buf.at[slot],
                 q2_buf.at[slot], k2_buf.at[slot], v2_buf.at[slot], o0, l0, o1, l1, o2, l2,
                 bias0, bias1, bias2, vaug0, vaug1, vaug2, nb1=nb1, per_step=per_step)


def _attn_blocks(i, q0, k0, v0, q1, k1, v1, q2, k2, v2, o0, l0, o1, l1, o2, l2,
                 bias0, bias1, bias2, vaug0, vaug1, vaug2, *, nb1, per_step):
    blk = ATT_BLK
    for u in range(per_step):
        rows = pl.ds(u * blk, blk)
        _attn_unit(i * per_step + u, q0.at[rows], k0, v0, bias0, vaug0.at[u], o0.at[rows], l0.at[rows])
        sub, n1 = divmod(u, nb1)
        rows1 = pl.ds(n1 * blk, blk)
        _attn_unit(n1, q1.at[sub, rows1], k1.at[sub], v1.at[sub], bias1, vaug1.at[u],
                   o1.at[sub, rows1], l1.at[sub, rows1])
        _attn_unit(0, q2.at[u], k2.at[u], v2.at[u], bias2, vaug2.at[u], o2.at[u], l2.at[u])


def _attention_parts(main3, att1, att2, rel_bias):
    B, S, _ = main3.shape
    blk, gw = ATT_BLK, ATT_GROUP_W
    steps = S // blk
    lens = [S // dil for _, dil in ATT_GROUPS]
    nbs = [L // blk for L in lens]
    assert all(win // dil == blk for win, dil in ATT_GROUPS)
    assert nbs[0] == steps and nbs[1] >= 2 and nbs[2] == 1
    c0 = COL_ATT // gw
    nb1 = nbs[1]
    G = ATT_BLOCKS_PER_STEP
    assert G % nb1 == 0 and steps % G == 0
    ns1 = G // nb1
    assert B * (steps // G) >= ATT_RING
    in_specs = [pl.BlockSpec(memory_space=pltpu.SMEM),
                pl.BlockSpec(memory_space=pl.ANY),
                pl.BlockSpec((None, S, gw), lambda b, i: (b, 0, c0 + 1)),
                pl.BlockSpec((None, S, gw), lambda b, i: (b, 0, c0 + 2)),
                pl.BlockSpec(memory_space=pl.ANY),
                pl.BlockSpec(memory_space=pl.ANY)]
    args = [rel_bias, main3, main3, main3, att1, att2]
    qi = np.arange(blk)[:, None]
    kj = np.arange(2 * blk)[None, :]
    m = blk + qi - kj
    for _, dil in ATT_GROUPS:
        in_specs.append(_const_spec((blk, 2 * blk)))
        args.append(jnp.asarray(_t5_bucket(np.clip(m, 0, blk) * dil)))
    out_specs, out_shapes = [], []
    for (_, dil), nb in zip(ATT_GROUPS, nbs):
        for w, dt in ((gw, BF16), (LSE_W, F32)):
            if nb == steps:
                out_specs.append(pl.BlockSpec((None, G * blk, w), lambda b, i: (b, i, 0)))
                out_shapes.append(jax.ShapeDtypeStruct((B, S, w), dt))
            elif nb == 1:
                out_specs.append(pl.BlockSpec((None, G, blk, w), lambda b, i: (b, i, 0, 0)))
                out_shapes.append(jax.ShapeDtypeStruct((B, dil, S // dil, w), dt))
            else:
                out_specs.append(pl.BlockSpec((None, ns1, lens[1], w), lambda b, i: (b, i, 0, 0)))
                out_shapes.append(jax.ShapeDtypeStruct((B, dil, S // dil, w), dt))
    heads = ATT_HEADS_PER_GROUP
    scratch = [pltpu.VMEM((2, heads, blk, 2 * blk), F32), pltpu.VMEM((2, heads, blk, 2 * blk), F32),
               pltpu.VMEM((heads, blk, blk), F32),
               pltpu.VMEM((G, heads, 2 * blk, 2 * ATT_HEAD_DIM), BF16),
               pltpu.VMEM((G, heads, 2 * blk, 2 * ATT_HEAD_DIM), BF16),
               pltpu.VMEM((G, heads, blk, 2 * ATT_HEAD_DIM), BF16)]
    scratch += [pltpu.VMEM((ATT_RING, G * blk, gw), BF16)]
    scratch += [pltpu.VMEM((ATT_RING, ns1, lens[1], gw), BF16)] * 3
    scratch += [pltpu.VMEM((ATT_RING, G, blk, gw), BF16)] * 3
    scratch += [pltpu.SemaphoreType.DMA((ATT_RING, 7))]
    return _CallParts(functools.partial(_attn_kernel, nb1=nb1, per_step=G), (B, steps // G),
                      in_specs, args, out_specs, out_shapes, scratch)


class _CallParts(NamedTuple):
    body: object
    grid: tuple
    in_specs: list
    args: list
    out_specs: list
    out_shapes: list
    scratch: list


def _call(parts, name):
    return pl.pallas_call(
        parts.body,
        out_shape=parts.out_shapes,
        grid=parts.grid,
        in_specs=parts.in_specs,
        out_specs=parts.out_specs,
        scratch_shapes=parts.scratch,
        compiler_params=_params(("arbitrary",) * len(parts.grid)),
        name=name,
    )(*parts.args)


def _mixers(main3, att1, att2, rel_bias, gn_g, gn_b):
    retg, = _call(_retention_parts(main3, gn_g, gn_b), "ret")
    att = _call(_attention_parts(main3, att1, att2, rel_bias), "attn")
    return retg, att[0::2], att[1::2]


def _merge_kernel(x_ref, g_ref, mod_ref, retg_ref, o0_ref, o1_ref, o2_ref, s0_ref, s1_ref, s2_ref,
                  wg_ref, wr_ref, wa_ref, wo_ref, out_ref, o1_scr, o2_scr, s1_scr, s2_scr):
    tm = x_ref.shape[0]
    for dil, o_ref, s_ref, o_scr, s_scr in ((ATT_GROUPS[1][1], o1_ref, s1_ref, o1_scr, s1_scr),
                                            (ATT_GROUPS[2][1], o2_ref, s2_ref, o2_scr, s2_scr)):
        n = tm // dil
        for r in range(dil):
            s_scr[pl.ds(r, n, stride=dil), :] = s_ref[0, r]
            for h in range(ATT_HEADS_PER_GROUP):
                hs = slice(h * ATT_HEAD_DIM, (h + 1) * ATT_HEAD_DIM)
                o_scr[h, pl.ds(r, n, stride=dil), :] = o_ref[0, r, :, hs].astype(F32)
    part = tm // MERGE_ROW_PARTS
    for p in range(MERGE_ROW_PARTS):
        rows = slice(p * part, (p + 1) * part)
        ret_out = jnp.dot(retg_ref[rows, :], wr_ref[...], preferred_element_type=F32)
        x = x_ref[rows, :]
        h = _modulated_norm(x, g_ref[...], mod_ref, 0).astype(BF16)
        gates = jax.nn.sigmoid(jnp.dot(h, wg_ref[...], preferred_element_type=F32))
        stats = (s0_ref[rows, :], s1_scr[rows, :], s2_scr[rows, :])
        top = jnp.maximum(jnp.maximum(stats[0], stats[1]), stats[2])
        a = [jnp.exp(st - top) for st in stats]
        lane = lax.broadcasted_iota(jnp.int32, top.shape, 1)
        heads = ATT_HEADS_PER_GROUP
        den = sum(ag * pltpu.roll(st, LSE_W - heads, axis=1) for ag, st in zip(a, stats))
        inv = 1.0 / jnp.where(lane < heads, den, 1.0)
        w0, w1, w2 = (ag * inv for ag in a)
        parts = []
        for hd in range(heads):
            hs = slice(hd * ATT_HEAD_DIM, (hd + 1) * ATT_HEAD_DIM)
            parts.append(w0[:, hd:hd + 1] * o0_ref[rows, hs].astype(F32)
                         + w1[:, hd:hd + 1] * o1_scr[hd, rows, :]
                         + w2[:, hd:hd + 1] * o2_scr[hd, rows, :])
        att = jnp.concatenate(parts, axis=-1).astype(BF16)
        att_out = jnp.dot(att, wa_ref[...], preferred_element_type=F32)
        merged = gates[:, :D_MODEL] * ret_out + gates[:, D_MODEL:] * att_out
        y = jnp.dot(merged.astype(BF16), wo_ref[...], preferred_element_type=F32)
        out_ref[rows, :] = x + mod_ref[0, :, 2 * D_MODEL:3 * D_MODEL] * y


def _merge(x2, norm_g, mod3, retg2, o, lse, w_gate, w_ret_out, w_att_out, w_o, batch, seq):
    tm = ROW_TM
    tps = seq // tm
    row = lambda w: pl.BlockSpec((tm, w), lambda i: (i, 0))
    sub = lambda d, w: pl.BlockSpec((1, d, tm // d, w), lambda i: (i // tps, 0, i % tps, 0))
    d1, d2 = ATT_GROUPS[1][1], ATT_GROUPS[2][1]
    return pl.pallas_call(
        _merge_kernel,
        out_shape=jax.ShapeDtypeStruct((batch * seq, D_MODEL), F32),
        grid=(batch * seq // tm,),
        in_specs=[row(D_MODEL),
                  _const_spec((1, D_MODEL)),
                  pl.BlockSpec((1, 1, 6 * D_MODEL), lambda i: (i // tps, 0, 0)),
                  row(RET_V_W),
                  row(ATT_GROUP_W), sub(d1, ATT_GROUP_W), sub(d2, ATT_GROUP_W),
                  row(LSE_W), sub(d1, LSE_W), sub(d2, LSE_W),
                  _const_spec((D_MODEL, 2 * D_MODEL)),
                  _const_spec((RET_V_W, D_MODEL)),
                  _const_spec((ATT_GROUP_W, D_MODEL)),
                  _const_spec((D_MODEL, D_MODEL))],
        out_specs=row(D_MODEL),
        scratch_shapes=[pltpu.VMEM((ATT_HEADS_PER_GROUP, tm, ATT_HEAD_DIM), F32)] * 2
                       + [pltpu.VMEM((tm, LSE_W), F32)] * 2,
        compiler_params=_params(("arbitrary",)),
        name="merge",
    )(x2, norm_g, mod3, retg2, *o, *lse, w_gate, w_ret_out, w_att_out, w_o)


def _mlp_kernel(x_ref, mod_ref, g2_ref, gf_ref, w1_ref, w2_ref, out_ref, *, final):
    part = x_ref.shape[0] // MLP_ROW_PARTS
    for p in range(MLP_ROW_PARTS):
        rows = slice(p * part, (p + 1) * part)
        x = x_ref[rows, :]
        h = _modulated_norm(x, g2_ref[...], mod_ref, 3).astype(BF16)
        u = jnp.maximum(jnp.dot(h, w1_ref[...], preferred_element_type=F32), 0.0)
        y = jnp.dot((u * u).astype(BF16), w2_ref[...], preferred_element_type=F32)
        x = x + mod_ref[0, :, 5 * D_MODEL:6 * D_MODEL] * y
        out_ref[rows, :] = _rms(x, gf_ref[...]) if final else x


def _mlp(x2, mod3, norm2_g, norm_f_g, w1, w2, seq, final):
    rows = x2.shape[0]
    tm = MLP_TM
    tps = seq // tm
    return pl.pallas_call(
        functools.partial(_mlp_kernel, final=final),
        out_shape=jax.ShapeDtypeStruct((rows, D_MODEL), F32),
        grid=(rows // tm,),
        in_specs=[pl.BlockSpec((tm, D_MODEL), lambda i: (i, 0)),
                  pl.BlockSpec((1, 1, 6 * D_MODEL), lambda i: (i // tps, 0, 0)),
                  _const_spec((1, D_MODEL)),
                  _const_spec((1, D_MODEL)),
                  _const_spec((D_MODEL, D_FF)),
                  _const_spec((D_FF, D_MODEL))],
        out_specs=pl.BlockSpec((tm, D_MODEL), lambda i: (i, 0)),
        compiler_params=_params(("arbitrary",)),
        name="mlp",
    )(x2, mod3, norm2_g, norm_f_g, w1, w2)


def kernel(x, c, w_ada, b_ada, norm1_g, w_in, rel_bias, ret_gn_g, ret_gn_b, w_ret_out, w_att_out,
           w_o, norm2_g, w_ff1, w_ff2, norm_f_g):
    B, S, D = x.shape
    depth = w_ada.shape[0]
    half = RET_DK // 2
    ang = np.arange(S, dtype=np.float64)[:, None] * ROPE_BASE ** (-np.arange(half, dtype=np.float64) / half)
    cos, sin = jnp.asarray(np.cos(ang), F32), jnp.asarray(np.sin(ang), F32)
    x2 = x.reshape(B * S, D)
    for l in range(depth):
        g1 = norm1_g[l].reshape(1, D)
        mod3 = _mod(c, w_ada[l], b_ada[l]).reshape(B, 1, 6 * D)
        main2, att1, att2, (w_gate_bf, w_ret_bf, w_o_bf, w_ff1_bf, w_ff2_bf) = _proj(
            x2, g1, mod3, cos, sin, w_in[l], [w_ret_out[l], w_o[l], w_ff1[l], w_ff2[l]], B, S)
        main3 = main2.reshape(B, S, main2.shape[1])
        retg, o, stats = _mixers(main3, att1, att2, rel_bias, ret_gn_g[l], ret_gn_b[l])
        o = [o[0].reshape(B * S, ATT_GROUP_W), o[1], o[2]]
        stats = [stats[0].reshape(B * S, LSE_W), stats[1], stats[2]]
        x2 = _merge(x2, g1, mod3, retg.reshape(B * S, RET_V_W), o, stats,
                    w_gate_bf, w_ret_bf, w_att_out[l].astype(BF16), w_o_bf, B, S)
        x2 = _mlp(x2, mod3, norm2_g[l].reshape(1, D), norm_f_g.reshape(1, D),
                  w_ff1_bf, w_ff2_bf, S, final=l == depth - 1)
    return x2.reshape(B, S, D)
```

```python
import functools
import math
from typing import NamedTuple

import jax
import jax.numpy as jnp
import numpy as np
from jax import lax
from jax.experimental import pallas as pl
from jax.experimental.pallas import tpu as pltpu

F32 = jnp.float32
BF16 = jnp.bfloat16

D_MODEL = 1024
RET_HEADS = 4
RET_DK = 256
RET_DV = 512
RET_C = 256
RET_LOG_GAMMA = tuple(math.log1p(-(2.0 ** (-5.0 - h))) for h in range(RET_HEADS))
RET_QK_W = RET_HEADS * RET_DK
RET_V_W = RET_HEADS * RET_DV
ATT_GROUPS = ((128, 1), (512, 4), (2048, 16))
ATT_HEADS_PER_GROUP = 4
ATT_HEAD_DIM = 128
ATT_GROUP_W = ATT_HEADS_PER_GROUP * ATT_HEAD_DIM
ATT_BLK = 128
ATT_BLOCKS_PER_STEP = 8
LANES = 128
BF16_SUBLANES = 16
LSE_W = LANES
REL_BUCKETS = 32
REL_MAX_DIST = 2048
D_FF = 4 * D_MODEL
RMS_EPS = 1e-6
GN_EPS = 1e-5
ROPE_BASE = 10000.0
NEG = -1e30
ATT_QSCALE = ATT_HEAD_DIM ** -0.5

COL_ATT = 2 * RET_QK_W + 2 * RET_V_W
COL_GATE = COL_ATT + 9 * ATT_GROUP_W

VMEM_LIMIT = 56 * 1024 * 1024

PROJ_TM = 256
PROJ_TN = 3 * ATT_GROUP_W
PROJ_MAIN_TILES = (COL_ATT + PROJ_TN) // PROJ_TN
PROJ_LAST_PIECES = 3
PROJ_W_STAGE_ROWS = 64
ROW_TM = 1024
MERGE_ROW_PARTS = 4
MLP_TM = 1024
MLP_ROW_PARTS = 2


def _params(sem):
    return pltpu.CompilerParams(dimension_semantics=sem, vmem_limit_bytes=VMEM_LIMIT)


def _const_spec(shape):
    zeros = (0,) * len(shape)
    return pl.BlockSpec(shape, lambda *_: zeros, pipeline_mode=pl.Buffered(1))


def _silu(t):
    return t * jax.nn.sigmoid(t)


def _rms(x, g):
    return x * lax.rsqrt(jnp.mean(x * x, axis=-1, keepdims=True) + RMS_EPS) * g


def _modulated_norm(x, g, mod_ref, k):
    shift = mod_ref[0, :, k * D_MODEL:(k + 1) * D_MODEL]
    scale = mod_ref[0, :, (k + 1) * D_MODEL:(k + 2) * D_MODEL]
    return _rms(x, g) * (1.0 + scale) + shift


def _mod_kernel(c_ref, w_ref, b_ref, o_ref):
    o_ref[...] = jnp.dot(_silu(c_ref[...]), w_ref[...], preferred_element_type=F32) + b_ref[...]


def _mod(c, w_ada, b_ada):
    B = c.shape[0]
    n = w_ada.shape[1]
    tn = n // 2
    return pl.pallas_call(
        _mod_kernel,
        out_shape=jax.ShapeDtypeStruct((B, n), F32),
        grid=(n // tn,),
        in_specs=[pl.BlockSpec((B, D_MODEL), lambda j: (0, 0)),
                  pl.BlockSpec((D_MODEL, tn), lambda j: (0, j)),
                  pl.BlockSpec((1, tn), lambda j: (0, j))],
        out_specs=pl.BlockSpec((B, tn), lambda j: (0, j)),
        compiler_params=_params(("arbitrary",)),
        name="mod",
    )(c, w_ada, b_ada.reshape(1, n))


def _proj_main_layout():
    spans = [(h * RET_DK, (h + 1) * RET_DK, "rot", (h % RET_HEADS, h >= RET_HEADS))
             for h in range(2 * RET_HEADS)]
    spans += [(2 * RET_QK_W, 2 * RET_QK_W + RET_V_W, "copy", None),
              (2 * RET_QK_W + RET_V_W, COL_ATT, "silu", None),
              (COL_ATT, COL_ATT + ATT_GROUP_W, "scale", ATT_QSCALE),
              (COL_ATT + ATT_GROUP_W, PROJ_MAIN_TILES * PROJ_TN, "copy", None)]
    tiles = [[] for _ in range(PROJ_MAIN_TILES)]
    for c0, c1, kind, arg in spans:
        for t in range(c0 // PROJ_TN, (c1 - 1) // PROJ_TN + 1):
            lo, hi = max(c0, t * PROJ_TN), min(c1, (t + 1) * PROJ_TN)
            assert kind != "rot" or (lo, hi) == (c0, c1)
            tiles[t].append((lo - t * PROJ_TN, hi - t * PROJ_TN, kind, arg))
    return tiles


PROJ_MAIN_LAYOUT = _proj_main_layout()


def _proj_kernel(x_ref, mod_ref, g_ref, cos_ref, sin_ref, w_hbm, *refs, tps, n_cast):
    cast_in, (main_ref, a1_ref, a2_ref) = refs[:n_cast], refs[n_cast:n_cast + 3]
    cast_out = refs[n_cast + 3:2 * n_cast + 3]
    hf_scr, hd1_scr = refs[2 * n_cast + 3:2 * n_cast + 5]
    lhs = refs[2 * n_cast + 5:2 * n_cast + 8]
    w_ref, stage, sems = refs[2 * n_cast + 8:2 * n_cast + 11]
    tm = x_ref.shape[0]
    step = pl.program_id(0)

    @pl.when(step == 0)
    def _():
        rows = stage.shape[1]
        chunks = w_ref.shape[0] // rows

        def fetch(c):
            return pltpu.make_async_copy(w_hbm.at[pl.ds(c * rows, rows), pl.ds(0, w_ref.shape[1])],
                                         stage.at[c % 2], sems.at[c % 2])

        fetch(0).start()
        for c in range(chunks):
            if c + 1 < chunks:
                fetch(c + 1).start()
            fetch(c).wait()
            w_ref[c * rows:(c + 1) * rows, :] = stage[c % 2].astype(BF16)

    for src, dst in zip(cast_in, cast_out):
        dst[...] = src[...].astype(BF16)

    def prepare(x_ref, mod_ref, lhs):
        h0_scr, h1_scr, h2_scr = lhs
        hf = _modulated_norm(x_ref[...], g_ref[...], mod_ref, 0)
        h0_scr[...] = hf.astype(BF16)
        d1, d2 = ATT_GROUPS[1][1], ATT_GROUPS[2][1]
        n1, n2 = tm // d1, tm // d2
        for c in range(D_MODEL // LANES):
            cs = slice(c * LANES, (c + 1) * LANES)
            hf_scr[c] = hf[:, cs]
            for r in range(d1):
                part = hf_scr[c, pl.ds(r, n1, stride=d1), :]
                hd1_scr[c, r * n1:(r + 1) * n1, :] = part
                h1_scr[r * n1:(r + 1) * n1, cs] = part.astype(BF16)
            for r in range(d2):
                src = pl.ds((r % d1) * n1 + r // d1, n2, stride=d1)
                h2_scr[r * n2:(r + 1) * n2, cs] = hd1_scr[c, src, :].astype(BF16)

    def w_tile(t):
        return w_ref[:, t * PROJ_TN:(t + 1) * PROJ_TN]

    def main_tile(t, segments, h0_scr, pieces=1):
        if pieces > 1:
            width = PROJ_TN // pieces
            for p in range(pieces):
                p0, p1 = p * width, (p + 1) * width
                acc = jnp.dot(h0_scr[...], w_ref[:, t * PROJ_TN + p0:t * PROJ_TN + p1],
                              preferred_element_type=F32)
                for c0, c1, kind, arg in segments:
                    lo, hi = max(c0, p0), min(c1, p1)
                    if lo < hi:
                        assert kind in ("scale", "copy")
                        seg = acc[:, lo - p0:hi - p0]
                        seg = seg * arg if kind == "scale" else seg
                        main_ref[:, t * PROJ_TN + lo:t * PROJ_TN + hi] = seg.astype(BF16)
            return
        acc = jnp.dot(h0_scr[...], w_tile(t), preferred_element_type=F32)
        if any(kind == "rot" for _, _, kind, _ in segments):
            cos, sin = cos_ref[...], sin_ref[...]
            row = lax.broadcasted_iota(jnp.int32, cos.shape, 0)
            pos = (((step % tps) * tm + row) % RET_C + 1).astype(F32)
        base = t * PROJ_TN
        for c0, c1, kind, arg in segments:
            seg = acc[:, c0:c1]
            if kind == "rot":
                head, is_key = arg
                rate = -RET_LOG_GAMMA[head] if is_key else RET_LOG_GAMMA[head]
                dec = jnp.exp(rate * pos) * ((RET_DK ** -0.5) if is_key else 1.0)
                cd, sd = cos * dec, sin * dec
                cm = (c0 + c1) // 2
                t1, t2 = acc[:, c0:cm], acc[:, cm:c1]
                main_ref[:, base + c0:base + cm] = (t1 * cd - t2 * sd).astype(BF16)
                main_ref[:, base + cm:base + c1] = (t1 * sd + t2 * cd).astype(BF16)
            elif kind == "silu":
                main_ref[:, base + c0:base + c1] = _silu(seg).astype(BF16)
            elif kind == "scale":
                main_ref[:, base + c0:base + c1] = (seg * arg).astype(BF16)
            else:
                main_ref[:, base + c0:base + c1] = seg.astype(BF16)

    def multiply(lhs):
        h0_scr, h1_scr, h2_scr = lhs
        for t, (dil, h_scr, a_ref) in enumerate(((ATT_GROUPS[1][1], h1_scr, a1_ref),
                                                 (ATT_GROUPS[2][1], h2_scr, a2_ref))):
            n = tm // dil
            acc = jnp.dot(h_scr[...], w_tile(PROJ_MAIN_TILES + t), preferred_element_type=F32)
            gw = ATT_GROUP_W
            for r in range(dil):
                rows = slice(r * n, (r + 1) * n)
                a_ref[0, r, :, :gw] = (acc[rows, :gw] * ATT_QSCALE).astype(BF16)
                a_ref[0, r, :, gw:] = acc[rows, gw:].astype(BF16)
        last = len(PROJ_MAIN_LAYOUT) - 1
        for t, segments in enumerate(PROJ_MAIN_LAYOUT):
            main_tile(t, segments, h0_scr, pieces=PROJ_LAST_PIECES if t == last else 1)

    prepare(x_ref, mod_ref, lhs)
    multiply(lhs)


def _proj(x2, norm_g, mod3, cos, sin, w_in, cast_weights, batch, seq):
    tm, tn = PROJ_TM, PROJ_TN
    tps = seq // tm
    steps = batch * seq // tm
    assert seq % tm == 0 and tm % RET_C == 0
    d1, d2 = ATT_GROUPS[1][1], ATT_GROUPS[2][1]
    assert COL_GATE == (PROJ_MAIN_TILES + 2) * tn
    sub = lambda d: pl.BlockSpec((1, d, tm // d, tn), lambda i: (i // tps, 0, i % tps, 0))
    gate_w = w_in.shape[1] - COL_GATE
    gate_rows = D_MODEL // steps
    assert D_MODEL % (steps * BF16_SUBLANES) == 0
    slab_in = [pl.BlockSpec((pl.Element(gate_rows), pl.Element(gate_w)), lambda i: (i * gate_rows, COL_GATE))]
    slab_out = [pl.BlockSpec((gate_rows, gate_w), lambda i: (i, 0))]
    slab_shapes = [jax.ShapeDtypeStruct((D_MODEL, gate_w), BF16)]
    for w in cast_weights:
        assert w.shape[0] % (steps * BF16_SUBLANES) == 0
        slab_in.append(pl.BlockSpec((w.shape[0] // steps, w.shape[1]), lambda i: (i, 0)))
        slab_out.append(slab_in[-1])
        slab_shapes.append(jax.ShapeDtypeStruct(w.shape, BF16))
    outs = pl.pallas_call(
        functools.partial(_proj_kernel, tps=tps, n_cast=len(slab_in)),
        out_shape=[jax.ShapeDtypeStruct((batch * seq, PROJ_MAIN_TILES * tn), BF16),
                   jax.ShapeDtypeStruct((batch, d1, seq // d1, tn), BF16),
                   jax.ShapeDtypeStruct((batch, d2, seq // d2, tn), BF16)] + slab_shapes,
        grid=(steps,),
        in_specs=[pl.BlockSpec((tm, D_MODEL), lambda i: (i, 0)),
                  pl.BlockSpec((1, 1, 6 * D_MODEL), lambda i: (i // tps, 0, 0)),
                  _const_spec((1, D_MODEL)),
                  pl.BlockSpec((tm, RET_DK // 2), lambda i: (i % tps, 0)),
                  pl.BlockSpec((tm, RET_DK // 2), lambda i: (i % tps, 0)),
                  pl.BlockSpec(memory_space=pl.ANY)] + slab_in,
        out_specs=[pl.BlockSpec((tm, PROJ_MAIN_TILES * tn), lambda i: (i, 0)), sub(d1), sub(d2)] + slab_out,
        scratch_shapes=[pltpu.VMEM((D_MODEL // LANES, tm, LANES), F32)] * 2
                       + [pltpu.VMEM((tm, D_MODEL), BF16)] * 3
                       + [pltpu.VMEM((D_MODEL, COL_GATE), BF16),
                          pltpu.VMEM((2, PROJ_W_STAGE_ROWS, COL_GATE), F32),
                          pltpu.SemaphoreType.DMA((2,))],
        compiler_params=_params(("arbitrary",)),
        name="proj",
    )(x2, mod3, norm_g, cos, sin, w_in, w_in, *cast_weights)
    return outs[0], outs[1], outs[2], outs[3:]


RET_RING = 3


def _ret_kernel(main_hbm, cdec_ref, gng_ref, gnb_ref, o_ref, state, st_in, q_buf, k_buf, v_buf, g_buf, sems):
    n_heads = pl.num_programs(1)
    step = pl.program_id(0) * n_heads + pl.program_id(1)
    last = pl.num_programs(0) * n_heads - 1
    bufs = (q_buf, k_buf, v_buf, g_buf)
    col0 = (0, RET_QK_W, 2 * RET_QK_W, 2 * RET_QK_W + RET_V_W)

    def copies(s, slot):
        b, h = s // n_heads, s % n_heads
        out = []
        for j, buf in enumerate(bufs):
            width = buf.shape[2]
            cols = pl.ds(pl.multiple_of(col0[j] + h * width, LANES), width)
            out.append(pltpu.make_async_copy(main_hbm.at[b, :, cols], buf.at[slot], sems.at[slot, j]))
        return out

    @pl.when(step == 0)
    def _():
        for s in range(RET_RING - 1):
            for cp in copies(s, s):
                cp.start()

    for slot in range(RET_RING):
        @pl.when(step % RET_RING == slot)
        def _(slot=slot):
            ahead = step + (RET_RING - 1)

            @pl.when(ahead <= last)
            def _():
                for cp in copies(ahead, (slot + RET_RING - 1) % RET_RING):
                    cp.start()

            for cp in copies(step, slot):
                cp.wait()
            _ret_body(q_buf.at[slot], k_buf.at[slot], v_buf.at[slot], g_buf.at[slot],
                      cdec_ref, gng_ref, gnb_ref, o_ref, state, st_in)


def _ret_body(q_ref, k_ref, v_ref, gate_ref, cdec_ref, gng_ref, gnb_ref, o_ref, state, st_in):
    C = RET_C
    nc = q_ref.shape[0] // C
    cdec = cdec_ref[0]
    gng = gng_ref[0]
    gnb = gnb_ref[0]
    chunk = lambda ci: pl.ds(pl.multiple_of(ci * C, C), C)

    state[...] = jnp.zeros_like(state)
    st_in[0] = jnp.zeros(st_in.shape[1:], BF16)

    def advance(ci, carry):
        rows = chunk(ci)
        kv = lax.dot_general(k_ref[rows, :], v_ref[rows, :], (((0,), (0,)), ((), ())),
                             preferred_element_type=F32)
        new = (state[...] + kv) * cdec
        state[...] = new
        st_in[ci + 1] = new.astype(BF16)
        return carry

    lax.fori_loop(0, nc - 1, advance, 0, unroll=True)

    qi = lax.broadcasted_iota(jnp.int32, (C, C), 0)
    kj = lax.broadcasted_iota(jnp.int32, (C, C), 1)
    causal = qi >= kj

    def emit(ci, carry):
        rows = chunk(ci)
        q = q_ref[rows, :]
        s = lax.dot_general(q, k_ref[rows, :], (((1,), (1,)), ((), ())), preferred_element_type=F32)
        s = jnp.where(causal, s, 0.0).astype(BF16)
        o = (jnp.dot(s, v_ref[rows, :], preferred_element_type=F32)
             + jnp.dot(q, st_in[ci], preferred_element_type=F32))
        mu = jnp.mean(o, axis=-1, keepdims=True)
        oc = o - mu
        var = jnp.mean(oc * oc, axis=-1, keepdims=True)
        on = oc * lax.rsqrt(var + GN_EPS) * gng + gnb
        o_ref[0, rows, :] = (gate_ref[rows, :].astype(F32) * on).astype(BF16)
        return carry

    lax.fori_loop(0, nc, emit, 0, unroll=True)


def _retention_parts(main3, gn_g, gn_b):
    B, S, _ = main3.shape
    H, C = RET_HEADS, RET_C
    assert S % C == 0
    assert B * H >= RET_RING
    cdec = jnp.exp(jnp.asarray(RET_LOG_GAMMA, F32) * C)[:, None, None]
    in_specs = [pl.BlockSpec(memory_space=pl.ANY),
                pl.BlockSpec((1, 1, 1), lambda b, h: (h, 0, 0)),
                pl.BlockSpec((1, 1, RET_DV), lambda b, h: (h, 0, 0)),
                pl.BlockSpec((1, 1, RET_DV), lambda b, h: (h, 0, 0))]
    args = [main3, cdec, gn_g.reshape(H, 1, RET_DV), gn_b.reshape(H, 1, RET_DV)]
    ring = lambda w: pltpu.VMEM((RET_RING, S, w), BF16)
    return _CallParts(_ret_kernel, (B, H), in_specs, args,
                      [pl.BlockSpec((1, S, RET_DV), lambda b, h: (b, 0, h))],
                      [jax.ShapeDtypeStruct((B, S, RET_V_W), BF16)],
                      [pltpu.VMEM((RET_DK, RET_DV), F32), pltpu.VMEM((S // C, RET_DK, RET_DV), BF16),
                       ring(RET_DK), ring(RET_DK), ring(RET_DV), ring(RET_DV),
                       pltpu.SemaphoreType.DMA((RET_RING, 4))])


def _t5_bucket(dist):
    dist = np.asarray(dist)
    max_exact = REL_BUCKETS // 2
    d_f = np.maximum(dist, 1).astype(np.float32)
    large = max_exact + (np.log(d_f / np.float32(max_exact)) / np.float32(math.log(REL_MAX_DIST / max_exact))
                         * np.float32(REL_BUCKETS - max_exact)).astype(np.int32)
    large = np.minimum(large, REL_BUCKETS - 1)
    return np.where(dist < max_exact, dist, large).astype(np.int32)


def _attn_unit(n, q_ref, k_ref, v_ref, bias_scr, vaug_scr, o_ref, stats_ref):
    blk, dh = ATT_BLK, ATT_HEAD_DIM
    windowed = k_ref.shape[0] > blk
    if windowed and isinstance(n, int):
        win, first = pl.ds(max(n - 1, 0) * blk, 2 * blk), int(n == 0)
    elif windowed:
        win = pl.ds(pl.multiple_of(jnp.maximum(n - 1, 0) * blk, blk), 2 * blk)
        first = (n == 0).astype(jnp.int32)
    heads = ATT_HEADS_PER_GROUP
    lane = lax.broadcasted_iota(jnp.int32, (blk, LSE_W), 1)
    stats = jnp.ones((blk, LSE_W), F32)
    for h in range(heads):
        hs = slice(h * dh, (h + 1) * dh)
        if windowed:
            kb, vb, bias = k_ref[win, hs], v_ref[win, hs], bias_scr[first, h]
        else:
            kb, vb, bias = k_ref[:, hs], v_ref[:, hs], bias_scr[h]
        vaug_scr[h, :, :dh] = vb
        s = lax.dot_general(q_ref[:, hs], kb, (((1,), (1,)), ((), ())), preferred_element_type=F32) + bias
        mx = jnp.max(s, axis=-1, keepdims=True)
        e = jnp.exp(s - mx).astype(BF16)
        o_den = jnp.dot(e, vaug_scr[h], preferred_element_type=F32)
        o_ref[:, hs] = o_den[:, :dh].astype(BF16)
        stats = jnp.where(lane == h, mx, jnp.where(lane == heads + h, o_den[:, dh:], stats))
    stats_ref[...] = stats


def _attn_init(tab_ref, bk0, bk1, bk2, bias0, bias1, bias2, vaug0, vaug1, vaug2):
    blk = ATT_BLK

    @pl.when((pl.program_id(0) == 0) & (pl.program_id(1) == 0))
    def _():
        for vaug in (vaug0, vaug1, vaug2):
            vaug[...] = jnp.ones(vaug.shape, BF16)
        qi = lax.broadcasted_iota(jnp.int32, (blk, 2 * blk), 0)
        kj = lax.broadcasted_iota(jnp.int32, (blk, 2 * blk), 1)
        m = blk + qi - kj
        band = (m >= 0) & (m <= blk)
        masked = jnp.full((blk, blk), NEG, F32)
        for gi, (bk_ref, bias_scr) in enumerate(((bk0, bias0), (bk1, bias1), (bk2, bias2))):
            bucket = bk_ref[...]
            for h in range(ATT_HEADS_PER_GROUP):
                col = gi * ATT_HEADS_PER_GROUP + h
                acc = jnp.zeros((blk, 2 * blk), F32)
                for t in range(REL_BUCKETS):
                    acc = jnp.where(bucket == t, tab_ref[t, col], acc)
                tile = jnp.where(band, acc, NEG)
                if len(bias_scr.shape) == 4:
                    bias_scr[0, h] = tile
                    bias_scr[1, h] = jnp.concatenate([tile[:, blk:], masked], axis=1)
                else:
                    bias_scr[h] = tile[:, blk:]


ATT_RING = 3


def _attn_kernel(tab_ref, main_hbm, k0, v0, att1_hbm, att2_hbm, bk0, bk1, bk2,
                 o0, l0, o1, l1, o2, l2, bias0, bias1, bias2, vaug0, vaug1, vaug2,
                 q0_buf, q1_buf, k1_buf, v1_buf, q2_buf, k2_buf, v2_buf, sems, *, nb1, per_step):
    n_steps = pl.num_programs(1)
    step = pl.program_id(0) * n_steps + pl.program_id(1)
    last = pl.num_programs(0) * n_steps - 1
    gw = ATT_GROUP_W
    ns1 = per_step // nb1

    def copies(s, slot):
        b, i = s // n_steps, s % n_steps
        rows0 = pl.ds(pl.multiple_of(i * (per_step * ATT_BLK), ATT_BLK), per_step * ATT_BLK)
        srcs = [main_hbm.at[b, rows0, pl.ds(COL_ATT, gw)]]
        srcs += [att1_hbm.at[b, pl.ds(i * ns1, ns1), :, pl.ds(part * gw, gw)] for part in range(3)]
        srcs += [att2_hbm.at[b, pl.ds(i * per_step, per_step), :, pl.ds(part * gw, gw)] for part in range(3)]
        bufs = (q0_buf, q1_buf, k1_buf, v1_buf, q2_buf, k2_buf, v2_buf)
        return [pltpu.make_async_copy(src, buf.at[slot], sems.at[slot, j])
                for j, (src, buf) in enumerate(zip(srcs, bufs))]

    slot = step % ATT_RING

    @pl.when(step == 0)
    def _():
        for s in range(ATT_RING - 1):
            for cp in copies(s, s):
                cp.start()

    ahead = step + (ATT_RING - 1)

    @pl.when(ahead <= last)
    def _():
        for cp in copies(ahead, ahead % ATT_RING):
            cp.start()

    _attn_init(tab_ref, bk0, bk1, bk2, bias0, bias1, bias2, vaug0, vaug1, vaug2)
    for cp in copies(step, slot):
        cp.wait()
    for k in range(ATT_RING):
        @pl.when(slot == k)
        def _(k=k):
            _attn_blocks(pl.program_id(1), q0_buf.at[k], k0, v0, q1_buf.at[k], k1_buf.at[k], v1_buf.at[k],
                         q2_buf.at[k], k2_buf.at[k], v2_buf.at[k], o0, l0, o1, l1, o2, l2,
                         bias0, bias1, bias2, vaug0, vaug1, vaug2, nb1=nb1, per_step=per_step)


def _attn_blocks(i, q0, k0, v0, q1, k1, v1, q2, k2, v2, o0, l0, o1, l1, o2, l2,
                 bias0, bias1, bias2, vaug0, vaug1, vaug2, *, nb1, per_step):
    blk = ATT_BLK
    for u in range(per_step):
        rows = pl.ds(u * blk, blk)
        _attn_unit(i * per_step + u, q0.at[rows], k0, v0, bias0, vaug0.at[u], o0.at[rows], l0.at[rows])
        sub, n1 = divmod(u, nb1)
        rows1 = pl.ds(n1 * blk, blk)
        _attn_unit(n1, q1.at[sub, rows1], k1.at[sub], v1.at[sub], bias1, vaug1.at[u],
                   o1.at[sub, rows1], l1.at[sub, rows1])
        _attn_unit(0, q2.at[u], k2.at[u], v2.at[u], bias2, vaug2.at[u], o2.at[u], l2.at[u])


def _attention_parts(main3, att1, att2, rel_bias):
    B, S, _ = main3.shape
    blk, gw = ATT_BLK, ATT_GROUP_W
    steps = S // blk
    lens = [S // dil for _, dil in ATT_GROUPS]
    nbs = [L // blk for L in lens]
    assert all(win // dil == blk for win, dil in ATT_GROUPS)
    assert nbs[0] == steps and nbs[1] >= 2 and nbs[2] == 1
    c0 = COL_ATT // gw
    nb1 = nbs[1]
    G = ATT_BLOCKS_PER_STEP
    assert G % nb1 == 0 and steps % G == 0
    ns1 = G // nb1
    assert B * (steps // G) >= ATT_RING
    in_specs = [pl.BlockSpec(memory_space=pltpu.SMEM),
                pl.BlockSpec(memory_space=pl.ANY),
                pl.BlockSpec((None, S, gw), lambda b, i: (b, 0, c0 + 1)),
                pl.BlockSpec((None, S, gw), lambda b, i: (b, 0, c0 + 2)),
                pl.BlockSpec(memory_space=pl.ANY),
                pl.BlockSpec(memory_space=pl.ANY)]
    args = [rel_bias, main3, main3, main3, att1, att2]
    qi = np.arange(blk)[:, None]
    kj = np.arange(2 * blk)[None, :]
    m = blk + qi - kj
    for _, dil in ATT_GROUPS:
        in_specs.append(_const_spec((blk, 2 * blk)))
        args.append(jnp.asarray(_t5_bucket(np.clip(m, 0, blk) * dil)))
    out_specs, out_shapes = [], []
    for (_, dil), nb in zip(ATT_GROUPS, nbs):
        for w, dt in ((gw, BF16), (LSE_W, F32)):
            if nb == steps:
                out_specs.append(pl.BlockSpec((None, G * blk, w), lambda b, i: (b, i, 0)))
                out_shapes.append(jax.ShapeDtypeStruct((B, S, w), dt))
            elif nb == 1:
                out_specs.append(pl.BlockSpec((None, G, blk, w), lambda b, i: (b, i, 0, 0)))
                out_shapes.append(jax.ShapeDtypeStruct((B, dil, S // dil, w), dt))
            else:
                out_specs.append(pl.BlockSpec((None, ns1, lens[1], w), lambda b, i: (b, i, 0, 0)))
                out_shapes.append(jax.ShapeDtypeStruct((B, dil, S // dil, w), dt))
    heads = ATT_HEADS_PER_GROUP
    scratch = [pltpu.VMEM((2, heads, blk, 2 * blk), F32), pltpu.VMEM((2, heads, blk, 2 * blk), F32),
               pltpu.VMEM((heads, blk, blk), F32),
               pltpu.VMEM((G, heads, 2 * blk, 2 * ATT_HEAD_DIM), BF16),
               pltpu.VMEM((G, heads, 2 * blk, 2 * ATT_HEAD_DIM), BF16),
               pltpu.VMEM((G, heads, blk, 2 * ATT_HEAD_DIM), BF16)]
    scratch += [pltpu.VMEM((ATT_RING, G * blk, gw), BF16)]
    scratch += [pltpu.VMEM((ATT_RING, ns1, lens[1], gw), BF16)] * 3
    scratch += [pltpu.VMEM((ATT_RING, G, blk, gw), BF16)] * 3
    scratch += [pltpu.SemaphoreType.DMA((ATT_RING, 7))]
    return _CallParts(functools.partial(_attn_kernel, nb1=nb1, per_step=G), (B, steps // G),
                      in_specs, args, out_specs, out_shapes, scratch)


class _CallParts(NamedTuple):
    body: object
    grid: tuple
    in_specs: list
    args: list
    out_specs: list
    out_shapes: list
    scratch: list


def _call(parts, name):
    return pl.pallas_call(
        parts.body,
        out_shape=parts.out_shapes,
        grid=parts.grid,
        in_specs=parts.in_specs,
        out_specs=parts.out_specs,
        scratch_shapes=parts.scratch,
        compiler_params=_params(("arbitrary",) * len(parts.grid)),
        name=name,
    )(*parts.args)


def _mixers(main3, att1, att2, rel_bias, gn_g, gn_b):
    retg, = _call(_retention_parts(main3, gn_g, gn_b), "ret")
    att = _call(_attention_parts(main3, att1, att2, rel_bias), "attn")
    return retg, att[0::2], att[1::2]


def _merge_kernel(x_ref, g_ref, mod_ref, retg_ref, o0_ref, o1_ref, o2_ref, s0_ref, s1_ref, s2_ref,
                  wg_ref, wr_ref, wa_ref, wo_ref, out_ref, o1_scr, o2_scr, s1_scr, s2_scr):
    tm = x_ref.shape[0]
    for dil, o_ref, s_ref, o_scr, s_scr in ((ATT_GROUPS[1][1], o1_ref, s1_ref, o1_scr, s1_scr),
                                            (ATT_GROUPS[2][1], o2_ref, s2_ref, o2_scr, s2_scr)):
        n = tm // dil
        for r in range(dil):
            s_scr[pl.ds(r, n, stride=dil), :] = s_ref[0, r]
            for h in range(ATT_HEADS_PER_GROUP):
                hs = slice(h * ATT_HEAD_DIM, (h + 1) * ATT_HEAD_DIM)
                o_scr[h, pl.ds(r, n, stride=dil), :] = o_ref[0, r, :, hs].astype(F32)
    part = tm // MERGE_ROW_PARTS
    for p in range(MERGE_ROW_PARTS):
        rows = slice(p * part, (p + 1) * part)
        ret_out = jnp.dot(retg_ref[rows, :], wr_ref[...], preferred_element_type=F32)
        x = x_ref[rows, :]
        h = _modulated_norm(x, g_ref[...], mod_ref, 0).astype(BF16)
        gates = jax.nn.sigmoid(jnp.dot(h, wg_ref[...], preferred_element_type=F32))
        stats = (s0_ref[rows, :], s1_scr[rows, :], s2_scr[rows, :])
        top = jnp.maximum(jnp.maximum(stats[0], stats[1]), stats[2])
        a = [jnp.exp(st - top) for st in stats]
        lane = lax.broadcasted_iota(jnp.int32, top.shape, 1)
        heads = ATT_HEADS_PER_GROUP
        den = sum(ag * pltpu.roll(st, LSE_W - heads, axis=1) for ag, st in zip(a, stats))
        inv = 1.0 / jnp.where(lane < heads, den, 1.0)
        w0, w1, w2 = (ag * inv for ag in a)
        parts = []
        for hd in range(heads):
            hs = slice(hd * ATT_HEAD_DIM, (hd + 1) * ATT_HEAD_DIM)
            parts.append(w0[:, hd:hd + 1] * o0_ref[rows, hs].astype(F32)
                         + w1[:, hd:hd + 1] * o1_scr[hd, rows, :]
                         + w2[:, hd:hd + 1] * o2_scr[hd, rows, :])
        att = jnp.concatenate(parts, axis=-1).astype(BF16)
        att_out = jnp.dot(att, wa_ref[...], preferred_element_type=F32)
        merged = gates[:, :D_MODEL] * ret_out + gates[:, D_MODEL:] * att_out
        y = jnp.dot(merged.astype(BF16), wo_ref[...], preferred_element_type=F32)
        out_ref[rows, :] = x + mod_ref[0, :, 2 * D_MODEL:3 * D_MODEL] * y


def _merge(x2, norm_g, mod3, retg2, o, lse, w_gate, w_ret_out, w_att_out, w_o, batch, seq):
    tm = ROW_TM
    tps = seq // tm
    row = lambda w: pl.BlockSpec((tm, w), lambda i: (i, 0))
    sub = lambda d, w: pl.BlockSpec((1, d, tm // d, w), lambda i: (i // tps, 0, i % tps, 0))
    d1, d2 = ATT_GROUPS[1][1], ATT_GROUPS[2][1]
    return pl.pallas_call(
        _merge_kernel,
        out_shape=jax.ShapeDtypeStruct((batch * seq, D_MODEL), F32),
        grid=(batch * seq // tm,),
        in_specs=[row(D_MODEL),
                  _const_spec((1, D_MODEL)),
                  pl.BlockSpec((1, 1, 6 * D_MODEL), lambda i: (i // tps, 0, 0)),
                  row(RET_V_W),
                  row(ATT_GROUP_W), sub(d1, ATT_GROUP_W), sub(d2, ATT_GROUP_W),
                  row(LSE_W), sub(d1, LSE_W), sub(d2, LSE_W),
                  _const_spec((D_MODEL, 2 * D_MODEL)),
                  _const_spec((RET_V_W, D_MODEL)),
                  _const_spec((ATT_GROUP_W, D_MODEL)),
                  _const_spec((D_MODEL, D_MODEL))],
        out_specs=row(D_MODEL),
        scratch_shapes=[pltpu.VMEM((ATT_HEADS_PER_GROUP, tm, ATT_HEAD_DIM), F32)] * 2
                       + [pltpu.VMEM((tm, LSE_W), F32)] * 2,
        compiler_params=_params(("arbitrary",)),
        name="merge",
    )(x2, norm_g, mod3, retg2, *o, *lse, w_gate, w_ret_out, w_att_out, w_o)


def _mlp_kernel(x_ref, mod_ref, g2_ref, gf_ref, w1_ref, w2_ref, out_ref, *, final):
    part = x_ref.shape[0] // MLP_ROW_PARTS
    for p in range(MLP_ROW_PARTS):
        rows = slice(p * part, (p + 1) * part)
        x = x_ref[rows, :]
        h = _modulated_norm(x, g2_ref[...], mod_ref, 3).astype(BF16)
        u = jnp.maximum(jnp.dot(h, w1_ref[...], preferred_element_type=F32), 0.0)
        y = jnp.dot((u * u).astype(BF16), w2_ref[...], preferred_element_type=F32)
        x = x + mod_ref[0, :, 5 * D_MODEL:6 * D_MODEL] * y
        out_ref[rows, :] = _rms(x, gf_ref[...]) if final else x


def _mlp(x2, mod3, norm2_g, norm_f_g, w1, w2, seq, final):
    rows = x2.shape[0]
    tm = MLP_TM
    tps = seq // tm
    return pl.pallas_call(
        functools.partial(_mlp_kernel, final=final),
        out_shape=jax.ShapeDtypeStruct((rows, D_MODEL), F32),
        grid=(rows // tm,),
        in_specs=[pl.BlockSpec((tm, D_MODEL), lambda i: (i, 0)),
                  pl.BlockSpec((1, 1, 6 * D_MODEL), lambda i: (i // tps, 0, 0)),
                  _const_spec((1, D_MODEL)),
                  _const_spec((1, D_MODEL)),
                  _const_spec((D_MODEL, D_FF)),
                  _const_spec((D_FF, D_MODEL))],
        out_specs=pl.BlockSpec((tm, D_MODEL), lambda i: (i, 0)),
        compiler_params=_params(("arbitrary",)),
        name="mlp",
    )(x2, mod3, norm2_g, norm_f_g, w1, w2)


def kernel(x, c, w_ada, b_ada, norm1_g, w_in, rel_bias, ret_gn_g, ret_gn_b, w_ret_out, w_att_out,
           w_o, norm2_g, w_ff1, w_ff2, norm_f_g):
    B, S, D = x.shape
    depth = w_ada.shape[0]
    half = RET_DK // 2
    ang = np.arange(S, dtype=np.float64)[:, None] * ROPE_BASE ** (-np.arange(half, dtype=np.float64) / half)
    cos, sin = jnp.asarray(np.cos(ang), F32), jnp.asarray(np.sin(ang), F32)
    x2 = x.reshape(B * S, D)
    for l in range(depth):
        g1 = norm1_g[l].reshape(1, D)
        mod3 = _mod(c, w_ada[l], b_ada[l]).reshape(B, 1, 6 * D)
        main2, att1, att2, (w_gate_bf, w_ret_bf, w_o_bf, w_ff1_bf, w_ff2_bf) = _proj(
            x2, g1, mod3, cos, sin, w_in[l], [w_ret_out[l], w_o[l], w_ff1[l], w_ff2[l]], B, S)
        main3 = main2.reshape(B, S, main2.shape[1])
        retg, o, stats = _mixers(main3, att1, att2, rel_bias, ret_gn_g[l], ret_gn_b[l])
        o = [o[0].reshape(B * S, ATT_GROUP_W), o[1], o[2]]
        stats = [stats[0].reshape(B * S, LSE_W), stats[1], stats[2]]
        x2 = _merge(x2, g1, mod3, retg.reshape(B * S, RET_V_W), o, stats,
                    w_gate_bf, w_ret_bf, w_att_out[l].astype(BF16), w_o_bf, B, S)
        x2 = _mlp(x2, mod3, norm2_g[l].reshape(1, D), norm_f_g.reshape(1, D),
                  w_ff1_bf, w_ff2_bf, S, final=l == depth - 1)
    return x2.reshape(B, S, D)
```

```python
import functools
import math
from typing import NamedTuple

import jax
import jax.numpy as jnp
import numpy as np
from jax import lax
from jax.experimental import pallas as pl
from jax.experimental.pallas import tpu as pltpu

F32 = jnp.float32
BF16 = jnp.bfloat16

D_MODEL = 1024
RET_HEADS = 4
RET_DK = 256
RET_DV = 512
RET_C = 256
RET_LOG_GAMMA = tuple(math.log1p(-(2.0 ** (-5.0 - h))) for h in range(RET_HEADS))
RET_QK_W = RET_HEADS * RET_DK
RET_V_W = RET_HEADS * RET_DV
ATT_GROUPS = ((128, 1), (512, 4), (2048, 16))
ATT_HEADS_PER_GROUP = 4
ATT_HEAD_DIM = 128
ATT_GROUP_W = ATT_HEADS_PER_GROUP * ATT_HEAD_DIM
ATT_BLK = 128
ATT_BLOCKS_PER_STEP = 8
LANES = 128
BF16_SUBLANES = 16
LSE_W = LANES
REL_BUCKETS = 32
REL_MAX_DIST = 2048
D_FF = 4 * D_MODEL
RMS_EPS = 1e-6
GN_EPS = 1e-5
ROPE_BASE = 10000.0
NEG = -1e30
ATT_QSCALE = ATT_HEAD_DIM ** -0.5

COL_ATT = 2 * RET_QK_W + 2 * RET_V_W
COL_GATE = COL_ATT + 9 * ATT_GROUP_W

VMEM_LIMIT = 56 * 1024 * 1024

PROJ_TM = 256
PROJ_TN = 3 * ATT_GROUP_W
PROJ_MAIN_TILES = (COL_ATT + PROJ_TN) // PROJ_TN
PROJ_LAST_PIECES = 3
PROJ_W_STAGE_ROWS = 64
ROW_TM = 512
MERGE_ROW_PARTS = 2
MLP_TM = 1024
MLP_ROW_PARTS = 2


def _params(sem):
    return pltpu.CompilerParams(dimension_semantics=sem, vmem_limit_bytes=VMEM_LIMIT)


def _const_spec(shape):
    zeros = (0,) * len(shape)
    return pl.BlockSpec(shape, lambda *_: zeros, pipeline_mode=pl.Buffered(1))


def _silu(t):
    return t * jax.nn.sigmoid(t)


def _rms(x, g):
    return x * lax.rsqrt(jnp.mean(x * x, axis=-1, keepdims=True) + RMS_EPS) * g


def _modulated_norm(x, g, mod_ref, k):
    shift = mod_ref[0, :, k * D_MODEL:(k + 1) * D_MODEL]
    scale = mod_ref[0, :, (k + 1) * D_MODEL:(k + 2) * D_MODEL]
    return _rms(x, g) * (1.0 + scale) + shift


def _mod_kernel(c_ref, w_ref, b_ref, o_ref):
    o_ref[...] = jnp.dot(_silu(c_ref[...]), w_ref[...], preferred_element_type=F32) + b_ref[...]


def _mod(c, w_ada, b_ada):
    B = c.shape[0]
    n = w_ada.shape[1]
    tn = n // 2
    return pl.pallas_call(
        _mod_kernel,
        out_shape=jax.ShapeDtypeStruct((B, n), F32),
        grid=(n // tn,),
        in_specs=[pl.BlockSpec((B, D_MODEL), lambda j: (0, 0)),
                  pl.BlockSpec((D_MODEL, tn), lambda j: (0, j)),
                  pl.BlockSpec((1, tn), lambda j: (0, j))],
        out_specs=pl.BlockSpec((B, tn), lambda j: (0, j)),
        compiler_params=_params(("arbitrary",)),
        name="mod",
    )(c, w_ada, b_ada.reshape(1, n))


def _proj_main_layout():
    spans = [(h * RET_DK, (h + 1) * RET_DK, "rot", (h % RET_HEADS, h >= RET_HEADS))
             for h in range(2 * RET_HEADS)]
    spans += [(2 * RET_QK_W, 2 * RET_QK_W + RET_V_W, "copy", None),
              (2 * RET_QK_W + RET_V_W, COL_ATT, "silu", None),
              (COL_ATT, COL_ATT + ATT_GROUP_W, "scale", ATT_QSCALE),
              (COL_ATT + ATT_GROUP_W, PROJ_MAIN_TILES * PROJ_TN, "copy", None)]
    tiles = [[] for _ in range(PROJ_MAIN_TILES)]
    for c0, c1, kind, arg in spans:
        for t in range(c0 // PROJ_TN, (c1 - 1) // PROJ_TN + 1):
            lo, hi = max(c0, t * PROJ_TN), min(c1, (t + 1) * PROJ_TN)
            assert kind != "rot" or (lo, hi) == (c0, c1)
            tiles[t].append((lo - t * PROJ_TN, hi - t * PROJ_TN, kind, arg))
    return tiles


PROJ_MAIN_LAYOUT = _proj_main_layout()


def _proj_kernel(x_ref, mod_ref, g_ref, cos_ref, sin_ref, w_hbm, *refs, tps, n_cast):
    cast_in, (main_ref, a1_ref, a2_ref) = refs[:n_cast], refs[n_cast:n_cast + 3]
    cast_out = refs[n_cast + 3:2 * n_cast + 3]
    hf_scr, hd1_scr = refs[2 * n_cast + 3:2 * n_cast + 5]
    lhs = refs[2 * n_cast + 5:2 * n_cast + 8]
    w_ref, stage, sems = refs[2 * n_cast + 8:2 * n_cast + 11]
    tm = x_ref.shape[0]
    step = pl.program_id(0)

    @pl.when(step == 0)
    def _():
        rows = stage.shape[1]
        chunks = w_ref.shape[0] // rows

        def fetch(c):
            return pltpu.make_async_copy(w_hbm.at[pl.ds(c * rows, rows), pl.ds(0, w_ref.shape[1])],
                                         stage.at[c % 2], sems.at[c % 2])

        fetch(0).start()
        for c in range(chunks):
            if c + 1 < chunks:
                fetch(c + 1).start()
            fetch(c).wait()
            w_ref[c * rows:(c + 1) * rows, :] = stage[c % 2].astype(BF16)

    for src, dst in zip(cast_in, cast_out):
        dst[...] = src[...].astype(BF16)

    def prepare(x_ref, mod_ref, lhs):
        h0_scr, h1_scr, h2_scr = lhs
        hf = _modulated_norm(x_ref[...], g_ref[...], mod_ref, 0)
        h0_scr[...] = hf.astype(BF16)
        d1, d2 = ATT_GROUPS[1][1], ATT_GROUPS[2][1]
        n1, n2 = tm // d1, tm // d2
        for c in range(D_MODEL // LANES):
            cs = slice(c * LANES, (c + 1) * LANES)
            hf_scr[c] = hf[:, cs]
            for r in range(d1):
                part = hf_scr[c, pl.ds(r, n1, stride=d1), :]
                hd1_scr[c, r * n1:(r + 1) * n1, :] = part
                h1_scr[r * n1:(r + 1) * n1, cs] = part.astype(BF16)
            for r in range(d2):
                src = pl.ds((r % d1) * n1 + r // d1, n2, stride=d1)
                h2_scr[r * n2:(r + 1) * n2, cs] = hd1_scr[c, src, :].astype(BF16)

    def w_tile(t):
        return w_ref[:, t * PROJ_TN:(t + 1) * PROJ_TN]

    def main_tile(t, segments, h0_scr, pieces=1):
        if pieces > 1:
            width = PROJ_TN // pieces
            for p in range(pieces):
                p0, p1 = p * width, (p + 1) * width
                acc = jnp.dot(h0_scr[...], w_ref[:, t * PROJ_TN + p0:t * PROJ_TN + p1],
                              preferred_element_type=F32)
                for c0, c1, kind, arg in segments:
                    lo, hi = max(c0, p0), min(c1, p1)
                    if lo < hi:
                        assert kind in ("scale", "copy")
                        seg = acc[:, lo - p0:hi - p0]
                        seg = seg * arg if kind == "scale" else seg
                        main_ref[:, t * PROJ_TN + lo:t * PROJ_TN + hi] = seg.astype(BF16)
            return
        acc = jnp.dot(h0_scr[...], w_tile(t), preferred_element_type=F32)
        if any(kind == "rot" for _, _, kind, _ in segments):
            cos, sin = cos_ref[...], sin_ref[...]
            row = lax.broadcasted_iota(jnp.int32, cos.shape, 0)
            pos = (((step % tps) * tm + row) % RET_C + 1).astype(F32)
        base = t * PROJ_TN
        for c0, c1, kind, arg in segments:
            seg = acc[:, c0:c1]
            if kind == "rot":
                head, is_key = arg
                rate = -RET_LOG_GAMMA[head] if is_key else RET_LOG_GAMMA[head]
                dec = jnp.exp(rate * pos) * ((RET_DK ** -0.5) if is_key else 1.0)
                cd, sd = cos * dec, sin * dec
                cm = (c0 + c1) // 2
                t1, t2 = acc[:, c0:cm], acc[:, cm:c1]
                main_ref[:, base + c0:base + cm] = (t1 * cd - t2 * sd).astype(BF16)
                main_ref[:, base + cm:base + c1] = (t1 * sd + t2 * cd).astype(BF16)
            elif kind == "silu":
                main_ref[:, base + c0:base + c1] = _silu(seg).astype(BF16)
            elif kind == "scale":
                main_ref[:, base + c0:base + c1] = (seg * arg).astype(BF16)
            else:
                main_ref[:, base + c0:base + c1] = seg.astype(BF16)

    def multiply(lhs):
        h0_scr, h1_scr, h2_scr = lhs
        for t, (dil, h_scr, a_ref) in enumerate(((ATT_GROUPS[1][1], h1_scr, a1_ref),
                                                 (ATT_GROUPS[2][1], h2_scr, a2_ref))):
            n = tm // dil
            acc = jnp.dot(h_scr[...], w_tile(PROJ_MAIN_TILES + t), preferred_element_type=F32)
            gw = ATT_GROUP_W
            for r in range(dil):
                rows = slice(r * n, (r + 1) * n)
                a_ref[0, r, :, :gw] = (acc[rows, :gw] * ATT_QSCALE).astype(BF16)
                a_ref[0, r, :, gw:] = acc[rows, gw:].astype(BF16)
        last = len(PROJ_MAIN_LAYOUT) - 1
        for t, segments in enumerate(PROJ_MAIN_LAYOUT):
            main_tile(t, segments, h0_scr, pieces=PROJ_LAST_PIECES if t == last else 1)

    prepare(x_ref, mod_ref, lhs)
    multiply(lhs)


def _proj(x2, norm_g, mod3, cos, sin, w_in, cast_weights, batch, seq):
    tm, tn = PROJ_TM, PROJ_TN
    tps = seq // tm
    steps = batch * seq // tm
    assert seq % tm == 0 and tm % RET_C == 0
    d1, d2 = ATT_GROUPS[1][1], ATT_GROUPS[2][1]
    assert COL_GATE == (PROJ_MAIN_TILES + 2) * tn
    sub = lambda d: pl.BlockSpec((1, d, tm // d, tn), lambda i: (i // tps, 0, i % tps, 0))
    gate_w = w_in.shape[1] - COL_GATE
    gate_rows = D_MODEL // steps
    assert D_MODEL % (steps * BF16_SUBLANES) == 0
    slab_in = [pl.BlockSpec((pl.Element(gate_rows), pl.Element(gate_w)), lambda i: (i * gate_rows, COL_GATE))]
    slab_out = [pl.BlockSpec((gate_rows, gate_w), lambda i: (i, 0))]
    slab_shapes = [jax.ShapeDtypeStruct((D_MODEL, gate_w), BF16)]
    for w in cast_weights:
        assert w.shape[0] % (steps * BF16_SUBLANES) == 0
        slab_in.append(pl.BlockSpec((w.shape[0] // steps, w.shape[1]), lambda i: (i, 0)))
        slab_out.append(slab_in[-1])
        slab_shapes.append(jax.ShapeDtypeStruct(w.shape, BF16))
    outs = pl.pallas_call(
        functools.partial(_proj_kernel, tps=tps, n_cast=len(slab_in)),
        out_shape=[jax.ShapeDtypeStruct((batch * seq, PROJ_MAIN_TILES * tn), BF16),
                   jax.ShapeDtypeStruct((batch, d1, seq // d1, tn), BF16),
                   jax.ShapeDtypeStruct((batch, d2, seq // d2, tn), BF16)] + slab_shapes,
        grid=(steps,),
        in_specs=[pl.BlockSpec((tm, D_MODEL), lambda i: (i, 0)),
                  pl.BlockSpec((1, 1, 6 * D_MODEL), lambda i: (i // tps, 0, 0)),
                  _const_spec((1, D_MODEL)),
                  pl.BlockSpec((tm, RET_DK // 2), lambda i: (i % tps, 0)),
                  pl.BlockSpec((tm, RET_DK // 2), lambda i: (i % tps, 0)),
                  pl.BlockSpec(memory_space=pl.ANY)] + slab_in,
        out_specs=[pl.BlockSpec((tm, PROJ_MAIN_TILES * tn), lambda i: (i, 0)), sub(d1), sub(d2)] + slab_out,
        scratch_shapes=[pltpu.VMEM((D_MODEL // LANES, tm, LANES), F32)] * 2
                       + [pltpu.VMEM((tm, D_MODEL), BF16)] * 3
                       + [pltpu.VMEM((D_MODEL, COL_GATE), BF16),
                          pltpu.VMEM((2, PROJ_W_STAGE_ROWS, COL_GATE), F32),
                          pltpu.SemaphoreType.DMA((2,))],
        compiler_params=_params(("arbitrary",)),
        name="proj",
    )(x2, mod3, norm_g, cos, sin, w_in, w_in, *cast_weights)
    return outs[0], outs[1], outs[2], outs[3:]


RET_RING = 3


def _ret_kernel(main_hbm, cdec_ref, gng_ref, gnb_ref, o_ref, state, st_in, q_buf, k_buf, v_buf, sems):
    n_heads = pl.num_programs(1)
    step = pl.program_id(0) * n_heads + pl.program_id(1)
    last = pl.num_programs(0) * n_heads - 1
    bufs = (q_buf, k_buf, v_buf)
    col0 = (0, RET_QK_W, 2 * RET_QK_W)

    def copies(s, slot):
        b, h = s // n_heads, s % n_heads
        out = []
        for j, buf in enumerate(bufs):
            width = buf.shape[2]
            cols = pl.ds(pl.multiple_of(col0[j] + h * width, LANES), width)
            out.append(pltpu.make_async_copy(main_hbm.at[b, :, cols], buf.at[slot], sems.at[slot, j]))
        return out

    @pl.when(step == 0)
    def _():
        for s in range(RET_RING - 1):
            for cp in copies(s, s):
                cp.start()

    for slot in range(RET_RING):
        @pl.when(step % RET_RING == slot)
        def _(slot=slot):
            ahead = step + (RET_RING - 1)

            @pl.when(ahead <= last)
            def _():
                for cp in copies(ahead, (slot + RET_RING - 1) % RET_RING):
                    cp.start()

            for cp in copies(step, slot):
                cp.wait()
            _ret_body(q_buf.at[slot], k_buf.at[slot], v_buf.at[slot],
                      cdec_ref, gng_ref, gnb_ref, o_ref, state, st_in)


def _ret_body(q_ref, k_ref, v_ref, cdec_ref, gng_ref, gnb_ref, o_ref, state, st_in):
    C = RET_C
    nc = q_ref.shape[0] // C
    cdec = cdec_ref[0]
    gng = gng_ref[0]
    gnb = gnb_ref[0]
    chunk = lambda ci: pl.ds(pl.multiple_of(ci * C, C), C)

    state[...] = jnp.zeros_like(state)
    st_in[0] = jnp.zeros(st_in.shape[1:], BF16)

    def advance(ci, carry):
        rows = chunk(ci)
        kv = lax.dot_general(k_ref[rows, :], v_ref[rows, :], (((0,), (0,)), ((), ())),
                             preferred_element_type=F32)
        new = (state[...] + kv) * cdec
        state[...] = new
        st_in[ci + 1] = new.astype(BF16)
        return carry

    lax.fori_loop(0, nc - 1, advance, 0, unroll=True)

    qi = lax.broadcasted_iota(jnp.int32, (C, C), 0)
    kj = lax.broadcasted_iota(jnp.int32, (C, C), 1)
    causal = qi >= kj

    def emit(ci, carry):
        rows = chunk(ci)
        q = q_ref[rows, :]
        s = lax.dot_general(q, k_ref[rows, :], (((1,), (1,)), ((), ())), preferred_element_type=F32)
        s = jnp.where(causal, s, 0.0).astype(BF16)
        o = (jnp.dot(s, v_ref[rows, :], preferred_element_type=F32)
             + jnp.dot(q, st_in[ci], preferred_element_type=F32))
        mu = jnp.mean(o, axis=-1, keepdims=True)
        oc = o - mu
        var = jnp.mean(oc * oc, axis=-1, keepdims=True)
        on = oc * lax.rsqrt(var + GN_EPS) * gng + gnb
        o_ref[0, rows, :] = on.astype(BF16)
        return carry

    lax.fori_loop(0, nc, emit, 0, unroll=True)


def _retention_parts(main3, gn_g, gn_b):
    B, S, _ = main3.shape
    H, C = RET_HEADS, RET_C
    assert S % C == 0
    assert B * H >= RET_RING
    cdec = jnp.exp(jnp.asarray(RET_LOG_GAMMA, F32) * C)[:, None, None]
    in_specs = [pl.BlockSpec(memory_space=pl.ANY),
                pl.BlockSpec((1, 1, 1), lambda b, h: (h, 0, 0)),
                pl.BlockSpec((1, 1, RET_DV), lambda b, h: (h, 0, 0)),
                pl.BlockSpec((1, 1, RET_DV), lambda b, h: (h, 0, 0))]
    args = [main3, cdec, gn_g.reshape(H, 1, RET_DV), gn_b.reshape(H, 1, RET_DV)]
    ring = lambda w: pltpu.VMEM((RET_RING, S, w), BF16)
    return _CallParts(_ret_kernel, (B, H), in_specs, args,
                      [pl.BlockSpec((1, S, RET_DV), lambda b, h: (b, 0, h))],
                      [jax.ShapeDtypeStruct((B, S, RET_V_W), BF16)],
                      [pltpu.VMEM((RET_DK, RET_DV), F32), pltpu.VMEM((S // C, RET_DK, RET_DV), BF16),
                       ring(RET_DK), ring(RET_DK), ring(RET_DV),
                       pltpu.SemaphoreType.DMA((RET_RING, 3))])


def _t5_bucket(dist):
    dist = np.asarray(dist)
    max_exact = REL_BUCKETS // 2
    d_f = np.maximum(dist, 1).astype(np.float32)
    large = max_exact + (np.log(d_f / np.float32(max_exact)) / np.float32(math.log(REL_MAX_DIST / max_exact))
                         * np.float32(REL_BUCKETS - max_exact)).astype(np.int32)
    large = np.minimum(large, REL_BUCKETS - 1)
    return np.where(dist < max_exact, dist, large).astype(np.int32)


def _attn_unit(n, q_ref, k_ref, v_ref, bias_scr, vaug_scr, o_ref, stats_ref):
    blk, dh = ATT_BLK, ATT_HEAD_DIM
    windowed = k_ref.shape[0] > blk
    if windowed and isinstance(n, int):
        win, first = pl.ds(max(n - 1, 0) * blk, 2 * blk), int(n == 0)
    elif windowed:
        win = pl.ds(pl.multiple_of(jnp.maximum(n - 1, 0) * blk, blk), 2 * blk)
        first = (n == 0).astype(jnp.int32)
    heads = ATT_HEADS_PER_GROUP
    lane = lax.broadcasted_iota(jnp.int32, (blk, LSE_W), 1)
    stats = jnp.ones((blk, LSE_W), F32)
    for h in range(heads):
        hs = slice(h * dh, (h + 1) * dh)
        if windowed:
            kb, vb, bias = k_ref[win, hs], v_ref[win, hs], bias_scr[first, h]
        else:
            kb, vb, bias = k_ref[:, hs], v_ref[:, hs], bias_scr[h]
        vaug_scr[h, :, :dh] = vb
        s = lax.dot_general(q_ref[:, hs], kb, (((1,), (1,)), ((), ())), preferred_element_type=F32) + bias
        mx = jnp.max(s, axis=-1, keepdims=True)
        e = jnp.exp(s - mx).astype(BF16)
        o_den = jnp.dot(e, vaug_scr[h], preferred_element_type=F32)
        o_ref[:, hs] = o_den[:, :dh].astype(BF16)
        stats = jnp.where(lane == h, mx, jnp.where(lane == heads + h, o_den[:, dh:], stats))
    stats_ref[...] = stats


def _attn_init(tab_ref, bk0, bk1, bk2, bias0, bias1, bias2, vaug0, vaug1, vaug2):
    blk = ATT_BLK

    @pl.when((pl.program_id(0) == 0) & (pl.program_id(1) == 0))
    def _():
        for vaug in (vaug0, vaug1, vaug2):
            vaug[...] = jnp.ones(vaug.shape, BF16)
        qi = lax.broadcasted_iota(jnp.int32, (blk, 2 * blk), 0)
        kj = lax.broadcasted_iota(jnp.int32, (blk, 2 * blk), 1)
        m = blk + qi - kj
        band = (m >= 0) & (m <= blk)
        masked = jnp.full((blk, blk), NEG, F32)
        for gi, (bk_ref, bias_scr) in enumerate(((bk0, bias0), (bk1, bias1), (bk2, bias2))):
            bucket = bk_ref[...]
            for h in range(ATT_HEADS_PER_GROUP):
                col = gi * ATT_HEADS_PER_GROUP + h
                acc = jnp.zeros((blk, 2 * blk), F32)
                for t in range(REL_BUCKETS):
                    acc = jnp.where(bucket == t, tab_ref[t, col], acc)
                tile = jnp.where(band, acc, NEG)
                if len(bias_scr.shape) == 4:
                    bias_scr[0, h] = tile
                    bias_scr[1, h] = jnp.concatenate([tile[:, blk:], masked], axis=1)
                else:
                    bias_scr[h] = tile[:, blk:]


ATT_RING = 3


def _attn_kernel(tab_ref, main_hbm, k0, v0, att1_hbm, att2_hbm, bk0, bk1, bk2,
                 o0, l0, o1, l1, o2, l2, bias0, bias1, bias2, vaug0, vaug1, vaug2,
                 q0_buf, q1_buf, k1_buf, v1_buf, q2_buf, k2_buf, v2_buf, sems, *, nb1, per_step):
    n_steps = pl.num_programs(1)
    step = pl.program_id(0) * n_steps + pl.program_id(1)
    last = pl.num_programs(0) * n_steps - 1
    gw = ATT_GROUP_W
    ns1 = per_step // nb1

    def copies(s, slot):
        b, i = s // n_steps, s % n_steps
        rows0 = pl.ds(pl.multiple_of(i * (per_step * ATT_BLK), ATT_BLK), per_step * ATT_BLK)
        srcs = [main_hbm.at[b, rows0, pl.ds(COL_ATT, gw)]]
        srcs += [att1_hbm.at[b, pl.ds(i * ns1, ns1), :, pl.ds(part * gw, gw)] for part in range(3)]
        srcs += [att2_hbm.at[b, pl.ds(i * per_step, per_step), :, pl.ds(part * gw, gw)] for part in range(3)]
        bufs = (q0_buf, q1_buf, k1_buf, v1_buf, q2_buf, k2_buf, v2_buf)
        return [pltpu.make_async_copy(src, buf.at[slot], sems.at[slot, j])
                for j, (src, buf) in enumerate(zip(srcs, bufs))]

    slot = step % ATT_RING

    @pl.when(step == 0)
    def _():
        for s in range(ATT_RING - 1):
            for cp in copies(s, s):
                cp.start()

    ahead = step + (ATT_RING - 1)

    @pl.when(ahead <= last)
    def _():
        for cp in copies(ahead, ahead % ATT_RING):
            cp.start()

    _attn_init(tab_ref, bk0, bk1, bk2, bias0, bias1, bias2, vaug0, vaug1, vaug2)
    for cp in copies(step, slot):
        cp.wait()
    _attn_blocks(pl.program_id(1), q0_buf.at[slot], k0, v0, q1_buf.at[slot], k1_buf.at[slot], v1_buf.at[slot],
                 q2_buf.at[slot], k2_buf.at[slot], v2_buf.at[slot], o0, l0, o1, l1, o2, l2,
                 bias0, bias1, bias2, vaug0, vaug1, vaug2, nb1=nb1, per_step=per_step)


def _attn_blocks(i, q0, k0, v0, q1, k1, v1, q2, k2, v2, o0, l0, o1, l1, o2, l2,
                 bias0, bias1, bias2, vaug0, vaug1, vaug2, *, nb1, per_step):
    blk = ATT_BLK
    for u in range(per_step):
        rows = pl.ds(u * blk, blk)
        _attn_unit(i * per_step + u, q0.at[rows], k0, v0, bias0, vaug0.at[u], o0.at[rows], l0.at[rows])
        sub, n1 = divmod(u, nb1)
        rows1 = pl.ds(n1 * blk, blk)
        _attn_unit(n1, q1.at[sub, rows1], k1.at[sub], v1.at[sub], bias1, vaug1.at[u],
                   o1.at[sub, rows1], l1.at[sub, rows1])
        _attn_unit(0, q2.at[u], k2.at[u], v2.at[u], bias2, vaug2.at[u], o2.at[u], l2.at[u])


def _attention_parts(main3, att1, att2, rel_bias):
    B, S, _ = main3.shape
    blk, gw = ATT_BLK, ATT_GROUP_W
    steps = S // blk
    lens = [S // dil for _, dil in ATT_GROUPS]
    nbs = [L // blk for L in lens]
    assert all(win // dil == blk for win, dil in ATT_GROUPS)
    assert nbs[0] == steps and nbs[1] >= 2 and nbs[2] == 1
    c0 = COL_ATT // gw
    nb1 = nbs[1]
    G = ATT_BLOCKS_PER_STEP
    assert G % nb1 == 0 and steps % G == 0
    ns1 = G // nb1
    assert B * (steps // G) >= ATT_RING
    in_specs = [pl.BlockSpec(memory_space=pltpu.SMEM),
                pl.BlockSpec(memory_space=pl.ANY),
                pl.BlockSpec((None, S, gw), lambda b, i: (b, 0, c0 + 1)),
                pl.BlockSpec((None, S, gw), lambda b, i: (b, 0, c0 + 2)),
                pl.BlockSpec(memory_space=pl.ANY),
                pl.BlockSpec(memory_space=pl.ANY)]
    args = [rel_bias, main3, main3, main3, att1, att2]
    qi = np.arange(blk)[:, None]
    kj = np.arange(2 * blk)[None, :]
    m = blk + qi - kj
    for _, dil in ATT_GROUPS:
        in_specs.append(_const_spec((blk, 2 * blk)))
        args.append(jnp.asarray(_t5_bucket(np.clip(m, 0, blk) * dil)))
    out_specs, out_shapes = [], []
    for (_, dil), nb in zip(ATT_GROUPS, nbs):
        for w, dt in ((gw, BF16), (LSE_W, F32)):
            if nb == steps:
                out_specs.append(pl.BlockSpec((None, G * blk, w), lambda b, i: (b, i, 0)))
                out_shapes.append(jax.ShapeDtypeStruct((B, S, w), dt))
            elif nb == 1:
                out_specs.append(pl.BlockSpec((None, G, blk, w), lambda b, i: (b, i, 0, 0)))
                out_shapes.append(jax.ShapeDtypeStruct((B, dil, S // dil, w), dt))
            else:
                out_specs.append(pl.BlockSpec((None, ns1, lens[1], w), lambda b, i: (b, i, 0, 0)))
                out_shapes.append(jax.ShapeDtypeStruct((B, dil, S // dil, w), dt))
    heads = ATT_HEADS_PER_GROUP
    scratch = [pltpu.VMEM((2, heads, blk, 2 * blk), F32), pltpu.VMEM((2, heads, blk, 2 * blk), F32),
               pltpu.VMEM((heads, blk, blk), F32),
               pltpu.VMEM((G, heads, 2 * blk, 2 * ATT_HEAD_DIM), BF16),
               pltpu.VMEM((G, heads, 2 * blk, 2 * ATT_HEAD_DIM), BF16),
               pltpu.VMEM((G, heads, blk, 2 * ATT_HEAD_DIM), BF16)]
    scratch += [pltpu.VMEM((ATT_RING, G * blk, gw), BF16)]
    scratch += [pltpu.VMEM((ATT_RING, ns1, lens[1], gw), BF16)] * 3
    scratch += [pltpu.VMEM((ATT_RING, G, blk, gw), BF16)] * 3
    scratch += [pltpu.SemaphoreType.DMA((ATT_RING, 7))]
    return _CallParts(functools.partial(_attn_kernel, nb1=nb1, per_step=G), (B, steps // G),
                      in_specs, args, out_specs, out_shapes, scratch)


class _CallParts(NamedTuple):
    body: object
    grid: tuple
    in_specs: list
    args: list
    out_specs: list
    out_shapes: list
    scratch: list


def _call(parts, name):
    return pl.pallas_call(
        parts.body,
        out_shape=parts.out_shapes,
        grid=parts.grid,
        in_specs=parts.in_specs,
        out_specs=parts.out_specs,
        scratch_shapes=parts.scratch,
        compiler_params=_params(("arbitrary",) * len(parts.grid)),
        name=name,
    )(*parts.args)


def _mixers(main3, att1, att2, rel_bias, gn_g, gn_b):
    retg, = _call(_retention_parts(main3, gn_g, gn_b), "ret")
    att = _call(_attention_parts(main3, att1, att2, rel_bias), "attn")
    return retg, att[0::2], att[1::2]


def _merge_kernel(x_ref, g_ref, mod_ref, retn_ref, rgate_ref, o0_ref, o1_ref, o2_ref, s0_ref, s1_ref, s2_ref,
                  wg_ref, wr_ref, wa_ref, wo_ref, out_ref, o1_scr, o2_scr, s1_scr, s2_scr):
    tm = x_ref.shape[0]
    for dil, o_ref, s_ref, o_scr, s_scr in ((ATT_GROUPS[1][1], o1_ref, s1_ref, o1_scr, s1_scr),
                                            (ATT_GROUPS[2][1], o2_ref, s2_ref, o2_scr, s2_scr)):
        n = tm // dil
        for r in range(dil):
            s_scr[pl.ds(r, n, stride=dil), :] = s_ref[0, r]
            for h in range(ATT_HEADS_PER_GROUP):
                hs = slice(h * ATT_HEAD_DIM, (h + 1) * ATT_HEAD_DIM)
                o_scr[h, pl.ds(r, n, stride=dil), :] = o_ref[0, r, :, hs].astype(F32)
    part = tm // MERGE_ROW_PARTS
    for p in range(MERGE_ROW_PARTS):
        rows = slice(p * part, (p + 1) * part)
        retg = retn_ref[rows, :] * rgate_ref[rows, :]
        ret_out = jnp.dot(retg, wr_ref[...], preferred_element_type=F32)
        x = x_ref[rows, :]
        h = _modulated_norm(x, g_ref[...], mod_ref, 0).astype(BF16)
        gates = jax.nn.sigmoid(jnp.dot(h, wg_ref[...], preferred_element_type=F32))
        stats = (s0_ref[rows, :], s1_scr[rows, :], s2_scr[rows, :])
        top = jnp.maximum(jnp.maximum(stats[0], stats[1]), stats[2])
        a = [jnp.exp(st - top) for st in stats]
        lane = lax.broadcasted_iota(jnp.int32, top.shape, 1)
        heads = ATT_HEADS_PER_GROUP
        den = sum(ag * pltpu.roll(st, LSE_W - heads, axis=1) for ag, st in zip(a, stats))
        inv = 1.0 / jnp.where(lane < heads, den, 1.0)
        w0, w1, w2 = (ag * inv for ag in a)
        parts = []
        for hd in range(heads):
            hs = slice(hd * ATT_HEAD_DIM, (hd + 1) * ATT_HEAD_DIM)
            parts.append(w0[:, hd:hd + 1] * o0_ref[rows, hs].astype(F32)
                         + w1[:, hd:hd + 1] * o1_scr[hd, rows, :]
                         + w2[:, hd:hd + 1] * o2_scr[hd, rows, :])
        att = jnp.concatenate(parts, axis=-1).astype(BF16)
        att_out = jnp.dot(att, wa_ref[...], preferred_element_type=F32)
        merged = gates[:, :D_MODEL] * ret_out + gates[:, D_MODEL:] * att_out
        y = jnp.dot(merged.astype(BF16), wo_ref[...], preferred_element_type=F32)
        out_ref[rows, :] = x + mod_ref[0, :, 2 * D_MODEL:3 * D_MODEL] * y


def _merge(x2, norm_g, mod3, retn2, main2, o, lse, w_gate, w_ret_out, w_att_out, w_o, batch, seq):
    tm = ROW_TM
    gate_col = (2 * RET_QK_W + RET_V_W) // RET_V_W
    assert gate_col * RET_V_W == 2 * RET_QK_W + RET_V_W
    tps = seq // tm
    row = lambda w: pl.BlockSpec((tm, w), lambda i: (i, 0))
    sub = lambda d, w: pl.BlockSpec((1, d, tm // d, w), lambda i: (i // tps, 0, i % tps, 0))
    d1, d2 = ATT_GROUPS[1][1], ATT_GROUPS[2][1]
    return pl.pallas_call(
        _merge_kernel,
        out_shape=jax.ShapeDtypeStruct((batch * seq, D_MODEL), F32),
        grid=(batch * seq // tm,),
        in_specs=[row(D_MODEL),
                  _const_spec((1, D_MODEL)),
                  pl.BlockSpec((1, 1, 6 * D_MODEL), lambda i: (i // tps, 0, 0)),
                  row(RET_V_W),
                  pl.BlockSpec((tm, RET_V_W), lambda i: (i, gate_col)),
                  row(ATT_GROUP_W), sub(d1, ATT_GROUP_W), sub(d2, ATT_GROUP_W),
                  row(LSE_W), sub(d1, LSE_W), sub(d2, LSE_W),
                  _const_spec((D_MODEL, 2 * D_MODEL)),
                  _const_spec((RET_V_W, D_MODEL)),
                  _const_spec((ATT_GROUP_W, D_MODEL)),
                  _const_spec((D_MODEL, D_MODEL))],
        out_specs=row(D_MODEL),
        scratch_shapes=[pltpu.VMEM((ATT_HEADS_PER_GROUP, tm, ATT_HEAD_DIM), F32)] * 2
                       + [pltpu.VMEM((tm, LSE_W), F32)] * 2,
        compiler_params=_params(("arbitrary",)),
        name="merge",
    )(x2, norm_g, mod3, retn2, main2, *o, *lse, w_gate, w_ret_out, w_att_out, w_o)


def _mlp_kernel(x_ref, mod_ref, g2_ref, gf_ref, w1_ref, w2_ref, out_ref, *, final):
    part = x_ref.shape[0] // MLP_ROW_PARTS
    for p in range(MLP_ROW_PARTS):
        rows = slice(p * part, (p + 1) * part)
        x = x_ref[rows, :]
        h = _modulated_norm(x, g2_ref[...], mod_ref, 3).astype(BF16)
        u = jnp.maximum(jnp.dot(h, w1_ref[...], preferred_element_type=F32), 0.0)
        y = jnp.dot((u * u).astype(BF16), w2_ref[...], preferred_element_type=F32)
        x = x + mod_ref[0, :, 5 * D_MODEL:6 * D_MODEL] * y
        out_ref[rows, :] = _rms(x, gf_ref[...]) if final else x


def _mlp(x2, mod3, norm2_g, norm_f_g, w1, w2, seq, final):
    rows = x2.shape[0]
    tm = MLP_TM
    tps = seq // tm
    return pl.pallas_call(
        functools.partial(_mlp_kernel, final=final),
        out_shape=jax.ShapeDtypeStruct((rows, D_MODEL), F32),
        grid=(rows // tm,),
        in_specs=[pl.BlockSpec((tm, D_MODEL), lambda i: (i, 0)),
                  pl.BlockSpec((1, 1, 6 * D_MODEL), lambda i: (i // tps, 0, 0)),
                  _const_spec((1, D_MODEL)),
                  _const_spec((1, D_MODEL)),
                  _const_spec((D_MODEL, D_FF)),
                  _const_spec((D_FF, D_MODEL))],
        out_specs=pl.BlockSpec((tm, D_MODEL), lambda i: (i, 0)),
        compiler_params=_params(("arbitrary",)),
        name="mlp",
    )(x2, mod3, norm2_g, norm_f_g, w1, w2)


def kernel(x, c, w_ada, b_ada, norm1_g, w_in, rel_bias, ret_gn_g, ret_gn_b, w_ret_out, w_att_out,
           w_o, norm2_g, w_ff1, w_ff2, norm_f_g):
    B, S, D = x.shape
    depth = w_ada.shape[0]
    half = RET_DK // 2
    ang = np.arange(S, dtype=np.float64)[:, None] * ROPE_BASE ** (-np.arange(half, dtype=np.float64) / half)
    cos, sin = jnp.asarray(np.cos(ang), F32), jnp.asarray(np.sin(ang), F32)
    x2 = x.reshape(B * S, D)
    for l in range(depth):
        g1 = norm1_g[l].reshape(1, D)
        mod3 = _mod(c, w_ada[l], b_ada[l]).reshape(B, 1, 6 * D)
        main2, att1, att2, (w_gate_bf, w_ret_bf, w_o_bf, w_ff1_bf, w_ff2_bf) = _proj(
            x2, g1, mod3, cos, sin, w_in[l], [w_ret_out[l], w_o[l], w_ff1[l], w_ff2[l]], B, S)
        main3 = main2.reshape(B, S, main2.shape[1])
        retg, o, stats = _mixers(main3, att1, att2, rel_bias, ret_gn_g[l], ret_gn_b[l])
        o = [o[0].reshape(B * S, ATT_GROUP_W), o[1], o[2]]
        stats = [stats[0].reshape(B * S, LSE_W), stats[1], stats[2]]
        x2 = _merge(x2, g1, mod3, retg.reshape(B * S, RET_V_W), main2, o, stats,
                    w_gate_bf, w_ret_bf, w_att_out[l].astype(BF16), w_o_bf, B, S)
        x2 = _mlp(x2, mod3, norm2_g[l].reshape(1, D), norm_f_g.reshape(1, D),
                  w_ff1_bf, w_ff2_bf, S, final=l == depth - 1)
    return x2.reshape(B, S, D)
```

```python
import functools
import math
from typing import NamedTuple

import jax
import jax.numpy as jnp
import numpy as np
from jax import lax
from jax.experimental import pallas as pl
from jax.experimental.pallas import tpu as pltpu

F32 = jnp.float32
BF16 = jnp.bfloat16

D_MODEL = 1024
RET_HEADS = 4
RET_DK = 256
RET_DV = 512
RET_C = 256
RET_LOG_GAMMA = tuple(math.log1p(-(2.0 ** (-5.0 - h))) for h in range(RET_HEADS))
RET_QK_W = RET_HEADS * RET_DK
RET_V_W = RET_HEADS * RET_DV
ATT_GROUPS = ((128, 1), (512, 4), (2048, 16))
ATT_HEADS_PER_GROUP = 4
ATT_HEAD_DIM = 128
ATT_GROUP_W = ATT_HEADS_PER_GROUP * ATT_HEAD_DIM
ATT_BLK = 128
ATT_BLOCKS_PER_STEP = 8
LANES = 128
BF16_SUBLANES = 16
LSE_W = LANES
REL_BUCKETS = 32
REL_MAX_DIST = 2048
D_FF = 4 * D_MODEL
RMS_EPS = 1e-6
GN_EPS = 1e-5
ROPE_BASE = 10000.0
NEG = -1e30
ATT_QSCALE = ATT_HEAD_DIM ** -0.5

COL_ATT = 2 * RET_QK_W + 2 * RET_V_W
COL_GATE = COL_ATT + 9 * ATT_GROUP_W

VMEM_LIMIT = 56 * 1024 * 1024

PROJ_TM = 256
PROJ_TN = 3 * ATT_GROUP_W
PROJ_MAIN_TILES = (COL_ATT + PROJ_TN) // PROJ_TN
PROJ_LAST_PIECES = 3
PROJ_W_STAGE_ROWS = 64
ROW_TM = 512
MERGE_ROW_PARTS = 2
MLP_TM = 1024
MLP_ROW_PARTS = 2


def _params(sem):
    return pltpu.CompilerParams(dimension_semantics=sem, vmem_limit_bytes=VMEM_LIMIT)


def _const_spec(shape):
    zeros = (0,) * len(shape)
    return pl.BlockSpec(shape, lambda *_: zeros, pipeline_mode=pl.Buffered(1))


def _silu(t):
    return t * jax.nn.sigmoid(t)


def _rms(x, g):
    return x * lax.rsqrt(jnp.mean(x * x, axis=-1, keepdims=True) + RMS_EPS) * g


def _modulated_norm(x, g, mod_ref, k):
    shift = mod_ref[0, :, k * D_MODEL:(k + 1) * D_MODEL]
    scale = mod_ref[0, :, (k + 1) * D_MODEL:(k + 2) * D_MODEL]
    return _rms(x, g) * (1.0 + scale) + shift


def _mod_kernel(c_ref, w_ref, b_ref, o_ref):
    o_ref[...] = jnp.dot(_silu(c_ref[...]), w_ref[...], preferred_element_type=F32) + b_ref[...]


def _mod(c, w_ada, b_ada):
    B = c.shape[0]
    n = w_ada.shape[1]
    tn = n // 2
    return pl.pallas_call(
        _mod_kernel,
        out_shape=jax.ShapeDtypeStruct((B, n), F32),
        grid=(n // tn,),
        in_specs=[pl.BlockSpec((B, D_MODEL), lambda j: (0, 0)),
                  pl.BlockSpec((D_MODEL, tn), lambda j: (0, j)),
                  pl.BlockSpec((1, tn), lambda j: (0, j))],
        out_specs=pl.BlockSpec((B, tn), lambda j: (0, j)),
        compiler_params=_params(("arbitrary",)),
        name="mod",
    )(c, w_ada, b_ada.reshape(1, n))


def _proj_main_layout():
    spans = [(h * RET_DK, (h + 1) * RET_DK, "rot", (h % RET_HEADS, h >= RET_HEADS))
             for h in range(2 * RET_HEADS)]
    spans += [(2 * RET_QK_W, 2 * RET_QK_W + RET_V_W, "copy", None),
              (2 * RET_QK_W + RET_V_W, COL_ATT, "silu", None),
              (COL_ATT, COL_ATT + ATT_GROUP_W, "scale", ATT_QSCALE),
              (COL_ATT + ATT_GROUP_W, PROJ_MAIN_TILES * PROJ_TN, "copy", None)]
    tiles = [[] for _ in range(PROJ_MAIN_TILES)]
    for c0, c1, kind, arg in spans:
        for t in range(c0 // PROJ_TN, (c1 - 1) // PROJ_TN + 1):
            lo, hi = max(c0, t * PROJ_TN), min(c1, (t + 1) * PROJ_TN)
            assert kind != "rot" or (lo, hi) == (c0, c1)
            tiles[t].append((lo - t * PROJ_TN, hi - t * PROJ_TN, kind, arg))
    return tiles


PROJ_MAIN_LAYOUT = _proj_main_layout()


def _proj_kernel(x_ref, mod_ref, g_ref, cos_ref, sin_ref, w_hbm, *refs, tps, n_cast):
    cast_in, (main_ref, a1_ref, a2_ref) = refs[:n_cast], refs[n_cast:n_cast + 3]
    cast_out = refs[n_cast + 3:2 * n_cast + 3]
    hf_scr, hd1_scr = refs[2 * n_cast + 3:2 * n_cast + 5]
    lhs = refs[2 * n_cast + 5:2 * n_cast + 8]
    w_ref, stage, sems = refs[2 * n_cast + 8:2 * n_cast + 11]
    tm = x_ref.shape[0]
    step = pl.program_id(0)

    @pl.when(step == 0)
    def _():
        rows = stage.shape[1]
        chunks = w_ref.shape[0] // rows

        def fetch(c):
            return pltpu.make_async_copy(w_hbm.at[pl.ds(c * rows, rows), pl.ds(0, w_ref.shape[1])],
                                         stage.at[c % 2], sems.at[c % 2])

        fetch(0).start()
        for c in range(chunks):
            if c + 1 < chunks:
                fetch(c + 1).start()
            fetch(c).wait()
            w_ref[c * rows:(c + 1) * rows, :] = stage[c % 2].astype(BF16)

    for src, dst in zip(cast_in, cast_out):
        dst[...] = src[...].astype(BF16)

    def prepare(x_ref, mod_ref, lhs):
        h0_scr, h1_scr, h2_scr = lhs
        hf = _modulated_norm(x_ref[...], g_ref[...], mod_ref, 0)
        h0_scr[...] = hf.astype(BF16)
        d1, d2 = ATT_GROUPS[1][1], ATT_GROUPS[2][1]
        n1, n2 = tm // d1, tm // d2
        for c in range(D_MODEL // LANES):
            cs = slice(c * LANES, (c + 1) * LANES)
            hf_scr[c] = hf[:, cs]
            for r in range(d1):
                part = hf_scr[c, pl.ds(r, n1, stride=d1), :]
                hd1_scr[c, r * n1:(r + 1) * n1, :] = part
                h1_scr[r * n1:(r + 1) * n1, cs] = part.astype(BF16)
            for r in range(d2):
                src = pl.ds((r % d1) * n1 + r // d1, n2, stride=d1)
                h2_scr[r * n2:(r + 1) * n2, cs] = hd1_scr[c, src, :].astype(BF16)

    def w_tile(t):
        return w_ref[:, t * PROJ_TN:(t + 1) * PROJ_TN]

    def main_tile(t, segments, h0_scr, pieces=1):
        if pieces > 1:
            width = PROJ_TN // pieces
            for p in range(pieces):
                p0, p1 = p * width, (p + 1) * width
                acc = jnp.dot(h0_scr[...], w_ref[:, t * PROJ_TN + p0:t * PROJ_TN + p1],
                              preferred_element_type=F32)
                for c0, c1, kind, arg in segments:
                    lo, hi = max(c0, p0), min(c1, p1)
                    if lo < hi:
                        assert kind in ("scale", "copy")
                        seg = acc[:, lo - p0:hi - p0]
                        seg = seg * arg if kind == "scale" else seg
                        main_ref[:, t * PROJ_TN + lo:t * PROJ_TN + hi] = seg.astype(BF16)
            return
        acc = jnp.dot(h0_scr[...], w_tile(t), preferred_element_type=F32)
        if any(kind == "rot" for _, _, kind, _ in segments):
            cos, sin = cos_ref[...], sin_ref[...]
            row = lax.broadcasted_iota(jnp.int32, cos.shape, 0)
            pos = (((step % tps) * tm + row) % RET_C + 1).astype(F32)
        base = t * PROJ_TN
        for c0, c1, kind, arg in segments:
            seg = acc[:, c0:c1]
            if kind == "rot":
                head, is_key = arg
                rate = -RET_LOG_GAMMA[head] if is_key else RET_LOG_GAMMA[head]
                dec = jnp.exp(rate * pos) * ((RET_DK ** -0.5) if is_key else 1.0)
                cd, sd = cos * dec, sin * dec
                cm = (c0 + c1) // 2
                t1, t2 = acc[:, c0:cm], acc[:, cm:c1]
                main_ref[:, base + c0:base + cm] = (t1 * cd - t2 * sd).astype(BF16)
                main_ref[:, base + cm:base + c1] = (t1 * sd + t2 * cd).astype(BF16)
            elif kind == "silu":
                main_ref[:, base + c0:base + c1] = _silu(seg).astype(BF16)
            elif kind == "scale":
                main_ref[:, base + c0:base + c1] = (seg * arg).astype(BF16)
            else:
                main_ref[:, base + c0:base + c1] = seg.astype(BF16)

    def multiply(lhs):
        h0_scr, h1_scr, h2_scr = lhs
        for t, (dil, h_scr, a_ref) in enumerate(((ATT_GROUPS[1][1], h1_scr, a1_ref),
                                                 (ATT_GROUPS[2][1], h2_scr, a2_ref))):
            n = tm // dil
            acc = jnp.dot(h_scr[...], w_tile(PROJ_MAIN_TILES + t), preferred_element_type=F32)
            gw = ATT_GROUP_W
            for r in range(dil):
                rows = slice(r * n, (r + 1) * n)
                a_ref[0, r, :, :gw] = (acc[rows, :gw] * ATT_QSCALE).astype(BF16)
                a_ref[0, r, :, gw:] = acc[rows, gw:].astype(BF16)
        last = len(PROJ_MAIN_LAYOUT) - 1
        for t, segments in enumerate(PROJ_MAIN_LAYOUT):
            main_tile(t, segments, h0_scr, pieces=PROJ_LAST_PIECES if t == last else 1)

    prepare(x_ref, mod_ref, lhs)
    multiply(lhs)


def _proj(x2, norm_g, mod3, cos, sin, w_in, cast_weights, batch, seq):
    tm, tn = PROJ_TM, PROJ_TN
    tps = seq // tm
    steps = batch * seq // tm
    assert seq % tm == 0 and tm % RET_C == 0
    d1, d2 = ATT_GROUPS[1][1], ATT_GROUPS[2][1]
    assert COL_GATE == (PROJ_MAIN_TILES + 2) * tn
    sub = lambda d: pl.BlockSpec((1, d, tm // d, tn), lambda i: (i // tps, 0, i % tps, 0))
    gate_w = w_in.shape[1] - COL_GATE
    gate_rows = D_MODEL // steps
    assert D_MODEL % (steps * BF16_SUBLANES) == 0
    slab_in = [pl.BlockSpec((pl.Element(gate_rows), pl.Element(gate_w)), lambda i: (i * gate_rows, COL_GATE))]
    slab_out = [pl.BlockSpec((gate_rows, gate_w), lambda i: (i, 0))]
    slab_shapes = [jax.ShapeDtypeStruct((D_MODEL, gate_w), BF16)]
    for w in cast_weights:
        assert w.shape[0] % (steps * BF16_SUBLANES) == 0
        slab_in.append(pl.BlockSpec((w.shape[0] // steps, w.shape[1]), lambda i: (i, 0)))
        slab_out.append(slab_in[-1])
        slab_shapes.append(jax.ShapeDtypeStruct(w.shape, BF16))
    outs = pl.pallas_call(
        functools.partial(_proj_kernel, tps=tps, n_cast=len(slab_in)),
        out_shape=[jax.ShapeDtypeStruct((batch * seq, PROJ_MAIN_TILES * tn), BF16),
                   jax.ShapeDtypeStruct((batch, d1, seq // d1, tn), BF16),
                   jax.ShapeDtypeStruct((batch, d2, seq // d2, tn), BF16)] + slab_shapes,
        grid=(steps,),
        in_specs=[pl.BlockSpec((tm, D_MODEL), lambda i: (i, 0)),
                  pl.BlockSpec((1, 1, 6 * D_MODEL), lambda i: (i // tps, 0, 0)),
                  _const_spec((1, D_MODEL)),
                  pl.BlockSpec((tm, RET_DK // 2), lambda i: (i % tps, 0)),
                  pl.BlockSpec((tm, RET_DK // 2), lambda i: (i % tps, 0)),
                  pl.BlockSpec(memory_space=pl.ANY)] + slab_in,
        out_specs=[pl.BlockSpec((tm, PROJ_MAIN_TILES * tn), lambda i: (i, 0)), sub(d1), sub(d2)] + slab_out,
        scratch_shapes=[pltpu.VMEM((D_MODEL // LANES, tm, LANES), F32)] * 2
                       + [pltpu.VMEM((tm, D_MODEL), BF16)] * 3
                       + [pltpu.VMEM((D_MODEL, COL_GATE), BF16),
                          pltpu.VMEM((2, PROJ_W_STAGE_ROWS, COL_GATE), F32),
                          pltpu.SemaphoreType.DMA((2,))],
        compiler_params=_params(("arbitrary",)),
        name="proj",
    )(x2, mod3, norm_g, cos, sin, w_in, w_in, *cast_weights)
    return outs[0], outs[1], outs[2], outs[3:]


RET_RING = 3


def _ret_kernel(main_hbm, cdec_ref, gng_ref, gnb_ref, o_ref, state, st_in, q_buf, k_buf, v_buf, sems):
    n_heads = pl.num_programs(1)
    step = pl.program_id(0) * n_heads + pl.program_id(1)
    last = pl.num_programs(0) * n_heads - 1
    bufs = (q_buf, k_buf, v_buf)
    col0 = (0, RET_QK_W, 2 * RET_QK_W)

    def copies(s, slot):
        b, h = s // n_heads, s % n_heads
        out = []
        for j, buf in enumerate(bufs):
            width = buf.shape[2]
            cols = pl.ds(pl.multiple_of(col0[j] + h * width, LANES), width)
            out.append(pltpu.make_async_copy(main_hbm.at[b, :, cols], buf.at[slot], sems.at[slot, j]))
        return out

    @pl.when(step == 0)
    def _():
        for s in range(RET_RING - 1):
            for j, cp in enumerate(copies(s, s)):
                cp.start(priority=j % 2)

    for slot in range(RET_RING):
        @pl.when(step % RET_RING == slot)
        def _(slot=slot):
            ahead = step + (RET_RING - 1)

            @pl.when(ahead <= last)
            def _():
                for j, cp in enumerate(copies(ahead, (slot + RET_RING - 1) % RET_RING)):
                    cp.start(priority=j % 2)

            for cp in copies(step, slot):
                cp.wait()
            _ret_body(q_buf.at[slot], k_buf.at[slot], v_buf.at[slot],
                      cdec_ref, gng_ref, gnb_ref, o_ref, state, st_in)


def _ret_body(q_ref, k_ref, v_ref, cdec_ref, gng_ref, gnb_ref, o_ref, state, st_in):
    C = RET_C
    nc = q_ref.shape[0] // C
    cdec = cdec_ref[0]
    gng = gng_ref[0]
    gnb = gnb_ref[0]
    chunk = lambda ci: pl.ds(pl.multiple_of(ci * C, C), C)

    state[...] = jnp.zeros_like(state)
    st_in[0] = jnp.zeros(st_in.shape[1:], BF16)

    def advance(ci, carry):
        rows = chunk(ci)
        kv = lax.dot_general(k_ref[rows, :], v_ref[rows, :], (((0,), (0,)), ((), ())),
                             preferred_element_type=F32)
        new = (state[...] + kv) * cdec
        state[...] = new
        st_in[ci + 1] = new.astype(BF16)
        return carry

    lax.fori_loop(0, nc - 1, advance, 0, unroll=True)

    qi = lax.broadcasted_iota(jnp.int32, (C, C), 0)
    kj = lax.broadcasted_iota(jnp.int32, (C, C), 1)
    causal = qi >= kj

    def emit(ci, carry):
        rows = chunk(ci)
        q = q_ref[rows, :]
        s = lax.dot_general(q, k_ref[rows, :], (((1,), (1,)), ((), ())), preferred_element_type=F32)
        s = jnp.where(causal, s, 0.0).astype(BF16)
        o = (jnp.dot(s, v_ref[rows, :], preferred_element_type=F32)
             + jnp.dot(q, st_in[ci], preferred_element_type=F32))
        mu = jnp.mean(o, axis=-1, keepdims=True)
        oc = o - mu
        var = jnp.mean(oc * oc, axis=-1, keepdims=True)
        on = oc * lax.rsqrt(var + GN_EPS) * gng + gnb
        o_ref[0, rows, :] = on.astype(BF16)
        return carry

    lax.fori_loop(0, nc, emit, 0, unroll=True)


def _retention_parts(main3, gn_g, gn_b):
    B, S, _ = main3.shape
    H, C = RET_HEADS, RET_C
    assert S % C == 0
    assert B * H >= RET_RING
    cdec = jnp.exp(jnp.asarray(RET_LOG_GAMMA, F32) * C)[:, None, None]
    in_specs = [pl.BlockSpec(memory_space=pl.ANY),
                pl.BlockSpec((1, 1, 1), lambda b, h: (h, 0, 0)),
                pl.BlockSpec((1, 1, RET_DV), lambda b, h: (h, 0, 0)),
                pl.BlockSpec((1, 1, RET_DV), lambda b, h: (h, 0, 0))]
    args = [main3, cdec, gn_g.reshape(H, 1, RET_DV), gn_b.reshape(H, 1, RET_DV)]
    ring = lambda w: pltpu.VMEM((RET_RING, S, w), BF16)
    return _CallParts(_ret_kernel, (B, H), in_specs, args,
                      [pl.BlockSpec((1, S, RET_DV), lambda b, h: (b, 0, h))],
                      [jax.ShapeDtypeStruct((B, S, RET_V_W), BF16)],
                      [pltpu.VMEM((RET_DK, RET_DV), F32), pltpu.VMEM((S // C, RET_DK, RET_DV), BF16),
                       ring(RET_DK), ring(RET_DK), ring(RET_DV),
                       pltpu.SemaphoreType.DMA((RET_RING, 3))])


def _t5_bucket(dist):
    dist = np.asarray(dist)
    max_exact = REL_BUCKETS // 2
    d_f = np.maximum(dist, 1).astype(np.float32)
    large = max_exact + (np.log(d_f / np.float32(max_exact)) / np.float32(math.log(REL_MAX_DIST / max_exact))
                         * np.float32(REL_BUCKETS - max_exact)).astype(np.int32)
    large = np.minimum(large, REL_BUCKETS - 1)
    return np.where(dist < max_exact, dist, large).astype(np.int32)


def _attn_unit(n, q_ref, k_ref, v_ref, bias_scr, vaug_scr, o_ref, stats_ref):
    blk, dh = ATT_BLK, ATT_HEAD_DIM
    windowed = k_ref.shape[0] > blk
    if windowed and isinstance(n, int):
        win, first = pl.ds(max(n - 1, 0) * blk, 2 * blk), int(n == 0)
    elif windowed:
        win = pl.ds(pl.multiple_of(jnp.maximum(n - 1, 0) * blk, blk), 2 * blk)
        first = (n == 0).astype(jnp.int32)
    heads = ATT_HEADS_PER_GROUP
    lane = lax.broadcasted_iota(jnp.int32, (blk, LSE_W), 1)
    stats = jnp.ones((blk, LSE_W), F32)
    for h in range(heads):
        hs = slice(h * dh, (h + 1) * dh)
        if windowed:
            kb, vb, bias = k_ref[win, hs], v_ref[win, hs], bias_scr[first, h]
        else:
            kb, vb, bias = k_ref[:, hs], v_ref[:, hs], bias_scr[h]
        vaug_scr[h, :, :dh] = vb
        s = lax.dot_general(q_ref[:, hs], kb, (((1,), (1,)), ((), ())), preferred_element_type=F32) + bias
        mx = jnp.max(s, axis=-1, keepdims=True)
        e = jnp.exp(s - mx).astype(BF16)
        o_den = jnp.dot(e, vaug_scr[h], preferred_element_type=F32)
        o_ref[:, hs] = o_den[:, :dh].astype(BF16)
        stats = jnp.where(lane == h, mx, jnp.where(lane == heads + h, o_den[:, dh:], stats))
    stats_ref[...] = stats


def _attn_init(tab_ref, bk0, bk1, bk2, bias0, bias1, bias2, vaug0, vaug1, vaug2):
    blk = ATT_BLK

    @pl.when((pl.program_id(0) == 0) & (pl.program_id(1) == 0))
    def _():
        for vaug in (vaug0, vaug1, vaug2):
            vaug[...] = jnp.ones(vaug.shape, BF16)
        qi = lax.broadcasted_iota(jnp.int32, (blk, 2 * blk), 0)
        kj = lax.broadcasted_iota(jnp.int32, (blk, 2 * blk), 1)
        m = blk + qi - kj
        band = (m >= 0) & (m <= blk)
        masked = jnp.full((blk, blk), NEG, F32)
        for gi, (bk_ref, bias_scr) in enumerate(((bk0, bias0), (bk1, bias1), (bk2, bias2))):
            bucket = bk_ref[...]
            for h in range(ATT_HEADS_PER_GROUP):
                col = gi * ATT_HEADS_PER_GROUP + h
                acc = jnp.zeros((blk, 2 * blk), F32)
                for t in range(REL_BUCKETS):
                    acc = jnp.where(bucket == t, tab_ref[t, col], acc)
                tile = jnp.where(band, acc, NEG)
                if len(bias_scr.shape) == 4:
                    bias_scr[0, h] = tile
                    bias_scr[1, h] = jnp.concatenate([tile[:, blk:], masked], axis=1)
                else:
                    bias_scr[h] = tile[:, blk:]


ATT_RING = 3


def _attn_kernel(tab_ref, main_hbm, k0, v0, att1_hbm, att2_hbm, bk0, bk1, bk2,
                 o0, l0, o1, l1, o2, l2, bias0, bias1, bias2, vaug0, vaug1, vaug2,
                 q0_buf, q1_buf, k1_buf, v1_buf, q2_buf, k2_buf, v2_buf, sems, *, nb1, per_step):
    n_steps = pl.num_programs(1)
    step = pl.program_id(0) * n_steps + pl.program_id(1)
    last = pl.num_programs(0) * n_steps - 1
    gw = ATT_GROUP_W
    ns1 = per_step // nb1

    def copies(s, slot):
        b, i = s // n_steps, s % n_steps
        rows0 = pl.ds(pl.multiple_of(i * (per_step * ATT_BLK), ATT_BLK), per_step * ATT_BLK)
        srcs = [main_hbm.at[b, rows0, pl.ds(COL_ATT, gw)]]
        srcs += [att1_hbm.at[b, pl.ds(i * ns1, ns1), :, pl.ds(part * gw, gw)] for part in range(3)]
        srcs += [att2_hbm.at[b, pl.ds(i * per_step, per_step), :, pl.ds(part * gw, gw)] for part in range(3)]
        bufs = (q0_buf, q1_buf, k1_buf, v1_buf, q2_buf, k2_buf, v2_buf)
        return [pltpu.make_async_copy(src, buf.at[slot], sems.at[slot, j])
                for j, (src, buf) in enumerate(zip(srcs, bufs))]

    slot = step % ATT_RING

    @pl.when(step == 0)
    def _():
        for s in range(ATT_RING - 1):
            for j, cp in enumerate(copies(s, s)):
                cp.start(priority=j % 2)

    ahead = step + (ATT_RING - 1)

    @pl.when(ahead <= last)
    def _():
        for j, cp in enumerate(copies(ahead, ahead % ATT_RING)):
            cp.start(priority=j % 2)

    _attn_init(tab_ref, bk0, bk1, bk2, bias0, bias1, bias2, vaug0, vaug1, vaug2)
    for cp in copies(step, slot):
        cp.wait()
    _attn_blocks(pl.program_id(1), q0_buf.at[slot], k0, v0, q1_buf.at[slot], k1_buf.at[slot], v1_buf.at[slot],
                 q2_buf.at[slot], k2_buf.at[slot], v2_buf.at[slot], o0, l0, o1, l1, o2, l2,
                 bias0, bias1, bias2, vaug0, vaug1, vaug2, nb1=nb1, per_step=per_step)


def _attn_blocks(i, q0, k0, v0, q1, k1, v1, q2, k2, v2, o0, l0, o1, l1, o2, l2,
                 bias0, bias1, bias2, vaug0, vaug1, vaug2, *, nb1, per_step):
    blk = ATT_BLK
    for u in range(per_step):
        rows = pl.ds(u * blk, blk)
        _attn_unit(i * per_step + u, q0.at[rows], k0, v0, bias0, vaug0.at[u], o0.at[rows], l0.at[rows])
        sub, n1 = divmod(u, nb1)
        rows1 = pl.ds(n1 * blk, blk)
        _attn_unit(n1, q1.at[sub, rows1], k1.at[sub], v1.at[sub], bias1, vaug1.at[u],
                   o1.at[sub, rows1], l1.at[sub, rows1])
        _attn_unit(0, q2.at[u], k2.at[u], v2.at[u], bias2, vaug2.at[u], o2.at[u], l2.at[u])


def _attention_parts(main3, att1, att2, rel_bias):
    B, S, _ = main3.shape
    blk, gw = ATT_BLK, ATT_GROUP_W
    steps = S // blk
    lens = [S // dil for _, dil in ATT_GROUPS]
    nbs = [L // blk for L in lens]
    assert all(win // dil == blk for win, dil in ATT_GROUPS)
    assert nbs[0] == steps and nbs[1] >= 2 and nbs[2] == 1
    c0 = COL_ATT // gw
    nb1 = nbs[1]
    G = ATT_BLOCKS_PER_STEP
    assert G % nb1 == 0 and steps % G == 0
    ns1 = G // nb1
    assert B * (steps // G) >= ATT_RING
    in_specs = [pl.BlockSpec(memory_space=pltpu.SMEM),
                pl.BlockSpec(memory_space=pl.ANY),
                pl.BlockSpec((None, S, gw), lambda b, i: (b, 0, c0 + 1)),
                pl.BlockSpec((None, S, gw), lambda b, i: (b, 0, c0 + 2)),
                pl.BlockSpec(memory_space=pl.ANY),
                pl.BlockSpec(memory_space=pl.ANY)]
    args = [rel_bias, main3, main3, main3, att1, att2]
    qi = np.arange(blk)[:, None]
    kj = np.arange(2 * blk)[None, :]
    m = blk + qi - kj
    for _, dil in ATT_GROUPS:
        in_specs.append(_const_spec((blk, 2 * blk)))
        args.append(jnp.asarray(_t5_bucket(np.clip(m, 0, blk) * dil)))
    out_specs, out_shapes = [], []
    for (_, dil), nb in zip(ATT_GROUPS, nbs):
        for w, dt in ((gw, BF16), (LSE_W, F32)):
            if nb == steps:
                out_specs.append(pl.BlockSpec((None, G * blk, w), lambda b, i: (b, i, 0)))
                out_shapes.append(jax.ShapeDtypeStruct((B, S, w), dt))
            elif nb == 1:
                out_specs.append(pl.BlockSpec((None, G, blk, w), lambda b, i: (b, i, 0, 0)))
                out_shapes.append(jax.ShapeDtypeStruct((B, dil, S // dil, w), dt))
            else:
                out_specs.append(pl.BlockSpec((None, ns1, lens[1], w), lambda b, i: (b, i, 0, 0)))
                out_shapes.append(jax.ShapeDtypeStruct((B, dil, S // dil, w), dt))
    heads = ATT_HEADS_PER_GROUP
    scratch = [pltpu.VMEM((2, heads, blk, 2 * blk), F32), pltpu.VMEM((2, heads, blk, 2 * blk), F32),
               pltpu.VMEM((heads, blk, blk), F32),
               pltpu.VMEM((G, heads, 2 * blk, 2 * ATT_HEAD_DIM), BF16),
               pltpu.VMEM((G, heads, 2 * blk, 2 * ATT_HEAD_DIM), BF16),
               pltpu.VMEM((G, heads, blk, 2 * ATT_HEAD_DIM), BF16)]
    scratch += [pltpu.VMEM((ATT_RING, G * blk, gw), BF16)]
    scratch += [pltpu.VMEM((ATT_RING, ns1, lens[1], gw), BF16)] * 3
    scratch += [pltpu.VMEM((ATT_RING, G, blk, gw), BF16)] * 3
    scratch += [pltpu.SemaphoreType.DMA((ATT_RING, 7))]
    return _CallParts(functools.partial(_attn_kernel, nb1=nb1, per_step=G), (B, steps // G),
                      in_specs, args, out_specs, out_shapes, scratch)


class _CallParts(NamedTuple):
    body: object
    grid: tuple
    in_specs: list
    args: list
    out_specs: list
    out_shapes: list
    scratch: list


def _call(parts, name):
    return pl.pallas_call(
        parts.body,
        out_shape=parts.out_shapes,
        grid=parts.grid,
        in_specs=parts.in_specs,
        out_specs=parts.out_specs,
        scratch_shapes=parts.scratch,
        compiler_params=_params(("arbitrary",) * len(parts.grid)),
        name=name,
    )(*parts.args)


def _mixers(main3, att1, att2, rel_bias, gn_g, gn_b):
    retg, = _call(_retention_parts(main3, gn_g, gn_b), "ret")
    att = _call(_attention_parts(main3, att1, att2, rel_bias), "attn")
    return retg, att[0::2], att[1::2]


def _merge_kernel(x_ref, g_ref, mod_ref, retn_ref, rgate_ref, o0_ref, o1_ref, o2_ref, s0_ref, s1_ref, s2_ref,
                  wg_ref, wr_ref, wa_ref, wo_ref, out_ref, o1_scr, o2_scr, s1_scr, s2_scr):
    tm = x_ref.shape[0]
    for dil, o_ref, s_ref, o_scr, s_scr in ((ATT_GROUPS[1][1], o1_ref, s1_ref, o1_scr, s1_scr),
                                            (ATT_GROUPS[2][1], o2_ref, s2_ref, o2_scr, s2_scr)):
        n = tm // dil
        for r in range(dil):
            s_scr[pl.ds(r, n, stride=dil), :] = s_ref[0, r]
            for h in range(ATT_HEADS_PER_GROUP):
                hs = slice(h * ATT_HEAD_DIM, (h + 1) * ATT_HEAD_DIM)
                o_scr[h, pl.ds(r, n, stride=dil), :] = o_ref[0, r, :, hs].astype(F32)
    part = tm // MERGE_ROW_PARTS
    for p in range(MERGE_ROW_PARTS):
        rows = slice(p * part, (p + 1) * part)
        retg = retn_ref[rows, :] * rgate_ref[rows, :]
        ret_out = jnp.dot(retg, wr_ref[...], preferred_element_type=F32)
        x = x_ref[rows, :]
        h = _modulated_norm(x, g_ref[...], mod_ref, 0).astype(BF16)
        gates = jax.nn.sigmoid(jnp.dot(h, wg_ref[...], preferred_element_type=F32))
        stats = (s0_ref[rows, :], s1_scr[rows, :], s2_scr[rows, :])
        top = jnp.maximum(jnp.maximum(stats[0], stats[1]), stats[2])
        a = [jnp.exp(st - top) for st in stats]
        lane = lax.broadcasted_iota(jnp.int32, top.shape, 1)
        heads = ATT_HEADS_PER_GROUP
        den = sum(ag * pltpu.roll(st, LSE_W - heads, axis=1) for ag, st in zip(a, stats))
        inv = 1.0 / jnp.where(lane < heads, den, 1.0)
        w0, w1, w2 = (ag * inv for ag in a)
        parts = []
        for hd in range(heads):
            hs = slice(hd * ATT_HEAD_DIM, (hd + 1) * ATT_HEAD_DIM)
            parts.append(w0[:, hd:hd + 1] * o0_ref[rows, hs].astype(F32)
                         + w1[:, hd:hd + 1] * o1_scr[hd, rows, :]
                         + w2[:, hd:hd + 1] * o2_scr[hd, rows, :])
        att = jnp.concatenate(parts, axis=-1).astype(BF16)
        att_out = jnp.dot(att, wa_ref[...], preferred_element_type=F32)
        merged = gates[:, :D_MODEL] * ret_out + gates[:, D_MODEL:] * att_out
        y = jnp.dot(merged.astype(BF16), wo_ref[...], preferred_element_type=F32)
        out_ref[rows, :] = x + mod_ref[0, :, 2 * D_MODEL:3 * D_MODEL] * y


def _merge(x2, norm_g, mod3, retn2, main2, o, lse, w_gate, w_ret_out, w_att_out, w_o, batch, seq):
    tm = ROW_TM
    gate_col = (2 * RET_QK_W + RET_V_W) // RET_V_W
    assert gate_col * RET_V_W == 2 * RET_QK_W + RET_V_W
    tps = seq // tm
    row = lambda w: pl.BlockSpec((tm, w), lambda i: (i, 0))
    sub = lambda d, w: pl.BlockSpec((1, d, tm // d, w), lambda i: (i // tps, 0, i % tps, 0))
    d1, d2 = ATT_GROUPS[1][1], ATT_GROUPS[2][1]
    return pl.pallas_call(
        _merge_kernel,
        out_shape=jax.ShapeDtypeStruct((batch * seq, D_MODEL), F32),
        grid=(batch * seq // tm,),
        in_specs=[row(D_MODEL),
                  _const_spec((1, D_MODEL)),
                  pl.BlockSpec((1, 1, 6 * D_MODEL), lambda i: (i // tps, 0, 0)),
                  row(RET_V_W),
                  pl.BlockSpec((tm, RET_V_W), lambda i: (i, gate_col)),
                  row(ATT_GROUP_W), sub(d1, ATT_GROUP_W), sub(d2, ATT_GROUP_W),
                  row(LSE_W), sub(d1, LSE_W), sub(d2, LSE_W),
                  _const_spec((D_MODEL, 2 * D_MODEL)),
                  _const_spec((RET_V_W, D_MODEL)),
                  _const_spec((ATT_GROUP_W, D_MODEL)),
                  _const_spec((D_MODEL, D_MODEL))],
        out_specs=row(D_MODEL),
        scratch_shapes=[pltpu.VMEM((ATT_HEADS_PER_GROUP, tm, ATT_HEAD_DIM), F32)] * 2
                       + [pltpu.VMEM((tm, LSE_W), F32)] * 2,
        compiler_params=_params(("arbitrary",)),
        name="merge",
    )(x2, norm_g, mod3, retn2, main2, *o, *lse, w_gate, w_ret_out, w_att_out, w_o)


def _mlp_kernel(x_ref, mod_ref, g2_ref, gf_ref, w1_ref, w2_ref, out_ref, *, final):
    part = x_ref.shape[0] // MLP_ROW_PARTS
    for p in range(MLP_ROW_PARTS):
        rows = slice(p * part, (p + 1) * part)
        x = x_ref[rows, :]
        h = _modulated_norm(x, g2_ref[...], mod_ref, 3).astype(BF16)
        u = jnp.maximum(jnp.dot(h, w1_ref[...], preferred_element_type=F32), 0.0)
        y = jnp.dot((u * u).astype(BF16), w2_ref[...], preferred_element_type=F32)
        x = x + mod_ref[0, :, 5 * D_MODEL:6 * D_MODEL] * y
        out_ref[rows, :] = _rms(x, gf_ref[...]) if final else x


def _mlp(x2, mod3, norm2_g, norm_f_g, w1, w2, seq, final):
    rows = x2.shape[0]
    tm = MLP_TM
    tps = seq // tm
    return pl.pallas_call(
        functools.partial(_mlp_kernel, final=final),
        out_shape=jax.ShapeDtypeStruct((rows, D_MODEL), F32),
        grid=(rows // tm,),
        in_specs=[pl.BlockSpec((tm, D_MODEL), lambda i: (i, 0)),
                  pl.BlockSpec((1, 1, 6 * D_MODEL), lambda i: (i // tps, 0, 0)),
                  _const_spec((1, D_MODEL)),
                  _const_spec((1, D_MODEL)),
                  _const_spec((D_MODEL, D_FF)),
                  _const_spec((D_FF, D_MODEL))],
        out_specs=pl.BlockSpec((tm, D_MODEL), lambda i: (i, 0)),
        compiler_params=_params(("arbitrary",)),
        name="mlp",
    )(x2, mod3, norm2_g, norm_f_g, w1, w2)


def kernel(x, c, w_ada, b_ada, norm1_g, w_in, rel_bias, ret_gn_g, ret_gn_b, w_ret_out, w_att_out,
           w_o, norm2_g, w_ff1, w_ff2, norm_f_g):
    B, S, D = x.shape
    depth = w_ada.shape[0]
    half = RET_DK // 2
    ang = np.arange(S, dtype=np.float64)[:, None] * ROPE_BASE ** (-np.arange(half, dtype=np.float64) / half)
    cos, sin = jnp.asarray(np.cos(ang), F32), jnp.asarray(np.sin(ang), F32)
    x2 = x.reshape(B * S, D)
    for l in range(depth):
        g1 = norm1_g[l].reshape(1, D)
        mod3 = _mod(c, w_ada[l], b_ada[l]).reshape(B, 1, 6 * D)
        main2, att1, att2, (w_gate_bf, w_ret_bf, w_o_bf, w_ff1_bf, w_ff2_bf) = _proj(
            x2, g1, mod3, cos, sin, w_in[l], [w_ret_out[l], w_o[l], w_ff1[l], w_ff2[l]], B, S)
        main3 = main2.reshape(B, S, main2.shape[1])
        retg, o, stats = _mixers(main3, att1, att2, rel_bias, ret_gn_g[l], ret_gn_b[l])
        o = [o[0].reshape(B * S, ATT_GROUP_W), o[1], o[2]]
        stats = [stats[0].reshape(B * S, LSE_W), stats[1], stats[2]]
        x2 = _merge(x2, g1, mod3, retg.reshape(B * S, RET_V_W), main2, o, stats,
                    w_gate_bf, w_ret_bf, w_att_out[l].astype(BF16), w_o_bf, B, S)
        x2 = _mlp(x2, mod3, norm2_g[l].reshape(1, D), norm_f_g.reshape(1, D),
                  w_ff1_bf, w_ff2_bf, S, final=l == depth - 1)
    return x2.reshape(B, S, D)
```
